```python
import math
import jax, jax.numpy as jnp
from jax import lax
import numpy as np

D_MODEL = 1024
BATCH = 2
SEQ = 8192
DEPTH = 1
DEC_BATCH = 4
DEC_SEQ = 4096
PAST_LEN = 128

HEAD_DIM = 64
HEADS_PER_GROUP = 4
GROUPS = ((128, 1), (512, 4), (2048, 16))
N_GROUPS = len(GROUPS)
N_ATT_HEADS = N_GROUPS * HEADS_PER_GROUP
ATT_W = N_ATT_HEADS * HEAD_DIM
ATT_OUT = HEADS_PER_GROUP * HEAD_DIM
F_GROUPS = 6
F_CH = 128
F_W = F_GROUPS * F_CH
IN_W = 3 * ATT_W + F_W
NUM_BUCKETS = 32
MAX_DISTANCE = 1024
N_EXPERTS = 16
CAPACITY_FACTOR = 2
D_FF = 2048
N_BRANCH = 2
EPS = 1e-6
NEG = -1e30

kernel_name = "hybrid_dilated_fnet_ec_encoder"


def rmsnorm(x, g):
    xf = x.astype(jnp.float32)
    y = xf * lax.rsqrt(jnp.mean(xf * xf, axis=-1, keepdims=True) + EPS) * g.astype(jnp.float32)
    return y.astype(x.dtype)


def t5_bucket(rel):
    nb = NUM_BUCKETS // 2
    max_exact = nb // 2
    ret = (rel > 0).astype(np.int32) * nb
    n = np.abs(rel)
    large = max_exact + (np.log(np.maximum(n, max_exact) / max_exact)
                         / np.log(MAX_DISTANCE / max_exact) * (nb - max_exact)).astype(np.int32)
    large = np.minimum(large, nb - 1)
    return (ret + np.where(n < max_exact, n, large)).astype(np.int32)


def dilated_group(q, k, v, bias_tab, dil, half_keys):
    B, S, H, Dh = q.shape
    blk = half_keys
    L = S // dil
    nb = -(-L // blk)
    Lp = nb * blk

    def to_blocks(t):
        t = t.reshape(B, L, dil, H, Dh)
        t = jnp.pad(t, ((0, 0), (0, Lp - L), (0, 0), (0, 0), (0, 0)))
        return t.reshape(B, nb, blk, dil, H, Dh)

    def windows(t):
        tp = jnp.pad(t, ((0, 0), (1, 1), (0, 0), (0, 0), (0, 0), (0, 0)))
        return jnp.concatenate([tp[:, :-2], tp[:, 1:-1], tp[:, 2:]], axis=2)

    qb = to_blocks(q * (1.0 / math.sqrt(Dh)))
    kw = windows(to_blocks(k))
    vw = windows(to_blocks(v))

    qi = np.arange(blk)[:, None]
    kj = np.arange(3 * blk)[None, :]
    delta = kj - blk - qi
    band = np.abs(delta) <= half_keys
    kglob = np.arange(nb)[:, None] * blk + np.arange(3 * blk)[None, :] - blk
    valid = (kglob >= 0) & (kglob < L)
    mask = band[None, :, :] & valid[:, None, :]
    bucket = t5_bucket(dil * delta)
    bias = jnp.transpose(bias_tab[bucket], (0, 2, 1)).astype(jnp.float32)

    s = jnp.einsum('bnqrhe,bnkrhe->bnqrhk', qb, kw, preferred_element_type=jnp.float32)
    s = s + bias[None, None, :, None, :, :]
    s = jnp.where(jnp.asarray(mask)[None, :, :, None, None, :], s, NEG)
    m = jnp.max(s, axis=-1, keepdims=True)
    p = jnp.exp(s - m)
    den = jnp.sum(p, axis=-1)
    o = jnp.einsum('bnqrhk,bnkrhe->bnqrhe', p, vw.astype(jnp.float32)) / den[..., None]
    lse = m[..., 0] + jnp.log(den)
    o = o.reshape(B, Lp, dil, H, Dh)[:, :L].reshape(B, S, H, Dh)
    lse = lse.reshape(B, Lp, dil, H)[:, :L].reshape(B, S, H)
    return o, lse


def fourier_mix(u):
    B, S, _ = u.shape
    uf = u.astype(jnp.float32).reshape(B, S, F_GROUPS, F_CH)
    y = jnp.fft.fftn(uf, axes=(1, 3), norm="ortho").real
    return y.reshape(B, S, F_W).astype(u.dtype)


def expert_choice(xn, w_router, w_eg, w_eu, w_ed):
    B, S, D = xn.shape
    T = B * S
    C = CAPACITY_FACTOR * T // N_EXPERTS
    xt = xn.reshape(T, D)
    aff = jax.nn.softmax((xt @ w_router).astype(jnp.float32), axis=-1)
    gate, idx = lax.top_k(aff.T, C)
    xe = xt[idx]
    h = jax.nn.silu(jnp.einsum('ecd,edf->ecf', xe, w_eg)) * jnp.einsum('ecd,edf->ecf', xe, w_eu)
    ye = jnp.einsum('ecf,efd->ecd', h, w_ed) * gate[..., None].astype(xn.dtype)
    out = jnp.zeros((T, D), xn.dtype).at[idx.reshape(-1)].add(ye.reshape(-1, D))
    return out.reshape(B, S, D)


def encoder(x, rel_bias, norm1_g, w_in, w_attn_br, w_four_br, w_gate, b_gate, w_out,
            norm2_g, w_router, w_exp_gate, w_exp_up, w_exp_down, final_g):
    B, S, D = x.shape
    for l in range(DEPTH):
        xn = rmsnorm(x, norm1_g[l])
        proj = xn @ w_in[l]
        q = proj[..., :ATT_W].reshape(B, S, N_GROUPS, HEADS_PER_GROUP, HEAD_DIM)
        k = proj[..., ATT_W:2 * ATT_W].reshape(B, S, N_GROUPS, HEADS_PER_GROUP, HEAD_DIM)
        v = proj[..., 2 * ATT_W:3 * ATT_W].reshape(B, S, N_GROUPS, HEADS_PER_GROUP, HEAD_DIM)
        u = proj[..., 3 * ATT_W:]

        outs, lses = [], []
        for g, (win, dil) in enumerate(GROUPS):
            o, lse = dilated_group(q[:, :, g], k[:, :, g], v[:, :, g],
                                   rel_bias[:, g * HEADS_PER_GROUP:(g + 1) * HEADS_PER_GROUP],
                                   dil, (win // 2) // dil)
            outs.append(o)
            lses.append(lse)
        wts = jax.nn.softmax(jnp.stack(lses, axis=0), axis=0)
        att = jnp.sum(wts[..., None] * jnp.stack(outs, axis=0), axis=0)
        att = att.reshape(B, S, ATT_OUT).astype(x.dtype)

        four = fourier_mix(u)

        gates = jax.nn.sigmoid((xn @ w_gate[l] + b_gate[l]).astype(jnp.float32))
        gates = gates.astype(x.dtype).reshape(B, S, N_BRANCH, D)
        mix = gates[:, :, 0] * (att @ w_attn_br[l]) + gates[:, :, 1] * (four @ w_four_br[l])
        x = x + mix @ w_out[l]

        x = x + expert_choice(rmsnorm(x, norm2_g[l]), w_router[l], w_exp_gate[l],
                              w_exp_up[l], w_exp_down[l])
    return rmsnorm(x, final_g)


def setup_inputs(seed: int = 0) -> dict:
    key = jax.random.key(seed)
    ks = jax.random.split(key, 16)
    f32 = jnp.float32

    def nrm(k, shape, scale):
        return jax.random.normal(k, shape, f32) * scale

    return {
        "x_prompt": nrm(ks[0], (BATCH, SEQ, D_MODEL), 1.0),
        "x_sample": nrm(ks[1], (DEC_BATCH, DEC_SEQ, D_MODEL), 1.0),
        "rel_bias": nrm(ks[2], (NUM_BUCKETS, N_ATT_HEADS), 0.5),
        "norm1_g": 1.0 + nrm(ks[3], (DEPTH, D_MODEL), 0.02),
        "w_in": nrm(ks[4], (DEPTH, D_MODEL, IN_W), D_MODEL ** -0.5),
        "w_attn_br": nrm(ks[5], (DEPTH, ATT_OUT, D_MODEL), ATT_OUT ** -0.5),
        "w_four_br": nrm(ks[6], (DEPTH, F_W, D_MODEL), F_W ** -0.5),
        "w_gate": nrm(ks[7], (DEPTH, D_MODEL, N_BRANCH * D_MODEL), D_MODEL ** -0.5),
        "b_gate": nrm(ks[8], (DEPTH, N_BRANCH * D_MODEL), 0.02),
        "w_out": nrm(ks[9], (DEPTH, D_MODEL, D_MODEL), D_MODEL ** -0.5),
        "norm2_g": 1.0 + nrm(ks[10], (DEPTH, D_MODEL), 0.02),
        "w_router": nrm(ks[11], (DEPTH, D_MODEL, N_EXPERTS), D_MODEL ** -0.5),
        "w_exp_gate": nrm(ks[12], (DEPTH, N_EXPERTS, D_MODEL, D_FF), D_MODEL ** -0.5),
        "w_exp_up": nrm(ks[13], (DEPTH, N_EXPERTS, D_MODEL, D_FF), D_MODEL ** -0.5),
        "w_exp_down": nrm(ks[14], (DEPTH, N_EXPERTS, D_FF, D_MODEL), D_FF ** -0.5),
        "final_g": 1.0 + nrm(ks[15], (D_MODEL,), 0.02),
    }


def reference(x_prompt, x_sample, rel_bias, norm1_g, w_in, w_attn_br, w_four_br, w_gate,
              b_gate, w_out, norm2_g, w_router, w_exp_gate, w_exp_up, w_exp_down, final_g):
    y_prompt = encoder(x_prompt, rel_bias, norm1_g, w_in, w_attn_br, w_four_br, w_gate, b_gate,
                       w_out, norm2_g, w_router, w_exp_gate, w_exp_up, w_exp_down, final_g)
    y_sample = encoder(x_sample, rel_bias, norm1_g, w_in, w_attn_br, w_four_br, w_gate, b_gate,
                       w_out, norm2_g, w_router, w_exp_gate, w_exp_up, w_exp_down, final_g)
    return (y_prompt, y_sample)
```

```python
import functools
import math

import numpy as np
import jax
import jax.numpy as jnp
from jax import lax
from jax.experimental import pallas as pl
from jax.experimental.pallas import tpu as pltpu

D_MODEL = 1024
HEAD_DIM = 64
HEADS_PER_GROUP = 4
GROUPS = ((128, 1), (512, 4), (2048, 16))
N_GROUPS = len(GROUPS)
GROUP_W = HEADS_PER_GROUP * HEAD_DIM
ATT_W = N_GROUPS * GROUP_W
QKV_W = 3 * ATT_W
F_GROUPS = 6
F_CH = 128
F_W = F_GROUPS * F_CH
NUM_BUCKETS = 32
MAX_DISTANCE = 1024
N_EXPERTS = 16
CAPACITY_FACTOR = 2
D_FF = 2048
EPS = 1e-6
NEG = -1e30

HALF_KEYS = 64
ATT_SUB = 128
ROUTE_TILE = 256
SLOT_CHUNK = 64
LANES = 128
V7X_VMEM_LIMIT = 56 * 1024 * 1024

F32 = jnp.float32
BF16 = jnp.bfloat16
I32 = jnp.int32


def _params(sem):
    return pltpu.CompilerParams(dimension_semantics=sem, vmem_limit_bytes=V7X_VMEM_LIMIT)


def _dot(a, b):
    return jnp.dot(a, b, preferred_element_type=F32)


def _dot_nt(a, b):
    return lax.dot_general(a, b, (((1,), (1,)), ((), ())), preferred_element_type=F32)


def _ones_where(mask, dtype=F32):
    return jnp.where(mask, jnp.ones((), F32), jnp.zeros((), F32)).astype(dtype)


def _in_proj_kernel(x_ref, g_ref, win_ref, wg_ref, bg_ref, cs_ref, qkv_ref, vr_ref, vi_ref, gates_ref):
    x = x_ref[...]
    ms = jnp.mean(x * x, axis=-1, keepdims=True)
    xn = (x * lax.rsqrt(ms + EPS) * g_ref[...]).astype(BF16)
    q = _dot(xn, win_ref[:, 0:ATT_W]) * (1.0 / math.sqrt(HEAD_DIM))
    qkv_ref[:, 0:ATT_W] = q.astype(BF16)
    for c in (1, 2):
        qkv_ref[:, c * ATT_W:(c + 1) * ATT_W] = _dot(xn, win_ref[:, c * ATT_W:(c + 1) * ATT_W]).astype(BF16)
    u = _dot(xn, win_ref[:, QKV_W:QKV_W + F_W]).astype(BF16)
    cs = cs_ref[...]
    for g in range(F_GROUPS):
        a = _dot(u[:, g * F_CH:(g + 1) * F_CH], cs)
        vr_ref[:, g * F_CH:(g + 1) * F_CH] = a[:, :F_CH].astype(BF16)
        vi_ref[:, g * F_CH:(g + 1) * F_CH] = (-a[:, F_CH:]).astype(BF16)
    z = _dot(xn, wg_ref[...]) + bg_ref[...]
    gates_ref[...] = 1.0 / (1.0 + jnp.exp(-z))


def _in_proj(x, g1, w_in, w_gate, b_gate, cs):
    t = x.shape[0]
    tm = 256
    const = lambda i: (0, 0)
    row = lambda i: (i, 0)
    return pl.pallas_call(
        _in_proj_kernel,
        grid=(t // tm,),
        in_specs=[
            pl.BlockSpec((tm, D_MODEL), row),
            pl.BlockSpec((1, D_MODEL), const),
            pl.BlockSpec(w_in.shape, const),
            pl.BlockSpec(w_gate.shape, const),
            pl.BlockSpec((1, 2 * D_MODEL), const),
            pl.BlockSpec(cs.shape, const),
        ],
        out_specs=[
            pl.BlockSpec((tm, QKV_W), row),
            pl.BlockSpec((tm, F_W), row),
            pl.BlockSpec((tm, F_W), row),
            pl.BlockSpec((tm, 2 * D_MODEL), row),
        ],
        out_shape=[
            jax.ShapeDtypeStruct((t, QKV_W), BF16),
            jax.ShapeDtypeStruct((t, F_W), BF16),
            jax.ShapeDtypeStruct((t, F_W), BF16),
            jax.ShapeDtypeStruct((t, 2 * D_MODEL), F32),
        ],
        compiler_params=_params(("parallel",)),
        name="in_proj",
    )(x, g1, w_in, w_gate, b_gate, cs)


def _attention_kernel(q_ref, kp_ref, kc_ref, kn_ref, vp_ref, vc_ref, vn_ref, bias_ref, o_ref, lse_ref, *, tq, length):
    i = pl.program_id(2)
    kwin = jnp.concatenate([kp_ref[0], kc_ref[0], kn_ref[0]], axis=0)
    vwin = jnp.concatenate([vp_ref[0], vc_ref[0], vn_ref[0]], axis=0)
    win = ATT_SUB + 2 * HALF_KEYS
    lane_head = lax.broadcasted_iota(I32, (ATT_SUB, GROUP_W), 1) // HEAD_DIM
    col = lax.broadcasted_iota(I32, (ATT_SUB, win), 1)
    for sb in range(tq // ATT_SUB):
        off = sb * ATT_SUB
        q = q_ref[0, off:off + ATT_SUB, :]
        kw = kwin[off:off + win]
        vw = vwin[off:off + win]
        first = i * tq + (off - HALF_KEYS)
        valid = (col >= -first) & (col < length - first)
        qs = jnp.concatenate(
            [jnp.where(lane_head == h, q, jnp.zeros_like(q)) for h in range(HEADS_PER_GROUP)], axis=0)
        s_all = _dot_nt(qs, kw)
        ps, ms, ls = [], [], []
        for h in range(HEADS_PER_GROUP):
            s = s_all[h * ATT_SUB:(h + 1) * ATT_SUB] + bias_ref[h]
            s = jnp.where(valid, s, NEG)
            m = jnp.max(s, axis=-1, keepdims=True)
            p = jnp.exp(s - m)
            ls.append(jnp.sum(p, axis=-1, keepdims=True))
            ms.append(m)
            ps.append(p.astype(BF16))
        o_all = _dot(jnp.concatenate(ps, axis=0), vw)
        out = jnp.zeros((ATT_SUB, GROUP_W), F32)
        lse = jnp.zeros((ATT_SUB, GROUP_W), F32)
        for h in range(HEADS_PER_GROUP):
            oh = o_all[h * ATT_SUB:(h + 1) * ATT_SUB] * (1.0 / ls[h])
            out = jnp.where(lane_head == h, oh, out)
            lse = jnp.where(lane_head == h, ms[h] + jnp.log(ls[h]), lse)
        o_ref[0, off:off + ATT_SUB, :] = out
        lse_ref[0, off:off + ATT_SUB, :] = lse


def _attention(qkv, bias, g, batch, seq):
    dil = GROUPS[g][1]
    length = seq // dil
    tq = min(length, 512)
    nb = length // tq
    hb = tq // HALF_KEYS
    last_halo = length // HALF_KEYS - 1
    ncol = QKV_W // GROUP_W
    view = qkv.reshape(batch, length, dil * QKV_W)

    def cur(c):
        return lambda b, r, i: (b, i, r * ncol + c)

    def prev(c):
        return lambda b, r, i: (b, jnp.maximum(i * hb - 1, 0), r * ncol + c)

    def nxt(c):
        return lambda b, r, i: (b, jnp.minimum((i + 1) * hb, last_halo), r * ncol + c)

    kc, vc = N_GROUPS + g, 2 * N_GROUPS + g
    blk = lambda rows: (1, rows, GROUP_W)
    out_map = lambda b, r, i: (b, i, r)
    o, lse = pl.pallas_call(
        functools.partial(_attention_kernel, tq=tq, length=length),
        grid=(batch, dil, nb),
        in_specs=[
            pl.BlockSpec(blk(tq), cur(g)),
            pl.BlockSpec(blk(HALF_KEYS), prev(kc)),
            pl.BlockSpec(blk(tq), cur(kc)),
            pl.BlockSpec(blk(HALF_KEYS), nxt(kc)),
            pl.BlockSpec(blk(HALF_KEYS), prev(vc)),
            pl.BlockSpec(blk(tq), cur(vc)),
            pl.BlockSpec(blk(HALF_KEYS), nxt(vc)),
            pl.BlockSpec(bias.shape, lambda b, r, i: (0, 0, 0)),
        ],
        out_specs=[pl.BlockSpec(blk(tq), out_map), pl.BlockSpec(blk(tq), out_map)],
        out_shape=[jax.ShapeDtypeStruct((batch, length, dil * GROUP_W), F32)] * 2,
        compiler_params=_params(("parallel", "parallel", "parallel")),
        name=f"attention_g{g}",
    )(view, view, view, view, view, view, view, bias)
    t = batch * seq
    return o.reshape(t, GROUP_W), lse.reshape(t, GROUP_W)


def _t5_bucket(rel):
    nb = NUM_BUCKETS // 2
    max_exact = nb // 2
    ret = (rel > 0).astype(np.int32) * nb
    n = np.abs(rel)
    large = max_exact + (np.log(np.maximum(n, max_exact) / max_exact)
                         / np.log(MAX_DISTANCE / max_exact) * (nb - max_exact)).astype(np.int32)
    large = np.minimum(large, nb - 1)
    return (ret + np.where(n < max_exact, n, large)).astype(np.int32)


def _attention_bias(rel_bias, g):
    dil = GROUPS[g][1]
    qi = np.arange(ATT_SUB)[:, None]
    kj = np.arange(ATT_SUB + 2 * HALF_KEYS)[None, :]
    delta = kj - HALF_KEYS - qi
    band = np.abs(delta) <= HALF_KEYS
    bucket = _t5_bucket(dil * delta)
    tab = rel_bias[:, g * HEADS_PER_GROUP:(g + 1) * HEADS_PER_GROUP].astype(F32)
    bias = jnp.transpose(tab[bucket], (2, 0, 1))
    return jnp.where(jnp.asarray(band)[None], bias, NEG)


def _dft_mats(n):
    k = np.arange(n)
    ang = 2.0 * np.pi * ((k[:, None] * k[None, :]) % n) / n
    return np.cos(ang), np.sin(ang)


def _fft_stage1_kernel(vr_ref, vi_ref, m1_ref, twc_ref, tws_ref, zr_ref, zi_ref, *, n1, m):
    x = jnp.concatenate([vr_ref[0], vi_ref[0]], axis=0)
    z = _dot(m1_ref[...], x)
    zr, zi = z[:n1], z[n1:]
    twc, tws = twc_ref[0], tws_ref[0]
    for j in range(m):
        c = twc[:, j:j + 1]
        s = tws[:, j:j + 1]
        a = zr[:, j * F_W:(j + 1) * F_W]
        b = zi[:, j * F_W:(j + 1) * F_W]
        zr_ref[0, :, j * F_W:(j + 1) * F_W] = (a * c + b * s).astype(BF16)
        zi_ref[0, :, j * F_W:(j + 1) * F_W] = (b * c - a * s).astype(BF16)


def _fft_stage2_kernel(zr_ref, zi_ref, m2_ref, o_ref, *, kc, scale):
    m2 = m2_ref[...]
    for j in range(kc):
        x = jnp.concatenate([zr_ref[0, j], zi_ref[0, j]], axis=0)
        o_ref[0, :, j * F_W:(j + 1) * F_W] = _dot(m2, x) * scale


def _fourier(vr, vi, batch, seq):
    n2 = LANES
    n1 = seq // n2
    m = 8
    c1, s1 = _dft_mats(n1)
    m1 = jnp.asarray(np.block([[c1, s1], [-s1, c1]]), BF16)
    c2, s2 = _dft_mats(n2)
    m2 = jnp.asarray(np.concatenate([c2, s2], axis=1), BF16)
    k1 = np.arange(n1)[:, None]
    sv = np.arange(n2)[None, :]
    ang = 2.0 * np.pi * ((k1 * sv) % seq) / seq
    to_blocks = lambda a: jnp.asarray(a.reshape(n1, n2 // m, m).transpose(1, 0, 2), F32)
    twc, tws = to_blocks(np.cos(ang)), to_blocks(np.sin(ang))

    v3 = lambda a: a.reshape(batch, n1, n2 * F_W)
    blk = (1, n1, m * F_W)
    dmap = lambda b, j: (b, 0, j)
    tmap = lambda b, j: (j, 0, 0)
    zr, zi = pl.pallas_call(
        functools.partial(_fft_stage1_kernel, n1=n1, m=m),
        grid=(batch, n2 // m),
        in_specs=[
            pl.BlockSpec(blk, dmap),
            pl.BlockSpec(blk, dmap),
            pl.BlockSpec(m1.shape, lambda b, j: (0, 0)),
            pl.BlockSpec((1, n1, m), tmap),
            pl.BlockSpec((1, n1, m), tmap),
        ],
        out_specs=[pl.BlockSpec(blk, dmap), pl.BlockSpec(blk, dmap)],
        out_shape=[jax.ShapeDtypeStruct((batch, n1, n2 * F_W), BF16)] * 2,
        compiler_params=_params(("parallel", "parallel")),
        name="fft_stage1",
    )(v3(vr), v3(vi), m1, twc, tws)

    kc = 8
    v4 = lambda a: a.reshape(batch, n1, n2, F_W)
    zblk = (1, kc, n2, F_W)
    zmap = lambda b, j: (b, j, 0, 0)
    out = pl.pallas_call(
        functools.partial(_fft_stage2_kernel, kc=kc, scale=1.0 / math.sqrt(seq * F_CH)),
        grid=(batch, n1 // kc),
        in_specs=[
            pl.BlockSpec(zblk, zmap),
            pl.BlockSpec(zblk, zmap),
            pl.BlockSpec(m2.shape, lambda b, j: (0, 0)),
        ],
        out_specs=pl.BlockSpec((1, n2, kc * F_W), lambda b, j: (b, 0, j)),
        out_shape=jax.ShapeDtypeStruct((batch, n2, n1 * F_W), F32),
        compiler_params=_params(("parallel", "parallel")),
        name="fft_stage2",
    )(v4(zr), v4(zi), m2)
    return out.reshape(batch * seq, F_W)


def _mix_kernel(x_ref, o0_ref, o1_ref, o2_ref, l0_ref, l1_ref, l2_ref, four_ref, gates_ref,
                wa_ref, wf_ref, wo_ref, g2_ref, wr_ref, x1_ref, xn_ref, aff_ref, afft_ref):
    l0, l1, l2 = l0_ref[...], l1_ref[...], l2_ref[...]
    mx = jnp.maximum(jnp.maximum(l0, l1), l2)
    e0, e1, e2 = jnp.exp(l0 - mx), jnp.exp(l1 - mx), jnp.exp(l2 - mx)
    att = (e0 * o0_ref[...] + e1 * o1_ref[...] + e2 * o2_ref[...]) * (1.0 / (e0 + e1 + e2))
    a_br = _dot(att.astype(BF16), wa_ref[...])
    f_br = _dot(four_ref[...].astype(BF16), wf_ref[...])
    mix = gates_ref[:, :D_MODEL] * a_br + gates_ref[:, D_MODEL:] * f_br
    x1 = x_ref[...] + _dot(mix.astype(BF16), wo_ref[...])
    x1_ref[...] = x1
    ms = jnp.mean(x1 * x1, axis=-1, keepdims=True)
    xn = x1 * lax.rsqrt(ms + EPS) * g2_ref[...]
    xn_ref[...] = xn.astype(BF16)
    logits = jnp.dot(xn, wr_ref[...], preferred_element_type=F32, precision=lax.Precision.HIGHEST)
    lane = lax.broadcasted_iota(I32, logits.shape, 1)
    logits = jnp.where(lane < N_EXPERTS, logits, NEG)
    p = jnp.exp(logits - jnp.max(logits, axis=-1, keepdims=True))
    aff = p * (1.0 / jnp.sum(p, axis=-1, keepdims=True))
    aff_ref[...] = aff
    afft_ref[...] = aff.T[:N_EXPERTS]


def _mix(x, os_, ls_, four, gates, w_attn, w_four, w_out, g2, w_router_pad):
    t = x.shape[0]
    tm = 256
    const = lambda i: (0, 0)
    row = lambda i: (i, 0)
    rows = lambda w: pl.BlockSpec((tm, w), row)
    full = lambda a: pl.BlockSpec(a.shape, const)
    return pl.pallas_call(
        _mix_kernel,
        grid=(t // tm,),
        in_specs=[rows(D_MODEL)] + [rows(GROUP_W)] * 6 + [rows(F_W), rows(2 * D_MODEL),
                  full(w_attn), full(w_four), full(w_out), full(g2), full(w_router_pad)],
        out_specs=[rows(D_MODEL), rows(D_MODEL), rows(LANES), pl.BlockSpec((N_EXPERTS, tm), lambda i: (0, i))],
        out_shape=[
            jax.ShapeDtypeStruct((t, D_MODEL), F32),
            jax.ShapeDtypeStruct((t, D_MODEL), BF16),
            jax.ShapeDtypeStruct((t, LANES), F32),
            jax.ShapeDtypeStruct((N_EXPERTS, t), F32),
        ],
        compiler_params=_params(("parallel",)),
        name="mix",
    )(x, *os_, *ls_, four, gates, w_attn, w_four, w_out, g2, w_router_pad)


def _route_kernel(afft_ref, su_ref, u_ref, tau_ref, need_ref, beq_ref, bsel_ref,
                  taut_ref, needt_ref, beqt_ref, bselt_ref, *, tokens):
    cap = CAPACITY_FACTOR * tokens // N_EXPERTS
    ntile = tokens // ROUTE_TILE
    shape = (N_EXPERTS, LANES)
    lane = lax.broadcasted_iota(I32, shape, 1)

    def keys(start, width):
        return lax.bitcast_convert_type(afft_ref[:, pl.ds(pl.multiple_of(start, LANES), width)], I32)

    def count(pred):
        def body(c, acc):
            return acc + _ones_where(pred(keys(c * LANES, LANES)))
        acc = lax.fori_loop(0, tokens // LANES, body, jnp.zeros(shape, F32))
        return jnp.sum(acc, axis=1, keepdims=True)

    def bit_body(i, prefix):
        cand = prefix | lax.shift_left(jnp.ones(shape, I32), jnp.full(shape, 30 - i, I32))
        tot = count(lambda k: k >= cand)
        return jnp.where(tot >= cap, cand, prefix)

    tau = lax.fori_loop(0, 31, bit_body, jnp.zeros(shape, I32))
    n_gt = count(lambda k: k > tau)
    need = cap - n_gt
    tau_col = tau[:, :1]

    def prefix_over_tiles(tab):
        return _dot(tab.astype(BF16), su_ref[...])

    def at_lane(tab, c):
        return jnp.sum(jnp.where(lane == c, tab, 0.0), axis=1, keepdims=True)

    def eq_body(c, tab):
        k = keys(c * ROUTE_TILE, ROUTE_TILE)
        cnt = jnp.sum(_ones_where(k == tau_col), axis=1, keepdims=True)
        return jnp.where(lane == c, cnt, tab)

    base_eq = prefix_over_tiles(lax.fori_loop(0, ntile, eq_body, jnp.zeros(shape, F32)))

    def sel_body(c, tab):
        k = keys(c * ROUTE_TILE, ROUTE_TILE)
        eq = k == tau_col
        eq_cum = _dot(_ones_where(eq, BF16), u_ref[...]) + at_lane(base_eq, c)
        sel = (k > tau_col) | (eq & (eq_cum <= need))
        cnt = jnp.sum(_ones_where(sel), axis=1, keepdims=True)
        return jnp.where(lane == c, cnt, tab)

    base_sel = prefix_over_tiles(lax.fori_loop(0, ntile, sel_body, jnp.zeros(shape, F32)))

    def transposed(val):
        return jnp.concatenate([val, jnp.zeros((LANES - N_EXPERTS, LANES), val.dtype)], axis=0).T

    tau_ref[...] = tau
    taut_ref[...] = transposed(tau)
    for val, ref, ref_t in ((jnp.broadcast_to(need, shape), need_ref, needt_ref),
                            (base_eq, beq_ref, beqt_ref), (base_sel, bsel_ref, bselt_ref)):
        ref[...] = val.astype(I32)
        ref_t[...] = transposed(val)


def _route(afft):
    tokens = afft.shape[1]
    idx = np.arange(LANES)
    su = jnp.asarray(idx[:, None] < idx[None, :], BF16)
    idx = np.arange(ROUTE_TILE)
    u = jnp.asarray(idx[:, None] <= idx[None, :], BF16)
    full = lambda a: pl.BlockSpec(a.shape, lambda i: (0,) * a.ndim)
    small = pl.BlockSpec((N_EXPERTS, LANES), lambda i: (0, 0))
    smallt = pl.BlockSpec((LANES, LANES), lambda i: (0, 0))
    return pl.pallas_call(
        functools.partial(_route_kernel, tokens=tokens),
        grid=(1,),
        in_specs=[full(afft), full(su), full(u)],
        out_specs=[small] * 4 + [smallt] * 4,
        out_shape=[jax.ShapeDtypeStruct((N_EXPERTS, LANES), I32)] * 4
        + [jax.ShapeDtypeStruct((LANES, LANES), I32)] + [jax.ShapeDtypeStruct((LANES, LANES), F32)] * 3,
        compiler_params=_params(("arbitrary",)),
        name="route",
    )(afft, su, u)


def _gather_kernel(tau_s, need_s, beq_s, bsel_s, aff_ref, x_ref, u_ref, xe_ref, acc_ref, *, sub, ntile, cap):
    e = pl.program_id(0)
    t = pl.program_id(1)

    @pl.when(t == 0)
    def _():
        acc_ref[...] = jnp.zeros_like(acc_ref)

    tau = tau_s[e]
    need = need_s[e].astype(F32)
    sub_iota = lax.broadcasted_iota(I32, (SLOT_CHUNK, ROUTE_TILE), 0).astype(F32)
    for j in range(sub):
        tile = t * sub + j
        k = lax.bitcast_convert_type(aff_ref[0, :, j * ROUTE_TILE:(j + 1) * ROUTE_TILE], I32)
        eq = k == tau

        def cum(mask):
            rows = jnp.broadcast_to(_ones_where(mask, BF16), (8, ROUTE_TILE))
            return _dot(rows, u_ref[...])[0:1]

        s_lo = bsel_s[e, tile]
        n_sel = bsel_s[e, tile + 1] - s_lo
        eq_cum = cum(eq) + beq_s[e, tile].astype(F32)
        sel = (k > tau) | (eq & (eq_cum <= need))
        slot = jnp.where(sel, cum(sel) + (s_lo.astype(F32) - 1.0), -1.0)
        s_al = (s_lo // 8) * 8
        nch = jnp.where(n_sel > 0, (s_lo - s_al + n_sel + SLOT_CHUNK - 1) // SLOT_CHUNK, 0)
        xt = x_ref[j * ROUTE_TILE:(j + 1) * ROUTE_TILE, :]

        def chunk(c, carry):
            base = pl.multiple_of(s_al + c * SLOT_CHUNK, 8)
            onehot = _ones_where(sub_iota + base.astype(F32) == slot, BF16)
            acc_ref[pl.ds(base, SLOT_CHUNK), :] += _dot(onehot, xt)
            return carry

        lax.fori_loop(0, nch, chunk, 0)

    @pl.when(t == ntile // sub - 1)
    def _():
        xe_ref[...] = acc_ref[0:cap, :].astype(BF16)


def _gather(tables, afft, xn, u):
    tokens = xn.shape[0]
    cap = CAPACITY_FACTOR * tokens // N_EXPERTS
    ntile = tokens // ROUTE_TILE
    sub = 4 if ntile % 4 == 0 else 1
    tt = sub * ROUTE_TILE
    grid_spec = pltpu.PrefetchScalarGridSpec(
        num_scalar_prefetch=4,
        grid=(N_EXPERTS, ntile // sub),
        in_specs=[
            pl.BlockSpec((1, 1, tt), lambda e, t, *_: (e, 0, t)),
            pl.BlockSpec((tt, D_MODEL), lambda e, t, *_: (t, 0)),
            pl.BlockSpec(u.shape, lambda e, t, *_: (0, 0)),
        ],
        out_specs=pl.BlockSpec((cap, D_MODEL), lambda e, t, *_: (e, 0)),
        scratch_shapes=[pltpu.VMEM((cap + ROUTE_TILE + SLOT_CHUNK, D_MODEL), F32)],
    )
    return pl.pallas_call(
        functools.partial(_gather_kernel, sub=sub, ntile=ntile, cap=cap),
        grid_spec=grid_spec,
        out_shape=jax.ShapeDtypeStruct((N_EXPERTS * cap, D_MODEL), BF16),
        compiler_params=_params(("arbitrary", "arbitrary")),
        name="gather",
    )(*tables, afft.reshape(N_EXPERTS, 1, tokens), xn, u)


def _ffn_kernel(xe_ref, wg_ref, wu_ref, wd_ref, ye_ref, acc_ref, *, cap, nf, tm):
    f = pl.program_id(1)
    wg, wu, wd = wg_ref[0], wu_ref[0], wd_ref[0]

    def body(i, carry):
        r = pl.ds(pl.multiple_of(i * tm, tm), tm)
        x = xe_ref[r, :]
        hg = _dot(x, wg)
        hu = _dot(x, wu)
        h = (hg * (1.0 / (1.0 + jnp.exp(-hg))) * hu).astype(BF16)
        acc_ref[r, :] += _dot(h, wd)
        return carry

    @pl.when(f == 0)
    def _():
        acc_ref[...] = jnp.zeros_like(acc_ref)

    lax.fori_loop(0, cap // tm, body, 0)

    @pl.when(f == nf - 1)
    def _():
        ye_ref[...] = acc_ref[...].astype(BF16)


def _ffn(xe, w_eg, w_eu, w_ed):
    cap = xe.shape[0] // N_EXPERTS
    tf = 512
    nf = D_FF // tf
    tm = min(cap, 256)
    return pl.pallas_call(
        functools.partial(_ffn_kernel, cap=cap, nf=nf, tm=tm),
        grid=(N_EXPERTS, nf),
        in_specs=[
            pl.BlockSpec((cap, D_MODEL), lambda e, f: (e, 0)),
            pl.BlockSpec((1, D_MODEL, tf), lambda e, f: (e, 0, f)),
            pl.BlockSpec((1, D_MODEL, tf), lambda e, f: (e, 0, f)),
            pl.BlockSpec((1, tf, D_MODEL), lambda e, f: (e, f, 0)),
        ],
        out_specs=pl.BlockSpec((cap, D_MODEL), lambda e, f: (e, 0)),
        out_shape=jax.ShapeDtypeStruct(xe.shape, BF16),
        scratch_shapes=[pltpu.VMEM((cap, D_MODEL), F32)],
        compiler_params=_params(("arbitrary", "arbitrary")),
        name="ffn",
    )(xe, w_eg, w_eu, w_ed)


def _combine_kernel(bsel_s, x1_ref, aff_ref, taut_ref, needt_ref, beqt_ref, bselt_ref, low_ref, gf_ref,
                    ye_hbm, y_ref, buf_ref, xbuf_ref, sem_ref, xsem_ref, *, cap, total):
    t = pl.program_id(0)

    def window(e, c):
        s_lo = bsel_s[e, t]
        s_al = (s_lo // 16) * 16
        start = jnp.minimum(e * cap + s_al + c * SLOT_CHUNK, total - SLOT_CHUNK)
        return pl.multiple_of(start, 16)

    def first_copy(e):
        return pltpu.make_async_copy(ye_hbm.at[pl.ds(window(e, 0), SLOT_CHUNK)], buf_ref.at[e], sem_ref.at[e])

    for e in range(N_EXPERTS):
        first_copy(e).start()

    aff = aff_ref[...]
    k = lax.bitcast_convert_type(aff, I32)
    tau = taut_ref[0:1, :]
    low = low_ref[...]
    eq = k == tau
    eq_cum = _dot(low, _ones_where(eq, BF16)) + beqt_ref[0]
    sel = (k > tau) | (eq & (eq_cum <= needt_ref[0:1, :]))
    slot = jnp.where(sel, _dot(low, _ones_where(sel, BF16)) + (bselt_ref[0] - 1.0), -1.0)

    lane = lax.broadcasted_iota(I32, (ROUTE_TILE, SLOT_CHUNK), 1).astype(F32)
    acc = x1_ref[...]
    for e in range(N_EXPERTS):
        s_lo = bsel_s[e, t]
        n_sel = bsel_s[e, t + 1] - s_lo
        slot_e = slot[:, e:e + 1]
        gate_e = aff[:, e:e + 1]
        first_copy(e).wait()
        rel = (window(e, 0) - e * cap).astype(F32)
        onehot = _ones_where(lane + rel == slot_e, BF16)
        acc = acc + gate_e * _dot(onehot, buf_ref[e])
        nch = (s_lo - (s_lo // 16) * 16 + n_sel + SLOT_CHUNK - 1) // SLOT_CHUNK

        def extra(c, a):
            w = window(e, c)
            cp = pltpu.make_async_copy(ye_hbm.at[pl.ds(w, SLOT_CHUNK)], xbuf_ref, xsem_ref.at[0])
            cp.start()
            cp.wait()
            oh = _ones_where(lane + (w - e * cap).astype(F32) == slot_e, BF16)
            return a + gate_e * _dot(oh, xbuf_ref[...])

        acc = lax.fori_loop(1, nch, extra, acc)

    ms = jnp.mean(acc * acc, axis=-1, keepdims=True)
    y_ref[...] = acc * lax.rsqrt(ms + EPS) * gf_ref[...]


def _combine(bsel_i, x1, aff, tables_t, ye, gf):
    tokens = x1.shape[0]
    cap = CAPACITY_FACTOR * tokens // N_EXPERTS
    ntile = tokens // ROUTE_TILE
    idx = np.arange(ROUTE_TILE)
    low = jnp.asarray(idx[:, None] >= idx[None, :], BF16)
    taut, needt, beqt, bselt = tables_t
    rowvec = pl.BlockSpec((8, LANES), lambda t, *_: (0, 0))
    tilevec = pl.BlockSpec((1, 1, LANES), lambda t, *_: (t, 0, 0))
    grid_spec = pltpu.PrefetchScalarGridSpec(
        num_scalar_prefetch=1,
        grid=(ntile,),
        in_specs=[
            pl.BlockSpec((ROUTE_TILE, D_MODEL), lambda t, *_: (t, 0)),
            pl.BlockSpec((ROUTE_TILE, LANES), lambda t, *_: (t, 0)),
            rowvec, rowvec, tilevec, tilevec,
            pl.BlockSpec(low.shape, lambda t, *_: (0, 0)),
            pl.BlockSpec((1, D_MODEL), lambda t, *_: (0, 0)),
            pl.BlockSpec(memory_space=pl.ANY),
        ],
        out_specs=pl.BlockSpec((ROUTE_TILE, D_MODEL), lambda t, *_: (t, 0)),
        scratch_shapes=[
            pltpu.VMEM((N_EXPERTS, SLOT_CHUNK, D_MODEL), BF16),
            pltpu.VMEM((SLOT_CHUNK, D_MODEL), BF16),
            pltpu.SemaphoreType.DMA((N_EXPERTS,)),
            pltpu.SemaphoreType.DMA((1,)),
        ],
    )
    return pl.pallas_call(
        functools.partial(_combine_kernel, cap=cap, total=N_EXPERTS * cap),
        grid_spec=grid_spec,
        out_shape=jax.ShapeDtypeStruct((tokens, D_MODEL), F32),
        compiler_params=_params(("arbitrary",)),
        name="combine",
    )(bsel_i, x1, aff, taut, needt, beqt.reshape(LANES, 1, LANES), bselt.reshape(LANES, 1, LANES), low, gf, ye)


def _encoder(x, w):
    batch, seq, _ = x.shape
    tokens = batch * seq
    xt = x.reshape(tokens, D_MODEL)
    qkv, vr, vi, gates = _in_proj(xt, w["g1"], w["w_in"], w["w_gate"], w["b_gate"], w["cs"])
    outs, lses = [], []
    for g in range(N_GROUPS):
        o, lse = _attention(qkv, w["bias"][g], g, batch, seq)
        outs.append(o)
        lses.append(lse)
    four = _fourier(vr, vi, batch, seq)
    x1, xn, aff, afft = _mix(xt, outs, lses, four, gates, w["w_attn"], w["w_four"], w["w_out"], w["g2"],
                             w["w_router"])
    tau, need, beq_i, bsel_i, taut, needt, beqt, bselt = _route(afft)
    tau_i, need_i = tau[:, 0], need[:, 0]
    idx = np.arange(ROUTE_TILE)
    u = jnp.asarray(idx[:, None] <= idx[None, :], BF16)
    xe = _gather((tau_i, need_i, beq_i, bsel_i), afft, xn, u)
    ye = _ffn(xe, w["w_eg"], w["w_eu"], w["w_ed"])
    y = _combine(bsel_i, x1, aff, (taut, needt, beqt, bselt), ye, w["gf"])
    return y.reshape(batch, seq, D_MODEL)


def _prepare_weights(rel_bias, norm1_g, w_in, w_attn_br, w_four_br, w_gate, b_gate, w_out,
                     norm2_g, w_router, w_exp_gate, w_exp_up, w_exp_down, final_g):
    c, s = _dft_mats(F_CH)
    return {
        "g1": norm1_g[0].reshape(1, D_MODEL),
        "w_in": w_in[0].astype(BF16),
        "w_gate": w_gate[0].astype(BF16),
        "b_gate": b_gate[0].reshape(1, 2 * D_MODEL),
        "cs": jnp.asarray(np.concatenate([c, s], axis=1), BF16),
        "bias": [_attention_bias(rel_bias, g) for g in range(N_GROUPS)],
        "w_attn": w_attn_br[0].astype(BF16),
        "w_four": w_four_br[0].astype(BF16),
        "w_out": w_out[0].astype(BF16),
        "g2": norm2_g[0].reshape(1, D_MODEL),
        "w_router": jnp.pad(w_router[0], ((0, 0), (0, LANES - N_EXPERTS))),
        "w_eg": w_exp_gate[0].astype(BF16),
        "w_eu": w_exp_up[0].astype(BF16),
        "w_ed": w_exp_down[0].astype(BF16),
        "gf": final_g.reshape(1, D_MODEL),
    }


def kernel(x_prompt, x_sample, rel_bias, norm1_g, w_in, w_attn_br, w_four_br, w_gate, b_gate, w_out,
           norm2_g, w_router, w_exp_gate, w_exp_up, w_exp_down, final_g):
    w = _prepare_weights(rel_bias, norm1_g, w_in, w_attn_br, w_four_br, w_gate, b_gate, w_out,
                         norm2_g, w_router, w_exp_gate, w_exp_up, w_exp_down, final_g)
    return (_encoder(x_prompt, w), _encoder(x_sample, w))
```

```python
import functools
import math

import numpy as np
import jax
import jax.numpy as jnp
from jax import lax
from jax.experimental import pallas as pl
from jax.experimental.pallas import tpu as pltpu

D_MODEL = 1024
HEAD_DIM = 64
HEADS_PER_GROUP = 4
GROUPS = ((128, 1), (512, 4), (2048, 16))
N_GROUPS = len(GROUPS)
GROUP_W = HEADS_PER_GROUP * HEAD_DIM
ATT_W = N_GROUPS * GROUP_W
QKV_W = 3 * ATT_W
F_GROUPS = 6
F_CH = 128
F_W = F_GROUPS * F_CH
NUM_BUCKETS = 32
MAX_DISTANCE = 1024
N_EXPERTS = 16
CAPACITY_FACTOR = 2
D_FF = 2048
EPS = 1e-6
NEG = -1e30

HALF_KEYS = 64
ATT_SUB = 128
ROUTE_TILE = 256
SLOT_CHUNK = 64
LANES = 128
V7X_VMEM_LIMIT = 56 * 1024 * 1024

F32 = jnp.float32
BF16 = jnp.bfloat16
I32 = jnp.int32


def _params(sem):
    return pltpu.CompilerParams(dimension_semantics=sem, vmem_limit_bytes=V7X_VMEM_LIMIT)


def _dot(a, b):
    return jnp.dot(a, b, preferred_element_type=F32)


def _dot_nt(a, b):
    return lax.dot_general(a, b, (((1,), (1,)), ((), ())), preferred_element_type=F32)


def _ones_where(mask, dtype=F32):
    return jnp.where(mask, jnp.ones((), F32), jnp.zeros((), F32)).astype(dtype)


def _in_proj_kernel(x_ref, g_ref, win_ref, wg_ref, bg_ref, cs_ref, qkv_ref, vr_ref, vi_ref, gates_ref):
    x = x_ref[...]
    ms = jnp.mean(x * x, axis=-1, keepdims=True)
    xn = (x * lax.rsqrt(ms + EPS) * g_ref[...]).astype(BF16)
    q = _dot(xn, win_ref[:, 0:ATT_W]) * (1.0 / math.sqrt(HEAD_DIM))
    qkv_ref[:, 0:ATT_W] = q.astype(BF16)
    for c in (1, 2):
        qkv_ref[:, c * ATT_W:(c + 1) * ATT_W] = _dot(xn, win_ref[:, c * ATT_W:(c + 1) * ATT_W]).astype(BF16)
    u = _dot(xn, win_ref[:, QKV_W:QKV_W + F_W]).astype(BF16)
    cs = cs_ref[...]
    for g in range(F_GROUPS):
        a = _dot(u[:, g * F_CH:(g + 1) * F_CH], cs)
        vr_ref[:, g * F_CH:(g + 1) * F_CH] = a[:, :F_CH].astype(BF16)
        vi_ref[:, g * F_CH:(g + 1) * F_CH] = (-a[:, F_CH:]).astype(BF16)
    z = _dot(xn, wg_ref[...]) + bg_ref[...]
    gates_ref[...] = (1.0 / (1.0 + jnp.exp(-z))).astype(BF16)


def _in_proj(x, g1, w_in, w_gate, b_gate, cs):
    t = x.shape[0]
    tm = 512
    const = lambda i: (0, 0)
    row = lambda i: (i, 0)
    return pl.pallas_call(
        _in_proj_kernel,
        grid=(t // tm,),
        in_specs=[
            pl.BlockSpec((tm, D_MODEL), row),
            pl.BlockSpec((1, D_MODEL), const),
            pl.BlockSpec(w_in.shape, const),
            pl.BlockSpec(w_gate.shape, const),
            pl.BlockSpec((1, 2 * D_MODEL), const),
            pl.BlockSpec(cs.shape, const),
        ],
        out_specs=[
            pl.BlockSpec((tm, QKV_W), row),
            pl.BlockSpec((tm, F_W), row),
            pl.BlockSpec((tm, F_W), row),
            pl.BlockSpec((tm, 2 * D_MODEL), row),
        ],
        out_shape=[
            jax.ShapeDtypeStruct((t, QKV_W), BF16),
            jax.ShapeDtypeStruct((t, F_W), BF16),
            jax.ShapeDtypeStruct((t, F_W), BF16),
            jax.ShapeDtypeStruct((t, 2 * D_MODEL), BF16),
        ],
        compiler_params=_params(("parallel",)),
        name="in_proj",
    )(x, g1, w_in, w_gate, b_gate, cs)


def _attention_kernel(q_ref, kp_ref, kc_ref, kn_ref, vp_ref, vc_ref, vn_ref, bias_ref, o_ref, lse_ref, *, tq, length):
    i = pl.program_id(2)
    kwin = jnp.concatenate([kp_ref[0], kc_ref[0], kn_ref[0]], axis=0)
    vwin = jnp.concatenate([vp_ref[0], vc_ref[0], vn_ref[0]], axis=0)
    win = ATT_SUB + 2 * HALF_KEYS
    lane_head = lax.broadcasted_iota(I32, (ATT_SUB, GROUP_W), 1) // HEAD_DIM
    col = lax.broadcasted_iota(I32, (ATT_SUB, win), 1)
    for sb in range(tq // ATT_SUB):
        off = sb * ATT_SUB
        q = q_ref[0, off:off + ATT_SUB, :]
        kw = kwin[off:off + win]
        vw = vwin[off:off + win]
        first = i * tq + (off - HALF_KEYS)
        valid = (col >= -first) & (col < length - first)
        qs = jnp.concatenate(
            [jnp.where(lane_head == h, q, jnp.zeros_like(q)) for h in range(HEADS_PER_GROUP)], axis=0)
        s_all = _dot_nt(qs, kw)
        ps, ms, ls = [], [], []
        for h in range(HEADS_PER_GROUP):
            s = s_all[h * ATT_SUB:(h + 1) * ATT_SUB] + bias_ref[h]
            s = jnp.where(valid, s, NEG)
            m = jnp.max(s, axis=-1, keepdims=True)
            p = jnp.exp(s - m)
            ls.append(jnp.sum(p, axis=-1, keepdims=True))
            ms.append(m)
            ps.append(p.astype(BF16))
        o_all = _dot(jnp.concatenate(ps, axis=0), vw)
        out = jnp.zeros((ATT_SUB, GROUP_W), F32)
        lse = jnp.zeros((ATT_SUB, GROUP_W), F32)
        for h in range(HEADS_PER_GROUP):
            oh = o_all[h * ATT_SUB:(h + 1) * ATT_SUB] * (1.0 / ls[h])
            out = jnp.where(lane_head == h, oh, out)
            lse = jnp.where(lane_head == h, ms[h] + jnp.log(ls[h]), lse)
        o_ref[0, off:off + ATT_SUB, :] = out
        lse_ref[0, off:off + ATT_SUB, :] = lse


def _attention(qkv, bias, g, batch, seq):
    dil = GROUPS[g][1]
    length = seq // dil
    tq = min(length, 512)
    nb = length // tq
    hb = tq // HALF_KEYS
    last_halo = length // HALF_KEYS - 1
    ncol = QKV_W // GROUP_W
    view = qkv.reshape(batch, length, dil * QKV_W)

    def cur(c):
        return lambda b, r, i: (b, i, r * ncol + c)

    def prev(c):
        return lambda b, r, i: (b, jnp.maximum(i * hb - 1, 0), r * ncol + c)

    def nxt(c):
        return lambda b, r, i: (b, jnp.minimum((i + 1) * hb, last_halo), r * ncol + c)

    kc, vc = N_GROUPS + g, 2 * N_GROUPS + g
    blk = lambda rows: (1, rows, GROUP_W)
    out_map = lambda b, r, i: (b, i, r)
    o, lse = pl.pallas_call(
        functools.partial(_attention_kernel, tq=tq, length=length),
        grid=(batch, dil, nb),
        in_specs=[
            pl.BlockSpec(blk(tq), cur(g)),
            pl.BlockSpec(blk(HALF_KEYS), prev(kc)),
            pl.BlockSpec(blk(tq), cur(kc)),
            pl.BlockSpec(blk(HALF_KEYS), nxt(kc)),
            pl.BlockSpec(blk(HALF_KEYS), prev(vc)),
            pl.BlockSpec(blk(tq), cur(vc)),
            pl.BlockSpec(blk(HALF_KEYS), nxt(vc)),
            pl.BlockSpec(bias.shape, lambda b, r, i: (0, 0, 0)),
        ],
        out_specs=[pl.BlockSpec(blk(tq), out_map), pl.BlockSpec(blk(tq), out_map)],
        out_shape=[jax.ShapeDtypeStruct((batch, length, dil * GROUP_W), F32)] * 2,
        compiler_params=_params(("parallel", "parallel", "parallel")),
        name=f"attention_g{g}",
    )(view, view, view, view, view, view, view, bias)
    t = batch * seq
    return o.reshape(t, GROUP_W), lse.reshape(t, GROUP_W)


def _t5_bucket(rel):
    nb = NUM_BUCKETS // 2
    max_exact = nb // 2
    ret = (rel > 0).astype(np.int32) * nb
    n = np.abs(rel)
    large = max_exact + (np.log(np.maximum(n, max_exact) / max_exact)
                         / np.log(MAX_DISTANCE / max_exact) * (nb - max_exact)).astype(np.int32)
    large = np.minimum(large, nb - 1)
    return (ret + np.where(n < max_exact, n, large)).astype(np.int32)


def _attention_bias(rel_bias, g):
    dil = GROUPS[g][1]
    qi = np.arange(ATT_SUB)[:, None]
    kj = np.arange(ATT_SUB + 2 * HALF_KEYS)[None, :]
    delta = kj - HALF_KEYS - qi
    band = np.abs(delta) <= HALF_KEYS
    bucket = _t5_bucket(dil * delta)
    tab = rel_bias[:, g * HEADS_PER_GROUP:(g + 1) * HEADS_PER_GROUP].astype(F32)
    onehot = jnp.asarray(bucket[..., None] == np.arange(NUM_BUCKETS), F32)
    bias = jnp.einsum("qkb,bh->hqk", onehot, tab, precision=lax.Precision.HIGHEST)
    return jnp.where(jnp.asarray(band)[None], bias, NEG)


def _dft_mats(n):
    k = np.arange(n)
    ang = 2.0 * np.pi * ((k[:, None] * k[None, :]) % n) / n
    return np.cos(ang), np.sin(ang)


def _fft_stage1_kernel(vr_ref, vi_ref, m1_ref, twc_ref, tws_ref, zr_ref, zi_ref, *, n1, m):
    x = jnp.concatenate([vr_ref[0], vi_ref[0]], axis=0)
    z = _dot(m1_ref[...], x)
    zr, zi = z[:n1], z[n1:]
    twc, tws = twc_ref[0], tws_ref[0]
    for j in range(m):
        c = twc[:, j:j + 1]
        s = tws[:, j:j + 1]
        a = zr[:, j * F_W:(j + 1) * F_W]
        b = zi[:, j * F_W:(j + 1) * F_W]
        zr_ref[0, :, j * F_W:(j + 1) * F_W] = (a * c + b * s).astype(BF16)
        zi_ref[0, :, j * F_W:(j + 1) * F_W] = (b * c - a * s).astype(BF16)


def _fft_stage2_kernel(zr_ref, zi_ref, m2_ref, o_ref, *, kc, scale):
    m2 = m2_ref[...]
    for j in range(kc):
        x = jnp.concatenate([zr_ref[0, j], zi_ref[0, j]], axis=0)
        o_ref[0, :, j * F_W:(j + 1) * F_W] = _dot(m2, x) * scale


def _fourier(vr, vi, batch, seq):
    n2 = LANES
    n1 = seq // n2
    m = 8
    c1, s1 = _dft_mats(n1)
    m1 = jnp.asarray(np.block([[c1, s1], [-s1, c1]]), BF16)
    c2, s2 = _dft_mats(n2)
    m2 = jnp.asarray(np.concatenate([c2, s2], axis=1), BF16)
    k1 = np.arange(n1)[:, None]
    sv = np.arange(n2)[None, :]
    ang = 2.0 * np.pi * ((k1 * sv) % seq) / seq
    to_blocks = lambda a: jnp.asarray(a.reshape(n1, n2 // m, m).transpose(1, 0, 2), F32)
    twc, tws = to_blocks(np.cos(ang)), to_blocks(np.sin(ang))

    v3 = lambda a: a.reshape(batch, n1, n2 * F_W)
    blk = (1, n1, m * F_W)
    dmap = lambda b, j: (b, 0, j)
    tmap = lambda b, j: (j, 0, 0)
    zr, zi = pl.pallas_call(
        functools.partial(_fft_stage1_kernel, n1=n1, m=m),
        grid=(batch, n2 // m),
        in_specs=[
            pl.BlockSpec(blk, dmap),
            pl.BlockSpec(blk, dmap),
            pl.BlockSpec(m1.shape, lambda b, j: (0, 0)),
            pl.BlockSpec((1, n1, m), tmap),
            pl.BlockSpec((1, n1, m), tmap),
        ],
        out_specs=[pl.BlockSpec(blk, dmap), pl.BlockSpec(blk, dmap)],
        out_shape=[jax.ShapeDtypeStruct((batch, n1, n2 * F_W), BF16)] * 2,
        compiler_params=_params(("parallel", "parallel")),
        name="fft_stage1",
    )(v3(vr), v3(vi), m1, twc, tws)

    kc = 8
    v4 = lambda a: a.reshape(batch, n1, n2, F_W)
    zblk = (1, kc, n2, F_W)
    zmap = lambda b, j: (b, j, 0, 0)
    out = pl.pallas_call(
        functools.partial(_fft_stage2_kernel, kc=kc, scale=1.0 / math.sqrt(seq * F_CH)),
        grid=(batch, n1 // kc),
        in_specs=[
            pl.BlockSpec(zblk, zmap),
            pl.BlockSpec(zblk, zmap),
            pl.BlockSpec(m2.shape, lambda b, j: (0, 0)),
        ],
        out_specs=pl.BlockSpec((1, n2, kc * F_W), lambda b, j: (b, 0, j)),
        out_shape=jax.ShapeDtypeStruct((batch, n2, n1 * F_W), F32),
        compiler_params=_params(("parallel", "parallel")),
        name="fft_stage2",
    )(v4(zr), v4(zi), m2)
    return out.reshape(batch * seq, F_W)


def _mix_kernel(x_ref, o0_ref, o1_ref, o2_ref, l0_ref, l1_ref, l2_ref, four_ref, gates_ref,
                wa_ref, wf_ref, wo_ref, g2_ref, wr_ref, x1_ref, xn_ref, aff_ref, afft_ref):
    l0, l1, l2 = l0_ref[...], l1_ref[...], l2_ref[...]
    mx = jnp.maximum(jnp.maximum(l0, l1), l2)
    e0, e1, e2 = jnp.exp(l0 - mx), jnp.exp(l1 - mx), jnp.exp(l2 - mx)
    att = (e0 * o0_ref[...] + e1 * o1_ref[...] + e2 * o2_ref[...]) * (1.0 / (e0 + e1 + e2))
    a_br = _dot(att.astype(BF16), wa_ref[...])
    f_br = _dot(four_ref[...].astype(BF16), wf_ref[...])
    mix = gates_ref[:, :D_MODEL] * a_br + gates_ref[:, D_MODEL:] * f_br
    x1 = x_ref[...] + _dot(mix.astype(BF16), wo_ref[...])
    x1_ref[...] = x1
    ms = jnp.mean(x1 * x1, axis=-1, keepdims=True)
    xn = x1 * lax.rsqrt(ms + EPS) * g2_ref[...]
    xn_ref[...] = xn.astype(BF16)
    logits = jnp.dot(xn, wr_ref[...], preferred_element_type=F32, precision=lax.Precision.HIGHEST)
    lane = lax.broadcasted_iota(I32, logits.shape, 1)
    logits = jnp.where(lane < N_EXPERTS, logits, NEG)
    p = jnp.exp(logits - jnp.max(logits, axis=-1, keepdims=True))
    aff = p * (1.0 / jnp.sum(p, axis=-1, keepdims=True))
    aff_ref[...] = aff
    afft_ref[...] = aff.T[:N_EXPERTS]


def _mix(x, os_, ls_, four, gates, w_attn, w_four, w_out, g2, w_router_pad):
    t = x.shape[0]
    tm = 512
    const = lambda i: (0, 0)
    row = lambda i: (i, 0)
    rows = lambda w: pl.BlockSpec((tm, w), row)
    full = lambda a: pl.BlockSpec(a.shape, const)
    return pl.pallas_call(
        _mix_kernel,
        grid=(t // tm,),
        in_specs=[rows(D_MODEL)] + [rows(GROUP_W)] * 6 + [rows(F_W), rows(2 * D_MODEL),
                  full(w_attn), full(w_four), full(w_out), full(g2), full(w_router_pad)],
        out_specs=[rows(D_MODEL), rows(D_MODEL), rows(LANES), pl.BlockSpec((N_EXPERTS, tm), lambda i: (0, i))],
        out_shape=[
            jax.ShapeDtypeStruct((t, D_MODEL), F32),
            jax.ShapeDtypeStruct((t, D_MODEL), BF16),
            jax.ShapeDtypeStruct((t, LANES), F32),
            jax.ShapeDtypeStruct((N_EXPERTS, t), F32),
        ],
        compiler_params=_params(("parallel",)),
        name="mix",
    )(x, *os_, *ls_, four, gates, w_attn, w_four, w_out, g2, w_router_pad)


def _route_kernel(afft_ref, su_ref, u_ref, tau_ref, need_ref, beq_ref, bsel_ref,
                  taut_ref, needt_ref, beqt_ref, bselt_ref, *, tokens):
    cap = CAPACITY_FACTOR * tokens // N_EXPERTS
    ntile = tokens // ROUTE_TILE
    shape = (N_EXPERTS, LANES)
    lane = lax.broadcasted_iota(I32, shape, 1)

    def keys(start, width):
        return lax.bitcast_convert_type(afft_ref[:, pl.ds(pl.multiple_of(start, LANES), width)], I32)

    def count(pred):
        def body(c, acc):
            return acc + _ones_where(pred(keys(c * LANES, LANES)))
        acc = lax.fori_loop(0, tokens // LANES, body, jnp.zeros(shape, F32))
        return jnp.sum(acc, axis=1, keepdims=True)

    def bit_body(i, prefix):
        cand = prefix | lax.shift_left(jnp.ones(shape, I32), jnp.full(shape, 30 - i, I32))
        tot = count(lambda k: k >= cand)
        return jnp.where(tot >= cap, cand, prefix)

    tau = lax.fori_loop(0, 31, bit_body, jnp.zeros(shape, I32))
    n_gt = count(lambda k: k > tau)
    need = cap - n_gt
    tau_col = tau[:, :1]

    def prefix_over_tiles(tab):
        return _dot(tab.astype(BF16), su_ref[...])

    def at_lane(tab, c):
        return jnp.sum(jnp.where(lane == c, tab, 0.0), axis=1, keepdims=True)

    def eq_body(c, tab):
        k = keys(c * ROUTE_TILE, ROUTE_TILE)
        cnt = jnp.sum(_ones_where(k == tau_col), axis=1, keepdims=True)
        return jnp.where(lane == c, cnt, tab)

    base_eq = prefix_over_tiles(lax.fori_loop(0, ntile, eq_body, jnp.zeros(shape, F32)))

    def sel_body(c, tab):
        k = keys(c * ROUTE_TILE, ROUTE_TILE)
        eq = k == tau_col
        eq_cum = _dot(_ones_where(eq, BF16), u_ref[...]) + at_lane(base_eq, c)
        sel = (k > tau_col) | (eq & (eq_cum <= need))
        cnt = jnp.sum(_ones_where(sel), axis=1, keepdims=True)
        return jnp.where(lane == c, cnt, tab)

    base_sel = prefix_over_tiles(lax.fori_loop(0, ntile, sel_body, jnp.zeros(shape, F32)))

    def transposed(val):
        return jnp.concatenate([val, jnp.zeros((LANES - N_EXPERTS, LANES), val.dtype)], axis=0).T

    tau_ref[...] = tau
    taut_ref[...] = transposed(tau)
    for val, ref, ref_t in ((jnp.broadcast_to(need, shape), need_ref, needt_ref),
                            (base_eq, beq_ref, beqt_ref), (base_sel, bsel_ref, bselt_ref)):
        ref[...] = val.astype(I32)
        ref_t[...] = transposed(val)


def _route(afft):
    tokens = afft.shape[1]
    idx = np.arange(LANES)
    su = jnp.asarray(idx[:, None] < idx[None, :], BF16)
    idx = np.arange(ROUTE_TILE)
    u = jnp.asarray(idx[:, None] <= idx[None, :], BF16)
    full = lambda a: pl.BlockSpec(a.shape, lambda i: (0,) * a.ndim)
    small = pl.BlockSpec((N_EXPERTS, LANES), lambda i: (0, 0))
    smallt = pl.BlockSpec((LANES, LANES), lambda i: (0, 0))
    return pl.pallas_call(
        functools.partial(_route_kernel, tokens=tokens),
        grid=(1,),
        in_specs=[full(afft), full(su), full(u)],
        out_specs=[small] * 4 + [smallt] * 4,
        out_shape=[jax.ShapeDtypeStruct((N_EXPERTS, LANES), I32)] * 4
        + [jax.ShapeDtypeStruct((LANES, LANES), I32)] + [jax.ShapeDtypeStruct((LANES, LANES), F32)] * 3,
        compiler_params=_params(("arbitrary",)),
        name="route",
    )(afft, su, u)


def _gather_kernel(tau_s, need_s, beq_s, bsel_s, aff_ref, x_ref, u_ref, xe_ref, acc_ref, *, sub, ntile, cap):
    e = pl.program_id(0)
    t = pl.program_id(1)

    @pl.when(t == 0)
    def _():
        acc_ref[...] = jnp.zeros_like(acc_ref)

    tau = tau_s[e]
    need = need_s[e].astype(F32)
    sub_iota = lax.broadcasted_iota(I32, (SLOT_CHUNK, ROUTE_TILE), 0).astype(F32)
    for j in range(sub):
        tile = t * sub + j
        k = lax.bitcast_convert_type(aff_ref[0, :, j * ROUTE_TILE:(j + 1) * ROUTE_TILE], I32)
        eq = k == tau

        def cum(mask):
            rows = jnp.broadcast_to(_ones_where(mask, BF16), (8, ROUTE_TILE))
            return _dot(rows, u_ref[...])[0:1]

        s_lo = bsel_s[e, tile]
        n_sel = bsel_s[e, tile + 1] - s_lo
        eq_cum = cum(eq) + beq_s[e, tile].astype(F32)
        sel = (k > tau) | (eq & (eq_cum <= need))
        slot = jnp.where(sel, cum(sel) + (s_lo.astype(F32) - 1.0), -1.0)
        s_al = (s_lo // 8) * 8
        nch = jnp.where(n_sel > 0, (s_lo - s_al + n_sel + SLOT_CHUNK - 1) // SLOT_CHUNK, 0)
        xt = x_ref[j * ROUTE_TILE:(j + 1) * ROUTE_TILE, :]

        def chunk(c, carry):
            base = pl.multiple_of(s_al + c * SLOT_CHUNK, 8)
            onehot = _ones_where(sub_iota + base.astype(F32) == slot, BF16)
            acc_ref[pl.ds(base, SLOT_CHUNK), :] += _dot(onehot, xt)
            return carry

        lax.fori_loop(0, nch, chunk, 0)

    @pl.when(t == ntile // sub - 1)
    def _():
        xe_ref[...] = acc_ref[0:cap, :].astype(BF16)


def _gather(tables, afft, xn, u):
    tokens = xn.shape[0]
    cap = CAPACITY_FACTOR * tokens // N_EXPERTS
    ntile = tokens // ROUTE_TILE
    sub = 4 if ntile % 4 == 0 else 1
    tt = sub * ROUTE_TILE
    grid_spec = pltpu.PrefetchScalarGridSpec(
        num_scalar_prefetch=4,
        grid=(N_EXPERTS, ntile // sub),
        in_specs=[
            pl.BlockSpec((1, 1, tt), lambda e, t, *_: (e, 0, t)),
            pl.BlockSpec((tt, D_MODEL), lambda e, t, *_: (t, 0)),
            pl.BlockSpec(u.shape, lambda e, t, *_: (0, 0)),
        ],
        out_specs=pl.BlockSpec((cap, D_MODEL), lambda e, t, *_: (e, 0)),
        scratch_shapes=[pltpu.VMEM((cap + ROUTE_TILE + SLOT_CHUNK, D_MODEL), F32)],
    )
    return pl.pallas_call(
        functools.partial(_gather_kernel, sub=sub, ntile=ntile, cap=cap),
        grid_spec=grid_spec,
        out_shape=jax.ShapeDtypeStruct((N_EXPERTS * cap, D_MODEL), BF16),
        compiler_params=_params(("arbitrary", "arbitrary")),
        name="gather",
    )(*tables, afft.reshape(N_EXPERTS, 1, tokens), xn, u)


def _ffn_kernel(xe_ref, wg_ref, wu_ref, wd_ref, ye_ref, acc_ref, *, cap, nf, tm):
    f = pl.program_id(1)
    wg, wu, wd = wg_ref[0].astype(BF16), wu_ref[0].astype(BF16), wd_ref[0].astype(BF16)

    @pl.when(f == 0)
    def _():
        acc_ref[...] = jnp.zeros_like(acc_ref)

    for i in range(cap // tm):
        r = slice(i * tm, (i + 1) * tm)
        x = xe_ref[r, :]
        hg = _dot(x, wg)
        hu = _dot(x, wu)
        h = (hg * (1.0 / (1.0 + jnp.exp(-hg))) * hu).astype(BF16)
        acc_ref[r, :] += _dot(h, wd)

    @pl.when(f == nf - 1)
    def _():
        ye_ref[...] = acc_ref[...].astype(BF16)


def _ffn(xe, w_eg, w_eu, w_ed):
    cap = xe.shape[0] // N_EXPERTS
    tf = 512
    nf = D_FF // tf
    tm = min(cap, 1024)
    return pl.pallas_call(
        functools.partial(_ffn_kernel, cap=cap, nf=nf, tm=tm),
        grid=(N_EXPERTS, nf),
        in_specs=[
            pl.BlockSpec((cap, D_MODEL), lambda e, f: (e, 0)),
            pl.BlockSpec((1, D_MODEL, tf), lambda e, f: (e, 0, f)),
            pl.BlockSpec((1, D_MODEL, tf), lambda e, f: (e, 0, f)),
            pl.BlockSpec((1, tf, D_MODEL), lambda e, f: (e, f, 0)),
        ],
        out_specs=pl.BlockSpec((cap, D_MODEL), lambda e, f: (e, 0)),
        out_shape=jax.ShapeDtypeStruct(xe.shape, BF16),
        scratch_shapes=[pltpu.VMEM((cap, D_MODEL), F32)],
        compiler_params=_params(("arbitrary", "arbitrary")),
        name="ffn",
    )(xe, w_eg, w_eu, w_ed)


def _combine_kernel(bsel_s, x1_ref, aff_ref, taut_ref, needt_ref, beqt_ref, bselt_ref, low_ref, gf_ref,
                    ye_hbm, y_ref, buf_ref, xbuf_ref, sem_ref, xsem_ref, *, cap, total, ntile):
    t = pl.program_id(0)
    par = t % 2

    def aligned(e, tile):
        return (bsel_s[e, tile] // 16) * 16

    def window(e, tile, c):
        start = jnp.minimum(e * cap + aligned(e, tile) + c * SLOT_CHUNK, total - SLOT_CHUNK)
        return pl.multiple_of(start, 16)

    def first_chunks(tile, buf):
        return [pltpu.make_async_copy(ye_hbm.at[pl.ds(window(e, tile, 0), SLOT_CHUNK)],
                                      buf_ref.at[buf, pl.ds(e * SLOT_CHUNK, SLOT_CHUNK)], sem_ref.at[buf, e])
                for e in range(N_EXPERTS)]

    @pl.when(t == 0)
    def _():
        for cp in first_chunks(0, 0):
            cp.start()

    @pl.when(t + 1 < ntile)
    def _():
        for cp in first_chunks(t + 1, 1 - par):
            cp.start()

    aff = aff_ref[...]
    k = lax.bitcast_convert_type(aff, I32)
    tau = taut_ref[0:1, :]
    low = low_ref[...]
    eq = k == tau
    eq_cum = _dot(low, _ones_where(eq, BF16)) + beqt_ref[0]
    sel = (k > tau) | (eq & (eq_cum <= needt_ref[0:1, :]))
    slot = jnp.where(sel, _dot(low, _ones_where(sel, BF16)) + (bselt_ref[0] - 1.0), -1.0)

    g_hi = aff.astype(BF16).astype(F32)
    g_lo = aff - g_hi
    lane = lax.broadcasted_iota(I32, (ROUTE_TILE, LANES), 1)
    first_half = lane < SLOT_CHUNK
    row_in_chunk = (lane % SLOT_CHUNK).astype(F32)
    a_hi, a_lo = [], []
    for e in range(0, N_EXPERTS, 2):
        def pick(v):
            return jnp.where(first_half, v[:, e:e + 1], v[:, e + 1:e + 2])
        rel0 = (window(e, t, 0) - e * cap).astype(F32)
        rel1 = (window(e + 1, t, 0) - (e + 1) * cap).astype(F32)
        hit = row_in_chunk + jnp.where(first_half, rel0, rel1) == pick(slot)
        a_hi.append(jnp.where(hit, pick(g_hi), 0.0).astype(BF16))
        a_lo.append(jnp.where(hit, pick(g_lo), 0.0).astype(BF16))
    for cp in first_chunks(t, par):
        cp.wait()
    rows = buf_ref[par]
    y_ref[...] = (x1_ref[...] + _dot(jnp.concatenate(a_hi, axis=1), rows)
                  + _dot(jnp.concatenate(a_lo, axis=1), rows))

    nch = [(bsel_s[e, t + 1] - aligned(e, t) + SLOT_CHUNK - 1) // SLOT_CHUNK for e in range(N_EXPERTS)]

    @pl.when(functools.reduce(jnp.maximum, nch) > 1)
    def _():
        lane64 = lax.broadcasted_iota(I32, (ROUTE_TILE, SLOT_CHUNK), 1).astype(F32)
        for e in range(N_EXPERTS):
            slot_e = slot[:, e:e + 1]

            def extra(c, carry):
                w = window(e, t, c)
                cp = pltpu.make_async_copy(ye_hbm.at[pl.ds(w, SLOT_CHUNK)], xbuf_ref, xsem_ref.at[0])
                cp.start()
                cp.wait()
                first = (aligned(e, t) + c * SLOT_CHUNK).astype(F32)
                hit = (lane64 + (w - e * cap).astype(F32) == slot_e) & (slot_e >= first)
                y_ref[...] += aff[:, e:e + 1] * _dot(_ones_where(hit, BF16), xbuf_ref[...])
                return carry

            lax.fori_loop(1, nch[e], extra, 0)

    acc = y_ref[...]
    ms = jnp.mean(acc * acc, axis=-1, keepdims=True)
    y_ref[...] = acc * lax.rsqrt(ms + EPS) * gf_ref[...]


def _combine(bsel_i, x1, aff, tables_t, ye, gf):
    tokens = x1.shape[0]
    cap = CAPACITY_FACTOR * tokens // N_EXPERTS
    ntile = tokens // ROUTE_TILE
    idx = np.arange(ROUTE_TILE)
    low = jnp.asarray(idx[:, None] >= idx[None, :], BF16)
    taut, needt, beqt, bselt = tables_t
    rowvec = pl.BlockSpec((8, LANES), lambda t, *_: (0, 0))
    tilevec = pl.BlockSpec((1, 1, LANES), lambda t, *_: (t, 0, 0))
    grid_spec = pltpu.PrefetchScalarGridSpec(
        num_scalar_prefetch=1,
        grid=(ntile,),
        in_specs=[
            pl.BlockSpec((ROUTE_TILE, D_MODEL), lambda t, *_: (t, 0)),
            pl.BlockSpec((ROUTE_TILE, LANES), lambda t, *_: (t, 0)),
            rowvec, rowvec, tilevec, tilevec,
            pl.BlockSpec(low.shape, lambda t, *_: (0, 0)),
            pl.BlockSpec((1, D_MODEL), lambda t, *_: (0, 0)),
            pl.BlockSpec(memory_space=pl.ANY),
        ],
        out_specs=pl.BlockSpec((ROUTE_TILE, D_MODEL), lambda t, *_: (t, 0)),
        scratch_shapes=[
            pltpu.VMEM((2, N_EXPERTS * SLOT_CHUNK, D_MODEL), BF16),
            pltpu.VMEM((SLOT_CHUNK, D_MODEL), BF16),
            pltpu.SemaphoreType.DMA((2, N_EXPERTS)),
            pltpu.SemaphoreType.DMA((1,)),
        ],
    )
    return pl.pallas_call(
        functools.partial(_combine_kernel, cap=cap, total=N_EXPERTS * cap, ntile=ntile),
        grid_spec=grid_spec,
        out_shape=jax.ShapeDtypeStruct((tokens, D_MODEL), F32),
        compiler_params=_params(("arbitrary",)),
        name="combine",
    )(bsel_i, x1, aff, taut, needt, beqt.reshape(LANES, 1, LANES), bselt.reshape(LANES, 1, LANES), low, gf, ye)


def _encoder(x, w):
    batch, seq, _ = x.shape
    tokens = batch * seq
    xt = x.reshape(tokens, D_MODEL)
    qkv, vr, vi, gates = _in_proj(xt, w["g1"], w["w_in"], w["w_gate"], w["b_gate"], w["cs"])
    outs, lses = [], []
    for g in range(N_GROUPS):
        o, lse = _attention(qkv, w["bias"][g], g, batch, seq)
        outs.append(o)
        lses.append(lse)
    four = _fourier(vr, vi, batch, seq)
    x1, xn, aff, afft = _mix(xt, outs, lses, four, gates, w["w_attn"], w["w_four"], w["w_out"], w["g2"],
                             w["w_router"])
    tau, need, beq_i, bsel_i, taut, needt, beqt, bselt = _route(afft)
    tau_i, need_i = tau[:, 0], need[:, 0]
    idx = np.arange(ROUTE_TILE)
    u = jnp.asarray(idx[:, None] <= idx[None, :], BF16)
    xe = _gather((tau_i, need_i, beq_i, bsel_i), afft, xn, u)
    ye = _ffn(xe, w["w_eg"], w["w_eu"], w["w_ed"])
    y = _combine(bsel_i, x1, aff, (taut, needt, beqt, bselt), ye, w["gf"])
    return y.reshape(batch, seq, D_MODEL)


def _prepare_weights(rel_bias, norm1_g, w_in, w_attn_br, w_four_br, w_gate, b_gate, w_out,
                     norm2_g, w_router, w_exp_gate, w_exp_up, w_exp_down, final_g):
    c, s = _dft_mats(F_CH)
    return {
        "g1": norm1_g[0].reshape(1, D_MODEL),
        "w_in": w_in[0].astype(BF16),
        "w_gate": w_gate[0].astype(BF16),
        "b_gate": b_gate[0].reshape(1, 2 * D_MODEL),
        "cs": jnp.asarray(np.concatenate([c, s], axis=1), BF16),
        "bias": [_attention_bias(rel_bias, g) for g in range(N_GROUPS)],
        "w_attn": w_attn_br[0].astype(BF16),
        "w_four": w_four_br[0].astype(BF16),
        "w_out": w_out[0].astype(BF16),
        "g2": norm2_g[0].reshape(1, D_MODEL),
        "w_router": jnp.pad(w_router[0], ((0, 0), (0, LANES - N_EXPERTS))),
        "w_eg": w_exp_gate[0],
        "w_eu": w_exp_up[0],
        "w_ed": w_exp_down[0],
        "gf": final_g.reshape(1, D_MODEL),
    }


def kernel(x_prompt, x_sample, rel_bias, norm1_g, w_in, w_attn_br, w_four_br, w_gate, b_gate, w_out,
           norm2_g, w_router, w_exp_gate, w_exp_up, w_exp_down, final_g):
    w = _prepare_weights(rel_bias, norm1_g, w_in, w_attn_br, w_four_br, w_gate, b_gate, w_out,
                         norm2_g, w_router, w_exp_gate, w_exp_up, w_exp_down, final_g)
    return (_encoder(x_prompt, w), _encoder(x_sample, w))
```

```python
import functools
import math

import numpy as np
import jax
import jax.numpy as jnp
from jax import lax
from jax.experimental import pallas as pl
from jax.experimental.pallas import tpu as pltpu

D_MODEL = 1024
HEAD_DIM = 64
HEADS_PER_GROUP = 4
GROUPS = ((128, 1), (512, 4), (2048, 16))
N_GROUPS = len(GROUPS)
GROUP_W = HEADS_PER_GROUP * HEAD_DIM
ATT_W = N_GROUPS * GROUP_W
QKV_W = 3 * ATT_W
F_GROUPS = 6
F_CH = 128
F_W = F_GROUPS * F_CH
NUM_BUCKETS = 32
MAX_DISTANCE = 1024
N_EXPERTS = 16
CAPACITY_FACTOR = 2
D_FF = 2048
EPS = 1e-6
NEG = -1e30

HALF_KEYS = 64
ATT_SUB = 128
TOKEN_TILE = 512
ROUTE_TILE = 256
SLOT_CHUNK = 64
ROW_ALIGN = 16
GATHER_BLOCK = SLOT_CHUNK + ROW_ALIGN
GATHER_STACK = GATHER_BLOCK + ROW_ALIGN
GATHER_PAD = 128
LANES = 128
V7X_VMEM_LIMIT = 56 * 1024 * 1024

F32 = jnp.float32
BF16 = jnp.bfloat16
I32 = jnp.int32


def _params(sem):
    return pltpu.CompilerParams(dimension_semantics=sem, vmem_limit_bytes=V7X_VMEM_LIMIT)


def _dot(a, b):
    return jnp.dot(a, b, preferred_element_type=F32)


def _dot_nt(a, b):
    return lax.dot_general(a, b, (((1,), (1,)), ((), ())), preferred_element_type=F32)


def _ones_where(mask, dtype=F32):
    return jnp.where(mask, jnp.ones((), F32), jnp.zeros((), F32)).astype(dtype)


def _in_proj_kernel(x_ref, g_ref, win_ref, wg_ref, bg_ref, cs_ref, qscale_ref, qkv0_ref, qkv1_ref, qkv2_ref,
                    vr_ref, vi_ref, gates_ref, slab_ref):
    x = x_ref[...]
    tm = x.shape[0]
    ms = jnp.mean(x * x, axis=-1, keepdims=True)
    xn = (x * lax.rsqrt(ms + EPS) * g_ref[...]).astype(BF16)
    nslab = ATT_W // LANES
    for g, out_ref in enumerate((qkv0_ref, qkv1_ref, qkv2_ref)):
        dil = GROUPS[g][1]
        res = _dot(xn, win_ref[:, g * ATT_W:(g + 1) * ATT_W]) * qscale_ref[...]
        if dil == 1:
            out_ref[0, 0] = res.astype(BF16)
            continue
        for j in range(nslab):
            slab_ref[j] = res[:, j * LANES:(j + 1) * LANES]
        rows = tm // dil
        for r in range(dil):
            cls = [slab_ref[j, pl.ds(r, rows, stride=dil), :] for j in range(nslab)]
            out_ref[0, r] = jnp.concatenate(cls, axis=1).astype(BF16)
    u = _dot(xn, win_ref[:, QKV_W:QKV_W + F_W]).astype(BF16)
    cs = cs_ref[...]
    for g in range(F_GROUPS):
        a = _dot(u[:, g * F_CH:(g + 1) * F_CH], cs)
        vr_ref[:, g * F_CH:(g + 1) * F_CH] = a[:, :F_CH].astype(BF16)
        vi_ref[:, g * F_CH:(g + 1) * F_CH] = (-a[:, F_CH:]).astype(BF16)
    z = _dot(xn, wg_ref[...]) + bg_ref[...]
    gates_ref[...] = (1.0 / (1.0 + jnp.exp(-z))).astype(BF16)


def _class_major_spec(tm, dil, width, per_batch):
    return pl.BlockSpec((1, dil, tm // dil, width), lambda i: (i // per_batch, 0, i % per_batch, 0))


def _in_proj(x, g1, w_in, w_gate, b_gate, cs, batch, seq):
    t = x.shape[0]
    tm = TOKEN_TILE
    per_batch = seq // tm
    const = lambda i: (0, 0)
    row = lambda i: (i, 0)
    qscale = np.ones((1, ATT_W), np.float32)
    qscale[:, :GROUP_W] = 1.0 / math.sqrt(HEAD_DIM)
    return pl.pallas_call(
        _in_proj_kernel,
        grid=(t // tm,),
        in_specs=[
            pl.BlockSpec((tm, D_MODEL), row),
            pl.BlockSpec((1, D_MODEL), const),
            pl.BlockSpec(w_in.shape, const),
            pl.BlockSpec(w_gate.shape, const),
            pl.BlockSpec((1, 2 * D_MODEL), const),
            pl.BlockSpec(cs.shape, const),
            pl.BlockSpec((1, ATT_W), const),
        ],
        out_specs=[_class_major_spec(tm, dil, ATT_W, per_batch) for _, dil in GROUPS] + [
            pl.BlockSpec((tm, F_W), row),
            pl.BlockSpec((tm, F_W), row),
            pl.BlockSpec((tm, 2 * D_MODEL), row),
        ],
        out_shape=[jax.ShapeDtypeStruct((batch, dil, seq // dil, ATT_W), BF16) for _, dil in GROUPS] + [
            jax.ShapeDtypeStruct((t, F_W), BF16),
            jax.ShapeDtypeStruct((t, F_W), BF16),
            jax.ShapeDtypeStruct((t, 2 * D_MODEL), BF16),
        ],
        scratch_shapes=[pltpu.VMEM((ATT_W // LANES, tm, LANES), F32)],
        compiler_params=_params(("parallel",)),
        name="in_proj",
    )(x, g1, w_in, w_gate, b_gate, cs, jnp.asarray(qscale))


def _attention_kernel(q_ref, kp_ref, kc_ref, kn_ref, vp_ref, vc_ref, vn_ref, bias_ref, o_ref, lse_ref, *, tq, length):
    i = pl.program_id(2)
    kwin = jnp.concatenate([kp_ref[0, 0], kc_ref[0, 0], kn_ref[0, 0]], axis=0)
    vwin = jnp.concatenate([vp_ref[0, 0], vc_ref[0, 0], vn_ref[0, 0]], axis=0)
    win = ATT_SUB + 2 * HALF_KEYS
    lane_head = lax.broadcasted_iota(I32, (ATT_SUB, GROUP_W), 1) // HEAD_DIM
    col = lax.broadcasted_iota(I32, (ATT_SUB, win), 1)
    for sb in range(tq // ATT_SUB):
        off = sb * ATT_SUB
        q = q_ref[0, 0, off:off + ATT_SUB, :]
        kw = kwin[off:off + win]
        vw = vwin[off:off + win]
        first = i * tq + (off - HALF_KEYS)
        valid = (col >= -first) & (col < length - first)
        qs = jnp.concatenate(
            [jnp.where(lane_head == h, q, jnp.zeros_like(q)) for h in range(HEADS_PER_GROUP)], axis=0)
        s_all = _dot_nt(qs, kw)
        ps, ms, ls = [], [], []
        for h in range(HEADS_PER_GROUP):
            s = s_all[h * ATT_SUB:(h + 1) * ATT_SUB] + bias_ref[h]
            s = jnp.where(valid, s, NEG)
            m = jnp.max(s, axis=-1, keepdims=True)
            p = jnp.exp(s - m)
            ls.append(jnp.sum(p, axis=-1, keepdims=True))
            ms.append(m)
            ps.append(p.astype(BF16))
        o_all = _dot(jnp.concatenate(ps, axis=0), vw)
        out = jnp.zeros((ATT_SUB, GROUP_W), F32)
        lse = jnp.zeros((ATT_SUB, GROUP_W), F32)
        for h in range(HEADS_PER_GROUP):
            oh = o_all[h * ATT_SUB:(h + 1) * ATT_SUB] * (1.0 / ls[h])
            out = jnp.where(lane_head == h, oh, out)
            lse = jnp.where(lane_head == h, ms[h] + jnp.log(ls[h]), lse)
        o_ref[0, 0, off:off + ATT_SUB, :] = out
        lse_ref[0, 0, off:off + ATT_SUB, :] = lse


def _attention(qkv, bias, g):
    batch, dil, length, _ = qkv.shape
    tq = min(length, 512)
    nb = length // tq
    hb = tq // HALF_KEYS
    last_halo = length // HALF_KEYS - 1

    def cur(c):
        return lambda b, r, i: (b, r, i, c)

    def prev(c):
        return lambda b, r, i: (b, r, jnp.maximum(i * hb - 1, 0), c)

    def nxt(c):
        return lambda b, r, i: (b, r, jnp.minimum((i + 1) * hb, last_halo), c)

    blk = lambda rows: (1, 1, rows, GROUP_W)
    return pl.pallas_call(
        functools.partial(_attention_kernel, tq=tq, length=length),
        grid=(batch, dil, nb),
        in_specs=[
            pl.BlockSpec(blk(tq), cur(0)),
            pl.BlockSpec(blk(HALF_KEYS), prev(1)),
            pl.BlockSpec(blk(tq), cur(1)),
            pl.BlockSpec(blk(HALF_KEYS), nxt(1)),
            pl.BlockSpec(blk(HALF_KEYS), prev(2)),
            pl.BlockSpec(blk(tq), cur(2)),
            pl.BlockSpec(blk(HALF_KEYS), nxt(2)),
            pl.BlockSpec(bias.shape, lambda b, r, i: (0, 0, 0)),
        ],
        out_specs=[pl.BlockSpec(blk(tq), cur(0))] * 2,
        out_shape=[jax.ShapeDtypeStruct((batch, dil, length, GROUP_W), F32)] * 2,
        compiler_params=_params(("parallel", "parallel", "parallel")),
        name=f"attention_g{g}",
    )(qkv, qkv, qkv, qkv, qkv, qkv, qkv, bias)


def _t5_bucket(rel):
    nb = NUM_BUCKETS // 2
    max_exact = nb // 2
    ret = (rel > 0).astype(np.int32) * nb
    n = np.abs(rel)
    large = max_exact + (np.log(np.maximum(n, max_exact) / max_exact)
                         / np.log(MAX_DISTANCE / max_exact) * (nb - max_exact)).astype(np.int32)
    large = np.minimum(large, nb - 1)
    return (ret + np.where(n < max_exact, n, large)).astype(np.int32)


def _attention_bias(rel_bias, g):
    dil = GROUPS[g][1]
    qi = np.arange(ATT_SUB)[:, None]
    kj = np.arange(ATT_SUB + 2 * HALF_KEYS)[None, :]
    delta = kj - HALF_KEYS - qi
    band = np.abs(delta) <= HALF_KEYS
    bucket = _t5_bucket(dil * delta)
    tab = rel_bias[:, g * HEADS_PER_GROUP:(g + 1) * HEADS_PER_GROUP].astype(F32)
    onehot = jnp.asarray(bucket[..., None] == np.arange(NUM_BUCKETS), F32)
    bias = jnp.einsum("qkb,bh->hqk", onehot, tab, precision=lax.Precision.HIGHEST)
    return jnp.where(jnp.asarray(band)[None], bias, NEG)


def _dft_mats(n):
    k = np.arange(n)
    ang = 2.0 * np.pi * ((k[:, None] * k[None, :]) % n) / n
    return np.cos(ang), np.sin(ang)


def _fft_stage1_kernel(vr_ref, vi_ref, m1_ref, twc_ref, tws_ref, zr_ref, zi_ref, *, n1, m):
    x = jnp.concatenate([vr_ref[0], vi_ref[0]], axis=0)
    z = _dot(m1_ref[...], x)
    zr, zi = z[:n1], z[n1:]
    twc, tws = twc_ref[0], tws_ref[0]
    for j in range(m):
        c = twc[:, j:j + 1]
        s = tws[:, j:j + 1]
        a = zr[:, j * F_W:(j + 1) * F_W]
        b = zi[:, j * F_W:(j + 1) * F_W]
        zr_ref[0, :, j * F_W:(j + 1) * F_W] = (a * c + b * s).astype(BF16)
        zi_ref[0, :, j * F_W:(j + 1) * F_W] = (b * c - a * s).astype(BF16)


def _fft_stage2_kernel(zr_ref, zi_ref, m2_ref, o_ref, *, kc, scale):
    m2 = m2_ref[...]
    for j in range(kc):
        x = jnp.concatenate([zr_ref[0, j], zi_ref[0, j]], axis=0)
        o_ref[0, :, j * F_W:(j + 1) * F_W] = _dot(m2, x) * scale


def _fourier(vr, vi, batch, seq):
    n2 = LANES
    n1 = seq // n2
    m = 8
    c1, s1 = _dft_mats(n1)
    m1 = jnp.asarray(np.block([[c1, s1], [-s1, c1]]), BF16)
    c2, s2 = _dft_mats(n2)
    m2 = jnp.asarray(np.concatenate([c2, s2], axis=1), BF16)
    k1 = np.arange(n1)[:, None]
    sv = np.arange(n2)[None, :]
    ang = 2.0 * np.pi * ((k1 * sv) % seq) / seq
    to_blocks = lambda a: jnp.asarray(a.reshape(n1, n2 // m, m).transpose(1, 0, 2), F32)
    twc, tws = to_blocks(np.cos(ang)), to_blocks(np.sin(ang))

    v3 = lambda a: a.reshape(batch, n1, n2 * F_W)
    blk = (1, n1, m * F_W)
    dmap = lambda b, j: (b, 0, j)
    tmap = lambda b, j: (j, 0, 0)
    zr, zi = pl.pallas_call(
        functools.partial(_fft_stage1_kernel, n1=n1, m=m),
        grid=(batch, n2 // m),
        in_specs=[
            pl.BlockSpec(blk, dmap),
            pl.BlockSpec(blk, dmap),
            pl.BlockSpec(m1.shape, lambda b, j: (0, 0)),
            pl.BlockSpec((1, n1, m), tmap),
            pl.BlockSpec((1, n1, m), tmap),
        ],
        out_specs=[pl.BlockSpec(blk, dmap), pl.BlockSpec(blk, dmap)],
        out_shape=[jax.ShapeDtypeStruct((batch, n1, n2 * F_W), BF16)] * 2,
        compiler_params=_params(("parallel", "parallel")),
        name="fft_stage1",
    )(v3(vr), v3(vi), m1, twc, tws)

    kc = 8
    v4 = lambda a: a.reshape(batch, n1, n2, F_W)
    zblk = (1, kc, n2, F_W)
    zmap = lambda b, j: (b, j, 0, 0)
    out = pl.pallas_call(
        functools.partial(_fft_stage2_kernel, kc=kc, scale=1.0 / math.sqrt(seq * F_CH)),
        grid=(batch, n1 // kc),
        in_specs=[
            pl.BlockSpec(zblk, zmap),
            pl.BlockSpec(zblk, zmap),
            pl.BlockSpec(m2.shape, lambda b, j: (0, 0)),
        ],
        out_specs=pl.BlockSpec((1, n2, kc * F_W), lambda b, j: (b, 0, j)),
        out_shape=jax.ShapeDtypeStruct((batch, n2, n1 * F_W), F32),
        compiler_params=_params(("parallel", "parallel")),
        name="fft_stage2",
    )(v4(zr), v4(zi), m2)
    return out.reshape(batch * seq, F_W)


def _mix_kernel(x_ref, o0_ref, o1_ref, o2_ref, l0_ref, l1_ref, l2_ref, four_ref, gates_ref,
                wa_ref, wf_ref, wo_ref, g2_ref, wrh_ref, wrl_ref, x1_ref, xn_ref, aff_ref, afft_ref, slab_ref):
    tm = x_ref.shape[0]

    def token_order(ref, dil):
        if dil == 1:
            return ref[0, 0]
        rows = tm // dil
        for r in range(dil):
            v = ref[0, r]
            for j in range(GROUP_W // LANES):
                slab_ref[j, pl.ds(r, rows, stride=dil), :] = v[:, j * LANES:(j + 1) * LANES]
        return jnp.concatenate([slab_ref[j] for j in range(GROUP_W // LANES)], axis=1)

    dils = [dil for _, dil in GROUPS]
    l0, l1, l2 = (token_order(r, d) for r, d in zip((l0_ref, l1_ref, l2_ref), dils))
    mx = jnp.maximum(jnp.maximum(l0, l1), l2)
    e0, e1, e2 = jnp.exp(l0 - mx), jnp.exp(l1 - mx), jnp.exp(l2 - mx)
    o0, o1, o2 = (token_order(r, d) for r, d in zip((o0_ref, o1_ref, o2_ref), dils))
    att = (e0 * o0 + e1 * o1 + e2 * o2) * (1.0 / (e0 + e1 + e2))
    a_br = _dot(att.astype(BF16), wa_ref[...])
    f_br = _dot(four_ref[...].astype(BF16), wf_ref[...])
    mix = gates_ref[:, :D_MODEL] * a_br + gates_ref[:, D_MODEL:] * f_br
    x1 = x_ref[...] + _dot(mix.astype(BF16), wo_ref[...])
    x1_ref[...] = x1
    ms = jnp.mean(x1 * x1, axis=-1, keepdims=True)
    xn = x1 * lax.rsqrt(ms + EPS) * g2_ref[...]
    xn_ref[...] = xn.astype(BF16)
    xh = xn.astype(BF16)
    xl = (xn - xh.astype(F32)).astype(BF16)
    logits = _dot(xh, wrh_ref[...]) + (_dot(xh, wrl_ref[...]) + _dot(xl, wrh_ref[...]))
    lane = lax.broadcasted_iota(I32, logits.shape, 1)
    logits = jnp.where(lane < N_EXPERTS, logits, NEG)
    p = jnp.exp(logits - jnp.max(logits, axis=-1, keepdims=True))
    aff = p * (1.0 / jnp.sum(p, axis=-1, keepdims=True))
    aff_ref[...] = aff
    afft_ref[...] = aff.T[:N_EXPERTS]


def _mix(x, os_, ls_, four, gates, w_attn, w_four, w_out, g2, w_router_hi, w_router_lo, seq):
    t = x.shape[0]
    tm = TOKEN_TILE
    per_batch = seq // tm
    const = lambda i: (0, 0)
    row = lambda i: (i, 0)
    rows = lambda w: pl.BlockSpec((tm, w), row)
    full = lambda a: pl.BlockSpec(a.shape, const)
    classes = [_class_major_spec(tm, dil, GROUP_W, per_batch) for _, dil in GROUPS]
    return pl.pallas_call(
        _mix_kernel,
        grid=(t // tm,),
        in_specs=[rows(D_MODEL)] + classes * 2 + [rows(F_W), rows(2 * D_MODEL),
                  full(w_attn), full(w_four), full(w_out), full(g2), full(w_router_hi), full(w_router_lo)],
        out_specs=[rows(D_MODEL), rows(D_MODEL), rows(LANES), pl.BlockSpec((N_EXPERTS, tm), lambda i: (0, i))],
        out_shape=[
            jax.ShapeDtypeStruct((t, D_MODEL), F32),
            jax.ShapeDtypeStruct((t, D_MODEL), BF16),
            jax.ShapeDtypeStruct((t, LANES), F32),
            jax.ShapeDtypeStruct((N_EXPERTS, t), F32),
        ],
        scratch_shapes=[pltpu.VMEM((GROUP_W // LANES, tm, LANES), F32)],
        compiler_params=_params(("parallel",)),
        name="mix",
    )(x, *os_, *ls_, four, gates, w_attn, w_four, w_out, g2, w_router_hi, w_router_lo)


def _route_kernel(afft_ref, su_ref, u_ref, tau_ref, need_ref, beq_ref, bsel_ref,
                  taut_ref, needt_ref, beqt_ref, bselt_ref, *, tokens):
    cap = CAPACITY_FACTOR * tokens // N_EXPERTS
    ntile = tokens // ROUTE_TILE
    shape = (N_EXPERTS, LANES)
    lane = lax.broadcasted_iota(I32, shape, 1)

    def keys(start, width):
        return lax.bitcast_convert_type(afft_ref[:, pl.ds(pl.multiple_of(start, LANES), width)], I32)

    def count(pred):
        def body(c, acc):
            return acc + _ones_where(pred(keys(c * LANES, LANES)))
        acc = lax.fori_loop(0, tokens // LANES, body, jnp.zeros(shape, F32))
        return jnp.sum(acc, axis=1, keepdims=True)

    def bit_body(i, prefix):
        cand = prefix | lax.shift_left(jnp.ones(shape, I32), jnp.full(shape, 30 - i, I32))
        tot = count(lambda k: k >= cand)
        return jnp.where(tot >= cap, cand, prefix)

    tau = lax.fori_loop(0, 31, bit_body, jnp.zeros(shape, I32))
    n_gt = count(lambda k: k > tau)
    need = cap - n_gt
    tau_col = tau[:, :1]

    def prefix_over_tiles(tab):
        return _dot(tab.astype(BF16), su_ref[...])

    def at_lane(tab, c):
        return jnp.sum(jnp.where(lane == c, tab, 0.0), axis=1, keepdims=True)

    def eq_body(c, tab):
        k = keys(c * ROUTE_TILE, ROUTE_TILE)
        cnt = jnp.sum(_ones_where(k == tau_col), axis=1, keepdims=True)
        return jnp.where(lane == c, cnt, tab)

    base_eq = prefix_over_tiles(lax.fori_loop(0, ntile, eq_body, jnp.zeros(shape, F32)))

    def sel_body(c, tab):
        k = keys(c * ROUTE_TILE, ROUTE_TILE)
        eq = k == tau_col
        eq_cum = _dot(_ones_where(eq, BF16), u_ref[...]) + at_lane(base_eq, c)
        sel = (k > tau_col) | (eq & (eq_cum <= need))
        cnt = jnp.sum(_ones_where(sel), axis=1, keepdims=True)
        return jnp.where(lane == c, cnt, tab)

    base_sel = prefix_over_tiles(lax.fori_loop(0, ntile, sel_body, jnp.zeros(shape, F32)))

    def transposed(val):
        return jnp.concatenate([val, jnp.zeros((LANES - N_EXPERTS, LANES), val.dtype)], axis=0).T

    tau_ref[...] = tau
    taut_ref[...] = transposed(tau)
    for val, ref, ref_t in ((jnp.broadcast_to(need, shape), need_ref, needt_ref),
                            (base_eq, beq_ref, beqt_ref), (base_sel, bsel_ref, bselt_ref)):
        ref[...] = val.astype(I32)
        ref_t[...] = transposed(val)


def _route(afft):
    tokens = afft.shape[1]
    idx = np.arange(LANES)
    su = jnp.asarray(idx[:, None] < idx[None, :], BF16)
    idx = np.arange(ROUTE_TILE)
    u = jnp.asarray(idx[:, None] <= idx[None, :], BF16)
    full = lambda a: pl.BlockSpec(a.shape, lambda i: (0,) * a.ndim)
    small = pl.BlockSpec((N_EXPERTS, LANES), lambda i: (0, 0))
    smallt = pl.BlockSpec((LANES, LANES), lambda i: (0, 0))
    return pl.pallas_call(
        functools.partial(_route_kernel, tokens=tokens),
        grid=(1,),
        in_specs=[full(afft), full(su), full(u)],
        out_specs=[small] * 4 + [smallt] * 4,
        out_shape=[jax.ShapeDtypeStruct((N_EXPERTS, LANES), I32)] * 4
        + [jax.ShapeDtypeStruct((LANES, LANES), I32)] + [jax.ShapeDtypeStruct((LANES, LANES), F32)] * 3,
        compiler_params=_params(("arbitrary",)),
        name="route",
    )(afft, su, u)


def _gather_kernel(bsel_s, afft_ref, tau_ref, need_ref, beq_ref, x_ref, u_ref, xe_hbm,
                   stage_ref, tail_ref, xbuf_ref, zeros_ref, sem_ref, xsem_ref, *, ntile, cap):
    t = pl.program_id(0)
    par = t % 2

    def aligned(e, tile):
        return pl.multiple_of((bsel_s[e, tile] // ROW_ALIGN) * ROW_ALIGN, ROW_ALIGN)

    def writes(tile, buf):
        return [pltpu.make_async_copy(stage_ref.at[buf, e], xe_hbm.at[e, pl.ds(aligned(e, tile), GATHER_BLOCK)],
                                      sem_ref.at[buf, e]) for e in range(N_EXPERTS)]

    @pl.when(t == 0)
    def _():
        tail_ref[...] = jnp.zeros_like(tail_ref)
        zeros_ref[...] = jnp.zeros_like(zeros_ref)
        fills = [pltpu.make_async_copy(zeros_ref, xe_hbm.at[e, pl.ds(cap, GATHER_PAD)], sem_ref.at[0, e])
                 for e in range(N_EXPERTS)]
        for cp in fills:
            cp.start()
        for cp in fills:
            cp.wait()

    k = lax.bitcast_convert_type(afft_ref[...], I32)
    tau = tau_ref[:, :1]
    lane = lax.broadcasted_iota(I32, (N_EXPERTS, LANES), 1)
    beq = jnp.sum(jnp.where(lane == t, beq_ref[...].astype(F32), 0.0), axis=1, keepdims=True)
    eq = k == tau
    eq_cum = _dot(_ones_where(eq, BF16), u_ref[...]) + beq
    sel = (k > tau) | (eq & (eq_cum <= need_ref[:, :1].astype(F32)))
    rank = jnp.where(sel, _dot(_ones_where(sel, BF16), u_ref[...]) - 1.0, -1e4)

    row = lax.broadcasted_iota(I32, (GATHER_STACK, ROUTE_TILE), 0)
    in_block = row < GATHER_BLOCK
    row_f = row.astype(F32)
    offs, shifts, pieces = [], [], []
    for e in range(N_EXPERTS):
        off = (bsel_s[e, t] - aligned(e, t)).astype(F32)
        shift = ((bsel_s[e, t + 1] // ROW_ALIGN) * ROW_ALIGN - aligned(e, t))
        target = jnp.where(in_block, row_f, row_f - float(GATHER_BLOCK) + shift.astype(F32))
        pieces.append(_ones_where(rank[e:e + 1, :] + off == target, BF16))
        offs.append(off)
        shifts.append(shift)
    res = _dot(jnp.concatenate(pieces, axis=0), x_ref[...])
    for e in range(N_EXPERTS):
        base = e * GATHER_STACK
        old = tail_ref[e]
        stage_ref[par, e, 0:ROW_ALIGN, :] = (res[base:base + ROW_ALIGN] + old).astype(BF16)
        stage_ref[par, e, ROW_ALIGN:GATHER_BLOCK, :] = res[base + ROW_ALIGN:base + GATHER_BLOCK].astype(BF16)
        tail_ref[e] = res[base + GATHER_BLOCK:base + GATHER_STACK] + jnp.where(shifts[e] == 0, old, 0.0)

    @pl.when(t > 0)
    def _():
        for cp in writes(t - 1, 1 - par):
            cp.wait()

    for cp in writes(t, par):
        cp.start()

    extra = [(jnp.maximum(bsel_s[e, t + 1] - aligned(e, t) - GATHER_BLOCK, 0) + SLOT_CHUNK - 1) // SLOT_CHUNK
             for e in range(N_EXPERTS)]

    @pl.when(functools.reduce(jnp.maximum, extra) > 0)
    def _():
        row64 = lax.broadcasted_iota(I32, (SLOT_CHUNK, ROUTE_TILE), 0).astype(F32)
        for e in range(N_EXPERTS):
            def chunk(c, carry):
                first = GATHER_BLOCK + c * SLOT_CHUNK
                onehot = _ones_where(rank[e:e + 1, :] + offs[e] == row64 + first.astype(F32), BF16)
                xbuf_ref[...] = _dot(onehot, x_ref[...]).astype(BF16)
                dst = pl.multiple_of(aligned(e, t) + first, ROW_ALIGN)
                cp = pltpu.make_async_copy(xbuf_ref, xe_hbm.at[e, pl.ds(dst, SLOT_CHUNK)], xsem_ref.at[0])
                cp.start()
                cp.wait()
                return carry

            lax.fori_loop(0, extra[e], chunk, 0)

    @pl.when(t == ntile - 1)
    def _():
        for cp in writes(t, par):
            cp.wait()


def _gather(bsel_i, afft, tau, need, beq_i, xn, u):
    tokens = xn.shape[0]
    cap = CAPACITY_FACTOR * tokens // N_EXPERTS
    ntile = tokens // ROUTE_TILE
    table = pl.BlockSpec((N_EXPERTS, LANES), lambda t, *_: (0, 0))
    grid_spec = pltpu.PrefetchScalarGridSpec(
        num_scalar_prefetch=1,
        grid=(ntile,),
        in_specs=[
            pl.BlockSpec((N_EXPERTS, ROUTE_TILE), lambda t, *_: (0, t)),
            table, table, table,
            pl.BlockSpec((ROUTE_TILE, D_MODEL), lambda t, *_: (t, 0)),
            pl.BlockSpec(u.shape, lambda t, *_: (0, 0)),
        ],
        out_specs=pl.BlockSpec(memory_space=pl.ANY),
        scratch_shapes=[
            pltpu.VMEM((2, N_EXPERTS, GATHER_BLOCK, D_MODEL), BF16),
            pltpu.VMEM((N_EXPERTS, ROW_ALIGN, D_MODEL), F32),
            pltpu.VMEM((SLOT_CHUNK, D_MODEL), BF16),
            pltpu.VMEM((GATHER_PAD, D_MODEL), BF16),
            pltpu.SemaphoreType.DMA((2, N_EXPERTS)),
            pltpu.SemaphoreType.DMA((1,)),
        ],
    )
    return pl.pallas_call(
        functools.partial(_gather_kernel, ntile=ntile, cap=cap),
        grid_spec=grid_spec,
        out_shape=jax.ShapeDtypeStruct((N_EXPERTS, cap + GATHER_PAD, D_MODEL), BF16),
        compiler_params=_params(("arbitrary",)),
        name="gather",
    )(bsel_i, afft, tau, need, beq_i, xn, u)


def _ffn_kernel(xe_ref, wg_ref, wu_ref, wd_ref, ye_ref, acc_ref, *, cap, nf, tm):
    f = pl.program_id(1)
    wg, wu, wd = wg_ref[0].astype(BF16), wu_ref[0].astype(BF16), wd_ref[0].astype(BF16)

    @pl.when(f == 0)
    def _():
        acc_ref[...] = jnp.zeros_like(acc_ref)

    for i in range(cap // tm):
        r = slice(i * tm, (i + 1) * tm)
        x = xe_ref[0, r, :]
        hg = _dot(x, wg)
        hu = _dot(x, wu)
        h = (hg * (1.0 / (1.0 + jnp.exp(-hg))) * hu).astype(BF16)
        acc_ref[r, :] += _dot(h, wd)

    @pl.when(f == nf - 1)
    def _():
        ye_ref[...] = acc_ref[...].astype(BF16)


def _ffn(xe, w_eg, w_eu, w_ed):
    cap = xe.shape[1] - GATHER_PAD
    tf = 512
    nf = D_FF // tf
    tm = min(cap, 1024)
    return pl.pallas_call(
        functools.partial(_ffn_kernel, cap=cap, nf=nf, tm=tm),
        grid=(N_EXPERTS, nf),
        in_specs=[
            pl.BlockSpec((1, cap, D_MODEL), lambda e, f: (e, 0, 0)),
            pl.BlockSpec((1, D_MODEL, tf), lambda e, f: (e, 0, f)),
            pl.BlockSpec((1, D_MODEL, tf), lambda e, f: (e, 0, f)),
            pl.BlockSpec((1, tf, D_MODEL), lambda e, f: (e, f, 0)),
        ],
        out_specs=pl.BlockSpec((cap, D_MODEL), lambda e, f: (e, 0)),
        out_shape=jax.ShapeDtypeStruct((N_EXPERTS * cap, D_MODEL), BF16),
        scratch_shapes=[pltpu.VMEM((cap, D_MODEL), F32)],
        compiler_params=_params(("arbitrary", "arbitrary")),
        name="ffn",
    )(xe, w_eg, w_eu, w_ed)


def _combine_kernel(bsel_s, x1_ref, aff_ref, taut_ref, needt_ref, beqt_ref, bselt_ref, low_ref, gf_ref,
                    ye_hbm, y_ref, buf_ref, xbuf_ref, sem_ref, xsem_ref, *, cap, total, ntile):
    t = pl.program_id(0)
    par = t % 2

    def aligned(e, tile):
        return (bsel_s[e, tile] // 16) * 16

    def window(e, tile, c):
        start = jnp.minimum(e * cap + aligned(e, tile) + c * SLOT_CHUNK, total - SLOT_CHUNK)
        return pl.multiple_of(start, 16)

    def first_chunks(tile, buf):
        return [pltpu.make_async_copy(ye_hbm.at[pl.ds(window(e, tile, 0), SLOT_CHUNK)],
                                      buf_ref.at[buf, pl.ds(e * SLOT_CHUNK, SLOT_CHUNK)], sem_ref.at[buf, e])
                for e in range(N_EXPERTS)]

    @pl.when(t == 0)
    def _():
        for cp in first_chunks(0, 0):
            cp.start()

    @pl.when(t + 1 < ntile)
    def _():
        for cp in first_chunks(t + 1, 1 - par):
            cp.start()

    aff = aff_ref[...]
    k = lax.bitcast_convert_type(aff, I32)
    tau = taut_ref[0:1, :]
    low = low_ref[...]
    eq = k == tau
    eq_cum = _dot(low, _ones_where(eq, BF16)) + beqt_ref[0]
    sel = (k > tau) | (eq & (eq_cum <= needt_ref[0:1, :]))
    slot = jnp.where(sel, _dot(low, _ones_where(sel, BF16)) + (bselt_ref[0] - 1.0), -1.0)

    g_hi = aff.astype(BF16).astype(F32)
    g_lo = aff - g_hi
    lane = lax.broadcasted_iota(I32, (ROUTE_TILE, LANES), 1)
    first_half = lane < SLOT_CHUNK
    row_in_chunk = (lane % SLOT_CHUNK).astype(F32)
    a_hi, a_lo = [], []
    for e in range(0, N_EXPERTS, 2):
        def pick(v):
            return jnp.where(first_half, v[:, e:e + 1], v[:, e + 1:e + 2])
        rel0 = (window(e, t, 0) - e * cap).astype(F32)
        rel1 = (window(e + 1, t, 0) - (e + 1) * cap).astype(F32)
        hit = row_in_chunk + jnp.where(first_half, rel0, rel1) == pick(slot)
        a_hi.append(jnp.where(hit, pick(g_hi), 0.0).astype(BF16))
        a_lo.append(jnp.where(hit, pick(g_lo), 0.0).astype(BF16))
    for cp in first_chunks(t, par):
        cp.wait()
    rows = buf_ref[par]
    y_ref[...] = (x1_ref[...] + _dot(jnp.concatenate(a_hi, axis=1), rows)
                  + _dot(jnp.concatenate(a_lo, axis=1), rows))

    nch = [(bsel_s[e, t + 1] - aligned(e, t) + SLOT_CHUNK - 1) // SLOT_CHUNK for e in range(N_EXPERTS)]

    @pl.when(functools.reduce(jnp.maximum, nch) > 1)
    def _():
        lane64 = lax.broadcasted_iota(I32, (ROUTE_TILE, SLOT_CHUNK), 1).astype(F32)
        for e in range(N_EXPERTS):
            slot_e = slot[:, e:e + 1]

            def extra(c, carry):
                w = window(e, t, c)
                cp = pltpu.make_async_copy(ye_hbm.at[pl.ds(w, SLOT_CHUNK)], xbuf_ref, xsem_ref.at[0])
                cp.start()
                cp.wait()
                first = (aligned(e, t) + c * SLOT_CHUNK).astype(F32)
                hit = (lane64 + (w - e * cap).astype(F32) == slot_e) & (slot_e >= first)
                y_ref[...] += aff[:, e:e + 1] * _dot(_ones_where(hit, BF16), xbuf_ref[...])
                return carry

            lax.fori_loop(1, nch[e], extra, 0)

    acc = y_ref[...]
    ms = jnp.mean(acc * acc, axis=-1, keepdims=True)
    y_ref[...] = acc * lax.rsqrt(ms + EPS) * gf_ref[...]


def _combine(bsel_i, x1, aff, tables_t, ye, gf):
    tokens = x1.shape[0]
    cap = CAPACITY_FACTOR * tokens // N_EXPERTS
    ntile = tokens // ROUTE_TILE
    idx = np.arange(ROUTE_TILE)
    low = jnp.asarray(idx[:, None] >= idx[None, :], BF16)
    taut, needt, beqt, bselt = tables_t
    rowvec = pl.BlockSpec((8, LANES), lambda t, *_: (0, 0))
    tilevec = pl.BlockSpec((1, 1, LANES), lambda t, *_: (t, 0, 0))
    grid_spec = pltpu.PrefetchScalarGridSpec(
        num_scalar_prefetch=1,
        grid=(ntile,),
        in_specs=[
            pl.BlockSpec((ROUTE_TILE, D_MODEL), lambda t, *_: (t, 0)),
            pl.BlockSpec((ROUTE_TILE, LANES), lambda t, *_: (t, 0)),
            rowvec, rowvec, tilevec, tilevec,
            pl.BlockSpec(low.shape, lambda t, *_: (0, 0)),
            pl.BlockSpec((1, D_MODEL), lambda t, *_: (0, 0)),
            pl.BlockSpec(memory_space=pl.ANY),
        ],
        out_specs=pl.BlockSpec((ROUTE_TILE, D_MODEL), lambda t, *_: (t, 0)),
        scratch_shapes=[
            pltpu.VMEM((2, N_EXPERTS * SLOT_CHUNK, D_MODEL), BF16),
            pltpu.VMEM((SLOT_CHUNK, D_MODEL), BF16),
            pltpu.SemaphoreType.DMA((2, N_EXPERTS)),
            pltpu.SemaphoreType.DMA((1,)),
        ],
    )
    return pl.pallas_call(
        functools.partial(_combine_kernel, cap=cap, total=N_EXPERTS * cap, ntile=ntile),
        grid_spec=grid_spec,
        out_shape=jax.ShapeDtypeStruct((tokens, D_MODEL), F32),
        compiler_params=_params(("arbitrary",)),
        name="combine",
    )(bsel_i, x1, aff, taut, needt, beqt.reshape(LANES, 1, LANES), bselt.reshape(LANES, 1, LANES), low, gf, ye)


def _encoder(x, w):
    batch, seq, _ = x.shape
    tokens = batch * seq
    xt = x.reshape(tokens, D_MODEL)
    *qkvs, vr, vi, gates = _in_proj(xt, w["g1"], w["w_in"], w["w_gate"], w["b_gate"], w["cs"], batch, seq)
    outs, lses = [], []
    for g in range(N_GROUPS):
        o, lse = _attention(qkvs[g], w["bias"][g], g)
        outs.append(o)
        lses.append(lse)
    four = _fourier(vr, vi, batch, seq)
    x1, xn, aff, afft = _mix(xt, outs, lses, four, gates, w["w_attn"], w["w_four"], w["w_out"], w["g2"],
                             w["w_router_hi"], w["w_router_lo"], seq)
    tau, need, beq_i, bsel_i, taut, needt, beqt, bselt = _route(afft)
    idx = np.arange(ROUTE_TILE)
    u = jnp.asarray(idx[:, None] <= idx[None, :], BF16)
    xe = _gather(bsel_i, afft, tau, need, beq_i, xn, u)
    ye = _ffn(xe, w["w_eg"], w["w_eu"], w["w_ed"])
    y = _combine(bsel_i, x1, aff, (taut, needt, beqt, bselt), ye, w["gf"])
    return y.reshape(batch, seq, D_MODEL)


def _prepare_weights(rel_bias, norm1_g, w_in, w_attn_br, w_four_br, w_gate, b_gate, w_out,
                     norm2_g, w_router, w_exp_gate, w_exp_up, w_exp_down, final_g):
    c, s = _dft_mats(F_CH)
    starts = [part * ATT_W + g * GROUP_W for g in range(N_GROUPS) for part in range(3)]
    w_in_grouped = jnp.concatenate([w_in[0][:, s0:s0 + GROUP_W] for s0 in starts] + [w_in[0][:, QKV_W:]], axis=1)
    w_router = jnp.pad(w_router[0], ((0, 0), (0, LANES - N_EXPERTS)))
    w_router_hi = w_router.astype(BF16)
    return {
        "g1": norm1_g[0].reshape(1, D_MODEL),
        "w_in": w_in_grouped.astype(BF16),
        "w_gate": w_gate[0].astype(BF16),
        "b_gate": b_gate[0].reshape(1, 2 * D_MODEL),
        "cs": jnp.asarray(np.concatenate([c, s], axis=1), BF16),
        "bias": [_attention_bias(rel_bias, g) for g in range(N_GROUPS)],
        "w_attn": w_attn_br[0].astype(BF16),
        "w_four": w_four_br[0].astype(BF16),
        "w_out": w_out[0].astype(BF16),
        "g2": norm2_g[0].reshape(1, D_MODEL),
        "w_router_hi": w_router_hi,
        "w_router_lo": (w_router - w_router_hi.astype(F32)).astype(BF16),
        "w_eg": w_exp_gate[0],
        "w_eu": w_exp_up[0],
        "w_ed": w_exp_down[0],
        "gf": final_g.reshape(1, D_MODEL),
    }


def kernel(x_prompt, x_sample, rel_bias, norm1_g, w_in, w_attn_br, w_four_br, w_gate, b_gate, w_out,
           norm2_g, w_router, w_exp_gate, w_exp_up, w_exp_down, final_g):
    w = _prepare_weights(rel_bias, norm1_g, w_in, w_attn_br, w_four_br, w_gate, b_gate, w_out,
                         norm2_g, w_router, w_exp_gate, w_exp_up, w_exp_down, final_g)
    return (_encoder(x_prompt, w), _encoder(x_sample, w))
```

```python
import functools
import math

import numpy as np
import jax
import jax.numpy as jnp
from jax import lax
from jax.experimental import pallas as pl
from jax.experimental.pallas import tpu as pltpu

D_MODEL = 1024
HEAD_DIM = 64
HEADS_PER_GROUP = 4
GROUPS = ((128, 1), (512, 4), (2048, 16))
N_GROUPS = len(GROUPS)
GROUP_W = HEADS_PER_GROUP * HEAD_DIM
ATT_W = N_GROUPS * GROUP_W
QKV_W = 3 * ATT_W
F_GROUPS = 6
F_CH = 128
F_W = F_GROUPS * F_CH
NUM_BUCKETS = 32
MAX_DISTANCE = 1024
N_EXPERTS = 16
CAPACITY_FACTOR = 2
D_FF = 2048
EPS = 1e-6
NEG = -1e30

HALF_KEYS = 64
ATT_SUB = 128
TOKEN_TILE = 512
ROUTE_TILE = 256
SLOT_CHUNK = 64
ROW_ALIGN = 16
GATHER_BLOCK = SLOT_CHUNK + ROW_ALIGN
GATHER_STACK = GATHER_BLOCK + ROW_ALIGN
GATHER_PAD = 128
LANES = 128
V7X_VMEM_LIMIT = 56 * 1024 * 1024

F32 = jnp.float32
BF16 = jnp.bfloat16
I32 = jnp.int32


def _params(sem):
    return pltpu.CompilerParams(dimension_semantics=sem, vmem_limit_bytes=V7X_VMEM_LIMIT)


def _dot(a, b):
    return jnp.dot(a, b, preferred_element_type=F32)


def _dot_nt(a, b):
    return lax.dot_general(a, b, (((1,), (1,)), ((), ())), preferred_element_type=F32)


def _ones_where(mask, dtype=F32):
    return jnp.where(mask, jnp.ones((), F32), jnp.zeros((), F32)).astype(dtype)


def _in_proj_kernel(x_ref, g_ref, win_ref, wg_ref, bg_ref, cs_ref, qscale_ref, qkv0_ref, qkv1_ref, qkv2_ref,
                    vr_ref, vi_ref, gates_ref, slab_ref):
    x = x_ref[...]
    tm = x.shape[0]
    ms = jnp.mean(x * x, axis=-1, keepdims=True)
    xn = (x * lax.rsqrt(ms + EPS) * g_ref[...]).astype(BF16)
    nslab = ATT_W // LANES
    for g, out_ref in enumerate((qkv0_ref, qkv1_ref, qkv2_ref)):
        dil = GROUPS[g][1]
        res = _dot(xn, win_ref[:, g * ATT_W:(g + 1) * ATT_W]) * qscale_ref[...]
        if dil == 1:
            out_ref[0, 0] = res.astype(BF16)
            continue
        for j in range(nslab):
            slab_ref[j] = res[:, j * LANES:(j + 1) * LANES]
        rows = tm // dil
        for r in range(dil):
            cls = [slab_ref[j, pl.ds(r, rows, stride=dil), :] for j in range(nslab)]
            out_ref[0, r] = jnp.concatenate(cls, axis=1).astype(BF16)
    u = _dot(xn, win_ref[:, QKV_W:QKV_W + F_W]).astype(BF16)
    cs = cs_ref[...]
    for g in range(F_GROUPS):
        a = _dot(u[:, g * F_CH:(g + 1) * F_CH], cs)
        vr_ref[:, g * F_CH:(g + 1) * F_CH] = a[:, :F_CH].astype(BF16)
        vi_ref[:, g * F_CH:(g + 1) * F_CH] = (-a[:, F_CH:]).astype(BF16)
    z = _dot(xn, wg_ref[...]) + bg_ref[...]
    gates_ref[...] = (1.0 / (1.0 + jnp.exp(-z))).astype(BF16)


def _class_major_spec(tm, dil, width, per_batch):
    return pl.BlockSpec((1, dil, tm // dil, width), lambda i: (i // per_batch, 0, i % per_batch, 0))


def _in_proj(x, g1, w_in, w_gate, b_gate, cs, batch, seq):
    t = x.shape[0]
    tm = TOKEN_TILE
    per_batch = seq // tm
    const = lambda i: (0, 0)
    row = lambda i: (i, 0)
    qscale = np.ones((1, ATT_W), np.float32)
    qscale[:, :GROUP_W] = 1.0 / math.sqrt(HEAD_DIM)
    return pl.pallas_call(
        _in_proj_kernel,
        grid=(t // tm,),
        in_specs=[
            pl.BlockSpec((tm, D_MODEL), row),
            pl.BlockSpec((1, D_MODEL), const),
            pl.BlockSpec(w_in.shape, const),
            pl.BlockSpec(w_gate.shape, const),
            pl.BlockSpec((1, 2 * D_MODEL), const),
            pl.BlockSpec(cs.shape, const),
            pl.BlockSpec((1, ATT_W), const),
        ],
        out_specs=[_class_major_spec(tm, dil, ATT_W, per_batch) for _, dil in GROUPS] + [
            pl.BlockSpec((tm, F_W), row),
            pl.BlockSpec((tm, F_W), row),
            pl.BlockSpec((tm, 2 * D_MODEL), row),
        ],
        out_shape=[jax.ShapeDtypeStruct((batch, dil, seq // dil, ATT_W), BF16) for _, dil in GROUPS] + [
            jax.ShapeDtypeStruct((t, F_W), BF16),
            jax.ShapeDtypeStruct((t, F_W), BF16),
            jax.ShapeDtypeStruct((t, 2 * D_MODEL), BF16),
        ],
        scratch_shapes=[pltpu.VMEM((ATT_W // LANES, tm, LANES), F32)],
        compiler_params=_params(("parallel",)),
        name="in_proj",
    )(x, g1, w_in, w_gate, b_gate, cs, jnp.asarray(qscale))


def _attention_kernel(q_ref, kp_ref, kc_ref, kn_ref, vp_ref, vc_ref, vn_ref, bias_ref, o_ref, lse_ref, *, tq, length):
    i = pl.program_id(2)
    kwin = jnp.concatenate([kp_ref[0, 0], kc_ref[0, 0], kn_ref[0, 0]], axis=0)
    vwin = jnp.concatenate([vp_ref[0, 0], vc_ref[0, 0], vn_ref[0, 0]], axis=0)
    win = ATT_SUB + 2 * HALF_KEYS
    lane_head = lax.broadcasted_iota(I32, (ATT_SUB, GROUP_W), 1) // HEAD_DIM
    col = lax.broadcasted_iota(I32, (ATT_SUB, win), 1)
    for sb in range(tq // ATT_SUB):
        off = sb * ATT_SUB
        q = q_ref[0, 0, off:off + ATT_SUB, :]
        kw = kwin[off:off + win]
        vw = vwin[off:off + win]
        first = i * tq + (off - HALF_KEYS)
        valid = (col >= -first) & (col < length - first)
        qs = jnp.concatenate(
            [jnp.where(lane_head == h, q, jnp.zeros_like(q)) for h in range(HEADS_PER_GROUP)], axis=0)
        s_all = _dot_nt(qs, kw)
        ps, ms, ls = [], [], []
        for h in range(HEADS_PER_GROUP):
            s = s_all[h * ATT_SUB:(h + 1) * ATT_SUB] + bias_ref[h]
            s = jnp.where(valid, s, NEG)
            m = jnp.max(s, axis=-1, keepdims=True)
            p = jnp.exp(s - m)
            ls.append(jnp.sum(p, axis=-1, keepdims=True))
            ms.append(m)
            ps.append(p.astype(BF16))
        o_all = _dot(jnp.concatenate(ps, axis=0), vw)
        out = jnp.zeros((ATT_SUB, GROUP_W), F32)
        lse = jnp.zeros((ATT_SUB, GROUP_W), F32)
        for h in range(HEADS_PER_GROUP):
            oh = o_all[h * ATT_SUB:(h + 1) * ATT_SUB] * (1.0 / ls[h])
            out = jnp.where(lane_head == h, oh, out)
            lse = jnp.where(lane_head == h, ms[h] + jnp.log(ls[h]), lse)
        o_ref[0, 0, off:off + ATT_SUB, :] = out
        lse_ref[0, 0, off:off + ATT_SUB, :] = lse


def _attention(qkv, bias, g):
    batch, dil, length, _ = qkv.shape
    tq = min(length, 512)
    nb = length // tq
    hb = tq // HALF_KEYS
    last_halo = length // HALF_KEYS - 1

    def cur(c):
        return lambda b, r, i: (b, r, i, c)

    def prev(c):
        return lambda b, r, i: (b, r, jnp.maximum(i * hb - 1, 0), c)

    def nxt(c):
        return lambda b, r, i: (b, r, jnp.minimum((i + 1) * hb, last_halo), c)

    blk = lambda rows: (1, 1, rows, GROUP_W)
    return pl.pallas_call(
        functools.partial(_attention_kernel, tq=tq, length=length),
        grid=(batch, dil, nb),
        in_specs=[
            pl.BlockSpec(blk(tq), cur(0)),
            pl.BlockSpec(blk(HALF_KEYS), prev(1)),
            pl.BlockSpec(blk(tq), cur(1)),
            pl.BlockSpec(blk(HALF_KEYS), nxt(1)),
            pl.BlockSpec(blk(HALF_KEYS), prev(2)),
            pl.BlockSpec(blk(tq), cur(2)),
            pl.BlockSpec(blk(HALF_KEYS), nxt(2)),
            pl.BlockSpec(bias.shape, lambda b, r, i: (0, 0, 0)),
        ],
        out_specs=[pl.BlockSpec(blk(tq), cur(0))] * 2,
        out_shape=[jax.ShapeDtypeStruct((batch, dil, length, GROUP_W), F32)] * 2,
        compiler_params=_params(("parallel", "parallel", "parallel")),
        name=f"attention_g{g}",
    )(qkv, qkv, qkv, qkv, qkv, qkv, qkv, bias)


def _t5_bucket(rel):
    nb = NUM_BUCKETS // 2
    max_exact = nb // 2
    ret = (rel > 0).astype(np.int32) * nb
    n = np.abs(rel)
    large = max_exact + (np.log(np.maximum(n, max_exact) / max_exact)
                         / np.log(MAX_DISTANCE / max_exact) * (nb - max_exact)).astype(np.int32)
    large = np.minimum(large, nb - 1)
    return (ret + np.where(n < max_exact, n, large)).astype(np.int32)


def _attention_bias(rel_bias, g):
    dil = GROUPS[g][1]
    qi = np.arange(ATT_SUB)[:, None]
    kj = np.arange(ATT_SUB + 2 * HALF_KEYS)[None, :]
    delta = kj - HALF_KEYS - qi
    band = np.abs(delta) <= HALF_KEYS
    bucket = _t5_bucket(dil * delta)
    tab = rel_bias[:, g * HEADS_PER_GROUP:(g + 1) * HEADS_PER_GROUP].astype(F32)
    onehot = jnp.asarray(bucket[..., None] == np.arange(NUM_BUCKETS), F32)
    bias = jnp.einsum("qkb,bh->hqk", onehot, tab, precision=lax.Precision.HIGHEST)
    return jnp.where(jnp.asarray(band)[None], bias, NEG)


def _dft_mats(n):
    k = np.arange(n)
    ang = 2.0 * np.pi * ((k[:, None] * k[None, :]) % n) / n
    return np.cos(ang), np.sin(ang)


def _fft_stage1_kernel(vr_ref, vi_ref, m1_ref, twc_ref, tws_ref, zr_ref, zi_ref, *, n1, m):
    x = jnp.concatenate([vr_ref[0], vi_ref[0]], axis=0)
    z = _dot(m1_ref[...], x)
    zr, zi = z[:n1], z[n1:]
    twc, tws = twc_ref[0], tws_ref[0]
    for j in range(m):
        c = twc[:, j:j + 1]
        s = tws[:, j:j + 1]
        a = zr[:, j * F_W:(j + 1) * F_W]
        b = zi[:, j * F_W:(j + 1) * F_W]
        zr_ref[0, :, j * F_W:(j + 1) * F_W] = (a * c + b * s).astype(BF16)
        zi_ref[0, :, j * F_W:(j + 1) * F_W] = (b * c - a * s).astype(BF16)


def _fft_stage2_kernel(zr_ref, zi_ref, m2_ref, o_ref, *, kc, scale):
    m2 = m2_ref[...]
    for j in range(kc):
        x = jnp.concatenate([zr_ref[0, j], zi_ref[0, j]], axis=0)
        o_ref[0, :, j * F_W:(j + 1) * F_W] = (_dot(m2, x) * scale).astype(BF16)


def _fourier(vr, vi, batch, seq):
    n2 = LANES
    n1 = seq // n2
    m = 8
    c1, s1 = _dft_mats(n1)
    m1 = jnp.asarray(np.block([[c1, s1], [-s1, c1]]), BF16)
    c2, s2 = _dft_mats(n2)
    m2 = jnp.asarray(np.concatenate([c2, s2], axis=1), BF16)
    k1 = np.arange(n1)[:, None]
    sv = np.arange(n2)[None, :]
    ang = 2.0 * np.pi * ((k1 * sv) % seq) / seq
    to_blocks = lambda a: jnp.asarray(a.reshape(n1, n2 // m, m).transpose(1, 0, 2), F32)
    twc, tws = to_blocks(np.cos(ang)), to_blocks(np.sin(ang))

    v3 = lambda a: a.reshape(batch, n1, n2 * F_W)
    blk = (1, n1, m * F_W)
    dmap = lambda b, j: (b, 0, j)
    tmap = lambda b, j: (j, 0, 0)
    zr, zi = pl.pallas_call(
        functools.partial(_fft_stage1_kernel, n1=n1, m=m),
        grid=(batch, n2 // m),
        in_specs=[
            pl.BlockSpec(blk, dmap),
            pl.BlockSpec(blk, dmap),
            pl.BlockSpec(m1.shape, lambda b, j: (0, 0)),
            pl.BlockSpec((1, n1, m), tmap),
            pl.BlockSpec((1, n1, m), tmap),
        ],
        out_specs=[pl.BlockSpec(blk, dmap), pl.BlockSpec(blk, dmap)],
        out_shape=[jax.ShapeDtypeStruct((batch, n1, n2 * F_W), BF16)] * 2,
        compiler_params=_params(("parallel", "parallel")),
        name="fft_stage1",
    )(v3(vr), v3(vi), m1, twc, tws)

    kc = 8
    v4 = lambda a: a.reshape(batch, n1, n2, F_W)
    zblk = (1, kc, n2, F_W)
    zmap = lambda b, j: (b, j, 0, 0)
    out = pl.pallas_call(
        functools.partial(_fft_stage2_kernel, kc=kc, scale=1.0 / math.sqrt(seq * F_CH)),
        grid=(batch, n1 // kc),
        in_specs=[
            pl.BlockSpec(zblk, zmap),
            pl.BlockSpec(zblk, zmap),
            pl.BlockSpec(m2.shape, lambda b, j: (0, 0)),
        ],
        out_specs=pl.BlockSpec((1, n2, kc * F_W), lambda b, j: (b, 0, j)),
        out_shape=jax.ShapeDtypeStruct((batch, n2, n1 * F_W), BF16),
        compiler_params=_params(("parallel", "parallel")),
        name="fft_stage2",
    )(v4(zr), v4(zi), m2)
    return out.reshape(batch * seq, F_W)


def _mix_kernel(x_ref, o0_ref, o1_ref, o2_ref, l0_ref, l1_ref, l2_ref, four_ref, gates_ref,
                wa_ref, wf_ref, wo_ref, g2_ref, wrh_ref, wrl_ref, x1_ref, xn_ref, aff_ref, afft_ref, slab_ref):
    tm = x_ref.shape[0]

    def token_order(ref, dil):
        if dil == 1:
            return ref[0, 0]
        rows = tm // dil
        for r in range(dil):
            v = ref[0, r]
            for j in range(GROUP_W // LANES):
                slab_ref[j, pl.ds(r, rows, stride=dil), :] = v[:, j * LANES:(j + 1) * LANES]
        return jnp.concatenate([slab_ref[j] for j in range(GROUP_W // LANES)], axis=1)

    dils = [dil for _, dil in GROUPS]
    l0, l1, l2 = (token_order(r, d) for r, d in zip((l0_ref, l1_ref, l2_ref), dils))
    mx = jnp.maximum(jnp.maximum(l0, l1), l2)
    e0, e1, e2 = jnp.exp(l0 - mx), jnp.exp(l1 - mx), jnp.exp(l2 - mx)
    o0, o1, o2 = (token_order(r, d) for r, d in zip((o0_ref, o1_ref, o2_ref), dils))
    att = (e0 * o0 + e1 * o1 + e2 * o2) * (1.0 / (e0 + e1 + e2))
    a_br = _dot(att.astype(BF16), wa_ref[...])
    f_br = _dot(four_ref[...], wf_ref[...])
    mix = gates_ref[:, :D_MODEL] * a_br + gates_ref[:, D_MODEL:] * f_br
    x1 = x_ref[...] + _dot(mix.astype(BF16), wo_ref[...])
    x1_ref[...] = x1
    ms = jnp.mean(x1 * x1, axis=-1, keepdims=True)
    xn = x1 * lax.rsqrt(ms + EPS) * g2_ref[...]
    xn_ref[...] = xn.astype(BF16)
    xh = xn.astype(BF16)
    xl = (xn - xh.astype(F32)).astype(BF16)
    logits = _dot(xh, wrh_ref[...]) + (_dot(xh, wrl_ref[...]) + _dot(xl, wrh_ref[...]))
    lane = lax.broadcasted_iota(I32, logits.shape, 1)
    logits = jnp.where(lane < N_EXPERTS, logits, NEG)
    p = jnp.exp(logits - jnp.max(logits, axis=-1, keepdims=True))
    aff = p * (1.0 / jnp.sum(p, axis=-1, keepdims=True))
    aff_ref[...] = aff
    afft_ref[...] = aff.T[:N_EXPERTS]


def _mix(x, os_, ls_, four, gates, w_attn, w_four, w_out, g2, w_router_hi, w_router_lo, seq):
    t = x.shape[0]
    tm = TOKEN_TILE
    per_batch = seq // tm
    const = lambda i: (0, 0)
    row = lambda i: (i, 0)
    rows = lambda w: pl.BlockSpec((tm, w), row)
    full = lambda a: pl.BlockSpec(a.shape, const)
    classes = [_class_major_spec(tm, dil, GROUP_W, per_batch) for _, dil in GROUPS]
    return pl.pallas_call(
        _mix_kernel,
        grid=(t // tm,),
        in_specs=[rows(D_MODEL)] + classes * 2 + [rows(F_W), rows(2 * D_MODEL),
                  full(w_attn), full(w_four), full(w_out), full(g2), full(w_router_hi), full(w_router_lo)],
        out_specs=[rows(D_MODEL), rows(D_MODEL), rows(LANES), pl.BlockSpec((N_EXPERTS, tm), lambda i: (0, i))],
        out_shape=[
            jax.ShapeDtypeStruct((t, D_MODEL), F32),
            jax.ShapeDtypeStruct((t, D_MODEL), BF16),
            jax.ShapeDtypeStruct((t, LANES), F32),
            jax.ShapeDtypeStruct((N_EXPERTS, t), F32),
        ],
        scratch_shapes=[pltpu.VMEM((GROUP_W // LANES, tm, LANES), F32)],
        compiler_params=_params(("parallel",)),
        name="mix",
    )(x, *os_, *ls_, four, gates, w_attn, w_four, w_out, g2, w_router_hi, w_router_lo)


def _route_kernel(afft_ref, su_ref, u_ref, tau_ref, need_ref, beq_ref, bsel_ref,
                  taut_ref, needt_ref, beqt_ref, bselt_ref, *, tokens):
    cap = CAPACITY_FACTOR * tokens // N_EXPERTS
    ntile = tokens // ROUTE_TILE
    shape = (N_EXPERTS, LANES)
    lane = lax.broadcasted_iota(I32, shape, 1)

    def keys(start, width):
        return lax.bitcast_convert_type(afft_ref[:, pl.ds(pl.multiple_of(start, LANES), width)], I32)

    span = min(tokens, 16 * LANES)

    def count(pred):
        def body(c, acc):
            hits = _ones_where(pred(keys(c * span, span)))
            for j in range(span // LANES):
                acc = acc + hits[:, j * LANES:(j + 1) * LANES]
            return acc
        acc = lax.fori_loop(0, tokens // span, body, jnp.zeros(shape, F32))
        return jnp.sum(acc, axis=1, keepdims=True)

    def bit_body(i, prefix):
        cand = prefix | lax.shift_left(jnp.ones(shape, I32), jnp.full(shape, 30 - i, I32))
        tot = count(lambda k: k >= cand[:, :1])
        return jnp.where(tot >= cap, cand, prefix)

    tau = lax.fori_loop(0, 31, bit_body, jnp.zeros(shape, I32))
    tau_col = tau[:, :1]
    n_gt = count(lambda k: k > tau_col)
    need = cap - n_gt

    def prefix_over_tiles(tab):
        return _dot(tab.astype(BF16), su_ref[...])

    def at_lane(tab, c):
        return jnp.sum(jnp.where(lane == c, tab, 0.0), axis=1, keepdims=True)

    def eq_body(c, tab):
        k = keys(c * ROUTE_TILE, ROUTE_TILE)
        cnt = jnp.sum(_ones_where(k == tau_col), axis=1, keepdims=True)
        return jnp.where(lane == c, cnt, tab)

    base_eq = prefix_over_tiles(lax.fori_loop(0, ntile, eq_body, jnp.zeros(shape, F32)))

    def sel_body(c, tab):
        k = keys(c * ROUTE_TILE, ROUTE_TILE)
        eq = k == tau_col
        eq_cum = _dot(_ones_where(eq, BF16), u_ref[...]) + at_lane(base_eq, c)
        sel = (k > tau_col) | (eq & (eq_cum <= need))
        cnt = jnp.sum(_ones_where(sel), axis=1, keepdims=True)
        return jnp.where(lane == c, cnt, tab)

    base_sel = prefix_over_tiles(lax.fori_loop(0, ntile, sel_body, jnp.zeros(shape, F32)))

    def transposed(val):
        return jnp.concatenate([val, jnp.zeros((LANES - N_EXPERTS, LANES), val.dtype)], axis=0).T

    tau_ref[...] = tau
    taut_ref[...] = transposed(tau)
    for val, ref, ref_t in ((jnp.broadcast_to(need, shape), need_ref, needt_ref),
                            (base_eq, beq_ref, beqt_ref), (base_sel, bsel_ref, bselt_ref)):
        ref[...] = val.astype(I32)
        ref_t[...] = transposed(val)


def _route(afft):
    tokens = afft.shape[1]
    idx = np.arange(LANES)
    su = jnp.asarray(idx[:, None] < idx[None, :], BF16)
    idx = np.arange(ROUTE_TILE)
    u = jnp.asarray(idx[:, None] <= idx[None, :], BF16)
    full = lambda a: pl.BlockSpec(a.shape, lambda i: (0,) * a.ndim)
    small = pl.BlockSpec((N_EXPERTS, LANES), lambda i: (0, 0))
    smallt = pl.BlockSpec((LANES, LANES), lambda i: (0, 0))
    return pl.pallas_call(
        functools.partial(_route_kernel, tokens=tokens),
        grid=(1,),
        in_specs=[full(afft), full(su), full(u)],
        out_specs=[small] * 4 + [smallt] * 4,
        out_shape=[jax.ShapeDtypeStruct((N_EXPERTS, LANES), I32)] * 4
        + [jax.ShapeDtypeStruct((LANES, LANES), I32)] + [jax.ShapeDtypeStruct((LANES, LANES), F32)] * 3,
        compiler_params=_params(("arbitrary",)),
        name="route",
    )(afft, su, u)


def _gather_kernel(bsel_s, afft_ref, tau_ref, need_ref, beq_ref, x_ref, u_ref, xe_hbm,
                   stage_ref, tail_ref, xbuf_ref, zeros_ref, sem_ref, xsem_ref, *, ntile, cap):
    t = pl.program_id(0)
    par = t % 2

    def aligned(e, tile):
        return pl.multiple_of((bsel_s[e, tile] // ROW_ALIGN) * ROW_ALIGN, ROW_ALIGN)

    def writes(tile, buf):
        return [pltpu.make_async_copy(stage_ref.at[buf, e], xe_hbm.at[e, pl.ds(aligned(e, tile), GATHER_BLOCK)],
                                      sem_ref.at[buf, e]) for e in range(N_EXPERTS)]

    @pl.when(t == 0)
    def _():
        tail_ref[...] = jnp.zeros_like(tail_ref)
        zeros_ref[...] = jnp.zeros_like(zeros_ref)
        fills = [pltpu.make_async_copy(zeros_ref, xe_hbm.at[e, pl.ds(cap, GATHER_PAD)], sem_ref.at[0, e])
                 for e in range(N_EXPERTS)]
        for cp in fills:
            cp.start()
        for cp in fills:
            cp.wait()

    k = lax.bitcast_convert_type(afft_ref[...], I32)
    tau = tau_ref[:, :1]
    lane = lax.broadcasted_iota(I32, (N_EXPERTS, LANES), 1)
    beq = jnp.sum(jnp.where(lane == t, beq_ref[...].astype(F32), 0.0), axis=1, keepdims=True)
    eq = k == tau
    eq_cum = _dot(_ones_where(eq, BF16), u_ref[...]) + beq
    sel = (k > tau) | (eq & (eq_cum <= need_ref[:, :1].astype(F32)))
    rank = jnp.where(sel, _dot(_ones_where(sel, BF16), u_ref[...]) - 1.0, -1e4)

    row = lax.broadcasted_iota(I32, (GATHER_STACK, ROUTE_TILE), 0)
    in_block = row < GATHER_BLOCK
    row_f = row.astype(F32)
    offs, shifts, pieces = [], [], []
    for e in range(N_EXPERTS):
        off = (bsel_s[e, t] - aligned(e, t)).astype(F32)
        shift = ((bsel_s[e, t + 1] // ROW_ALIGN) * ROW_ALIGN - aligned(e, t))
        target = jnp.where(in_block, row_f, row_f - float(GATHER_BLOCK) + shift.astype(F32))
        pieces.append(_ones_where(rank[e:e + 1, :] + off == target, BF16))
        offs.append(off)
        shifts.append(shift)
    res = _dot(jnp.concatenate(pieces, axis=0), x_ref[...])
    for e in range(N_EXPERTS):
        base = e * GATHER_STACK
        old = tail_ref[e]
        stage_ref[par, e, 0:ROW_ALIGN, :] = (res[base:base + ROW_ALIGN] + old).astype(BF16)
        stage_ref[par, e, ROW_ALIGN:GATHER_BLOCK, :] = res[base + ROW_ALIGN:base + GATHER_BLOCK].astype(BF16)
        tail_ref[e] = res[base + GATHER_BLOCK:base + GATHER_STACK] + jnp.where(shifts[e] == 0, old, 0.0)

    @pl.when(t > 0)
    def _():
        for cp in writes(t - 1, 1 - par):
            cp.wait()

    for cp in writes(t, par):
        cp.start()

    extra = [(jnp.maximum(bsel_s[e, t + 1] - aligned(e, t) - GATHER_BLOCK, 0) + SLOT_CHUNK - 1) // SLOT_CHUNK
             for e in range(N_EXPERTS)]

    @pl.when(functools.reduce(jnp.maximum, extra) > 0)
    def _():
        row64 = lax.broadcasted_iota(I32, (SLOT_CHUNK, ROUTE_TILE), 0).astype(F32)
        for e in range(N_EXPERTS):
            def chunk(c, carry):
                first = GATHER_BLOCK + c * SLOT_CHUNK
                onehot = _ones_where(rank[e:e + 1, :] + offs[e] == row64 + first.astype(F32), BF16)
                xbuf_ref[...] = _dot(onehot, x_ref[...]).astype(BF16)
                dst = pl.multiple_of(aligned(e, t) + first, ROW_ALIGN)
                cp = pltpu.make_async_copy(xbuf_ref, xe_hbm.at[e, pl.ds(dst, SLOT_CHUNK)], xsem_ref.at[0])
                cp.start()
                cp.wait()
                return carry

            lax.fori_loop(0, extra[e], chunk, 0)

    @pl.when(t == ntile - 1)
    def _():
        for cp in writes(t, par):
            cp.wait()


def _gather(bsel_i, afft, tau, need, beq_i, xn, u):
    tokens = xn.shape[0]
    cap = CAPACITY_FACTOR * tokens // N_EXPERTS
    ntile = tokens // ROUTE_TILE
    table = pl.BlockSpec((N_EXPERTS, LANES), lambda t, *_: (0, 0))
    grid_spec = pltpu.PrefetchScalarGridSpec(
        num_scalar_prefetch=1,
        grid=(ntile,),
        in_specs=[
            pl.BlockSpec((N_EXPERTS, ROUTE_TILE), lambda t, *_: (0, t)),
            table, table, table,
            pl.BlockSpec((ROUTE_TILE, D_MODEL), lambda t, *_: (t, 0)),
            pl.BlockSpec(u.shape, lambda t, *_: (0, 0)),
        ],
        out_specs=pl.BlockSpec(memory_space=pl.ANY),
        scratch_shapes=[
            pltpu.VMEM((2, N_EXPERTS, GATHER_BLOCK, D_MODEL), BF16),
            pltpu.VMEM((N_EXPERTS, ROW_ALIGN, D_MODEL), F32),
            pltpu.VMEM((SLOT_CHUNK, D_MODEL), BF16),
            pltpu.VMEM((GATHER_PAD, D_MODEL), BF16),
            pltpu.SemaphoreType.DMA((2, N_EXPERTS)),
            pltpu.SemaphoreType.DMA((1,)),
        ],
    )
    return pl.pallas_call(
        functools.partial(_gather_kernel, ntile=ntile, cap=cap),
        grid_spec=grid_spec,
        out_shape=jax.ShapeDtypeStruct((N_EXPERTS, cap + GATHER_PAD, D_MODEL), BF16),
        compiler_params=_params(("arbitrary",)),
        name="gather",
    )(bsel_i, afft, tau, need, beq_i, xn, u)


def _ffn_kernel(xe_ref, wg_ref, wu_ref, wd_ref, ye_ref, acc_ref, *, cap, nf, tm):
    f = pl.program_id(1)
    wg, wu, wd = wg_ref[0].astype(BF16), wu_ref[0].astype(BF16), wd_ref[0].astype(BF16)

    @pl.when(f == 0)
    def _():
        acc_ref[...] = jnp.zeros_like(acc_ref)

    for i in range(cap // tm):
        r = slice(i * tm, (i + 1) * tm)
        x = xe_ref[0, r, :]
        hg = _dot(x, wg)
        hu = _dot(x, wu)
        h = (hg * (1.0 / (1.0 + jnp.exp(-hg))) * hu).astype(BF16)
        acc_ref[r, :] += _dot(h, wd)

    @pl.when(f == nf - 1)
    def _():
        ye_ref[...] = acc_ref[...].astype(BF16)


def _ffn(xe, w_eg, w_eu, w_ed):
    cap = xe.shape[1] - GATHER_PAD
    tf = 512
    nf = D_FF // tf
    tm = min(cap, 1024)
    return pl.pallas_call(
        functools.partial(_ffn_kernel, cap=cap, nf=nf, tm=tm),
        grid=(N_EXPERTS, nf),
        in_specs=[
            pl.BlockSpec((1, cap, D_MODEL), lambda e, f: (e, 0, 0)),
            pl.BlockSpec((1, D_MODEL, tf), lambda e, f: (e, 0, f)),
            pl.BlockSpec((1, D_MODEL, tf), lambda e, f: (e, 0, f)),
            pl.BlockSpec((1, tf, D_MODEL), lambda e, f: (e, f, 0)),
        ],
        out_specs=pl.BlockSpec((cap, D_MODEL), lambda e, f: (e, 0)),
        out_shape=jax.ShapeDtypeStruct((N_EXPERTS * cap, D_MODEL), BF16),
        scratch_shapes=[pltpu.VMEM((cap, D_MODEL), F32)],
        compiler_params=_params(("arbitrary", "arbitrary")),
        name="ffn",
    )(xe, w_eg, w_eu, w_ed)


def _combine_kernel(bsel_s, x1_ref, aff_ref, taut_ref, needt_ref, beqt_ref, bselt_ref, low_ref, spread_ref, gf_ref,
                    ye_hbm, y_ref, buf_ref, xbuf_ref, sem_ref, xsem_ref, *, cap, total, ntile):
    t = pl.program_id(0)
    par = t % 2

    def aligned(e, tile):
        return (bsel_s[e, tile] // 16) * 16

    def window(e, tile, c):
        start = jnp.minimum(e * cap + aligned(e, tile) + c * SLOT_CHUNK, total - SLOT_CHUNK)
        return pl.multiple_of(start, 16)

    def first_chunks(tile, buf):
        return [pltpu.make_async_copy(ye_hbm.at[pl.ds(window(e, tile, 0), SLOT_CHUNK)],
                                      buf_ref.at[buf, pl.ds(e * SLOT_CHUNK, SLOT_CHUNK)], sem_ref.at[buf, e])
                for e in range(N_EXPERTS)]

    @pl.when(t == 0)
    def _():
        for cp in first_chunks(0, 0):
            cp.start()

    @pl.when(t + 1 < ntile)
    def _():
        for cp in first_chunks(t + 1, 1 - par):
            cp.start()

    aff = aff_ref[...]
    k = lax.bitcast_convert_type(aff, I32)
    tau = taut_ref[0:1, :]
    low = low_ref[...]
    eq = k == tau
    eq_cum = _dot(low, _ones_where(eq, BF16)) + beqt_ref[0]
    sel = (k > tau) | (eq & (eq_cum <= needt_ref[0:1, :]))
    slot = jnp.where(sel, _dot(low, _ones_where(sel, BF16)) + (bselt_ref[0] - 1.0), -1.0)

    lane = lax.broadcasted_iota(I32, (1, LANES), 1)
    rel = jnp.zeros((1, LANES), F32)
    for e in range(N_EXPERTS):
        rel = jnp.where(lane == e, (window(e, t, 0) - e * cap).astype(F32), rel)
    d = slot - rel
    d = jnp.where(sel & (d >= 0.0) & (d < float(SLOT_CHUNK)), d, -1.0)
    g_hi = aff.astype(BF16)
    g_lo = (aff - g_hi.astype(F32)).astype(BF16)
    spread = spread_ref[...]
    wide = lax.broadcasted_iota(I32, (ROUTE_TILE, N_EXPERTS * SLOT_CHUNK), 1)
    hit = _dot(d.astype(BF16), spread) == (wide % SLOT_CHUNK).astype(F32)
    a_hi = jnp.where(hit, _dot(g_hi, spread), 0.0).astype(BF16)
    a_lo = jnp.where(hit, _dot(g_lo, spread), 0.0).astype(BF16)
    for cp in first_chunks(t, par):
        cp.wait()
    rows = buf_ref[par]
    y_ref[...] = x1_ref[...] + _dot(a_hi, rows) + _dot(a_lo, rows)

    nch = [(bsel_s[e, t + 1] - aligned(e, t) + SLOT_CHUNK - 1) // SLOT_CHUNK for e in range(N_EXPERTS)]

    @pl.when(functools.reduce(jnp.maximum, nch) > 1)
    def _():
        lane64 = lax.broadcasted_iota(I32, (ROUTE_TILE, SLOT_CHUNK), 1).astype(F32)
        for e in range(N_EXPERTS):
            slot_e = slot[:, e:e + 1]

            def extra(c, carry):
                w = window(e, t, c)
                cp = pltpu.make_async_copy(ye_hbm.at[pl.ds(w, SLOT_CHUNK)], xbuf_ref, xsem_ref.at[0])
                cp.start()
                cp.wait()
                first = (aligned(e, t) + c * SLOT_CHUNK).astype(F32)
                hit = (lane64 + (w - e * cap).astype(F32) == slot_e) & (slot_e >= first)
                y_ref[...] += aff[:, e:e + 1] * _dot(_ones_where(hit, BF16), xbuf_ref[...])
                return carry

            lax.fori_loop(1, nch[e], extra, 0)

    acc = y_ref[...]
    ms = jnp.mean(acc * acc, axis=-1, keepdims=True)
    y_ref[...] = acc * lax.rsqrt(ms + EPS) * gf_ref[...]


def _combine(bsel_i, x1, aff, tables_t, ye, gf):
    tokens = x1.shape[0]
    cap = CAPACITY_FACTOR * tokens // N_EXPERTS
    ntile = tokens // ROUTE_TILE
    idx = np.arange(ROUTE_TILE)
    low = jnp.asarray(idx[:, None] >= idx[None, :], BF16)
    spread = jnp.asarray(np.arange(LANES)[:, None] == np.arange(N_EXPERTS * SLOT_CHUNK)[None, :] // SLOT_CHUNK, BF16)
    taut, needt, beqt, bselt = tables_t
    rowvec = pl.BlockSpec((8, LANES), lambda t, *_: (0, 0))
    tilevec = pl.BlockSpec((1, 1, LANES), lambda t, *_: (t, 0, 0))
    grid_spec = pltpu.PrefetchScalarGridSpec(
        num_scalar_prefetch=1,
        grid=(ntile,),
        in_specs=[
            pl.BlockSpec((ROUTE_TILE, D_MODEL), lambda t, *_: (t, 0)),
            pl.BlockSpec((ROUTE_TILE, LANES), lambda t, *_: (t, 0)),
            rowvec, rowvec, tilevec, tilevec,
            pl.BlockSpec(low.shape, lambda t, *_: (0, 0)),
            pl.BlockSpec(spread.shape, lambda t, *_: (0, 0)),
            pl.BlockSpec((1, D_MODEL), lambda t, *_: (0, 0)),
            pl.BlockSpec(memory_space=pl.ANY),
        ],
        out_specs=pl.BlockSpec((ROUTE_TILE, D_MODEL), lambda t, *_: (t, 0)),
        scratch_shapes=[
            pltpu.VMEM((2, N_EXPERTS * SLOT_CHUNK, D_MODEL), BF16),
            pltpu.VMEM((SLOT_CHUNK, D_MODEL), BF16),
            pltpu.SemaphoreType.DMA((2, N_EXPERTS)),
            pltpu.SemaphoreType.DMA((1,)),
        ],
    )
    return pl.pallas_call(
        functools.partial(_combine_kernel, cap=cap, total=N_EXPERTS * cap, ntile=ntile),
        grid_spec=grid_spec,
        out_shape=jax.ShapeDtypeStruct((tokens, D_MODEL), F32),
        compiler_params=_params(("arbitrary",)),
        name="combine",
    )(bsel_i, x1, aff, taut, needt, beqt.reshape(LANES, 1, LANES), bselt.reshape(LANES, 1, LANES), low, spread,
      gf, ye)


def _encoder(x, w):
    batch, seq, _ = x.shape
    tokens = batch * seq
    xt = x.reshape(tokens, D_MODEL)
    *qkvs, vr, vi, gates = _in_proj(xt, w["g1"], w["w_in"], w["w_gate"], w["b_gate"], w["cs"], batch, seq)
    outs, lses = [], []
    for g in range(N_GROUPS):
        o, lse = _attention(qkvs[g], w["bias"][g], g)
        outs.append(o)
        lses.append(lse)
    four = _fourier(vr, vi, batch, seq)
    x1, xn, aff, afft = _mix(xt, outs, lses, four, gates, w["w_attn"], w["w_four"], w["w_out"], w["g2"],
                             w["w_router_hi"], w["w_router_lo"], seq)
    tau, need, beq_i, bsel_i, taut, needt, beqt, bselt = _route(afft)
    idx = np.arange(ROUTE_TILE)
    u = jnp.asarray(idx[:, None] <= idx[None, :], BF16)
    xe = _gather(bsel_i, afft, tau, need, beq_i, xn, u)
    ye = _ffn(xe, w["w_eg"], w["w_eu"], w["w_ed"])
    y = _combine(bsel_i, x1, aff, (taut, needt, beqt, bselt), ye, w["gf"])
    return y.reshape(batch, seq, D_MODEL)


def _prepare_weights(rel_bias, norm1_g, w_in, w_attn_br, w_four_br, w_gate, b_gate, w_out,
                     norm2_g, w_router, w_exp_gate, w_exp_up, w_exp_down, final_g):
    c, s = _dft_mats(F_CH)
    starts = [part * ATT_W + g * GROUP_W for g in range(N_GROUPS) for part in range(3)]
    w_in_grouped = jnp.concatenate([w_in[0][:, s0:s0 + GROUP_W] for s0 in starts] + [w_in[0][:, QKV_W:]], axis=1)
    w_router = jnp.pad(w_router[0], ((0, 0), (0, LANES - N_EXPERTS)))
    w_router_hi = w_router.astype(BF16)
    return {
        "g1": norm1_g[0].reshape(1, D_MODEL),
        "w_in": w_in_grouped.astype(BF16),
        "w_gate": w_gate[0].astype(BF16),
        "b_gate": b_gate[0].reshape(1, 2 * D_MODEL),
        "cs": jnp.asarray(np.concatenate([c, s], axis=1), BF16),
        "bias": [_attention_bias(rel_bias, g) for g in range(N_GROUPS)],
        "w_attn": w_attn_br[0].astype(BF16),
        "w_four": w_four_br[0].astype(BF16),
        "w_out": w_out[0].astype(BF16),
        "g2": norm2_g[0].reshape(1, D_MODEL),
        "w_router_hi": w_router_hi,
        "w_router_lo": (w_router - w_router_hi.astype(F32)).astype(BF16),
        "w_eg": w_exp_gate[0],
        "w_eu": w_exp_up[0],
        "w_ed": w_exp_down[0],
        "gf": final_g.reshape(1, D_MODEL),
    }


def kernel(x_prompt, x_sample, rel_bias, norm1_g, w_in, w_attn_br, w_four_br, w_gate, b_gate, w_out,
           norm2_g, w_router, w_exp_gate, w_exp_up, w_exp_down, final_g):
    w = _prepare_weights(rel_bias, norm1_g, w_in, w_attn_br, w_four_br, w_gate, b_gate, w_out,
                         norm2_g, w_router, w_exp_gate, w_exp_up, w_exp_down, final_g)
    return (_encoder(x_prompt, w), _encoder(x_sample, w))
```

```python
import functools
import math

import numpy as np
import jax
import jax.numpy as jnp
from jax import lax
from jax.experimental import pallas as pl
from jax.experimental.pallas import tpu as pltpu

D_MODEL = 1024
HEAD_DIM = 64
HEADS_PER_GROUP = 4
GROUPS = ((128, 1), (512, 4), (2048, 16))
N_GROUPS = len(GROUPS)
GROUP_W = HEADS_PER_GROUP * HEAD_DIM
ATT_W = N_GROUPS * GROUP_W
QKV_W = 3 * ATT_W
F_GROUPS = 6
F_CH = 128
F_W = F_GROUPS * F_CH
NUM_BUCKETS = 32
MAX_DISTANCE = 1024
N_EXPERTS = 16
CAPACITY_FACTOR = 2
D_FF = 2048
EPS = 1e-6
NEG = -1e30

HALF_KEYS = 64
ATT_SUB = 128
TOKEN_TILE = 512
ROUTE_TILE = 256
FFN_CHUNK = 256
SLOT_CHUNK = 64
ROW_ALIGN = 16
GATHER_BLOCK = SLOT_CHUNK + ROW_ALIGN
GATHER_STACK = GATHER_BLOCK + ROW_ALIGN
GATHER_PAD = 128
LANES = 128
V7X_VMEM_LIMIT = 56 * 1024 * 1024

F32 = jnp.float32
BF16 = jnp.bfloat16
I32 = jnp.int32


def _params(sem):
    return pltpu.CompilerParams(dimension_semantics=sem, vmem_limit_bytes=V7X_VMEM_LIMIT)


def _dot(a, b):
    return jnp.dot(a, b, preferred_element_type=F32)


def _dot_nt(a, b):
    return lax.dot_general(a, b, (((1,), (1,)), ((), ())), preferred_element_type=F32)


def _ones_where(mask, dtype=F32):
    return jnp.where(mask, jnp.ones((), F32), jnp.zeros((), F32)).astype(dtype)


def _in_proj_kernel(x_ref, g_ref, win_ref, wg_ref, bg_ref, cs_ref, qscale_ref, qkv0_ref, qkv1_ref, qkv2_ref,
                    vr_ref, vi_ref, gates_ref, slab_ref):
    x = x_ref[...]
    tm = x.shape[0]
    ms = jnp.mean(x * x, axis=-1, keepdims=True)
    xn = (x * lax.rsqrt(ms + EPS) * g_ref[...]).astype(BF16)
    nslab = ATT_W // LANES
    for g, out_ref in enumerate((qkv0_ref, qkv1_ref, qkv2_ref)):
        dil = GROUPS[g][1]
        res = _dot(xn, win_ref[:, g * ATT_W:(g + 1) * ATT_W]) * qscale_ref[...]
        if dil == 1:
            out_ref[0, 0] = res.astype(BF16)
            continue
        for j in range(nslab):
            slab_ref[j] = res[:, j * LANES:(j + 1) * LANES]
        rows = tm // dil
        for r in range(dil):
            cls = [slab_ref[j, pl.ds(r, rows, stride=dil), :] for j in range(nslab)]
            out_ref[0, r] = jnp.concatenate(cls, axis=1).astype(BF16)
    u = _dot(xn, win_ref[:, QKV_W:QKV_W + F_W]).astype(BF16)
    cs = cs_ref[...]
    for g in range(F_GROUPS):
        a = _dot(u[:, g * F_CH:(g + 1) * F_CH], cs)
        vr_ref[:, g * F_CH:(g + 1) * F_CH] = a[:, :F_CH].astype(BF16)
        vi_ref[:, g * F_CH:(g + 1) * F_CH] = (-a[:, F_CH:]).astype(BF16)
    z = _dot(xn, wg_ref[...]) + bg_ref[...]
    gates_ref[...] = (1.0 / (1.0 + jnp.exp(-z))).astype(BF16)


def _class_major_spec(tm, dil, width, per_batch):
    return pl.BlockSpec((1, dil, tm // dil, width), lambda i: (i // per_batch, 0, i % per_batch, 0))


def _in_proj(x, g1, w_in, w_gate, b_gate, cs, batch, seq):
    t = x.shape[0]
    tm = TOKEN_TILE
    per_batch = seq // tm
    const = lambda i: (0, 0)
    row = lambda i: (i, 0)
    qscale = np.ones((1, ATT_W), np.float32)
    qscale[:, :GROUP_W] = 1.0 / math.sqrt(HEAD_DIM)
    return pl.pallas_call(
        _in_proj_kernel,
        grid=(t // tm,),
        in_specs=[
            pl.BlockSpec((tm, D_MODEL), row),
            pl.BlockSpec((1, D_MODEL), const),
            pl.BlockSpec(w_in.shape, const),
            pl.BlockSpec(w_gate.shape, const),
            pl.BlockSpec((1, 2 * D_MODEL), const),
            pl.BlockSpec(cs.shape, const),
            pl.BlockSpec((1, ATT_W), const),
        ],
        out_specs=[_class_major_spec(tm, dil, ATT_W, per_batch) for _, dil in GROUPS] + [
            pl.BlockSpec((tm, F_W), row),
            pl.BlockSpec((tm, F_W), row),
            pl.BlockSpec((tm, 2 * D_MODEL), row),
        ],
        out_shape=[jax.ShapeDtypeStruct((batch, dil, seq // dil, ATT_W), BF16) for _, dil in GROUPS] + [
            jax.ShapeDtypeStruct((t, F_W), BF16),
            jax.ShapeDtypeStruct((t, F_W), BF16),
            jax.ShapeDtypeStruct((t, 2 * D_MODEL), BF16),
        ],
        scratch_shapes=[pltpu.VMEM((ATT_W // LANES, tm, LANES), F32)],
        compiler_params=_params(("parallel",)),
        name="in_proj",
    )(x, g1, w_in, w_gate, b_gate, cs, jnp.asarray(qscale))


def _attention_kernel(q_ref, kp_ref, kc_ref, kn_ref, vp_ref, vc_ref, vn_ref, bias_ref, o_ref, lse_ref, *, tq, length):
    i = pl.program_id(2)
    kwin = jnp.concatenate([kp_ref[0, 0], kc_ref[0, 0], kn_ref[0, 0]], axis=0)
    vwin = jnp.concatenate([vp_ref[0, 0], vc_ref[0, 0], vn_ref[0, 0]], axis=0)
    win = ATT_SUB + 2 * HALF_KEYS
    lane_head = lax.broadcasted_iota(I32, (ATT_SUB, GROUP_W), 1) // HEAD_DIM
    col = lax.broadcasted_iota(I32, (ATT_SUB, win), 1)
    for sb in range(tq // ATT_SUB):
        off = sb * ATT_SUB
        q = q_ref[0, 0, off:off + ATT_SUB, :]
        kw = kwin[off:off + win]
        vw = vwin[off:off + win]
        first = i * tq + (off - HALF_KEYS)
        valid = (col >= -first) & (col < length - first)
        qs = jnp.concatenate(
            [jnp.where(lane_head == h, q, jnp.zeros_like(q)) for h in range(HEADS_PER_GROUP)], axis=0)
        s_all = _dot_nt(qs, kw)
        ps, ms, ls = [], [], []
        for h in range(HEADS_PER_GROUP):
            s = s_all[h * ATT_SUB:(h + 1) * ATT_SUB] + bias_ref[h]
            s = jnp.where(valid, s, NEG)
            m = jnp.max(s, axis=-1, keepdims=True)
            p = jnp.exp(s - m)
            ls.append(jnp.sum(p, axis=-1, keepdims=True))
            ms.append(m)
            ps.append(p.astype(BF16))
        o_all = _dot(jnp.concatenate(ps, axis=0), vw)
        out = jnp.zeros((ATT_SUB, GROUP_W), F32)
        lse = jnp.zeros((ATT_SUB, GROUP_W), F32)
        for h in range(HEADS_PER_GROUP):
            oh = o_all[h * ATT_SUB:(h + 1) * ATT_SUB] * (1.0 / ls[h])
            out = jnp.where(lane_head == h, oh, out)
            lse = jnp.where(lane_head == h, ms[h] + jnp.log(ls[h]), lse)
        o_ref[0, 0, off:off + ATT_SUB, :] = out
        lse_ref[0, 0, off:off + ATT_SUB, :] = lse


def _attention(qkv, bias, g):
    batch, dil, length, _ = qkv.shape
    tq = min(length, 512)
    nb = length // tq
    hb = tq // HALF_KEYS
    last_halo = length // HALF_KEYS - 1

    def cur(c):
        return lambda b, r, i: (b, r, i, c)

    def prev(c):
        return lambda b, r, i: (b, r, jnp.maximum(i * hb - 1, 0), c)

    def nxt(c):
        return lambda b, r, i: (b, r, jnp.minimum((i + 1) * hb, last_halo), c)

    blk = lambda rows: (1, 1, rows, GROUP_W)
    return pl.pallas_call(
        functools.partial(_attention_kernel, tq=tq, length=length),
        grid=(batch, dil, nb),
        in_specs=[
            pl.BlockSpec(blk(tq), cur(0)),
            pl.BlockSpec(blk(HALF_KEYS), prev(1)),
            pl.BlockSpec(blk(tq), cur(1)),
            pl.BlockSpec(blk(HALF_KEYS), nxt(1)),
            pl.BlockSpec(blk(HALF_KEYS), prev(2)),
            pl.BlockSpec(blk(tq), cur(2)),
            pl.BlockSpec(blk(HALF_KEYS), nxt(2)),
            pl.BlockSpec(bias.shape, lambda b, r, i: (0, 0, 0)),
        ],
        out_specs=[pl.BlockSpec(blk(tq), cur(0))] * 2,
        out_shape=[jax.ShapeDtypeStruct((batch, dil, length, GROUP_W), F32)] * 2,
        compiler_params=_params(("parallel", "parallel", "parallel")),
        name=f"attention_g{g}",
    )(qkv, qkv, qkv, qkv, qkv, qkv, qkv, bias)


def _t5_bucket(rel):
    nb = NUM_BUCKETS // 2
    max_exact = nb // 2
    ret = (rel > 0).astype(np.int32) * nb
    n = np.abs(rel)
    large = max_exact + (np.log(np.maximum(n, max_exact) / max_exact)
                         / np.log(MAX_DISTANCE / max_exact) * (nb - max_exact)).astype(np.int32)
    large = np.minimum(large, nb - 1)
    return (ret + np.where(n < max_exact, n, large)).astype(np.int32)


def _attention_bias(rel_bias, g):
    dil = GROUPS[g][1]
    qi = np.arange(ATT_SUB)[:, None]
    kj = np.arange(ATT_SUB + 2 * HALF_KEYS)[None, :]
    delta = kj - HALF_KEYS - qi
    band = np.abs(delta) <= HALF_KEYS
    bucket = _t5_bucket(dil * delta)
    tab = rel_bias[:, g * HEADS_PER_GROUP:(g + 1) * HEADS_PER_GROUP].astype(F32)
    onehot = jnp.asarray(bucket[..., None] == np.arange(NUM_BUCKETS), F32)
    bias = jnp.einsum("qkb,bh->hqk", onehot, tab, precision=lax.Precision.HIGHEST)
    return jnp.where(jnp.asarray(band)[None], bias, NEG)


def _dft_mats(n):
    k = np.arange(n)
    ang = 2.0 * np.pi * ((k[:, None] * k[None, :]) % n) / n
    return np.cos(ang), np.sin(ang)


def _fft_stage1_kernel(vr_ref, vi_ref, m1_ref, twc_ref, tws_ref, zr_ref, zi_ref, *, n1, m):
    x = jnp.concatenate([vr_ref[0], vi_ref[0]], axis=0)
    z = _dot(m1_ref[...], x)
    zr, zi = z[:n1], z[n1:]
    twc, tws = twc_ref[0], tws_ref[0]
    for j in range(m):
        c = twc[:, j:j + 1]
        s = tws[:, j:j + 1]
        a = zr[:, j * F_W:(j + 1) * F_W]
        b = zi[:, j * F_W:(j + 1) * F_W]
        zr_ref[0, :, j * F_W:(j + 1) * F_W] = (a * c + b * s).astype(BF16)
        zi_ref[0, :, j * F_W:(j + 1) * F_W] = (b * c - a * s).astype(BF16)


def _fft_stage2_kernel(zr_ref, zi_ref, m2_ref, o_ref, *, kc, scale):
    m2 = m2_ref[...]
    for j in range(kc):
        x = jnp.concatenate([zr_ref[0, j], zi_ref[0, j]], axis=0)
        o_ref[0, :, j * F_W:(j + 1) * F_W] = (_dot(m2, x) * scale).astype(BF16)


def _fourier(vr, vi, batch, seq):
    n2 = LANES
    n1 = seq // n2
    m = 8
    c1, s1 = _dft_mats(n1)
    m1 = jnp.asarray(np.block([[c1, s1], [-s1, c1]]), BF16)
    c2, s2 = _dft_mats(n2)
    m2 = jnp.asarray(np.concatenate([c2, s2], axis=1), BF16)
    k1 = np.arange(n1)[:, None]
    sv = np.arange(n2)[None, :]
    ang = 2.0 * np.pi * ((k1 * sv) % seq) / seq
    to_blocks = lambda a: jnp.asarray(a.reshape(n1, n2 // m, m).transpose(1, 0, 2), F32)
    twc, tws = to_blocks(np.cos(ang)), to_blocks(np.sin(ang))

    v3 = lambda a: a.reshape(batch, n1, n2 * F_W)
    blk = (1, n1, m * F_W)
    dmap = lambda b, j: (b, 0, j)
    tmap = lambda b, j: (j, 0, 0)
    zr, zi = pl.pallas_call(
        functools.partial(_fft_stage1_kernel, n1=n1, m=m),
        grid=(batch, n2 // m),
        in_specs=[
            pl.BlockSpec(blk, dmap),
            pl.BlockSpec(blk, dmap),
            pl.BlockSpec(m1.shape, lambda b, j: (0, 0)),
            pl.BlockSpec((1, n1, m), tmap),
            pl.BlockSpec((1, n1, m), tmap),
        ],
        out_specs=[pl.BlockSpec(blk, dmap), pl.BlockSpec(blk, dmap)],
        out_shape=[jax.ShapeDtypeStruct((batch, n1, n2 * F_W), BF16)] * 2,
        compiler_params=_params(("parallel", "parallel")),
        name="fft_stage1",
    )(v3(vr), v3(vi), m1, twc, tws)

    kc = 8
    v4 = lambda a: a.reshape(batch, n1, n2, F_W)
    zblk = (1, kc, n2, F_W)
    zmap = lambda b, j: (b, j, 0, 0)
    out = pl.pallas_call(
        functools.partial(_fft_stage2_kernel, kc=kc, scale=1.0 / math.sqrt(seq * F_CH)),
        grid=(batch, n1 // kc),
        in_specs=[
            pl.BlockSpec(zblk, zmap),
            pl.BlockSpec(zblk, zmap),
            pl.BlockSpec(m2.shape, lambda b, j: (0, 0)),
        ],
        out_specs=pl.BlockSpec((1, n2, kc * F_W), lambda b, j: (b, 0, j)),
        out_shape=jax.ShapeDtypeStruct((batch, n2, n1 * F_W), BF16),
        compiler_params=_params(("parallel", "parallel")),
        name="fft_stage2",
    )(v4(zr), v4(zi), m2)
    return out.reshape(batch * seq, F_W)


def _mix_kernel(x_ref, o0_ref, o1_ref, o2_ref, l0_ref, l1_ref, l2_ref, four_ref, gates_ref,
                wa_ref, wf_ref, wo_ref, g2_ref, wrh_ref, wrl_ref, x1_ref, xn_ref, aff_ref, afft_ref, slab_ref):
    tm = x_ref.shape[0]

    def token_order(ref, dil):
        if dil == 1:
            return ref[0, 0]
        rows = tm // dil
        for r in range(dil):
            v = ref[0, r]
            for j in range(GROUP_W // LANES):
                slab_ref[j, pl.ds(r, rows, stride=dil), :] = v[:, j * LANES:(j + 1) * LANES]
        return jnp.concatenate([slab_ref[j] for j in range(GROUP_W // LANES)], axis=1)

    dils = [dil for _, dil in GROUPS]
    l0, l1, l2 = (token_order(r, d) for r, d in zip((l0_ref, l1_ref, l2_ref), dils))
    mx = jnp.maximum(jnp.maximum(l0, l1), l2)
    e0, e1, e2 = jnp.exp(l0 - mx), jnp.exp(l1 - mx), jnp.exp(l2 - mx)
    o0, o1, o2 = (token_order(r, d) for r, d in zip((o0_ref, o1_ref, o2_ref), dils))
    att = (e0 * o0 + e1 * o1 + e2 * o2) * (1.0 / (e0 + e1 + e2))
    a_br = _dot(att.astype(BF16), wa_ref[...])
    f_br = _dot(four_ref[...], wf_ref[...])
    mix = gates_ref[:, :D_MODEL] * a_br + gates_ref[:, D_MODEL:] * f_br
    x1 = x_ref[...] + _dot(mix.astype(BF16), wo_ref[...])
    x1_ref[...] = x1
    ms = jnp.mean(x1 * x1, axis=-1, keepdims=True)
    xn = x1 * lax.rsqrt(ms + EPS) * g2_ref[...]
    xn_ref[...] = xn.astype(BF16)
    xh = xn.astype(BF16)
    xl = (xn - xh.astype(F32)).astype(BF16)
    logits = _dot(xh, wrh_ref[...]) + (_dot(xh, wrl_ref[...]) + _dot(xl, wrh_ref[...]))
    lane = lax.broadcasted_iota(I32, logits.shape, 1)
    logits = jnp.where(lane < N_EXPERTS, logits, NEG)
    p = jnp.exp(logits - jnp.max(logits, axis=-1, keepdims=True))
    aff = p * (1.0 / jnp.sum(p, axis=-1, keepdims=True))
    aff_ref[...] = aff
    afft_ref[...] = aff.T[:N_EXPERTS]


def _mix(x, os_, ls_, four, gates, w_attn, w_four, w_out, g2, w_router_hi, w_router_lo, seq):
    t = x.shape[0]
    tm = TOKEN_TILE
    per_batch = seq // tm
    const = lambda i: (0, 0)
    row = lambda i: (i, 0)
    rows = lambda w: pl.BlockSpec((tm, w), row)
    full = lambda a: pl.BlockSpec(a.shape, const)
    classes = [_class_major_spec(tm, dil, GROUP_W, per_batch) for _, dil in GROUPS]
    return pl.pallas_call(
        _mix_kernel,
        grid=(t // tm,),
        in_specs=[rows(D_MODEL)] + classes * 2 + [rows(F_W), rows(2 * D_MODEL),
                  full(w_attn), full(w_four), full(w_out), full(g2), full(w_router_hi), full(w_router_lo)],
        out_specs=[rows(D_MODEL), rows(D_MODEL), rows(LANES), pl.BlockSpec((N_EXPERTS, tm), lambda i: (0, i))],
        out_shape=[
            jax.ShapeDtypeStruct((t, D_MODEL), F32),
            jax.ShapeDtypeStruct((t, D_MODEL), BF16),
            jax.ShapeDtypeStruct((t, LANES), F32),
            jax.ShapeDtypeStruct((N_EXPERTS, t), F32),
        ],
        scratch_shapes=[pltpu.VMEM((GROUP_W // LANES, tm, LANES), F32)],
        compiler_params=_params(("parallel",)),
        name="mix",
    )(x, *os_, *ls_, four, gates, w_attn, w_four, w_out, g2, w_router_hi, w_router_lo)


def _route_kernel(afft_ref, su_ref, u_ref, tau_ref, need_ref, beq_ref, bsel_ref,
                  taut_ref, needt_ref, beqt_ref, bselt_ref, *, tokens):
    cap = CAPACITY_FACTOR * tokens // N_EXPERTS
    ntile = tokens // ROUTE_TILE
    shape = (N_EXPERTS, LANES)
    lane = lax.broadcasted_iota(I32, shape, 1)

    def keys(start, width):
        return lax.bitcast_convert_type(afft_ref[:, pl.ds(pl.multiple_of(start, LANES), width)], I32)

    span = min(tokens, 16 * LANES)

    def count(pred):
        def body(c, acc):
            hits = _ones_where(pred(keys(c * span, span)))
            for j in range(span // LANES):
                acc = acc + hits[:, j * LANES:(j + 1) * LANES]
            return acc
        acc = lax.fori_loop(0, tokens // span, body, jnp.zeros(shape, F32))
        return jnp.sum(acc, axis=1, keepdims=True)

    def bit_body(i, prefix):
        cand = prefix | lax.shift_left(jnp.ones(shape, I32), jnp.full(shape, 30 - i, I32))
        tot = count(lambda k: k >= cand[:, :1])
        return jnp.where(tot >= cap, cand, prefix)

    tau = lax.fori_loop(0, 31, bit_body, jnp.zeros(shape, I32))
    tau_col = tau[:, :1]
    n_gt = count(lambda k: k > tau_col)
    need = cap - n_gt

    def prefix_over_tiles(tab):
        return _dot(tab.astype(BF16), su_ref[...])

    def at_lane(tab, c):
        return jnp.sum(jnp.where(lane == c, tab, 0.0), axis=1, keepdims=True)

    def eq_body(c, tab):
        k = keys(c * ROUTE_TILE, ROUTE_TILE)
        cnt = jnp.sum(_ones_where(k == tau_col), axis=1, keepdims=True)
        return jnp.where(lane == c, cnt, tab)

    base_eq = prefix_over_tiles(lax.fori_loop(0, ntile, eq_body, jnp.zeros(shape, F32)))

    def sel_body(c, tab):
        k = keys(c * ROUTE_TILE, ROUTE_TILE)
        eq = k == tau_col
        eq_cum = _dot(_ones_where(eq, BF16), u_ref[...]) + at_lane(base_eq, c)
        sel = (k > tau_col) | (eq & (eq_cum <= need))
        cnt = jnp.sum(_ones_where(sel), axis=1, keepdims=True)
        return jnp.where(lane == c, cnt, tab)

    base_sel = prefix_over_tiles(lax.fori_loop(0, ntile, sel_body, jnp.zeros(shape, F32)))

    def transposed(val):
        return jnp.concatenate([val, jnp.zeros((LANES - N_EXPERTS, LANES), val.dtype)], axis=0).T

    tau_ref[...] = tau
    taut_ref[...] = transposed(tau)
    for val, ref, ref_t in ((jnp.broadcast_to(need, shape), need_ref, needt_ref),
                            (base_eq, beq_ref, beqt_ref), (base_sel, bsel_ref, bselt_ref)):
        ref[...] = val.astype(I32)
        ref_t[...] = transposed(val)


def _route(afft):
    tokens = afft.shape[1]
    idx = np.arange(LANES)
    su = jnp.asarray(idx[:, None] < idx[None, :], BF16)
    idx = np.arange(ROUTE_TILE)
    u = jnp.asarray(idx[:, None] <= idx[None, :], BF16)
    full = lambda a: pl.BlockSpec(a.shape, lambda i: (0,) * a.ndim)
    small = pl.BlockSpec((N_EXPERTS, LANES), lambda i: (0, 0))
    smallt = pl.BlockSpec((LANES, LANES), lambda i: (0, 0))
    return pl.pallas_call(
        functools.partial(_route_kernel, tokens=tokens),
        grid=(1,),
        in_specs=[full(afft), full(su), full(u)],
        out_specs=[small] * 4 + [smallt] * 4,
        out_shape=[jax.ShapeDtypeStruct((N_EXPERTS, LANES), I32)] * 4
        + [jax.ShapeDtypeStruct((LANES, LANES), I32)] + [jax.ShapeDtypeStruct((LANES, LANES), F32)] * 3,
        compiler_params=_params(("arbitrary",)),
        name="route",
    )(afft, su, u)


def _gather_kernel(bsel_s, afft_ref, tau_ref, need_ref, beq_ref, x_ref, u_ref, xe_hbm,
                   stage_ref, tail_ref, xbuf_ref, zeros_ref, sem_ref, xsem_ref, *, ntile, cap):
    t = pl.program_id(0)
    par = t % 2

    def aligned(e, tile):
        return pl.multiple_of((bsel_s[e, tile] // ROW_ALIGN) * ROW_ALIGN, ROW_ALIGN)

    def writes(tile, buf):
        return [pltpu.make_async_copy(stage_ref.at[buf, e], xe_hbm.at[e, pl.ds(aligned(e, tile), GATHER_BLOCK)],
                                      sem_ref.at[buf, e]) for e in range(N_EXPERTS)]

    @pl.when(t == 0)
    def _():
        tail_ref[...] = jnp.zeros_like(tail_ref)
        zeros_ref[...] = jnp.zeros_like(zeros_ref)
        fills = [pltpu.make_async_copy(zeros_ref, xe_hbm.at[e, pl.ds(cap, GATHER_PAD)], sem_ref.at[0, e])
                 for e in range(N_EXPERTS)]
        for cp in fills:
            cp.start()
        for cp in fills:
            cp.wait()

    k = lax.bitcast_convert_type(afft_ref[...], I32)
    tau = tau_ref[:, :1]
    lane = lax.broadcasted_iota(I32, (N_EXPERTS, LANES), 1)
    beq = jnp.sum(jnp.where(lane == t, beq_ref[...].astype(F32), 0.0), axis=1, keepdims=True)
    eq = k == tau
    eq_cum = _dot(_ones_where(eq, BF16), u_ref[...]) + beq
    sel = (k > tau) | (eq & (eq_cum <= need_ref[:, :1].astype(F32)))
    rank = jnp.where(sel, _dot(_ones_where(sel, BF16), u_ref[...]) - 1.0, -1e4)

    row = lax.broadcasted_iota(I32, (GATHER_STACK, ROUTE_TILE), 0)
    in_block = row < GATHER_BLOCK
    row_f = row.astype(F32)
    offs, shifts, pieces = [], [], []
    for e in range(N_EXPERTS):
        off = (bsel_s[e, t] - aligned(e, t)).astype(F32)
        shift = ((bsel_s[e, t + 1] // ROW_ALIGN) * ROW_ALIGN - aligned(e, t))
        target = jnp.where(in_block, row_f, row_f - float(GATHER_BLOCK) + shift.astype(F32))
        pieces.append(_ones_where(rank[e:e + 1, :] + off == target, BF16))
        offs.append(off)
        shifts.append(shift)
    res = _dot(jnp.concatenate(pieces, axis=0), x_ref[...])
    for e in range(N_EXPERTS):
        base = e * GATHER_STACK
        old = tail_ref[e]
        stage_ref[par, e, 0:ROW_ALIGN, :] = (res[base:base + ROW_ALIGN] + old).astype(BF16)
        stage_ref[par, e, ROW_ALIGN:GATHER_BLOCK, :] = res[base + ROW_ALIGN:base + GATHER_BLOCK].astype(BF16)
        tail_ref[e] = res[base + GATHER_BLOCK:base + GATHER_STACK] + jnp.where(shifts[e] == 0, old, 0.0)

    @pl.when(t > 0)
    def _():
        for cp in writes(t - 1, 1 - par):
            cp.wait()

    for cp in writes(t, par):
        cp.start()

    extra = [(jnp.maximum(bsel_s[e, t + 1] - aligned(e, t) - GATHER_BLOCK, 0) + SLOT_CHUNK - 1) // SLOT_CHUNK
             for e in range(N_EXPERTS)]

    @pl.when(functools.reduce(jnp.maximum, extra) > 0)
    def _():
        row64 = lax.broadcasted_iota(I32, (SLOT_CHUNK, ROUTE_TILE), 0).astype(F32)
        for e in range(N_EXPERTS):
            def chunk(c, carry):
                first = GATHER_BLOCK + c * SLOT_CHUNK
                onehot = _ones_where(rank[e:e + 1, :] + offs[e] == row64 + first.astype(F32), BF16)
                xbuf_ref[...] = _dot(onehot, x_ref[...]).astype(BF16)
                dst = pl.multiple_of(aligned(e, t) + first, ROW_ALIGN)
                cp = pltpu.make_async_copy(xbuf_ref, xe_hbm.at[e, pl.ds(dst, SLOT_CHUNK)], xsem_ref.at[0])
                cp.start()
                cp.wait()
                return carry

            lax.fori_loop(0, extra[e], chunk, 0)

    @pl.when(t == ntile - 1)
    def _():
        for cp in writes(t, par):
            cp.wait()


def _gather(bsel_i, afft, tau, need, beq_i, xn, u):
    tokens = xn.shape[0]
    cap = CAPACITY_FACTOR * tokens // N_EXPERTS
    ntile = tokens // ROUTE_TILE
    table = pl.BlockSpec((N_EXPERTS, LANES), lambda t, *_: (0, 0))
    grid_spec = pltpu.PrefetchScalarGridSpec(
        num_scalar_prefetch=1,
        grid=(ntile,),
        in_specs=[
            pl.BlockSpec((N_EXPERTS, ROUTE_TILE), lambda t, *_: (0, t)),
            table, table, table,
            pl.BlockSpec((ROUTE_TILE, D_MODEL), lambda t, *_: (t, 0)),
            pl.BlockSpec(u.shape, lambda t, *_: (0, 0)),
        ],
        out_specs=pl.BlockSpec(memory_space=pl.ANY),
        scratch_shapes=[
            pltpu.VMEM((2, N_EXPERTS, GATHER_BLOCK, D_MODEL), BF16),
            pltpu.VMEM((N_EXPERTS, ROW_ALIGN, D_MODEL), F32),
            pltpu.VMEM((SLOT_CHUNK, D_MODEL), BF16),
            pltpu.VMEM((GATHER_PAD, D_MODEL), BF16),
            pltpu.SemaphoreType.DMA((2, N_EXPERTS)),
            pltpu.SemaphoreType.DMA((1,)),
        ],
    )
    return pl.pallas_call(
        functools.partial(_gather_kernel, ntile=ntile, cap=cap),
        grid_spec=grid_spec,
        out_shape=jax.ShapeDtypeStruct((N_EXPERTS, cap + GATHER_PAD, D_MODEL), BF16),
        compiler_params=_params(("arbitrary",)),
        name="gather",
    )(bsel_i, afft, tau, need, beq_i, xn, u)


def _ffn_kernel(xe_ref, wg_ref, wu_ref, wd_ref, ye_ref, acc_ref, *, cap, nf, tm):
    f = pl.program_id(1)
    tf = wg_ref.shape[2]
    chunks = [slice(j * FFN_CHUNK, (j + 1) * FFN_CHUNK) for j in range(tf // FFN_CHUNK)]
    cast = {}

    def weight(name, ref, j):
        if (name, j) not in cast:
            cast[name, j] = (ref[0, chunks[j], :] if name == "d" else ref[0, :, chunks[j]]).astype(BF16)
        return cast[name, j]

    @pl.when(f == 0)
    def _():
        acc_ref[...] = jnp.zeros_like(acc_ref)

    for i in range(cap // tm):
        r = slice(i * tm, (i + 1) * tm)
        x = xe_ref[0, r, :]
        y = None
        for j in range(len(chunks)):
            hg = _dot(x, weight("g", wg_ref, j))
            hu = _dot(x, weight("u", wu_ref, j))
            h = (hg * (1.0 / (1.0 + jnp.exp(-hg))) * hu).astype(BF16)
            part = _dot(h, weight("d", wd_ref, j))
            y = part if y is None else y + part
        acc_ref[r, :] += y

    @pl.when(f == nf - 1)
    def _():
        ye_ref[...] = acc_ref[...].astype(BF16)


def _ffn(xe, w_eg, w_eu, w_ed):
    cap = xe.shape[1] - GATHER_PAD
    tf = 512
    nf = D_FF // tf
    tm = min(cap, 1024)
    return pl.pallas_call(
        functools.partial(_ffn_kernel, cap=cap, nf=nf, tm=tm),
        grid=(N_EXPERTS, nf),
        in_specs=[
            pl.BlockSpec((1, cap, D_MODEL), lambda e, f: (e, 0, 0)),
            pl.BlockSpec((1, D_MODEL, tf), lambda e, f: (e, 0, f)),
            pl.BlockSpec((1, D_MODEL, tf), lambda e, f: (e, 0, f)),
            pl.BlockSpec((1, tf, D_MODEL), lambda e, f: (e, f, 0)),
        ],
        out_specs=pl.BlockSpec((cap, D_MODEL), lambda e, f: (e, 0)),
        out_shape=jax.ShapeDtypeStruct((N_EXPERTS * cap, D_MODEL), BF16),
        scratch_shapes=[pltpu.VMEM((cap, D_MODEL), F32)],
        compiler_params=_params(("arbitrary", "arbitrary")),
        name="ffn",
    )(xe, w_eg, w_eu, w_ed)


def _combine_kernel(bsel_s, x1_ref, aff_ref, taut_ref, needt_ref, beqt_ref, bselt_ref, low_ref, spread_ref, gf_ref,
                    ye_hbm, y_ref, buf_ref, xbuf_ref, sem_ref, xsem_ref, *, cap, total, ntile):
    t = pl.program_id(0)
    par = t % 2

    def aligned(e, tile):
        return (bsel_s[e, tile] // 16) * 16

    def window(e, tile, c):
        start = jnp.minimum(e * cap + aligned(e, tile) + c * SLOT_CHUNK, total - SLOT_CHUNK)
        return pl.multiple_of(start, 16)

    def first_chunks(tile, buf):
        return [pltpu.make_async_copy(ye_hbm.at[pl.ds(window(e, tile, 0), SLOT_CHUNK)],
                                      buf_ref.at[buf, pl.ds(e * SLOT_CHUNK, SLOT_CHUNK)], sem_ref.at[buf, e])
                for e in range(N_EXPERTS)]

    @pl.when(t == 0)
    def _():
        for cp in first_chunks(0, 0):
            cp.start()

    @pl.when(t + 1 < ntile)
    def _():
        for cp in first_chunks(t + 1, 1 - par):
            cp.start()

    aff = aff_ref[...]
    k = lax.bitcast_convert_type(aff, I32)
    tau = taut_ref[0:1, :]
    low = low_ref[...]
    eq = k == tau
    eq_cum = _dot(low, _ones_where(eq, BF16)) + beqt_ref[0]
    sel = (k > tau) | (eq & (eq_cum <= needt_ref[0:1, :]))
    slot = jnp.where(sel, _dot(low, _ones_where(sel, BF16)) + (bselt_ref[0] - 1.0), -1.0)

    lane = lax.broadcasted_iota(I32, (1, LANES), 1)
    rel = jnp.zeros((1, LANES), F32)
    for e in range(N_EXPERTS):
        rel = jnp.where(lane == e, (window(e, t, 0) - e * cap).astype(F32), rel)
    d = slot - rel
    d = jnp.where(sel & (d >= 0.0) & (d < float(SLOT_CHUNK)), d, -1.0)
    spread = spread_ref[...]
    wide = lax.broadcasted_iota(I32, (ROUTE_TILE, N_EXPERTS * SLOT_CHUNK), 1)
    hit = _dot(d.astype(BF16), spread) == (wide % SLOT_CHUNK).astype(F32)
    onehot_gate = jnp.where(hit, _dot(aff.astype(BF16), spread), 0.0).astype(BF16)
    for cp in first_chunks(t, par):
        cp.wait()
    y_ref[...] = x1_ref[...] + _dot(onehot_gate, buf_ref[par])

    nch = [(bsel_s[e, t + 1] - aligned(e, t) + SLOT_CHUNK - 1) // SLOT_CHUNK for e in range(N_EXPERTS)]

    @pl.when(functools.reduce(jnp.maximum, nch) > 1)
    def _():
        lane64 = lax.broadcasted_iota(I32, (ROUTE_TILE, SLOT_CHUNK), 1).astype(F32)
        for e in range(N_EXPERTS):
            slot_e = slot[:, e:e + 1]

            def extra(c, carry):
                w = window(e, t, c)
                cp = pltpu.make_async_copy(ye_hbm.at[pl.ds(w, SLOT_CHUNK)], xbuf_ref, xsem_ref.at[0])
                cp.start()
                cp.wait()
                first = (aligned(e, t) + c * SLOT_CHUNK).astype(F32)
                hit = (lane64 + (w - e * cap).astype(F32) == slot_e) & (slot_e >= first)
                y_ref[...] += aff[:, e:e + 1] * _dot(_ones_where(hit, BF16), xbuf_ref[...])
                return carry

            lax.fori_loop(1, nch[e], extra, 0)

    acc = y_ref[...]
    ms = jnp.mean(acc * acc, axis=-1, keepdims=True)
    y_ref[...] = acc * lax.rsqrt(ms + EPS) * gf_ref[...]


def _combine(bsel_i, x1, aff, tables_t, ye, gf):
    tokens = x1.shape[0]
    cap = CAPACITY_FACTOR * tokens // N_EXPERTS
    ntile = tokens // ROUTE_TILE
    idx = np.arange(ROUTE_TILE)
    low = jnp.asarray(idx[:, None] >= idx[None, :], BF16)
    spread = jnp.asarray(np.arange(LANES)[:, None] == np.arange(N_EXPERTS * SLOT_CHUNK)[None, :] // SLOT_CHUNK, BF16)
    taut, needt, beqt, bselt = tables_t
    rowvec = pl.BlockSpec((8, LANES), lambda t, *_: (0, 0))
    tilevec = pl.BlockSpec((1, 1, LANES), lambda t, *_: (t, 0, 0))
    grid_spec = pltpu.PrefetchScalarGridSpec(
        num_scalar_prefetch=1,
        grid=(ntile,),
        in_specs=[
            pl.BlockSpec((ROUTE_TILE, D_MODEL), lambda t, *_: (t, 0)),
            pl.BlockSpec((ROUTE_TILE, LANES), lambda t, *_: (t, 0)),
            rowvec, rowvec, tilevec, tilevec,
            pl.BlockSpec(low.shape, lambda t, *_: (0, 0)),
            pl.BlockSpec(spread.shape, lambda t, *_: (0, 0)),
            pl.BlockSpec((1, D_MODEL), lambda t, *_: (0, 0)),
            pl.BlockSpec(memory_space=pl.ANY),
        ],
        out_specs=pl.BlockSpec((ROUTE_TILE, D_MODEL), lambda t, *_: (t, 0)),
        scratch_shapes=[
            pltpu.VMEM((2, N_EXPERTS * SLOT_CHUNK, D_MODEL), BF16),
            pltpu.VMEM((SLOT_CHUNK, D_MODEL), BF16),
            pltpu.SemaphoreType.DMA((2, N_EXPERTS)),
            pltpu.SemaphoreType.DMA((1,)),
        ],
    )
    return pl.pallas_call(
        functools.partial(_combine_kernel, cap=cap, total=N_EXPERTS * cap, ntile=ntile),
        grid_spec=grid_spec,
        out_shape=jax.ShapeDtypeStruct((tokens, D_MODEL), F32),
        compiler_params=_params(("arbitrary",)),
        name="combine",
    )(bsel_i, x1, aff, taut, needt, beqt.reshape(LANES, 1, LANES), bselt.reshape(LANES, 1, LANES), low, spread,
      gf, ye)


def _encoder(x, w):
    batch, seq, _ = x.shape
    tokens = batch * seq
    xt = x.reshape(tokens, D_MODEL)
    *qkvs, vr, vi, gates = _in_proj(xt, w["g1"], w["w_in"], w["w_gate"], w["b_gate"], w["cs"], batch, seq)
    outs, lses = [], []
    for g in range(N_GROUPS):
        o, lse = _attention(qkvs[g], w["bias"][g], g)
        outs.append(o)
        lses.append(lse)
    four = _fourier(vr, vi, batch, seq)
    x1, xn, aff, afft = _mix(xt, outs, lses, four, gates, w["w_attn"], w["w_four"], w["w_out"], w["g2"],
                             w["w_router_hi"], w["w_router_lo"], seq)
    tau, need, beq_i, bsel_i, taut, needt, beqt, bselt = _route(afft)
    idx = np.arange(ROUTE_TILE)
    u = jnp.asarray(idx[:, None] <= idx[None, :], BF16)
    xe = _gather(bsel_i, afft, tau, need, beq_i, xn, u)
    ye = _ffn(xe, w["w_eg"], w["w_eu"], w["w_ed"])
    y = _combine(bsel_i, x1, aff, (taut, needt, beqt, bselt), ye, w["gf"])
    return y.reshape(batch, seq, D_MODEL)


def _prepare_weights(rel_bias, norm1_g, w_in, w_attn_br, w_four_br, w_gate, b_gate, w_out,
                     norm2_g, w_router, w_exp_gate, w_exp_up, w_exp_down, final_g):
    c, s = _dft_mats(F_CH)
    starts = [part * ATT_W + g * GROUP_W for g in range(N_GROUPS) for part in range(3)]
    w_in_grouped = jnp.concatenate([w_in[0][:, s0:s0 + GROUP_W] for s0 in starts] + [w_in[0][:, QKV_W:]], axis=1)
    w_router = jnp.pad(w_router[0], ((0, 0), (0, LANES - N_EXPERTS)))
    w_router_hi = w_router.astype(BF16)
    return {
        "g1": norm1_g[0].reshape(1, D_MODEL),
        "w_in": w_in_grouped.astype(BF16),
        "w_gate": w_gate[0].astype(BF16),
        "b_gate": b_gate[0].reshape(1, 2 * D_MODEL),
        "cs": jnp.asarray(np.concatenate([c, s], axis=1), BF16),
        "bias": [_attention_bias(rel_bias, g) for g in range(N_GROUPS)],
        "w_attn": w_attn_br[0].astype(BF16),
        "w_four": w_four_br[0].astype(BF16),
        "w_out": w_out[0].astype(BF16),
        "g2": norm2_g[0].reshape(1, D_MODEL),
        "w_router_hi": w_router_hi,
        "w_router_lo": (w_router - w_router_hi.astype(F32)).astype(BF16),
        "w_eg": w_exp_gate[0],
        "w_eu": w_exp_up[0],
        "w_ed": w_exp_down[0],
        "gf": final_g.reshape(1, D_MODEL),
    }


def kernel(x_prompt, x_sample, rel_bias, norm1_g, w_in, w_attn_br, w_four_br, w_gate, b_gate, w_out,
           norm2_g, w_router, w_exp_gate, w_exp_up, w_exp_down, final_g):
    w = _prepare_weights(rel_bias, norm1_g, w_in, w_attn_br, w_four_br, w_gate, b_gate, w_out,
                         norm2_g, w_router, w_exp_gate, w_exp_up, w_exp_down, final_g)
    return (_encoder(x_prompt, w), _encoder(x_sample, w))
```

```python
import functools
import math

import numpy as np
import jax
import jax.numpy as jnp
from jax import lax
from jax.experimental import pallas as pl
from jax.experimental.pallas import tpu as pltpu

D_MODEL = 1024
HEAD_DIM = 64
HEADS_PER_GROUP = 4
GROUPS = ((128, 1), (512, 4), (2048, 16))
N_GROUPS = len(GROUPS)
GROUP_W = HEADS_PER_GROUP * HEAD_DIM
ATT_W = N_GROUPS * GROUP_W
QKV_W = 3 * ATT_W
F_GROUPS = 6
F_CH = 128
F_W = F_GROUPS * F_CH
NUM_BUCKETS = 32
MAX_DISTANCE = 1024
N_EXPERTS = 16
CAPACITY_FACTOR = 2
D_FF = 2048
EPS = 1e-6
NEG = -1e30

HALF_KEYS = 64
ATT_SUB = 128
ATT_OUT_ROWS = 8192
TOKEN_TILE = 512
ROUTE_TILE = 256
FFN_CHUNK = 256
SLOT_CHUNK = 64
ROW_ALIGN = 16
GATHER_BLOCK = SLOT_CHUNK + ROW_ALIGN
GATHER_STACK = GATHER_BLOCK + ROW_ALIGN
GATHER_PAD = 128
LANES = 128
V7X_VMEM_LIMIT = 56 * 1024 * 1024

F32 = jnp.float32
BF16 = jnp.bfloat16
I32 = jnp.int32


def _params(sem):
    return pltpu.CompilerParams(dimension_semantics=sem, vmem_limit_bytes=V7X_VMEM_LIMIT)


def _dot(a, b):
    return jnp.dot(a, b, preferred_element_type=F32)


def _dot_nt(a, b):
    return lax.dot_general(a, b, (((1,), (1,)), ((), ())), preferred_element_type=F32)


def _ones_where(mask, dtype=F32):
    return jnp.where(mask, jnp.ones((), F32), jnp.zeros((), F32)).astype(dtype)


def _in_proj_kernel(x_ref, g_ref, win_ref, wg_ref, bg_ref, cs_ref, qscale_ref, qkv0_ref, qkv1_ref, qkv2_ref,
                    vr_ref, vi_ref, gates_ref, slab_ref):
    x = x_ref[...]
    tm = x.shape[0]
    ms = jnp.mean(x * x, axis=-1, keepdims=True)
    xn = (x * lax.rsqrt(ms + EPS) * g_ref[...]).astype(BF16)
    nslab = ATT_W // LANES
    for g, out_ref in enumerate((qkv0_ref, qkv1_ref, qkv2_ref)):
        dil = GROUPS[g][1]
        res = _dot(xn, win_ref[:, g * ATT_W:(g + 1) * ATT_W]) * qscale_ref[...]
        if dil == 1:
            out_ref[0, 0] = res.astype(BF16)
            continue
        for j in range(nslab):
            slab_ref[j] = res[:, j * LANES:(j + 1) * LANES]
        rows = tm // dil
        for r in range(dil):
            cls = [slab_ref[j, pl.ds(r, rows, stride=dil), :] for j in range(nslab)]
            out_ref[0, r] = jnp.concatenate(cls, axis=1).astype(BF16)
    u = _dot(xn, win_ref[:, QKV_W:QKV_W + F_W]).astype(BF16)
    cs = cs_ref[...]
    for g in range(F_GROUPS):
        a = _dot(u[:, g * F_CH:(g + 1) * F_CH], cs)
        vr_ref[:, g * F_CH:(g + 1) * F_CH] = a[:, :F_CH].astype(BF16)
        vi_ref[:, g * F_CH:(g + 1) * F_CH] = (-a[:, F_CH:]).astype(BF16)
    z = _dot(xn, wg_ref[...]) + bg_ref[...]
    gates_ref[...] = (1.0 / (1.0 + jnp.exp(-z))).astype(BF16)


def _class_major_spec(tm, dil, width, per_batch):
    return pl.BlockSpec((1, dil, tm // dil, width), lambda i: (i // per_batch, 0, i % per_batch, 0))


def _in_proj(x, g1, w_in, w_gate, b_gate, cs, batch, seq):
    t = x.shape[0]
    tm = TOKEN_TILE
    per_batch = seq // tm
    const = lambda i: (0, 0)
    row = lambda i: (i, 0)
    qscale = np.ones((1, ATT_W), np.float32)
    qscale[:, :GROUP_W] = 1.0 / math.sqrt(HEAD_DIM)
    return pl.pallas_call(
        _in_proj_kernel,
        grid=(t // tm,),
        in_specs=[
            pl.BlockSpec((tm, D_MODEL), row),
            pl.BlockSpec((1, D_MODEL), const),
            pl.BlockSpec(w_in.shape, const),
            pl.BlockSpec(w_gate.shape, const),
            pl.BlockSpec((1, 2 * D_MODEL), const),
            pl.BlockSpec(cs.shape, const),
            pl.BlockSpec((1, ATT_W), const),
        ],
        out_specs=[_class_major_spec(tm, dil, ATT_W, per_batch) for _, dil in GROUPS] + [
            pl.BlockSpec((tm, F_W), row),
            pl.BlockSpec((tm, F_W), row),
            pl.BlockSpec((tm, 2 * D_MODEL), row),
        ],
        out_shape=[jax.ShapeDtypeStruct((batch, dil, seq // dil, ATT_W), BF16) for _, dil in GROUPS] + [
            jax.ShapeDtypeStruct((t, F_W), BF16),
            jax.ShapeDtypeStruct((t, F_W), BF16),
            jax.ShapeDtypeStruct((t, 2 * D_MODEL), BF16),
        ],
        scratch_shapes=[pltpu.VMEM((ATT_W // LANES, tm, LANES), F32)],
        compiler_params=_params(("parallel",)),
        name="in_proj",
    )(x, g1, w_in, w_gate, b_gate, cs, jnp.asarray(qscale))


def _attention_kernel(q_ref, kp_ref, kc_ref, kn_ref, vp_ref, vc_ref, vn_ref, bias_ref, o_ref, lse_ref, *,
                      tq, length, dil, rc):
    i = pl.program_id(1)
    win = ATT_SUB + 2 * HALF_KEYS
    lane_head = lax.broadcasted_iota(I32, (ATT_SUB, GROUP_W), 1) // HEAD_DIM
    col = lax.broadcasted_iota(I32, (ATT_SUB, win), 1)
    for c, sb in [(c, sb) for c in range(rc) for sb in range(tq // ATT_SUB)]:
        r = pl.program_id(2) * rc + c
        if sb == 0:
            kwin = jnp.concatenate([kp_ref[0, c], kc_ref[0, c], kn_ref[0, c]], axis=0)
            vwin = jnp.concatenate([vp_ref[0, c], vc_ref[0, c], vn_ref[0, c]], axis=0)
        off = sb * ATT_SUB
        q = q_ref[0, c, off:off + ATT_SUB, :]
        kw = kwin[off:off + win]
        vw = vwin[off:off + win]
        first = i * tq + (off - HALF_KEYS)
        valid = (col >= -first) & (col < length - first)
        qs = jnp.concatenate(
            [jnp.where(lane_head == h, q, jnp.zeros_like(q)) for h in range(HEADS_PER_GROUP)], axis=0)
        s_all = _dot_nt(qs, kw)
        ps, ms, ls = [], [], []
        for h in range(HEADS_PER_GROUP):
            s = s_all[h * ATT_SUB:(h + 1) * ATT_SUB] + bias_ref[h]
            s = jnp.where(valid, s, NEG)
            m = jnp.max(s, axis=-1, keepdims=True)
            p = jnp.exp(s - m)
            ls.append(jnp.sum(p, axis=-1, keepdims=True))
            ms.append(m)
            ps.append(p.astype(BF16))
        o_all = _dot(jnp.concatenate(ps, axis=0), vw)
        out = jnp.zeros((ATT_SUB, GROUP_W), F32)
        lse = jnp.zeros((ATT_SUB, GROUP_W), F32)
        for h in range(HEADS_PER_GROUP):
            oh = o_all[h * ATT_SUB:(h + 1) * ATT_SUB] * (1.0 / ls[h])
            out = jnp.where(lane_head == h, oh, out)
            lse = jnp.where(lane_head == h, ms[h] + jnp.log(ls[h]), lse)
        rows = pl.ds(off * dil + r, ATT_SUB, stride=dil) if dil > 1 else pl.ds(off, ATT_SUB)
        for j in range(GROUP_W // LANES):
            o_ref[j, rows, :] = out[:, j * LANES:(j + 1) * LANES]
            lse_ref[j, rows, :] = lse[:, j * LANES:(j + 1) * LANES]


def _attention(qkv, bias, g):
    batch, dil, length, _ = qkv.shape
    tq = min(length, 512, ATT_OUT_ROWS // dil)
    nb = length // tq
    hb = tq // HALF_KEYS
    last_halo = length // HALF_KEYS - 1
    rc = min(dil, max(1, 512 // tq))

    def cur(c):
        return lambda b, i, r: (b, r, i, c)

    def prev(c):
        return lambda b, i, r: (b, r, jnp.maximum(i * hb - 1, 0), c)

    def nxt(c):
        return lambda b, i, r: (b, r, jnp.minimum((i + 1) * hb, last_halo), c)

    blk = lambda rows: (1, rc, rows, GROUP_W)
    out_spec = pl.BlockSpec((GROUP_W // LANES, tq * dil, LANES), lambda b, i, r: (0, b * nb + i, 0))
    return pl.pallas_call(
        functools.partial(_attention_kernel, tq=tq, length=length, dil=dil, rc=rc),
        grid=(batch, nb, dil // rc),
        in_specs=[
            pl.BlockSpec(blk(tq), cur(0)),
            pl.BlockSpec(blk(HALF_KEYS), prev(1)),
            pl.BlockSpec(blk(tq), cur(1)),
            pl.BlockSpec(blk(HALF_KEYS), nxt(1)),
            pl.BlockSpec(blk(HALF_KEYS), prev(2)),
            pl.BlockSpec(blk(tq), cur(2)),
            pl.BlockSpec(blk(HALF_KEYS), nxt(2)),
            pl.BlockSpec(bias.shape, lambda b, i, r: (0, 0, 0)),
        ],
        out_specs=[out_spec] * 2,
        out_shape=[jax.ShapeDtypeStruct((GROUP_W // LANES, batch * dil * length, LANES), F32)] * 2,
        compiler_params=_params(("parallel", "parallel", "arbitrary")),
        name=f"attention_g{g}",
    )(qkv, qkv, qkv, qkv, qkv, qkv, qkv, bias)


def _t5_bucket(rel):
    nb = NUM_BUCKETS // 2
    max_exact = nb // 2
    ret = (rel > 0).astype(np.int32) * nb
    n = np.abs(rel)
    large = max_exact + (np.log(np.maximum(n, max_exact) / max_exact)
                         / np.log(MAX_DISTANCE / max_exact) * (nb - max_exact)).astype(np.int32)
    large = np.minimum(large, nb - 1)
    return (ret + np.where(n < max_exact, n, large)).astype(np.int32)


def _attention_bias(rel_bias, g):
    dil = GROUPS[g][1]
    qi = np.arange(ATT_SUB)[:, None]
    kj = np.arange(ATT_SUB + 2 * HALF_KEYS)[None, :]
    delta = kj - HALF_KEYS - qi
    band = np.abs(delta) <= HALF_KEYS
    bucket = _t5_bucket(dil * delta)
    tab = rel_bias[:, g * HEADS_PER_GROUP:(g + 1) * HEADS_PER_GROUP].astype(F32)
    onehot = jnp.asarray(bucket[..., None] == np.arange(NUM_BUCKETS), F32)
    bias = jnp.einsum("qkb,bh->hqk", onehot, tab, precision=lax.Precision.HIGHEST)
    return jnp.where(jnp.asarray(band)[None], bias, NEG)


def _dft_mats(n):
    k = np.arange(n)
    ang = 2.0 * np.pi * ((k[:, None] * k[None, :]) % n) / n
    return np.cos(ang), np.sin(ang)


def _fft_stage1_kernel(vr_ref, vi_ref, m1_ref, twc_ref, tws_ref, zr_ref, zi_ref, *, n1, m):
    x = jnp.concatenate([vr_ref[0], vi_ref[0]], axis=0)
    z = _dot(m1_ref[...], x)
    zr, zi = z[:n1], z[n1:]
    twc, tws = twc_ref[0], tws_ref[0]
    for j in range(m):
        c = twc[:, j:j + 1]
        s = tws[:, j:j + 1]
        a = zr[:, j * F_W:(j + 1) * F_W]
        b = zi[:, j * F_W:(j + 1) * F_W]
        zr_ref[0, :, j * F_W:(j + 1) * F_W] = (a * c + b * s).astype(BF16)
        zi_ref[0, :, j * F_W:(j + 1) * F_W] = (b * c - a * s).astype(BF16)


def _fft_stage2_kernel(zr_ref, zi_ref, m2_ref, o_ref, *, kc, scale):
    m2 = m2_ref[...]
    for j in range(kc):
        x = jnp.concatenate([zr_ref[0, j], zi_ref[0, j]], axis=0)
        o_ref[0, :, j * F_W:(j + 1) * F_W] = (_dot(m2, x) * scale).astype(BF16)


def _fourier(vr, vi, batch, seq):
    n2 = LANES
    n1 = seq // n2
    m = 8
    c1, s1 = _dft_mats(n1)
    m1 = jnp.asarray(np.block([[c1, s1], [-s1, c1]]), BF16)
    c2, s2 = _dft_mats(n2)
    m2 = jnp.asarray(np.concatenate([c2, s2], axis=1), BF16)
    k1 = np.arange(n1)[:, None]
    sv = np.arange(n2)[None, :]
    ang = 2.0 * np.pi * ((k1 * sv) % seq) / seq
    to_blocks = lambda a: jnp.asarray(a.reshape(n1, n2 // m, m).transpose(1, 0, 2), F32)
    twc, tws = to_blocks(np.cos(ang)), to_blocks(np.sin(ang))

    v3 = lambda a: a.reshape(batch, n1, n2 * F_W)
    blk = (1, n1, m * F_W)
    dmap = lambda b, j: (b, 0, j)
    tmap = lambda b, j: (j, 0, 0)
    zr, zi = pl.pallas_call(
        functools.partial(_fft_stage1_kernel, n1=n1, m=m),
        grid=(batch, n2 // m),
        in_specs=[
            pl.BlockSpec(blk, dmap),
            pl.BlockSpec(blk, dmap),
            pl.BlockSpec(m1.shape, lambda b, j: (0, 0)),
            pl.BlockSpec((1, n1, m), tmap),
            pl.BlockSpec((1, n1, m), tmap),
        ],
        out_specs=[pl.BlockSpec(blk, dmap), pl.BlockSpec(blk, dmap)],
        out_shape=[jax.ShapeDtypeStruct((batch, n1, n2 * F_W), BF16)] * 2,
        compiler_params=_params(("parallel", "parallel")),
        name="fft_stage1",
    )(v3(vr), v3(vi), m1, twc, tws)

    kc = 8
    v4 = lambda a: a.reshape(batch, n1, n2, F_W)
    zblk = (1, kc, n2, F_W)
    zmap = lambda b, j: (b, j, 0, 0)
    out = pl.pallas_call(
        functools.partial(_fft_stage2_kernel, kc=kc, scale=1.0 / math.sqrt(seq * F_CH)),
        grid=(batch, n1 // kc),
        in_specs=[
            pl.BlockSpec(zblk, zmap),
            pl.BlockSpec(zblk, zmap),
            pl.BlockSpec(m2.shape, lambda b, j: (0, 0)),
        ],
        out_specs=pl.BlockSpec((1, n2, kc * F_W), lambda b, j: (b, 0, j)),
        out_shape=jax.ShapeDtypeStruct((batch, n2, n1 * F_W), BF16),
        compiler_params=_params(("parallel", "parallel")),
        name="fft_stage2",
    )(v4(zr), v4(zi), m2)
    return out.reshape(batch * seq, F_W)


def _mix_kernel(x_ref, o0_ref, o1_ref, o2_ref, l0_ref, l1_ref, l2_ref, four_ref, gates_ref,
                wa_ref, wf_ref, wo_ref, g2_ref, wrh_ref, wrl_ref, x1_ref, xn_ref, aff_ref, afft_ref):
    def slabs(ref):
        return jnp.concatenate([ref[j] for j in range(GROUP_W // LANES)], axis=1)

    f_br = _dot(four_ref[...], wf_ref[...])
    l0, l1, l2 = slabs(l0_ref), slabs(l1_ref), slabs(l2_ref)
    mx = jnp.maximum(jnp.maximum(l0, l1), l2)
    e0, e1, e2 = jnp.exp(l0 - mx), jnp.exp(l1 - mx), jnp.exp(l2 - mx)
    att = (e0 * slabs(o0_ref) + e1 * slabs(o1_ref) + e2 * slabs(o2_ref)) * (1.0 / (e0 + e1 + e2))
    a_br = _dot(att.astype(BF16), wa_ref[...])
    mix = gates_ref[:, :D_MODEL] * a_br + gates_ref[:, D_MODEL:] * f_br
    x1 = x_ref[...] + _dot(mix.astype(BF16), wo_ref[...])
    x1_ref[...] = x1
    ms = jnp.mean(x1 * x1, axis=-1, keepdims=True)
    xn = x1 * lax.rsqrt(ms + EPS) * g2_ref[...]
    xn_ref[...] = xn.astype(BF16)
    xh = xn.astype(BF16)
    xl = (xn - xh.astype(F32)).astype(BF16)
    logits = _dot(xh, wrh_ref[...]) + (_dot(xh, wrl_ref[...]) + _dot(xl, wrh_ref[...]))
    lane = lax.broadcasted_iota(I32, logits.shape, 1)
    logits = jnp.where(lane < N_EXPERTS, logits, NEG)
    p = jnp.exp(logits - jnp.max(logits, axis=-1, keepdims=True))
    aff = p * (1.0 / jnp.sum(p, axis=-1, keepdims=True))
    aff_ref[...] = aff
    afft_ref[...] = aff.T[:N_EXPERTS]


def _mix(x, os_, ls_, four, gates, w_attn, w_four, w_out, g2, w_router_hi, w_router_lo):
    t = x.shape[0]
    tm = TOKEN_TILE
    const = lambda i: (0, 0)
    row = lambda i: (i, 0)
    rows = lambda w: pl.BlockSpec((tm, w), row)
    full = lambda a: pl.BlockSpec(a.shape, const)
    slab = pl.BlockSpec((GROUP_W // LANES, tm, LANES), lambda i: (0, i, 0))
    return pl.pallas_call(
        _mix_kernel,
        grid=(t // tm,),
        in_specs=[rows(D_MODEL)] + [slab] * 6 + [rows(F_W), rows(2 * D_MODEL),
                  full(w_attn), full(w_four), full(w_out), full(g2), full(w_router_hi), full(w_router_lo)],
        out_specs=[rows(D_MODEL), rows(D_MODEL), rows(LANES), pl.BlockSpec((N_EXPERTS, tm), lambda i: (0, i))],
        out_shape=[
            jax.ShapeDtypeStruct((t, D_MODEL), F32),
            jax.ShapeDtypeStruct((t, D_MODEL), BF16),
            jax.ShapeDtypeStruct((t, LANES), F32),
            jax.ShapeDtypeStruct((N_EXPERTS, t), F32),
        ],
        compiler_params=_params(("parallel",)),
        name="mix",
    )(x, *os_, *ls_, four, gates, w_attn, w_four, w_out, g2, w_router_hi, w_router_lo)


def _route_kernel(afft_ref, su_ref, u_ref, tau_ref, need_ref, beq_ref, bsel_ref,
                  taut_ref, needt_ref, beqt_ref, bselt_ref, *, tokens):
    cap = CAPACITY_FACTOR * tokens // N_EXPERTS
    ntile = tokens // ROUTE_TILE
    shape = (N_EXPERTS, LANES)
    lane = lax.broadcasted_iota(I32, shape, 1)

    def keys(start, width):
        return lax.bitcast_convert_type(afft_ref[:, pl.ds(pl.multiple_of(start, LANES), width)], I32)

    span = min(tokens, 16 * LANES)

    def count(pred):
        def body(c, acc):
            hits = _ones_where(pred(keys(c * span, span)))
            for j in range(span // LANES):
                acc = acc + hits[:, j * LANES:(j + 1) * LANES]
            return acc
        acc = lax.fori_loop(0, tokens // span, body, jnp.zeros(shape, F32))
        return jnp.sum(acc, axis=1, keepdims=True)

    def bit_body(i, prefix):
        cand = prefix | lax.shift_left(jnp.ones(shape, I32), jnp.full(shape, 30 - i, I32))
        tot = count(lambda k: k >= cand[:, :1])
        return jnp.where(tot >= cap, cand, prefix)

    tau = lax.fori_loop(0, 31, bit_body, jnp.zeros(shape, I32))
    tau_col = tau[:, :1]
    n_gt = count(lambda k: k > tau_col)
    need = cap - n_gt

    def prefix_over_tiles(tab):
        return _dot(tab.astype(BF16), su_ref[...])

    def at_lane(tab, c):
        return jnp.sum(jnp.where(lane == c, tab, 0.0), axis=1, keepdims=True)

    def eq_body(c, tab):
        k = keys(c * ROUTE_TILE, ROUTE_TILE)
        cnt = jnp.sum(_ones_where(k == tau_col), axis=1, keepdims=True)
        return jnp.where(lane == c, cnt, tab)

    base_eq = prefix_over_tiles(lax.fori_loop(0, ntile, eq_body, jnp.zeros(shape, F32)))

    def sel_body(c, tab):
        k = keys(c * ROUTE_TILE, ROUTE_TILE)
        eq = k == tau_col
        eq_cum = _dot(_ones_where(eq, BF16), u_ref[...]) + at_lane(base_eq, c)
        sel = (k > tau_col) | (eq & (eq_cum <= need))
        cnt = jnp.sum(_ones_where(sel), axis=1, keepdims=True)
        return jnp.where(lane == c, cnt, tab)

    base_sel = prefix_over_tiles(lax.fori_loop(0, ntile, sel_body, jnp.zeros(shape, F32)))

    def transposed(val):
        return jnp.concatenate([val, jnp.zeros((LANES - N_EXPERTS, LANES), val.dtype)], axis=0).T

    tau_ref[...] = tau
    taut_ref[...] = transposed(tau)
    for val, ref, ref_t in ((jnp.broadcast_to(need, shape), need_ref, needt_ref),
                            (base_eq, beq_ref, beqt_ref), (base_sel, bsel_ref, bselt_ref)):
        ref[...] = val.astype(I32)
        ref_t[...] = transposed(val)


def _route(afft):
    tokens = afft.shape[1]
    idx = np.arange(LANES)
    su = jnp.asarray(idx[:, None] < idx[None, :], BF16)
    idx = np.arange(ROUTE_TILE)
    u = jnp.asarray(idx[:, None] <= idx[None, :], BF16)
    full = lambda a: pl.BlockSpec(a.shape, lambda i: (0,) * a.ndim)
    small = pl.BlockSpec((N_EXPERTS, LANES), lambda i: (0, 0))
    smallt = pl.BlockSpec((LANES, LANES), lambda i: (0, 0))
    return pl.pallas_call(
        functools.partial(_route_kernel, tokens=tokens),
        grid=(1,),
        in_specs=[full(afft), full(su), full(u)],
        out_specs=[small] * 4 + [smallt] * 4,
        out_shape=[jax.ShapeDtypeStruct((N_EXPERTS, LANES), I32)] * 4
        + [jax.ShapeDtypeStruct((LANES, LANES), I32)] + [jax.ShapeDtypeStruct((LANES, LANES), F32)] * 3,
        compiler_params=_params(("arbitrary",)),
        name="route",
    )(afft, su, u)


def _gather_kernel(bsel_s, afft_ref, tau_ref, need_ref, beq_ref, x_ref, u_ref, xe_hbm,
                   stage_ref, tail_ref, xbuf_ref, zeros_ref, sem_ref, xsem_ref, *, ntile, cap):
    t = pl.program_id(0)
    par = t % 2

    def aligned(e, tile):
        return pl.multiple_of((bsel_s[e, tile] // ROW_ALIGN) * ROW_ALIGN, ROW_ALIGN)

    def writes(tile, buf):
        return [pltpu.make_async_copy(stage_ref.at[buf, e], xe_hbm.at[e, pl.ds(aligned(e, tile), GATHER_BLOCK)],
                                      sem_ref.at[buf, e]) for e in range(N_EXPERTS)]

    @pl.when(t == 0)
    def _():
        tail_ref[...] = jnp.zeros_like(tail_ref)
        zeros_ref[...] = jnp.zeros_like(zeros_ref)
        fills = [pltpu.make_async_copy(zeros_ref, xe_hbm.at[e, pl.ds(cap, GATHER_PAD)], sem_ref.at[0, e])
                 for e in range(N_EXPERTS)]
        for cp in fills:
            cp.start()
        for cp in fills:
            cp.wait()

    k = lax.bitcast_convert_type(afft_ref[...], I32)
    tau = tau_ref[:, :1]
    lane = lax.broadcasted_iota(I32, (N_EXPERTS, LANES), 1)
    beq = jnp.sum(jnp.where(lane == t, beq_ref[...].astype(F32), 0.0), axis=1, keepdims=True)
    eq = k == tau
    eq_cum = _dot(_ones_where(eq, BF16), u_ref[...]) + beq
    sel = (k > tau) | (eq & (eq_cum <= need_ref[:, :1].astype(F32)))
    rank = jnp.where(sel, _dot(_ones_where(sel, BF16), u_ref[...]) - 1.0, -1e4)

    row = lax.broadcasted_iota(I32, (GATHER_STACK, ROUTE_TILE), 0)
    in_block = row < GATHER_BLOCK
    row_f = row.astype(F32)
    offs, shifts, pieces = [], [], []
    for e in range(N_EXPERTS):
        off = (bsel_s[e, t] - aligned(e, t)).astype(F32)
        shift = ((bsel_s[e, t + 1] // ROW_ALIGN) * ROW_ALIGN - aligned(e, t))
        target = jnp.where(in_block, row_f, row_f - float(GATHER_BLOCK) + shift.astype(F32))
        pieces.append(_ones_where(rank[e:e + 1, :] + off == target, BF16))
        offs.append(off)
        shifts.append(shift)
    res = _dot(jnp.concatenate(pieces, axis=0), x_ref[...])
    for e in range(N_EXPERTS):
        base = e * GATHER_STACK
        old = tail_ref[e]
        stage_ref[par, e, 0:ROW_ALIGN, :] = (res[base:base + ROW_ALIGN] + old).astype(BF16)
        stage_ref[par, e, ROW_ALIGN:GATHER_BLOCK, :] = res[base + ROW_ALIGN:base + GATHER_BLOCK].astype(BF16)
        tail_ref[e] = res[base + GATHER_BLOCK:base + GATHER_STACK] + jnp.where(shifts[e] == 0, old, 0.0)

    @pl.when(t > 0)
    def _():
        for cp in writes(t - 1, 1 - par):
            cp.wait()

    for cp in writes(t, par):
        cp.start()

    extra = [(jnp.maximum(bsel_s[e, t + 1] - aligned(e, t) - GATHER_BLOCK, 0) + SLOT_CHUNK - 1) // SLOT_CHUNK
             for e in range(N_EXPERTS)]

    @pl.when(functools.reduce(jnp.maximum, extra) > 0)
    def _():
        row64 = lax.broadcasted_iota(I32, (SLOT_CHUNK, ROUTE_TILE), 0).astype(F32)
        for e in range(N_EXPERTS):
            def chunk(c, carry):
                first = GATHER_BLOCK + c * SLOT_CHUNK
                onehot = _ones_where(rank[e:e + 1, :] + offs[e] == row64 + first.astype(F32), BF16)
                xbuf_ref[...] = _dot(onehot, x_ref[...]).astype(BF16)
                dst = pl.multiple_of(aligned(e, t) + first, ROW_ALIGN)
                cp = pltpu.make_async_copy(xbuf_ref, xe_hbm.at[e, pl.ds(dst, SLOT_CHUNK)], xsem_ref.at[0])
                cp.start()
                cp.wait()
                return carry

            lax.fori_loop(0, extra[e], chunk, 0)

    @pl.when(t == ntile - 1)
    def _():
        for cp in writes(t, par):
            cp.wait()


def _gather(bsel_i, afft, tau, need, beq_i, xn, u):
    tokens = xn.shape[0]
    cap = CAPACITY_FACTOR * tokens // N_EXPERTS
    ntile = tokens // ROUTE_TILE
    table = pl.BlockSpec((N_EXPERTS, LANES), lambda t, *_: (0, 0))
    grid_spec = pltpu.PrefetchScalarGridSpec(
        num_scalar_prefetch=1,
        grid=(ntile,),
        in_specs=[
            pl.BlockSpec((N_EXPERTS, ROUTE_TILE), lambda t, *_: (0, t)),
            table, table, table,
            pl.BlockSpec((ROUTE_TILE, D_MODEL), lambda t, *_: (t, 0)),
            pl.BlockSpec(u.shape, lambda t, *_: (0, 0)),
        ],
        out_specs=pl.BlockSpec(memory_space=pl.ANY),
        scratch_shapes=[
            pltpu.VMEM((2, N_EXPERTS, GATHER_BLOCK, D_MODEL), BF16),
            pltpu.VMEM((N_EXPERTS, ROW_ALIGN, D_MODEL), F32),
            pltpu.VMEM((SLOT_CHUNK, D_MODEL), BF16),
            pltpu.VMEM((GATHER_PAD, D_MODEL), BF16),
            pltpu.SemaphoreType.DMA((2, N_EXPERTS)),
            pltpu.SemaphoreType.DMA((1,)),
        ],
    )
    return pl.pallas_call(
        functools.partial(_gather_kernel, ntile=ntile, cap=cap),
        grid_spec=grid_spec,
        out_shape=jax.ShapeDtypeStruct((N_EXPERTS, cap + GATHER_PAD, D_MODEL), BF16),
        compiler_params=_params(("arbitrary",)),
        name="gather",
    )(bsel_i, afft, tau, need, beq_i, xn, u)


def _ffn_kernel(xe_ref, wg_ref, wu_ref, wd_ref, ye_ref, acc_ref, *, cap, nf, tm):
    f = pl.program_id(1)
    tf = wg_ref.shape[2]
    chunks = [slice(j * FFN_CHUNK, (j + 1) * FFN_CHUNK) for j in range(tf // FFN_CHUNK)]
    cast = {}

    def weight(name, ref, j):
        if (name, j) not in cast:
            cast[name, j] = (ref[0, chunks[j], :] if name == "d" else ref[0, :, chunks[j]]).astype(BF16)
        return cast[name, j]

    @pl.when(f == 0)
    def _():
        acc_ref[...] = jnp.zeros_like(acc_ref)

    for i in range(cap // tm):
        r = slice(i * tm, (i + 1) * tm)
        x = xe_ref[0, r, :]
        y = None
        for j in range(len(chunks)):
            hg = _dot(x, weight("g", wg_ref, j))
            hu = _dot(x, weight("u", wu_ref, j))
            h = (hg * (1.0 / (1.0 + jnp.exp(-hg))) * hu).astype(BF16)
            part = _dot(h, weight("d", wd_ref, j))
            y = part if y is None else y + part
        acc_ref[r, :] += y

    @pl.when(f == nf - 1)
    def _():
        ye_ref[...] = acc_ref[...].astype(BF16)


def _ffn(xe, w_eg, w_eu, w_ed):
    cap = xe.shape[1] - GATHER_PAD
    tf = 512
    nf = D_FF // tf
    tm = min(cap, 1024)
    return pl.pallas_call(
        functools.partial(_ffn_kernel, cap=cap, nf=nf, tm=tm),
        grid=(N_EXPERTS, nf),
        in_specs=[
            pl.BlockSpec((1, cap, D_MODEL), lambda e, f: (e, 0, 0)),
            pl.BlockSpec((1, D_MODEL, tf), lambda e, f: (e, 0, f)),
            pl.BlockSpec((1, D_MODEL, tf), lambda e, f: (e, 0, f)),
            pl.BlockSpec((1, tf, D_MODEL), lambda e, f: (e, f, 0)),
        ],
        out_specs=pl.BlockSpec((cap, D_MODEL), lambda e, f: (e, 0)),
        out_shape=jax.ShapeDtypeStruct((N_EXPERTS * cap, D_MODEL), BF16),
        scratch_shapes=[pltpu.VMEM((cap, D_MODEL), F32)],
        compiler_params=_params(("arbitrary", "arbitrary")),
        name="ffn",
    )(xe, w_eg, w_eu, w_ed)


def _combine_kernel(bsel_s, x1_ref, aff_ref, taut_ref, needt_ref, beqt_ref, bselt_ref, low_ref, spread_ref, gf_ref,
                    ye_hbm, y_ref, buf_ref, xbuf_ref, sem_ref, xsem_ref, *, cap, total, ntile):
    t = pl.program_id(0)
    par = t % 2

    def aligned(e, tile):
        return (bsel_s[e, tile] // 16) * 16

    def window(e, tile, c):
        start = jnp.minimum(e * cap + aligned(e, tile) + c * SLOT_CHUNK, total - SLOT_CHUNK)
        return pl.multiple_of(start, 16)

    def first_chunks(tile, buf):
        return [pltpu.make_async_copy(ye_hbm.at[pl.ds(window(e, tile, 0), SLOT_CHUNK)],
                                      buf_ref.at[buf, pl.ds(e * SLOT_CHUNK, SLOT_CHUNK)], sem_ref.at[buf, e])
                for e in range(N_EXPERTS)]

    @pl.when(t == 0)
    def _():
        for cp in first_chunks(0, 0):
            cp.start()

    @pl.when(t + 1 < ntile)
    def _():
        for cp in first_chunks(t + 1, 1 - par):
            cp.start()

    aff = aff_ref[...]
    k = lax.bitcast_convert_type(aff, I32)
    tau = taut_ref[0:1, :]
    low = low_ref[...]
    eq = k == tau
    eq_cum = _dot(low, _ones_where(eq, BF16)) + beqt_ref[0]
    sel = (k > tau) | (eq & (eq_cum <= needt_ref[0:1, :]))
    slot = jnp.where(sel, _dot(low, _ones_where(sel, BF16)) + (bselt_ref[0] - 1.0), -1.0)

    lane = lax.broadcasted_iota(I32, (1, LANES), 1)
    rel = jnp.zeros((1, LANES), F32)
    for e in range(N_EXPERTS):
        rel = jnp.where(lane == e, (window(e, t, 0) - e * cap).astype(F32), rel)
    d = slot - rel
    d = jnp.where(sel & (d >= 0.0) & (d < float(SLOT_CHUNK)), d, -1.0)
    spread = spread_ref[...]
    wide = lax.broadcasted_iota(I32, (ROUTE_TILE, N_EXPERTS * SLOT_CHUNK), 1)
    hit = _dot(d.astype(BF16), spread) == (wide % SLOT_CHUNK).astype(F32)
    onehot_gate = jnp.where(hit, _dot(aff.astype(BF16), spread), 0.0).astype(BF16)
    for cp in first_chunks(t, par):
        cp.wait()
    y_ref[...] = x1_ref[...] + _dot(onehot_gate, buf_ref[par])

    nch = [(bsel_s[e, t + 1] - aligned(e, t) + SLOT_CHUNK - 1) // SLOT_CHUNK for e in range(N_EXPERTS)]

    @pl.when(functools.reduce(jnp.maximum, nch) > 1)
    def _():
        lane64 = lax.broadcasted_iota(I32, (ROUTE_TILE, SLOT_CHUNK), 1).astype(F32)
        for e in range(N_EXPERTS):
            slot_e = slot[:, e:e + 1]

            def extra(c, carry):
                w = window(e, t, c)
                cp = pltpu.make_async_copy(ye_hbm.at[pl.ds(w, SLOT_CHUNK)], xbuf_ref, xsem_ref.at[0])
                cp.start()
                cp.wait()
                first = (aligned(e, t) + c * SLOT_CHUNK).astype(F32)
                hit = (lane64 + (w - e * cap).astype(F32) == slot_e) & (slot_e >= first)
                y_ref[...] += aff[:, e:e + 1] * _dot(_ones_where(hit, BF16), xbuf_ref[...])
                return carry

            lax.fori_loop(1, nch[e], extra, 0)

    acc = y_ref[...]
    ms = jnp.mean(acc * acc, axis=-1, keepdims=True)
    y_ref[...] = acc * lax.rsqrt(ms + EPS) * gf_ref[...]


def _combine(bsel_i, x1, aff, tables_t, ye, gf):
    tokens = x1.shape[0]
    cap = CAPACITY_FACTOR * tokens // N_EXPERTS
    ntile = tokens // ROUTE_TILE
    idx = np.arange(ROUTE_TILE)
    low = jnp.asarray(idx[:, None] >= idx[None, :], BF16)
    spread = jnp.asarray(np.arange(LANES)[:, None] == np.arange(N_EXPERTS * SLOT_CHUNK)[None, :] // SLOT_CHUNK, BF16)
    taut, needt, beqt, bselt = tables_t
    rowvec = pl.BlockSpec((8, LANES), lambda t, *_: (0, 0))
    tilevec = pl.BlockSpec((1, 1, LANES), lambda t, *_: (t, 0, 0))
    grid_spec = pltpu.PrefetchScalarGridSpec(
        num_scalar_prefetch=1,
        grid=(ntile,),
        in_specs=[
            pl.BlockSpec((ROUTE_TILE, D_MODEL), lambda t, *_: (t, 0)),
            pl.BlockSpec((ROUTE_TILE, LANES), lambda t, *_: (t, 0)),
            rowvec, rowvec, tilevec, tilevec,
            pl.BlockSpec(low.shape, lambda t, *_: (0, 0)),
            pl.BlockSpec(spread.shape, lambda t, *_: (0, 0)),
            pl.BlockSpec((1, D_MODEL), lambda t, *_: (0, 0)),
            pl.BlockSpec(memory_space=pl.ANY),
        ],
        out_specs=pl.BlockSpec((ROUTE_TILE, D_MODEL), lambda t, *_: (t, 0)),
        scratch_shapes=[
            pltpu.VMEM((2, N_EXPERTS * SLOT_CHUNK, D_MODEL), BF16),
            pltpu.VMEM((SLOT_CHUNK, D_MODEL), BF16),
            pltpu.SemaphoreType.DMA((2, N_EXPERTS)),
            pltpu.SemaphoreType.DMA((1,)),
        ],
    )
    return pl.pallas_call(
        functools.partial(_combine_kernel, cap=cap, total=N_EXPERTS * cap, ntile=ntile),
        grid_spec=grid_spec,
        out_shape=jax.ShapeDtypeStruct((tokens, D_MODEL), F32),
        compiler_params=_params(("arbitrary",)),
        name="combine",
    )(bsel_i, x1, aff, taut, needt, beqt.reshape(LANES, 1, LANES), bselt.reshape(LANES, 1, LANES), low, spread,
      gf, ye)


def _encoder(x, w):
    batch, seq, _ = x.shape
    tokens = batch * seq
    xt = x.reshape(tokens, D_MODEL)
    *qkvs, vr, vi, gates = _in_proj(xt, w["g1"], w["w_in"], w["w_gate"], w["b_gate"], w["cs"], batch, seq)
    outs, lses = [], []
    for g in range(N_GROUPS):
        o, lse = _attention(qkvs[g], w["bias"][g], g)
        outs.append(o)
        lses.append(lse)
    four = _fourier(vr, vi, batch, seq)
    x1, xn, aff, afft = _mix(xt, outs, lses, four, gates, w["w_attn"], w["w_four"], w["w_out"], w["g2"],
                             w["w_router_hi"], w["w_router_lo"])
    tau, need, beq_i, bsel_i, taut, needt, beqt, bselt = _route(afft)
    idx = np.arange(ROUTE_TILE)
    u = jnp.asarray(idx[:, None] <= idx[None, :], BF16)
    xe = _gather(bsel_i, afft, tau, need, beq_i, xn, u)
    ye = _ffn(xe, w["w_eg"], w["w_eu"], w["w_ed"])
    y = _combine(bsel_i, x1, aff, (taut, needt, beqt, bselt), ye, w["gf"])
    return y.reshape(batch, seq, D_MODEL)


def _prepare_weights(rel_bias, norm1_g, w_in, w_attn_br, w_four_br, w_gate, b_gate, w_out,
                     norm2_g, w_router, w_exp_gate, w_exp_up, w_exp_down, final_g):
    c, s = _dft_mats(F_CH)
    starts = [part * ATT_W + g * GROUP_W for g in range(N_GROUPS) for part in range(3)]
    w_in_grouped = jnp.concatenate([w_in[0][:, s0:s0 + GROUP_W] for s0 in starts] + [w_in[0][:, QKV_W:]], axis=1)
    w_router = jnp.pad(w_router[0], ((0, 0), (0, LANES - N_EXPERTS)))
    w_router_hi = w_router.astype(BF16)
    return {
        "g1": norm1_g[0].reshape(1, D_MODEL),
        "w_in": w_in_grouped.astype(BF16),
        "w_gate": w_gate[0].astype(BF16),
        "b_gate": b_gate[0].reshape(1, 2 * D_MODEL),
        "cs": jnp.asarray(np.concatenate([c, s], axis=1), BF16),
        "bias": [_attention_bias(rel_bias, g) for g in range(N_GROUPS)],
        "w_attn": w_attn_br[0].astype(BF16),
        "w_four": w_four_br[0].astype(BF16),
        "w_out": w_out[0].astype(BF16),
        "g2": norm2_g[0].reshape(1, D_MODEL),
        "w_router_hi": w_router_hi,
        "w_router_lo": (w_router - w_router_hi.astype(F32)).astype(BF16),
        "w_eg": w_exp_gate[0],
        "w_eu": w_exp_up[0],
        "w_ed": w_exp_down[0],
        "gf": final_g.reshape(1, D_MODEL),
    }


def kernel(x_prompt, x_sample, rel_bias, norm1_g, w_in, w_attn_br, w_four_br, w_gate, b_gate, w_out,
           norm2_g, w_router, w_exp_gate, w_exp_up, w_exp_down, final_g):
    w = _prepare_weights(rel_bias, norm1_g, w_in, w_attn_br, w_four_br, w_gate, b_gate, w_out,
                         norm2_g, w_router, w_exp_gate, w_exp_up, w_exp_down, final_g)
    return (_encoder(x_prompt, w), _encoder(x_sample, w))
```

```python
import functools
import math

import numpy as np
import jax
import jax.numpy as jnp
from jax import lax
from jax.experimental import pallas as pl
from jax.experimental.pallas import tpu as pltpu

D_MODEL = 1024
HEAD_DIM = 64
HEADS_PER_GROUP = 4
GROUPS = ((128, 1), (512, 4), (2048, 16))
N_GROUPS = len(GROUPS)
GROUP_W = HEADS_PER_GROUP * HEAD_DIM
ATT_W = N_GROUPS * GROUP_W
QKV_W = 3 * ATT_W
F_GROUPS = 6
F_CH = 128
F_W = F_GROUPS * F_CH
NUM_BUCKETS = 32
MAX_DISTANCE = 1024
N_EXPERTS = 16
CAPACITY_FACTOR = 2
D_FF = 2048
EPS = 1e-6
NEG = -1e30

HALF_KEYS = 64
ATT_SUB = 128
ATT_OUT_ROWS = 8192
TOKEN_TILE = 512
ROUTE_TILE = 256
FFN_CHUNK = 256
SLOT_CHUNK = 64
ROW_ALIGN = 16
GATHER_BLOCK = SLOT_CHUNK + ROW_ALIGN
GATHER_STACK = GATHER_BLOCK + ROW_ALIGN
GATHER_PAD = GATHER_BLOCK
LANES = 128
V7X_VMEM_LIMIT = 56 * 1024 * 1024

F32 = jnp.float32
BF16 = jnp.bfloat16
I32 = jnp.int32


def _params(sem):
    return pltpu.CompilerParams(dimension_semantics=sem, vmem_limit_bytes=V7X_VMEM_LIMIT)


def _dot(a, b):
    return jnp.dot(a, b, preferred_element_type=F32)


def _dot_nt(a, b):
    return lax.dot_general(a, b, (((1,), (1,)), ((), ())), preferred_element_type=F32)


def _floor_pow2(x, m):
    return x & ~(m - 1)


def _cdiv_pow2(x, m):
    return (x + (m - 1)) >> (m.bit_length() - 1)


def _ones_where(mask, dtype=F32):
    return jnp.where(mask, jnp.ones((), F32), jnp.zeros((), F32)).astype(dtype)


def _in_proj_kernel(x_ref, g_ref, win_ref, wg_ref, bg_ref, cs_ref, qscale_ref, qkv0_ref, qkv1_ref, qkv2_ref,
                    vr_ref, vi_ref, gates_ref, slab_ref):
    tm = x_ref.shape[0]
    half = tm // 2
    nslab = ATT_W // LANES
    cs = cs_ref[...]
    for h in range(2):
        rows = slice(h * half, (h + 1) * half)
        x = x_ref[rows, :]
        ms = jnp.mean(x * x, axis=-1, keepdims=True)
        xn = (x * lax.rsqrt(ms + EPS) * g_ref[...]).astype(BF16)
        for g, out_ref in enumerate((qkv0_ref, qkv1_ref, qkv2_ref)):
            dil = GROUPS[g][1]
            res = _dot(xn, win_ref[:, g * ATT_W:(g + 1) * ATT_W]) * qscale_ref[...]
            if dil == 1:
                out_ref[0, 0, rows, :] = res.astype(BF16)
                continue
            for j in range(nslab):
                slab_ref[j, rows, :] = res[:, j * LANES:(j + 1) * LANES]
            n = half // dil
            for r in range(dil):
                cls = [slab_ref[j, pl.ds(h * half + r, n, stride=dil), :] for j in range(nslab)]
                out_ref[0, r, h * n:(h + 1) * n, :] = jnp.concatenate(cls, axis=1).astype(BF16)
        u = _dot(xn, win_ref[:, QKV_W:QKV_W + F_W]).astype(BF16)
        for g in range(F_GROUPS):
            a = _dot(u[:, g * F_CH:(g + 1) * F_CH], cs)
            vr_ref[rows, g * F_CH:(g + 1) * F_CH] = a[:, :F_CH].astype(BF16)
            vi_ref[rows, g * F_CH:(g + 1) * F_CH] = (-a[:, F_CH:]).astype(BF16)
        z = _dot(xn, wg_ref[...]) + bg_ref[...]
        gates_ref[rows, :] = (1.0 / (1.0 + jnp.exp(-z))).astype(BF16)


def _class_major_spec(tm, dil, width, per_batch):
    return pl.BlockSpec((1, dil, tm // dil, width), lambda i: (i // per_batch, 0, i % per_batch, 0))


def _in_proj(x, g1, w_in, w_gate, b_gate, cs, batch, seq):
    t = x.shape[0]
    tm = TOKEN_TILE
    per_batch = seq // tm
    const = lambda i: (0, 0)
    row = lambda i: (i, 0)
    qscale = np.ones((1, ATT_W), np.float32)
    qscale[:, :GROUP_W] = 1.0 / math.sqrt(HEAD_DIM)
    return pl.pallas_call(
        _in_proj_kernel,
        grid=(t // tm,),
        in_specs=[
            pl.BlockSpec((tm, D_MODEL), row),
            pl.BlockSpec((1, D_MODEL), const),
            pl.BlockSpec(w_in.shape, const),
            pl.BlockSpec(w_gate.shape, const),
            pl.BlockSpec((1, 2 * D_MODEL), const),
            pl.BlockSpec(cs.shape, const),
            pl.BlockSpec((1, ATT_W), const),
        ],
        out_specs=[_class_major_spec(tm, dil, ATT_W, per_batch) for _, dil in GROUPS] + [
            pl.BlockSpec((tm, F_W), row),
            pl.BlockSpec((tm, F_W), row),
            pl.BlockSpec((tm, 2 * D_MODEL), row),
        ],
        out_shape=[jax.ShapeDtypeStruct((batch, dil, seq // dil, ATT_W), BF16) for _, dil in GROUPS] + [
            jax.ShapeDtypeStruct((t, F_W), BF16),
            jax.ShapeDtypeStruct((t, F_W), BF16),
            jax.ShapeDtypeStruct((t, 2 * D_MODEL), BF16),
        ],
        scratch_shapes=[pltpu.VMEM((ATT_W // LANES, tm, LANES), F32)],
        compiler_params=_params(("parallel",)),
        name="in_proj",
    )(x, g1, w_in, w_gate, b_gate, cs, jnp.asarray(qscale))


def _attention_kernel(q_ref, kp_ref, kc_ref, kn_ref, vp_ref, vc_ref, vn_ref, bias_ref, o_ref, lse_ref, *,
                      tq, length, dil, rc):
    i = pl.program_id(1)
    win = ATT_SUB + 2 * HALF_KEYS
    lane_head = lax.broadcasted_iota(I32, (ATT_SUB, GROUP_W), 1) // HEAD_DIM
    col = lax.broadcasted_iota(I32, (ATT_SUB, win), 1)
    for c, sb in [(c, sb) for c in range(rc) for sb in range(tq // ATT_SUB)]:
        r = pl.program_id(2) * rc + c
        if sb == 0:
            kwin = jnp.concatenate([kp_ref[0, c], kc_ref[0, c], kn_ref[0, c]], axis=0)
            vwin = jnp.concatenate([vp_ref[0, c], vc_ref[0, c], vn_ref[0, c]], axis=0)
        off = sb * ATT_SUB
        q = q_ref[0, c, off:off + ATT_SUB, :]
        kw = kwin[off:off + win]
        vw = vwin[off:off + win]
        first = i * tq + (off - HALF_KEYS)
        valid = (col >= -first) & (col < length - first)
        qs = jnp.concatenate(
            [jnp.where(lane_head == h, q, jnp.zeros_like(q)) for h in range(HEADS_PER_GROUP)], axis=0)
        s_all = _dot_nt(qs, kw)
        ps, ms, ls = [], [], []
        for h in range(HEADS_PER_GROUP):
            s = s_all[h * ATT_SUB:(h + 1) * ATT_SUB] + bias_ref[h]
            s = jnp.where(valid, s, NEG)
            m = jnp.max(s, axis=-1, keepdims=True)
            p = jnp.exp(s - m)
            ls.append(jnp.sum(p, axis=-1, keepdims=True))
            ms.append(m)
            ps.append(p.astype(BF16))
        o_all = _dot(jnp.concatenate(ps, axis=0), vw)
        out = jnp.zeros((ATT_SUB, GROUP_W), F32)
        lse = jnp.zeros((ATT_SUB, GROUP_W), F32)
        for h in range(HEADS_PER_GROUP):
            oh = o_all[h * ATT_SUB:(h + 1) * ATT_SUB] * (1.0 / ls[h])
            out = jnp.where(lane_head == h, oh, out)
            lse = jnp.where(lane_head == h, ms[h] + jnp.log(ls[h]), lse)
        rows = pl.ds(off * dil + r, ATT_SUB, stride=dil) if dil > 1 else pl.ds(off, ATT_SUB)
        for j in range(GROUP_W // LANES):
            o_ref[j, rows, :] = out[:, j * LANES:(j + 1) * LANES]
            lse_ref[j, rows, :] = lse[:, j * LANES:(j + 1) * LANES]


def _attention(qkv, bias, g):
    batch, dil, length, _ = qkv.shape
    tq = min(length, 512, ATT_OUT_ROWS // dil)
    nb = length // tq
    hb = tq // HALF_KEYS
    last_halo = length // HALF_KEYS - 1
    rc = min(dil, max(1, 512 // tq))

    def cur(c):
        return lambda b, i, r: (b, r, i, c)

    def prev(c):
        return lambda b, i, r: (b, r, jnp.maximum(i * hb - 1, 0), c)

    def nxt(c):
        return lambda b, i, r: (b, r, jnp.minimum((i + 1) * hb, last_halo), c)

    blk = lambda rows: (1, rc, rows, GROUP_W)
    out_spec = pl.BlockSpec((GROUP_W // LANES, tq * dil, LANES), lambda b, i, r: (0, b * nb + i, 0))
    return pl.pallas_call(
        functools.partial(_attention_kernel, tq=tq, length=length, dil=dil, rc=rc),
        grid=(batch, nb, dil // rc),
        in_specs=[
            pl.BlockSpec(blk(tq), cur(0)),
            pl.BlockSpec(blk(HALF_KEYS), prev(1)),
            pl.BlockSpec(blk(tq), cur(1)),
            pl.BlockSpec(blk(HALF_KEYS), nxt(1)),
            pl.BlockSpec(blk(HALF_KEYS), prev(2)),
            pl.BlockSpec(blk(tq), cur(2)),
            pl.BlockSpec(blk(HALF_KEYS), nxt(2)),
            pl.BlockSpec(bias.shape, lambda b, i, r: (0, 0, 0)),
        ],
        out_specs=[out_spec] * 2,
        out_shape=[jax.ShapeDtypeStruct((GROUP_W // LANES, batch * dil * length, LANES), F32)] * 2,
        compiler_params=_params(("parallel", "parallel", "arbitrary")),
        name=f"attention_g{g}",
    )(qkv, qkv, qkv, qkv, qkv, qkv, qkv, bias)


def _t5_bucket(rel):
    nb = NUM_BUCKETS // 2
    max_exact = nb // 2
    ret = (rel > 0).astype(np.int32) * nb
    n = np.abs(rel)
    large = max_exact + (np.log(np.maximum(n, max_exact) / max_exact)
                         / np.log(MAX_DISTANCE / max_exact) * (nb - max_exact)).astype(np.int32)
    large = np.minimum(large, nb - 1)
    return (ret + np.where(n < max_exact, n, large)).astype(np.int32)


def _attention_bias(rel_bias, g):
    dil = GROUPS[g][1]
    qi = np.arange(ATT_SUB)[:, None]
    kj = np.arange(ATT_SUB + 2 * HALF_KEYS)[None, :]
    delta = kj - HALF_KEYS - qi
    band = np.abs(delta) <= HALF_KEYS
    bucket = _t5_bucket(dil * delta)
    tab = rel_bias[:, g * HEADS_PER_GROUP:(g + 1) * HEADS_PER_GROUP].astype(F32)
    onehot = jnp.asarray(bucket[..., None] == np.arange(NUM_BUCKETS), F32)
    bias = jnp.einsum("qkb,bh->hqk", onehot, tab, precision=lax.Precision.HIGHEST)
    return jnp.where(jnp.asarray(band)[None], bias, NEG)


def _dft_mats(n):
    k = np.arange(n)
    ang = 2.0 * np.pi * ((k[:, None] * k[None, :]) % n) / n
    return np.cos(ang), np.sin(ang)


def _fft_stage1_kernel(vr_ref, vi_ref, m1_ref, twc_ref, tws_ref, zr_ref, zi_ref, *, n1, m):
    x = jnp.concatenate([vr_ref[0], vi_ref[0]], axis=0)
    z = _dot(m1_ref[...], x)
    zr, zi = z[:n1], z[n1:]
    twc, tws = twc_ref[0], tws_ref[0]
    for j in range(m):
        c = twc[:, j:j + 1]
        s = tws[:, j:j + 1]
        a = zr[:, j * F_W:(j + 1) * F_W]
        b = zi[:, j * F_W:(j + 1) * F_W]
        zr_ref[0, :, j * F_W:(j + 1) * F_W] = (a * c + b * s).astype(BF16)
        zi_ref[0, :, j * F_W:(j + 1) * F_W] = (b * c - a * s).astype(BF16)


def _fft_stage2_kernel(zr_ref, zi_ref, m2_ref, o_ref, *, kc, scale):
    m2 = m2_ref[...]
    for j in range(kc):
        x = jnp.concatenate([zr_ref[0, j], zi_ref[0, j]], axis=0)
        o_ref[0, :, j * F_W:(j + 1) * F_W] = (_dot(m2, x) * scale).astype(BF16)


def _fourier(vr, vi, batch, seq):
    n2 = LANES
    n1 = seq // n2
    m = 8
    c1, s1 = _dft_mats(n1)
    m1 = jnp.asarray(np.block([[c1, s1], [-s1, c1]]), BF16)
    c2, s2 = _dft_mats(n2)
    m2 = jnp.asarray(np.concatenate([c2, s2], axis=1), BF16)
    k1 = np.arange(n1)[:, None]
    sv = np.arange(n2)[None, :]
    ang = 2.0 * np.pi * ((k1 * sv) % seq) / seq
    to_blocks = lambda a: jnp.asarray(a.reshape(n1, n2 // m, m).transpose(1, 0, 2), F32)
    twc, tws = to_blocks(np.cos(ang)), to_blocks(np.sin(ang))

    v3 = lambda a: a.reshape(batch, n1, n2 * F_W)
    blk = (1, n1, m * F_W)
    dmap = lambda b, j: (b, 0, j)
    tmap = lambda b, j: (j, 0, 0)
    zr, zi = pl.pallas_call(
        functools.partial(_fft_stage1_kernel, n1=n1, m=m),
        grid=(batch, n2 // m),
        in_specs=[
            pl.BlockSpec(blk, dmap),
            pl.BlockSpec(blk, dmap),
            pl.BlockSpec(m1.shape, lambda b, j: (0, 0)),
            pl.BlockSpec((1, n1, m), tmap),
            pl.BlockSpec((1, n1, m), tmap),
        ],
        out_specs=[pl.BlockSpec(blk, dmap), pl.BlockSpec(blk, dmap)],
        out_shape=[jax.ShapeDtypeStruct((batch, n1, n2 * F_W), BF16)] * 2,
        compiler_params=_params(("parallel", "parallel")),
        name="fft_stage1",
    )(v3(vr), v3(vi), m1, twc, tws)

    kc = 8
    v4 = lambda a: a.reshape(batch, n1, n2, F_W)
    zblk = (1, kc, n2, F_W)
    zmap = lambda b, j: (b, j, 0, 0)
    out = pl.pallas_call(
        functools.partial(_fft_stage2_kernel, kc=kc, scale=1.0 / math.sqrt(seq * F_CH)),
        grid=(batch, n1 // kc),
        in_specs=[
            pl.BlockSpec(zblk, zmap),
            pl.BlockSpec(zblk, zmap),
            pl.BlockSpec(m2.shape, lambda b, j: (0, 0)),
        ],
        out_specs=pl.BlockSpec((1, n2, kc * F_W), lambda b, j: (b, 0, j)),
        out_shape=jax.ShapeDtypeStruct((batch, n2, n1 * F_W), BF16),
        compiler_params=_params(("parallel", "parallel")),
        name="fft_stage2",
    )(v4(zr), v4(zi), m2)
    return out.reshape(batch * seq, F_W)


def _mix_kernel(x_ref, o0_ref, o1_ref, o2_ref, l0_ref, l1_ref, l2_ref, four_ref, gates_ref,
                wa_ref, wf_ref, wo_ref, g2_ref, wrh_ref, wrl_ref, x1_ref, xn_ref, aff_ref, afft_ref):
    tm = x_ref.shape[0]
    half = tm // 2
    for rows in (slice(0, half), slice(half, tm)):
        def slabs(ref):
            return jnp.concatenate([ref[j, rows, :] for j in range(GROUP_W // LANES)], axis=1)

        f_br = _dot(four_ref[rows, :], wf_ref[...])
        l0, l1, l2 = slabs(l0_ref), slabs(l1_ref), slabs(l2_ref)
        mx = jnp.maximum(jnp.maximum(l0, l1), l2)
        e0, e1, e2 = jnp.exp(l0 - mx), jnp.exp(l1 - mx), jnp.exp(l2 - mx)
        att = (e0 * slabs(o0_ref) + e1 * slabs(o1_ref) + e2 * slabs(o2_ref)) * (1.0 / (e0 + e1 + e2))
        a_br = _dot(att.astype(BF16), wa_ref[...])
        mix = gates_ref[rows, :D_MODEL] * a_br + gates_ref[rows, D_MODEL:] * f_br
        x1 = x_ref[rows, :] + _dot(mix.astype(BF16), wo_ref[...])
        x1_ref[rows, :] = x1
        ms = jnp.mean(x1 * x1, axis=-1, keepdims=True)
        xn = x1 * lax.rsqrt(ms + EPS) * g2_ref[...]
        xn_ref[rows, :] = xn.astype(BF16)
        xh = xn.astype(BF16)
        xl = (xn - xh.astype(F32)).astype(BF16)
        logits = _dot(xh, wrh_ref[...]) + (_dot(xh, wrl_ref[...]) + _dot(xl, wrh_ref[...]))
        lane = lax.broadcasted_iota(I32, logits.shape, 1)
        logits = jnp.where(lane < N_EXPERTS, logits, NEG)
        p = jnp.exp(logits - jnp.max(logits, axis=-1, keepdims=True))
        aff = p * (1.0 / jnp.sum(p, axis=-1, keepdims=True))
        aff_ref[rows, :] = aff
        afft_ref[:, rows] = aff.T[:N_EXPERTS]


def _mix(x, os_, ls_, four, gates, w_attn, w_four, w_out, g2, w_router_hi, w_router_lo):
    t = x.shape[0]
    tm = TOKEN_TILE
    const = lambda i: (0, 0)
    row = lambda i: (i, 0)
    rows = lambda w: pl.BlockSpec((tm, w), row)
    full = lambda a: pl.BlockSpec(a.shape, const)
    slab = pl.BlockSpec((GROUP_W // LANES, tm, LANES), lambda i: (0, i, 0))
    return pl.pallas_call(
        _mix_kernel,
        grid=(t // tm,),
        in_specs=[rows(D_MODEL)] + [slab] * 6 + [rows(F_W), rows(2 * D_MODEL),
                  full(w_attn), full(w_four), full(w_out), full(g2), full(w_router_hi), full(w_router_lo)],
        out_specs=[rows(D_MODEL), rows(D_MODEL), rows(LANES), pl.BlockSpec((N_EXPERTS, tm), lambda i: (0, i))],
        out_shape=[
            jax.ShapeDtypeStruct((t, D_MODEL), F32),
            jax.ShapeDtypeStruct((t, D_MODEL), BF16),
            jax.ShapeDtypeStruct((t, LANES), F32),
            jax.ShapeDtypeStruct((N_EXPERTS, t), F32),
        ],
        compiler_params=_params(("parallel",)),
        name="mix",
    )(x, *os_, *ls_, four, gates, w_attn, w_four, w_out, g2, w_router_hi, w_router_lo)


def _route_kernel(afft_ref, su_ref, u_ref, tau_ref, need_ref, beq_ref, bsel_ref,
                  taut_ref, needt_ref, beqt_ref, bselt_ref, *, tokens):
    cap = CAPACITY_FACTOR * tokens // N_EXPERTS
    ntile = tokens // ROUTE_TILE
    shape = (N_EXPERTS, LANES)
    lane = lax.broadcasted_iota(I32, shape, 1)

    def keys(start, width):
        return lax.bitcast_convert_type(afft_ref[:, pl.ds(pl.multiple_of(start, LANES), width)], I32)

    span = min(tokens, 16 * LANES)

    def count(pred):
        def body(c, acc):
            hits = _ones_where(pred(keys(c * span, span)))
            for j in range(span // LANES):
                acc = acc + hits[:, j * LANES:(j + 1) * LANES]
            return acc
        acc = lax.fori_loop(0, tokens // span, body, jnp.zeros(shape, F32))
        return jnp.sum(acc, axis=1, keepdims=True)

    def bit_body(i, prefix):
        cand = prefix | lax.shift_left(jnp.ones(shape, I32), jnp.full(shape, 30 - i, I32))
        tot = count(lambda k: k >= cand[:, :1])
        return jnp.where(tot >= cap, cand, prefix)

    tau = lax.fori_loop(0, 31, bit_body, jnp.zeros(shape, I32))
    tau_col = tau[:, :1]
    n_gt = count(lambda k: k > tau_col)
    need = cap - n_gt

    def prefix_over_tiles(tab):
        return _dot(tab.astype(BF16), su_ref[...])

    def at_lane(tab, c):
        return jnp.sum(jnp.where(lane == c, tab, 0.0), axis=1, keepdims=True)

    def eq_body(c, tab):
        k = keys(c * ROUTE_TILE, ROUTE_TILE)
        cnt = jnp.sum(_ones_where(k == tau_col), axis=1, keepdims=True)
        return jnp.where(lane == c, cnt, tab)

    base_eq = prefix_over_tiles(lax.fori_loop(0, ntile, eq_body, jnp.zeros(shape, F32)))

    def sel_body(c, tab):
        k = keys(c * ROUTE_TILE, ROUTE_TILE)
        eq = k == tau_col
        eq_cum = _dot(_ones_where(eq, BF16), u_ref[...]) + at_lane(base_eq, c)
        sel = (k > tau_col) | (eq & (eq_cum <= need))
        cnt = jnp.sum(_ones_where(sel), axis=1, keepdims=True)
        return jnp.where(lane == c, cnt, tab)

    base_sel = prefix_over_tiles(lax.fori_loop(0, ntile, sel_body, jnp.zeros(shape, F32)))

    def transposed(val):
        return jnp.concatenate([val, jnp.zeros((LANES - N_EXPERTS, LANES), val.dtype)], axis=0).T

    tau_ref[...] = tau
    taut_ref[...] = transposed(tau)
    for val, ref, ref_t in ((jnp.broadcast_to(need, shape), need_ref, needt_ref),
                            (base_eq, beq_ref, beqt_ref), (base_sel, bsel_ref, bselt_ref)):
        ref[...] = val.astype(I32)
        ref_t[...] = transposed(val)


def _route(afft):
    tokens = afft.shape[1]
    idx = np.arange(LANES)
    su = jnp.asarray(idx[:, None] < idx[None, :], BF16)
    idx = np.arange(ROUTE_TILE)
    u = jnp.asarray(idx[:, None] <= idx[None, :], BF16)
    full = lambda a: pl.BlockSpec(a.shape, lambda i: (0,) * a.ndim)
    small = pl.BlockSpec((N_EXPERTS, LANES), lambda i: (0, 0))
    smallt = pl.BlockSpec((LANES, LANES), lambda i: (0, 0))
    return pl.pallas_call(
        functools.partial(_route_kernel, tokens=tokens),
        grid=(1,),
        in_specs=[full(afft), full(su), full(u)],
        out_specs=[small] * 4 + [smallt] * 4,
        out_shape=[jax.ShapeDtypeStruct((N_EXPERTS, LANES), I32)] * 4
        + [jax.ShapeDtypeStruct((LANES, LANES), I32)] + [jax.ShapeDtypeStruct((LANES, LANES), F32)] * 3,
        compiler_params=_params(("arbitrary",)),
        name="route",
    )(afft, su, u)


def _gather_kernel(bsel_s, afft_ref, tau_ref, need_ref, beq_ref, x_ref, u_ref, xe_hbm,
                   stage_ref, tail_ref, xbuf_ref, sem_ref, xsem_ref, *, ntile, cap):
    t = pl.program_id(0)
    par = t & 1

    def aligned(e, tile):
        return pl.multiple_of(_floor_pow2(bsel_s[e, tile], ROW_ALIGN), ROW_ALIGN)

    def write(e, tile, buf, first_row=None):
        first_row = aligned(e, tile) if first_row is None else first_row
        return pltpu.make_async_copy(stage_ref.at[buf, e], xe_hbm.at[e, pl.ds(first_row, GATHER_BLOCK)],
                                     sem_ref.at[buf, e])

    @pl.when(t == 0)
    def _():
        tail_ref[...] = jnp.zeros_like(tail_ref)
        stage_ref[1] = jnp.zeros(stage_ref.shape[1:], BF16)
        for e in range(N_EXPERTS):
            write(e, 0, 1, first_row=cap).start()

    k = lax.bitcast_convert_type(afft_ref[...], I32)
    tau = tau_ref[:, :1]
    lane = lax.broadcasted_iota(I32, (N_EXPERTS, LANES), 1)
    beq = jnp.sum(jnp.where(lane == t, beq_ref[...].astype(F32), 0.0), axis=1, keepdims=True)
    eq = k == tau
    eq_cum = _dot(_ones_where(eq, BF16), u_ref[...]) + beq
    sel = (k > tau) | (eq & (eq_cum <= need_ref[:, :1].astype(F32)))
    rank = jnp.where(sel, _dot(_ones_where(sel, BF16), u_ref[...]) - 1.0, -1e4)

    row = lax.broadcasted_iota(I32, (GATHER_STACK, ROUTE_TILE), 0)
    in_block = row < GATHER_BLOCK
    row_f = row.astype(F32)
    offs, shifts, pieces = [], [], []
    for e in range(N_EXPERTS):
        off = (bsel_s[e, t] - aligned(e, t)).astype(F32)
        shift = _floor_pow2(bsel_s[e, t + 1], ROW_ALIGN) - aligned(e, t)
        target = jnp.where(in_block, row_f, row_f - float(GATHER_BLOCK) + shift.astype(F32))
        pieces.append(_ones_where(rank[e:e + 1, :] + off == target, BF16))
        offs.append(off)
        shifts.append(shift)
    res = _dot(jnp.concatenate(pieces, axis=0), x_ref[...])
    for e in range(N_EXPERTS):
        base = e * GATHER_STACK
        old = tail_ref[e]
        stage_ref[par, e, 0:ROW_ALIGN, :] = (res[base:base + ROW_ALIGN] + old).astype(BF16)
        stage_ref[par, e, ROW_ALIGN:GATHER_BLOCK, :] = res[base + ROW_ALIGN:base + GATHER_BLOCK].astype(BF16)
        tail_ref[e] = res[base + GATHER_BLOCK:base + GATHER_STACK] + jnp.where(shifts[e] == 0, old, 0.0)
        write(e, jnp.maximum(t - 1, 0), 1 - par).wait()
        write(e, t, par).start()

    extra = [_cdiv_pow2(jnp.maximum(bsel_s[e, t + 1] - aligned(e, t) - GATHER_BLOCK, 0), SLOT_CHUNK)
             for e in range(N_EXPERTS)]

    @pl.when(functools.reduce(jnp.maximum, extra) > 0)
    def _():
        row64 = lax.broadcasted_iota(I32, (SLOT_CHUNK, ROUTE_TILE), 0).astype(F32)
        for e in range(N_EXPERTS):
            def chunk(c, carry):
                first = GATHER_BLOCK + c * SLOT_CHUNK
                onehot = _ones_where(rank[e:e + 1, :] + offs[e] == row64 + first.astype(F32), BF16)
                xbuf_ref[...] = _dot(onehot, x_ref[...]).astype(BF16)
                dst = pl.multiple_of(aligned(e, t) + first, ROW_ALIGN)
                cp = pltpu.make_async_copy(xbuf_ref, xe_hbm.at[e, pl.ds(dst, SLOT_CHUNK)], xsem_ref.at[0])
                cp.start()
                cp.wait()
                return carry

            lax.fori_loop(0, extra[e], chunk, 0)

    @pl.when(t == ntile - 1)
    def _():
        for e in range(N_EXPERTS):
            write(e, t, par).wait()


def _gather(bsel_i, afft, tau, need, beq_i, xn, u):
    tokens = xn.shape[0]
    cap = CAPACITY_FACTOR * tokens // N_EXPERTS
    ntile = tokens // ROUTE_TILE
    table = pl.BlockSpec((N_EXPERTS, LANES), lambda t, *_: (0, 0))
    grid_spec = pltpu.PrefetchScalarGridSpec(
        num_scalar_prefetch=1,
        grid=(ntile,),
        in_specs=[
            pl.BlockSpec((N_EXPERTS, ROUTE_TILE), lambda t, *_: (0, t)),
            table, table, table,
            pl.BlockSpec((ROUTE_TILE, D_MODEL), lambda t, *_: (t, 0)),
            pl.BlockSpec(u.shape, lambda t, *_: (0, 0)),
        ],
        out_specs=pl.BlockSpec(memory_space=pl.ANY),
        scratch_shapes=[
            pltpu.VMEM((2, N_EXPERTS, GATHER_BLOCK, D_MODEL), BF16),
            pltpu.VMEM((N_EXPERTS, ROW_ALIGN, D_MODEL), F32),
            pltpu.VMEM((SLOT_CHUNK, D_MODEL), BF16),
            pltpu.SemaphoreType.DMA((2, N_EXPERTS)),
            pltpu.SemaphoreType.DMA((1,)),
        ],
    )
    return pl.pallas_call(
        functools.partial(_gather_kernel, ntile=ntile, cap=cap),
        grid_spec=grid_spec,
        out_shape=jax.ShapeDtypeStruct((N_EXPERTS, cap + GATHER_PAD, D_MODEL), BF16),
        compiler_params=_params(("arbitrary",)),
        name="gather",
    )(bsel_i, afft, tau, need, beq_i, xn, u)


def _ffn_kernel(xe_ref, wg_ref, wu_ref, wd_ref, ye_ref, acc_ref, *, cap, nf, tm):
    f = pl.program_id(1)
    tf = wg_ref.shape[2]
    chunks = [slice(j * FFN_CHUNK, (j + 1) * FFN_CHUNK) for j in range(tf // FFN_CHUNK)]
    cast = {}

    def weight(name, ref, j):
        if (name, j) not in cast:
            cast[name, j] = (ref[0, chunks[j], :] if name == "d" else ref[0, :, chunks[j]]).astype(BF16)
        return cast[name, j]

    @pl.when(f == 0)
    def _():
        acc_ref[...] = jnp.zeros_like(acc_ref)

    for i in range(cap // tm):
        r = slice(i * tm, (i + 1) * tm)
        x = xe_ref[0, r, :]
        y = None
        for j in range(len(chunks)):
            hg = _dot(x, weight("g", wg_ref, j))
            hu = _dot(x, weight("u", wu_ref, j))
            h = (hg * (1.0 / (1.0 + jnp.exp(-hg))) * hu).astype(BF16)
            part = _dot(h, weight("d", wd_ref, j))
            y = part if y is None else y + part
        acc_ref[r, :] += y

    @pl.when(f == nf - 1)
    def _():
        ye_ref[...] = acc_ref[...].astype(BF16)


def _ffn(xe, w_eg, w_eu, w_ed):
    cap = xe.shape[1] - GATHER_PAD
    tf = 512
    nf = D_FF // tf
    tm = min(cap, 1024)
    return pl.pallas_call(
        functools.partial(_ffn_kernel, cap=cap, nf=nf, tm=tm),
        grid=(N_EXPERTS, nf),
        in_specs=[
            pl.BlockSpec((1, cap, D_MODEL), lambda e, f: (e, 0, 0)),
            pl.BlockSpec((1, D_MODEL, tf), lambda e, f: (e, 0, f)),
            pl.BlockSpec((1, D_MODEL, tf), lambda e, f: (e, 0, f)),
            pl.BlockSpec((1, tf, D_MODEL), lambda e, f: (e, f, 0)),
        ],
        out_specs=pl.BlockSpec((cap, D_MODEL), lambda e, f: (e, 0)),
        out_shape=jax.ShapeDtypeStruct((N_EXPERTS * cap, D_MODEL), BF16),
        scratch_shapes=[pltpu.VMEM((cap, D_MODEL), F32)],
        compiler_params=_params(("arbitrary", "arbitrary")),
        name="ffn",
    )(xe, w_eg, w_eu, w_ed)


def _combine_kernel(bsel_s, x1_ref, aff_ref, taut_ref, needt_ref, beqt_ref, bselt_ref, low_ref, spread_ref, gf_ref,
                    ye_hbm, y_ref, buf_ref, xbuf_ref, sem_ref, xsem_ref, *, cap, total, ntile):
    t = pl.program_id(0)
    par = t & 1

    def aligned(e, tile):
        return _floor_pow2(bsel_s[e, tile], ROW_ALIGN)

    def window(e, tile, c):
        start = jnp.minimum(e * cap + aligned(e, tile) + c * SLOT_CHUNK, total - SLOT_CHUNK)
        return pl.multiple_of(start, ROW_ALIGN)

    def first_chunks(tile, buf):
        return [pltpu.make_async_copy(ye_hbm.at[pl.ds(window(e, tile, 0), SLOT_CHUNK)],
                                      buf_ref.at[buf, pl.ds(e * SLOT_CHUNK, SLOT_CHUNK)], sem_ref.at[buf, e])
                for e in range(N_EXPERTS)]

    @pl.when(t == 0)
    def _():
        for cp in first_chunks(0, 0):
            cp.start()

    for cp in first_chunks(jnp.minimum(t + 1, ntile - 1), 1 - par):
        cp.start()

    aff = aff_ref[...]
    k = lax.bitcast_convert_type(aff, I32)
    tau = taut_ref[0:1, :]
    low = low_ref[...]
    eq = k == tau
    eq_cum = _dot(low, _ones_where(eq, BF16)) + beqt_ref[0]
    sel = (k > tau) | (eq & (eq_cum <= needt_ref[0:1, :]))
    slot = jnp.where(sel, _dot(low, _ones_where(sel, BF16)) + (bselt_ref[0] - 1.0), -1.0)

    lane = lax.broadcasted_iota(I32, (1, LANES), 1)
    rel = jnp.zeros((1, LANES), F32)
    for e in range(N_EXPERTS):
        rel = jnp.where(lane == e, (window(e, t, 0) - e * cap).astype(F32), rel)
    d = slot - rel
    d = jnp.where(sel & (d >= 0.0) & (d < float(SLOT_CHUNK)), d, -1.0)
    spread = spread_ref[...]
    wide = lax.broadcasted_iota(I32, (ROUTE_TILE, N_EXPERTS * SLOT_CHUNK), 1)
    hit = _dot(d.astype(BF16), spread) == (wide & (SLOT_CHUNK - 1)).astype(F32)
    onehot_gate = jnp.where(hit, _dot(aff.astype(BF16), spread), 0.0).astype(BF16)
    for cp in first_chunks(t, par):
        cp.wait()
    y_ref[...] = x1_ref[...] + _dot(onehot_gate, buf_ref[par])

    nch = [_cdiv_pow2(bsel_s[e, t + 1] - aligned(e, t), SLOT_CHUNK) for e in range(N_EXPERTS)]

    @pl.when(functools.reduce(jnp.maximum, nch) > 1)
    def _():
        lane64 = lax.broadcasted_iota(I32, (ROUTE_TILE, SLOT_CHUNK), 1).astype(F32)
        for e in range(N_EXPERTS):
            slot_e = slot[:, e:e + 1]

            def extra(c, carry):
                w = window(e, t, c)
                cp = pltpu.make_async_copy(ye_hbm.at[pl.ds(w, SLOT_CHUNK)], xbuf_ref, xsem_ref.at[0])
                cp.start()
                cp.wait()
                first = (aligned(e, t) + c * SLOT_CHUNK).astype(F32)
                hit = (lane64 + (w - e * cap).astype(F32) == slot_e) & (slot_e >= first)
                y_ref[...] += aff[:, e:e + 1] * _dot(_ones_where(hit, BF16), xbuf_ref[...])
                return carry

            lax.fori_loop(1, nch[e], extra, 0)

    acc = y_ref[...]
    ms = jnp.mean(acc * acc, axis=-1, keepdims=True)
    y_ref[...] = acc * lax.rsqrt(ms + EPS) * gf_ref[...]

    @pl.when(t == ntile - 1)
    def _():
        for cp in first_chunks(t, 1 - par):
            cp.wait()


def _combine(bsel_i, x1, aff, tables_t, ye, gf):
    tokens = x1.shape[0]
    cap = CAPACITY_FACTOR * tokens // N_EXPERTS
    ntile = tokens // ROUTE_TILE
    idx = np.arange(ROUTE_TILE)
    low = jnp.asarray(idx[:, None] >= idx[None, :], BF16)
    spread = jnp.asarray(np.arange(LANES)[:, None] == np.arange(N_EXPERTS * SLOT_CHUNK)[None, :] // SLOT_CHUNK, BF16)
    taut, needt, beqt, bselt = tables_t
    rowvec = pl.BlockSpec((8, LANES), lambda t, *_: (0, 0))
    tilevec = pl.BlockSpec((1, 1, LANES), lambda t, *_: (t, 0, 0))
    grid_spec = pltpu.PrefetchScalarGridSpec(
        num_scalar_prefetch=1,
        grid=(ntile,),
        in_specs=[
            pl.BlockSpec((ROUTE_TILE, D_MODEL), lambda t, *_: (t, 0)),
            pl.BlockSpec((ROUTE_TILE, LANES), lambda t, *_: (t, 0)),
            rowvec, rowvec, tilevec, tilevec,
            pl.BlockSpec(low.shape, lambda t, *_: (0, 0)),
            pl.BlockSpec(spread.shape, lambda t, *_: (0, 0)),
            pl.BlockSpec((1, D_MODEL), lambda t, *_: (0, 0)),
            pl.BlockSpec(memory_space=pl.ANY),
        ],
        out_specs=pl.BlockSpec((ROUTE_TILE, D_MODEL), lambda t, *_: (t, 0)),
        scratch_shapes=[
            pltpu.VMEM((2, N_EXPERTS * SLOT_CHUNK, D_MODEL), BF16),
            pltpu.VMEM((SLOT_CHUNK, D_MODEL), BF16),
            pltpu.SemaphoreType.DMA((2, N_EXPERTS)),
            pltpu.SemaphoreType.DMA((1,)),
        ],
    )
    return pl.pallas_call(
        functools.partial(_combine_kernel, cap=cap, total=N_EXPERTS * cap, ntile=ntile),
        grid_spec=grid_spec,
        out_shape=jax.ShapeDtypeStruct((tokens, D_MODEL), F32),
        compiler_params=_params(("arbitrary",)),
        name="combine",
    )(bsel_i, x1, aff, taut, needt, beqt.reshape(LANES, 1, LANES), bselt.reshape(LANES, 1, LANES), low, spread,
      gf, ye)


def _encoder(x, w):
    batch, seq, _ = x.shape
    tokens = batch * seq
    xt = x.reshape(tokens, D_MODEL)
    *qkvs, vr, vi, gates = _in_proj(xt, w["g1"], w["w_in"], w["w_gate"], w["b_gate"], w["cs"], batch, seq)
    outs, lses = [], []
    for g in range(N_GROUPS):
        o, lse = _attention(qkvs[g], w["bias"][g], g)
        outs.append(o)
        lses.append(lse)
    four = _fourier(vr, vi, batch, seq)
    x1, xn, aff, afft = _mix(xt, outs, lses, four, gates, w["w_attn"], w["w_four"], w["w_out"], w["g2"],
                             w["w_router_hi"], w["w_router_lo"])
    tau, need, beq_i, bsel_i, taut, needt, beqt, bselt = _route(afft)
    idx = np.arange(ROUTE_TILE)
    u = jnp.asarray(idx[:, None] <= idx[None, :], BF16)
    xe = _gather(bsel_i, afft, tau, need, beq_i, xn, u)
    ye = _ffn(xe, w["w_eg"], w["w_eu"], w["w_ed"])
    y = _combine(bsel_i, x1, aff, (taut, needt, beqt, bselt), ye, w["gf"])
    return y.reshape(batch, seq, D_MODEL)


def _prepare_weights(rel_bias, norm1_g, w_in, w_attn_br, w_four_br, w_gate, b_gate, w_out,
                     norm2_g, w_router, w_exp_gate, w_exp_up, w_exp_down, final_g):
    c, s = _dft_mats(F_CH)
    starts = [part * ATT_W + g * GROUP_W for g in range(N_GROUPS) for part in range(3)]
    w_in_grouped = jnp.concatenate([w_in[0][:, s0:s0 + GROUP_W] for s0 in starts] + [w_in[0][:, QKV_W:]], axis=1)
    w_router = jnp.pad(w_router[0], ((0, 0), (0, LANES - N_EXPERTS)))
    w_router_hi = w_router.astype(BF16)
    return {
        "g1": norm1_g[0].reshape(1, D_MODEL),
        "w_in": w_in_grouped.astype(BF16),
        "w_gate": w_gate[0].astype(BF16),
        "b_gate": b_gate[0].reshape(1, 2 * D_MODEL),
        "cs": jnp.asarray(np.concatenate([c, s], axis=1), BF16),
        "bias": [_attention_bias(rel_bias, g) for g in range(N_GROUPS)],
        "w_attn": w_attn_br[0].astype(BF16),
        "w_four": w_four_br[0].astype(BF16),
        "w_out": w_out[0].astype(BF16),
        "g2": norm2_g[0].reshape(1, D_MODEL),
        "w_router_hi": w_router_hi,
        "w_router_lo": (w_router - w_router_hi.astype(F32)).astype(BF16),
        "w_eg": w_exp_gate[0],
        "w_eu": w_exp_up[0],
        "w_ed": w_exp_down[0],
        "gf": final_g.reshape(1, D_MODEL),
    }


def kernel(x_prompt, x_sample, rel_bias, norm1_g, w_in, w_attn_br, w_four_br, w_gate, b_gate, w_out,
           norm2_g, w_router, w_exp_gate, w_exp_up, w_exp_down, final_g):
    w = _prepare_weights(rel_bias, norm1_g, w_in, w_attn_br, w_four_br, w_gate, b_gate, w_out,
                         norm2_g, w_router, w_exp_gate, w_exp_up, w_exp_down, final_g)
    return (_encoder(x_prompt, w), _encoder(x_sample, w))
```

```python
import functools
import math

import numpy as np
import jax
import jax.numpy as jnp
from jax import lax
from jax.experimental import pallas as pl
from jax.experimental.pallas import tpu as pltpu

D_MODEL = 1024
HEAD_DIM = 64
HEADS_PER_GROUP = 4
GROUPS = ((128, 1), (512, 4), (2048, 16))
N_GROUPS = len(GROUPS)
GROUP_W = HEADS_PER_GROUP * HEAD_DIM
ATT_W = N_GROUPS * GROUP_W
QKV_W = 3 * ATT_W
F_GROUPS = 6
F_CH = 128
F_W = F_GROUPS * F_CH
NUM_BUCKETS = 32
MAX_DISTANCE = 1024
N_EXPERTS = 16
CAPACITY_FACTOR = 2
D_FF = 2048
EPS = 1e-6
NEG = -1e30

HALF_KEYS = 64
ATT_SUB = 128
ATT_OUT_ROWS = 8192
TOKEN_TILE = 512
ROUTE_TILE = 256
FFT_STEP_ROWS = 1024
FFN_CHUNK = 256
SLOT_CHUNK = 64
ROW_ALIGN = 16
GATHER_BLOCK = SLOT_CHUNK + ROW_ALIGN
GATHER_STACK = GATHER_BLOCK + ROW_ALIGN
GATHER_PAD = GATHER_BLOCK
LANES = 128
V7X_VMEM_LIMIT = 56 * 1024 * 1024

F32 = jnp.float32
BF16 = jnp.bfloat16
I32 = jnp.int32


def _params(sem):
    return pltpu.CompilerParams(dimension_semantics=sem, vmem_limit_bytes=V7X_VMEM_LIMIT)


def _dot(a, b):
    return jnp.dot(a, b, preferred_element_type=F32)


def _dot_nt(a, b):
    return lax.dot_general(a, b, (((1,), (1,)), ((), ())), preferred_element_type=F32)


def _floor_pow2(x, m):
    return x & ~(m - 1)


def _cdiv_pow2(x, m):
    return (x + (m - 1)) >> (m.bit_length() - 1)


def _ones_where(mask, dtype=F32):
    return jnp.where(mask, jnp.ones((), F32), jnp.zeros((), F32)).astype(dtype)


def _in_proj_kernel(x_ref, g_ref, win_ref, wg_ref, bg_ref, cs_ref, qscale_ref, qkv0_ref, qkv1_ref, qkv2_ref,
                    vr_ref, vi_ref, gates_ref, slab_ref):
    tm = x_ref.shape[0]
    half = tm // 2
    nslab = ATT_W // LANES
    cs = cs_ref[...]
    for h in range(2):
        rows = slice(h * half, (h + 1) * half)
        x = x_ref[rows, :]
        ms = jnp.mean(x * x, axis=-1, keepdims=True)
        xn = (x * lax.rsqrt(ms + EPS) * g_ref[...]).astype(BF16)
        for g, out_ref in enumerate((qkv0_ref, qkv1_ref, qkv2_ref)):
            dil = GROUPS[g][1]
            res = _dot(xn, win_ref[:, g * ATT_W:(g + 1) * ATT_W]) * qscale_ref[...]
            if dil == 1:
                out_ref[0, 0, rows, :] = res.astype(BF16)
                continue
            for j in range(nslab):
                slab_ref[j, rows, :] = res[:, j * LANES:(j + 1) * LANES]
            n = half // dil
            for r in range(dil):
                cls = [slab_ref[j, pl.ds(h * half + r, n, stride=dil), :] for j in range(nslab)]
                out_ref[0, r, h * n:(h + 1) * n, :] = jnp.concatenate(cls, axis=1).astype(BF16)
        u = _dot(xn, win_ref[:, QKV_W:QKV_W + F_W]).astype(BF16)
        for g in range(F_GROUPS):
            a = _dot(u[:, g * F_CH:(g + 1) * F_CH], cs)
            vr_ref[rows, g * F_CH:(g + 1) * F_CH] = a[:, :F_CH].astype(BF16)
            vi_ref[rows, g * F_CH:(g + 1) * F_CH] = (-a[:, F_CH:]).astype(BF16)
        z = _dot(xn, wg_ref[...]) + bg_ref[...]
        gates_ref[rows, :] = (1.0 / (1.0 + jnp.exp(-z))).astype(BF16)


def _class_major_spec(tm, dil, width, per_batch):
    return pl.BlockSpec((1, dil, tm // dil, width), lambda i: (i // per_batch, 0, i % per_batch, 0))


def _in_proj(x, g1, w_in, w_gate, b_gate, cs, batch, seq):
    t = x.shape[0]
    tm = TOKEN_TILE
    per_batch = seq // tm
    const = lambda i: (0, 0)
    row = lambda i: (i, 0)
    qscale = np.ones((1, ATT_W), np.float32)
    qscale[:, :GROUP_W] = 1.0 / math.sqrt(HEAD_DIM)
    return pl.pallas_call(
        _in_proj_kernel,
        grid=(t // tm,),
        in_specs=[
            pl.BlockSpec((tm, D_MODEL), row),
            pl.BlockSpec((1, D_MODEL), const),
            pl.BlockSpec(w_in.shape, const),
            pl.BlockSpec(w_gate.shape, const),
            pl.BlockSpec((1, 2 * D_MODEL), const),
            pl.BlockSpec(cs.shape, const),
            pl.BlockSpec((1, ATT_W), const),
        ],
        out_specs=[_class_major_spec(tm, dil, ATT_W, per_batch) for _, dil in GROUPS] + [
            pl.BlockSpec((tm, F_W), row),
            pl.BlockSpec((tm, F_W), row),
            pl.BlockSpec((tm, 2 * D_MODEL), row),
        ],
        out_shape=[jax.ShapeDtypeStruct((batch, dil, seq // dil, ATT_W), BF16) for _, dil in GROUPS] + [
            jax.ShapeDtypeStruct((t, F_W), BF16),
            jax.ShapeDtypeStruct((t, F_W), BF16),
            jax.ShapeDtypeStruct((t, 2 * D_MODEL), BF16),
        ],
        scratch_shapes=[pltpu.VMEM((ATT_W // LANES, tm, LANES), F32)],
        compiler_params=_params(("parallel",)),
        name="in_proj",
    )(x, g1, w_in, w_gate, b_gate, cs, jnp.asarray(qscale))


def _attention_kernel(q_ref, kp_ref, kc_ref, kn_ref, vp_ref, vc_ref, vn_ref, bias_ref, o_ref, lse_ref, *,
                      tq, length, dil, rc):
    i = pl.program_id(1)
    win = ATT_SUB + 2 * HALF_KEYS
    nsub = tq // ATT_SUB
    lane_head = lax.broadcasted_iota(I32, (ATT_SUB, GROUP_W), 1) // HEAD_DIM
    at_start = (i == 0).astype(I32)
    at_end = (i == length // tq - 1).astype(I32) * 2
    for c, sb in [(c, sb) for c in range(rc) for sb in range(nsub)]:
        r = pl.program_id(2) * rc + c
        if sb == 0:
            kwin = jnp.concatenate([kp_ref[0, c], kc_ref[0, c], kn_ref[0, c]], axis=0)
            vwin = jnp.concatenate([vp_ref[0, c], vc_ref[0, c], vn_ref[0, c]], axis=0)
        off = sb * ATT_SUB
        q = q_ref[0, c, off:off + ATT_SUB, :]
        kw = kwin[off:off + win]
        vw = vwin[off:off + win]
        variant = (at_start if sb == 0 else 0) + (at_end if sb == nsub - 1 else 0)
        qs = jnp.concatenate(
            [jnp.where(lane_head == h, q, jnp.zeros_like(q)) for h in range(HEADS_PER_GROUP)], axis=0)
        s_all = _dot_nt(qs, kw)
        ps, ms, ls = [], [], []
        for h in range(HEADS_PER_GROUP):
            s = s_all[h * ATT_SUB:(h + 1) * ATT_SUB] + bias_ref[variant, h]
            m = jnp.max(s, axis=-1, keepdims=True)
            p = jnp.exp(s - m)
            ls.append(jnp.sum(p, axis=-1, keepdims=True))
            ms.append(m)
            ps.append(p.astype(BF16))
        o_all = _dot(jnp.concatenate(ps, axis=0), vw)
        out = jnp.zeros((ATT_SUB, GROUP_W), F32)
        lse = jnp.zeros((ATT_SUB, GROUP_W), F32)
        for h in range(HEADS_PER_GROUP):
            oh = o_all[h * ATT_SUB:(h + 1) * ATT_SUB] * (1.0 / ls[h])
            out = jnp.where(lane_head == h, oh, out)
            lse = jnp.where(lane_head == h, ms[h] + jnp.log(ls[h]), lse)
        rows = pl.ds(off * dil + r, ATT_SUB, stride=dil) if dil > 1 else pl.ds(off, ATT_SUB)
        for j in range(GROUP_W // LANES):
            o_ref[j, rows, :] = out[:, j * LANES:(j + 1) * LANES]
            lse_ref[j, rows, :] = lse[:, j * LANES:(j + 1) * LANES]


def _attention(qkv, bias, g):
    batch, dil, length, _ = qkv.shape
    tq = min(length, 512, ATT_OUT_ROWS // dil)
    nb = length // tq
    hb = tq // HALF_KEYS
    last_halo = length // HALF_KEYS - 1
    rc = min(dil, max(1, 512 // tq))

    def cur(c):
        return lambda b, i, r: (b, r, i, c)

    def prev(c):
        return lambda b, i, r: (b, r, jnp.maximum(i * hb - 1, 0), c)

    def nxt(c):
        return lambda b, i, r: (b, r, jnp.minimum((i + 1) * hb, last_halo), c)

    blk = lambda rows: (1, rc, rows, GROUP_W)
    out_spec = pl.BlockSpec((GROUP_W // LANES, tq * dil, LANES), lambda b, i, r: (0, b * nb + i, 0))
    return pl.pallas_call(
        functools.partial(_attention_kernel, tq=tq, length=length, dil=dil, rc=rc),
        grid=(batch, nb, dil // rc),
        in_specs=[
            pl.BlockSpec(blk(tq), cur(0)),
            pl.BlockSpec(blk(HALF_KEYS), prev(1)),
            pl.BlockSpec(blk(tq), cur(1)),
            pl.BlockSpec(blk(HALF_KEYS), nxt(1)),
            pl.BlockSpec(blk(HALF_KEYS), prev(2)),
            pl.BlockSpec(blk(tq), cur(2)),
            pl.BlockSpec(blk(HALF_KEYS), nxt(2)),
            pl.BlockSpec(bias.shape, lambda b, i, r: (0, 0, 0, 0)),
        ],
        out_specs=[out_spec] * 2,
        out_shape=[jax.ShapeDtypeStruct((GROUP_W // LANES, batch * dil * length, LANES), F32)] * 2,
        compiler_params=_params(("parallel", "parallel", "arbitrary")),
        name=f"attention_g{g}",
    )(qkv, qkv, qkv, qkv, qkv, qkv, qkv, bias)


def _t5_bucket(rel):
    nb = NUM_BUCKETS // 2
    max_exact = nb // 2
    ret = (rel > 0).astype(np.int32) * nb
    n = np.abs(rel)
    large = max_exact + (np.log(np.maximum(n, max_exact) / max_exact)
                         / np.log(MAX_DISTANCE / max_exact) * (nb - max_exact)).astype(np.int32)
    large = np.minimum(large, nb - 1)
    return (ret + np.where(n < max_exact, n, large)).astype(np.int32)


def _attention_bias(rel_bias, g):
    dil = GROUPS[g][1]
    qi = np.arange(ATT_SUB)[:, None]
    kj = np.arange(ATT_SUB + 2 * HALF_KEYS)[None, :]
    delta = kj - HALF_KEYS - qi
    band = np.abs(delta) <= HALF_KEYS
    bucket = _t5_bucket(dil * delta)
    tab = rel_bias[:, g * HEADS_PER_GROUP:(g + 1) * HEADS_PER_GROUP].astype(F32)
    onehot = jnp.asarray(bucket[..., None] == np.arange(NUM_BUCKETS), F32)
    bias = jnp.einsum("qkb,bh->hqk", onehot, tab, precision=lax.Precision.HIGHEST)
    masks = [band & ((kj >= HALF_KEYS) | ((v & 1) == 0)) & ((kj < ATT_SUB + HALF_KEYS) | ((v & 2) == 0))
             for v in range(4)]
    return jnp.where(jnp.asarray(np.stack(masks))[:, None], bias[None], NEG)


def _dft_mats(n):
    k = np.arange(n)
    ang = 2.0 * np.pi * ((k[:, None] * k[None, :]) % n) / n
    return np.cos(ang), np.sin(ang)


def _fft_stage1_kernel(vr_ref, vi_ref, m1_ref, twc_ref, tws_ref, zr_ref, zi_ref, *, n1, m):
    x = jnp.concatenate([vr_ref[0], vi_ref[0]], axis=0)
    z = _dot(m1_ref[...], x)
    zr, zi = z[:n1], z[n1:]
    twc, tws = twc_ref[0], tws_ref[0]
    for j in range(m):
        c = twc[:, j:j + 1]
        s = tws[:, j:j + 1]
        a = zr[:, j * F_W:(j + 1) * F_W]
        b = zi[:, j * F_W:(j + 1) * F_W]
        zr_ref[0, :, j * F_W:(j + 1) * F_W] = (a * c + b * s).astype(BF16)
        zi_ref[0, :, j * F_W:(j + 1) * F_W] = (b * c - a * s).astype(BF16)


def _fft_stage2_kernel(zr_ref, zi_ref, m2_ref, o_ref, *, kc, scale):
    m2 = m2_ref[...]
    for j in range(kc):
        x = jnp.concatenate([zr_ref[0, j], zi_ref[0, j]], axis=0)
        o_ref[0, :, j * F_W:(j + 1) * F_W] = (_dot(m2, x) * scale).astype(BF16)


def _fourier(vr, vi, batch, seq):
    n2 = LANES
    n1 = seq // n2
    m = min(n2, FFT_STEP_ROWS // n1)
    c1, s1 = _dft_mats(n1)
    m1 = jnp.asarray(np.block([[c1, s1], [-s1, c1]]), BF16)
    c2, s2 = _dft_mats(n2)
    m2 = jnp.asarray(np.concatenate([c2, s2], axis=1), BF16)
    k1 = np.arange(n1)[:, None]
    sv = np.arange(n2)[None, :]
    ang = 2.0 * np.pi * ((k1 * sv) % seq) / seq
    to_blocks = lambda a: jnp.asarray(a.reshape(n1, n2 // m, m).transpose(1, 0, 2), F32)
    twc, tws = to_blocks(np.cos(ang)), to_blocks(np.sin(ang))

    v3 = lambda a: a.reshape(batch, n1, n2 * F_W)
    blk = (1, n1, m * F_W)
    dmap = lambda b, j: (b, 0, j)
    tmap = lambda b, j: (j, 0, 0)
    zr, zi = pl.pallas_call(
        functools.partial(_fft_stage1_kernel, n1=n1, m=m),
        grid=(batch, n2 // m),
        in_specs=[
            pl.BlockSpec(blk, dmap),
            pl.BlockSpec(blk, dmap),
            pl.BlockSpec(m1.shape, lambda b, j: (0, 0)),
            pl.BlockSpec((1, n1, m), tmap),
            pl.BlockSpec((1, n1, m), tmap),
        ],
        out_specs=[pl.BlockSpec(blk, dmap), pl.BlockSpec(blk, dmap)],
        out_shape=[jax.ShapeDtypeStruct((batch, n1, n2 * F_W), BF16)] * 2,
        compiler_params=_params(("parallel", "parallel")),
        name="fft_stage1",
    )(v3(vr), v3(vi), m1, twc, tws)

    kc = min(n1, FFT_STEP_ROWS // n2 * 2)
    v4 = lambda a: a.reshape(batch, n1, n2, F_W)
    zblk = (1, kc, n2, F_W)
    zmap = lambda b, j: (b, j, 0, 0)
    out = pl.pallas_call(
        functools.partial(_fft_stage2_kernel, kc=kc, scale=1.0 / math.sqrt(seq * F_CH)),
        grid=(batch, n1 // kc),
        in_specs=[
            pl.BlockSpec(zblk, zmap),
            pl.BlockSpec(zblk, zmap),
            pl.BlockSpec(m2.shape, lambda b, j: (0, 0)),
        ],
        out_specs=pl.BlockSpec((1, n2, kc * F_W), lambda b, j: (b, 0, j)),
        out_shape=jax.ShapeDtypeStruct((batch, n2, n1 * F_W), BF16),
        compiler_params=_params(("parallel", "parallel")),
        name="fft_stage2",
    )(v4(zr), v4(zi), m2)
    return out.reshape(batch * seq, F_W)


def _mix_kernel(x_ref, o0_ref, o1_ref, o2_ref, l0_ref, l1_ref, l2_ref, four_ref, gates_ref,
                wa_ref, wf_ref, wo_ref, g2_ref, wrh_ref, wrl_ref, x1_ref, xn_ref, aff_ref, afft_ref):
    tm = x_ref.shape[0]
    half = tm // 2
    for rows in (slice(0, half), slice(half, tm)):
        def slabs(ref):
            return jnp.concatenate([ref[j, rows, :] for j in range(GROUP_W // LANES)], axis=1)

        f_br = _dot(four_ref[rows, :], wf_ref[...])
        l0, l1, l2 = slabs(l0_ref), slabs(l1_ref), slabs(l2_ref)
        mx = jnp.maximum(jnp.maximum(l0, l1), l2)
        e0, e1, e2 = jnp.exp(l0 - mx), jnp.exp(l1 - mx), jnp.exp(l2 - mx)
        att = (e0 * slabs(o0_ref) + e1 * slabs(o1_ref) + e2 * slabs(o2_ref)) * (1.0 / (e0 + e1 + e2))
        a_br = _dot(att.astype(BF16), wa_ref[...])
        mix = gates_ref[rows, :D_MODEL] * a_br + gates_ref[rows, D_MODEL:] * f_br
        x1 = x_ref[rows, :] + _dot(mix.astype(BF16), wo_ref[...])
        x1_ref[rows, :] = x1
        ms = jnp.mean(x1 * x1, axis=-1, keepdims=True)
        xn = x1 * lax.rsqrt(ms + EPS) * g2_ref[...]
        xn_ref[rows, :] = xn.astype(BF16)
        xh = xn.astype(BF16)
        xl = (xn - xh.astype(F32)).astype(BF16)
        logits = _dot(xh, wrh_ref[...]) + (_dot(xh, wrl_ref[...]) + _dot(xl, wrh_ref[...]))
        lane = lax.broadcasted_iota(I32, logits.shape, 1)
        logits = jnp.where(lane < N_EXPERTS, logits, NEG)
        p = jnp.exp(logits - jnp.max(logits, axis=-1, keepdims=True))
        aff = p * (1.0 / jnp.sum(p, axis=-1, keepdims=True))
        aff_ref[rows, :] = aff
        afft_ref[:, rows] = aff.T[:N_EXPERTS]


def _mix(x, os_, ls_, four, gates, w_attn, w_four, w_out, g2, w_router_hi, w_router_lo):
    t = x.shape[0]
    tm = TOKEN_TILE
    const = lambda i: (0, 0)
    row = lambda i: (i, 0)
    rows = lambda w: pl.BlockSpec((tm, w), row)
    full = lambda a: pl.BlockSpec(a.shape, const)
    slab = pl.BlockSpec((GROUP_W // LANES, tm, LANES), lambda i: (0, i, 0))
    return pl.pallas_call(
        _mix_kernel,
        grid=(t // tm,),
        in_specs=[rows(D_MODEL)] + [slab] * 6 + [rows(F_W), rows(2 * D_MODEL),
                  full(w_attn), full(w_four), full(w_out), full(g2), full(w_router_hi), full(w_router_lo)],
        out_specs=[rows(D_MODEL), rows(D_MODEL), rows(LANES), pl.BlockSpec((N_EXPERTS, tm), lambda i: (0, i))],
        out_shape=[
            jax.ShapeDtypeStruct((t, D_MODEL), F32),
            jax.ShapeDtypeStruct((t, D_MODEL), BF16),
            jax.ShapeDtypeStruct((t, LANES), F32),
            jax.ShapeDtypeStruct((N_EXPERTS, t), F32),
        ],
        compiler_params=_params(("parallel",)),
        name="mix",
    )(x, *os_, *ls_, four, gates, w_attn, w_four, w_out, g2, w_router_hi, w_router_lo)


def _route_kernel(afft_ref, su_ref, u_ref, tau_ref, need_ref, beq_ref, bsel_ref,
                  taut_ref, needt_ref, beqt_ref, bselt_ref, *, tokens):
    cap = CAPACITY_FACTOR * tokens // N_EXPERTS
    ntile = tokens // ROUTE_TILE
    shape = (N_EXPERTS, LANES)
    lane = lax.broadcasted_iota(I32, shape, 1)

    def keys(start, width):
        return lax.bitcast_convert_type(afft_ref[:, pl.ds(pl.multiple_of(start, LANES), width)], I32)

    span = min(tokens, 16 * LANES)

    def count(pred):
        def body(c, acc):
            hits = _ones_where(pred(keys(c * span, span)))
            for j in range(span // LANES):
                acc = acc + hits[:, j * LANES:(j + 1) * LANES]
            return acc
        acc = lax.fori_loop(0, tokens // span, body, jnp.zeros(shape, F32))
        return jnp.sum(acc, axis=1, keepdims=True)

    def bit_body(i, prefix):
        cand = prefix | lax.shift_left(jnp.ones(shape, I32), jnp.full(shape, 30 - i, I32))
        tot = count(lambda k: k >= cand[:, :1])
        return jnp.where(tot >= cap, cand, prefix)

    tau = lax.fori_loop(0, 31, bit_body, jnp.zeros(shape, I32))
    tau_col = tau[:, :1]
    n_gt = count(lambda k: k > tau_col)
    need = cap - n_gt

    def prefix_over_tiles(tab):
        return _dot(tab.astype(BF16), su_ref[...])

    def at_lane(tab, c):
        return jnp.sum(jnp.where(lane == c, tab, 0.0), axis=1, keepdims=True)

    def eq_body(c, tab):
        k = keys(c * ROUTE_TILE, ROUTE_TILE)
        cnt = jnp.sum(_ones_where(k == tau_col), axis=1, keepdims=True)
        return jnp.where(lane == c, cnt, tab)

    base_eq = prefix_over_tiles(lax.fori_loop(0, ntile, eq_body, jnp.zeros(shape, F32)))

    def sel_body(c, tab):
        k = keys(c * ROUTE_TILE, ROUTE_TILE)
        eq = k == tau_col
        eq_cum = _dot(_ones_where(eq, BF16), u_ref[...]) + at_lane(base_eq, c)
        sel = (k > tau_col) | (eq & (eq_cum <= need))
        cnt = jnp.sum(_ones_where(sel), axis=1, keepdims=True)
        return jnp.where(lane == c, cnt, tab)

    base_sel = prefix_over_tiles(lax.fori_loop(0, ntile, sel_body, jnp.zeros(shape, F32)))

    def transposed(val):
        return jnp.concatenate([val, jnp.zeros((LANES - N_EXPERTS, LANES), val.dtype)], axis=0).T

    tau_ref[...] = tau
    taut_ref[...] = transposed(tau)
    for val, ref, ref_t in ((jnp.broadcast_to(need, shape), need_ref, needt_ref),
                            (base_eq, beq_ref, beqt_ref), (base_sel, bsel_ref, bselt_ref)):
        ref[...] = val.astype(I32)
        ref_t[...] = transposed(val)


def _route(afft):
    tokens = afft.shape[1]
    idx = np.arange(LANES)
    su = jnp.asarray(idx[:, None] < idx[None, :], BF16)
    idx = np.arange(ROUTE_TILE)
    u = jnp.asarray(idx[:, None] <= idx[None, :], BF16)
    full = lambda a: pl.BlockSpec(a.shape, lambda i: (0,) * a.ndim)
    small = pl.BlockSpec((N_EXPERTS, LANES), lambda i: (0, 0))
    smallt = pl.BlockSpec((LANES, LANES), lambda i: (0, 0))
    return pl.pallas_call(
        functools.partial(_route_kernel, tokens=tokens),
        grid=(1,),
        in_specs=[full(afft), full(su), full(u)],
        out_specs=[small] * 4 + [smallt] * 4,
        out_shape=[jax.ShapeDtypeStruct((N_EXPERTS, LANES), I32)] * 4
        + [jax.ShapeDtypeStruct((LANES, LANES), I32)] + [jax.ShapeDtypeStruct((LANES, LANES), F32)] * 3,
        compiler_params=_params(("arbitrary",)),
        name="route",
    )(afft, su, u)


def _gather_kernel(bsel_s, afft_ref, tau_ref, need_ref, beq_ref, x_ref, u_ref, xe_hbm,
                   stage_ref, tail_ref, xbuf_ref, sem_ref, xsem_ref, *, ntile, cap):
    t = pl.program_id(0)
    par = t & 1

    def aligned(e, tile):
        return pl.multiple_of(_floor_pow2(bsel_s[e, tile], ROW_ALIGN), ROW_ALIGN)

    def write(e, tile, buf, first_row=None):
        first_row = aligned(e, tile) if first_row is None else first_row
        return pltpu.make_async_copy(stage_ref.at[buf, e], xe_hbm.at[e, pl.ds(first_row, GATHER_BLOCK)],
                                     sem_ref.at[buf, e])

    @pl.when(t == 0)
    def _():
        tail_ref[...] = jnp.zeros_like(tail_ref)
        stage_ref[1] = jnp.zeros(stage_ref.shape[1:], BF16)
        for e in range(N_EXPERTS):
            write(e, 0, 1, first_row=cap).start()

    k = lax.bitcast_convert_type(afft_ref[...], I32)
    tau = tau_ref[:, :1]
    lane = lax.broadcasted_iota(I32, (N_EXPERTS, LANES), 1)
    beq = jnp.sum(jnp.where(lane == t, beq_ref[...].astype(F32), 0.0), axis=1, keepdims=True)
    eq = k == tau
    eq_cum = _dot(_ones_where(eq, BF16), u_ref[...]) + beq
    sel = (k > tau) | (eq & (eq_cum <= need_ref[:, :1].astype(F32)))
    rank = jnp.where(sel, _dot(_ones_where(sel, BF16), u_ref[...]) - 1.0, -1e4)

    row = lax.broadcasted_iota(I32, (GATHER_STACK, ROUTE_TILE), 0)
    in_block = row < GATHER_BLOCK
    row_f = row.astype(F32)
    offs, shifts, pieces = [], [], []
    for e in range(N_EXPERTS):
        off = (bsel_s[e, t] - aligned(e, t)).astype(F32)
        shift = _floor_pow2(bsel_s[e, t + 1], ROW_ALIGN) - aligned(e, t)
        target = jnp.where(in_block, row_f, row_f - float(GATHER_BLOCK) + shift.astype(F32))
        pieces.append(_ones_where(rank[e:e + 1, :] + off == target, BF16))
        offs.append(off)
        shifts.append(shift)
    res = _dot(jnp.concatenate(pieces, axis=0), x_ref[...])
    for e in range(N_EXPERTS):
        base = e * GATHER_STACK
        old = tail_ref[e]
        stage_ref[par, e, 0:ROW_ALIGN, :] = (res[base:base + ROW_ALIGN] + old).astype(BF16)
        stage_ref[par, e, ROW_ALIGN:GATHER_BLOCK, :] = res[base + ROW_ALIGN:base + GATHER_BLOCK].astype(BF16)
        tail_ref[e] = res[base + GATHER_BLOCK:base + GATHER_STACK] + jnp.where(shifts[e] == 0, old, 0.0)
        write(e, jnp.maximum(t - 1, 0), 1 - par).wait()
        write(e, t, par).start()

    extra = [_cdiv_pow2(jnp.maximum(bsel_s[e, t + 1] - aligned(e, t) - GATHER_BLOCK, 0), SLOT_CHUNK)
             for e in range(N_EXPERTS)]

    @pl.when(functools.reduce(jnp.maximum, extra) > 0)
    def _():
        row64 = lax.broadcasted_iota(I32, (SLOT_CHUNK, ROUTE_TILE), 0).astype(F32)
        for e in range(N_EXPERTS):
            def chunk(c, carry):
                first = GATHER_BLOCK + c * SLOT_CHUNK
                onehot = _ones_where(rank[e:e + 1, :] + offs[e] == row64 + first.astype(F32), BF16)
                xbuf_ref[...] = _dot(onehot, x_ref[...]).astype(BF16)
                dst = pl.multiple_of(aligned(e, t) + first, ROW_ALIGN)
                cp = pltpu.make_async_copy(xbuf_ref, xe_hbm.at[e, pl.ds(dst, SLOT_CHUNK)], xsem_ref.at[0])
                cp.start()
                cp.wait()
                return carry

            lax.fori_loop(0, extra[e], chunk, 0)

    @pl.when(t == ntile - 1)
    def _():
        for e in range(N_EXPERTS):
            write(e, t, par).wait()


def _gather(bsel_i, afft, tau, need, beq_i, xn, u):
    tokens = xn.shape[0]
    cap = CAPACITY_FACTOR * tokens // N_EXPERTS
    ntile = tokens // ROUTE_TILE
    table = pl.BlockSpec((N_EXPERTS, LANES), lambda t, *_: (0, 0))
    grid_spec = pltpu.PrefetchScalarGridSpec(
        num_scalar_prefetch=1,
        grid=(ntile,),
        in_specs=[
            pl.BlockSpec((N_EXPERTS, ROUTE_TILE), lambda t, *_: (0, t)),
            table, table, table,
            pl.BlockSpec((ROUTE_TILE, D_MODEL), lambda t, *_: (t, 0)),
            pl.BlockSpec(u.shape, lambda t, *_: (0, 0)),
        ],
        out_specs=pl.BlockSpec(memory_space=pl.ANY),
        scratch_shapes=[
            pltpu.VMEM((2, N_EXPERTS, GATHER_BLOCK, D_MODEL), BF16),
            pltpu.VMEM((N_EXPERTS, ROW_ALIGN, D_MODEL), F32),
            pltpu.VMEM((SLOT_CHUNK, D_MODEL), BF16),
            pltpu.SemaphoreType.DMA((2, N_EXPERTS)),
            pltpu.SemaphoreType.DMA((1,)),
        ],
    )
    return pl.pallas_call(
        functools.partial(_gather_kernel, ntile=ntile, cap=cap),
        grid_spec=grid_spec,
        out_shape=jax.ShapeDtypeStruct((N_EXPERTS, cap + GATHER_PAD, D_MODEL), BF16),
        compiler_params=_params(("arbitrary",)),
        name="gather",
    )(bsel_i, afft, tau, need, beq_i, xn, u)


def _ffn_kernel(xe_ref, wg_ref, wu_ref, wd_ref, ye_ref, acc_ref, *, cap, nf, tm):
    f = pl.program_id(1)
    tf = wg_ref.shape[2]
    chunks = [slice(j * FFN_CHUNK, (j + 1) * FFN_CHUNK) for j in range(tf // FFN_CHUNK)]
    cast = {}

    def weight(name, ref, j):
        if (name, j) not in cast:
            cast[name, j] = (ref[0, chunks[j], :] if name == "d" else ref[0, :, chunks[j]]).astype(BF16)
        return cast[name, j]

    @pl.when(f == 0)
    def _():
        acc_ref[...] = jnp.zeros_like(acc_ref)

    for i in range(cap // tm):
        r = slice(i * tm, (i + 1) * tm)
        x = xe_ref[0, r, :]
        y = None
        for j in range(len(chunks)):
            hg = _dot(x, weight("g", wg_ref, j))
            hu = _dot(x, weight("u", wu_ref, j))
            h = (hg * (1.0 / (1.0 + jnp.exp(-hg))) * hu).astype(BF16)
            part = _dot(h, weight("d", wd_ref, j))
            y = part if y is None else y + part
        acc_ref[r, :] += y

    @pl.when(f == nf - 1)
    def _():
        ye_ref[...] = acc_ref[...].astype(BF16)


def _ffn(xe, w_eg, w_eu, w_ed):
    cap = xe.shape[1] - GATHER_PAD
    tf = 512
    nf = D_FF // tf
    tm = min(cap, 1024)
    return pl.pallas_call(
        functools.partial(_ffn_kernel, cap=cap, nf=nf, tm=tm),
        grid=(N_EXPERTS, nf),
        in_specs=[
            pl.BlockSpec((1, cap, D_MODEL), lambda e, f: (e, 0, 0)),
            pl.BlockSpec((1, D_MODEL, tf), lambda e, f: (e, 0, f)),
            pl.BlockSpec((1, D_MODEL, tf), lambda e, f: (e, 0, f)),
            pl.BlockSpec((1, tf, D_MODEL), lambda e, f: (e, f, 0)),
        ],
        out_specs=pl.BlockSpec((cap, D_MODEL), lambda e, f: (e, 0)),
        out_shape=jax.ShapeDtypeStruct((N_EXPERTS * cap, D_MODEL), BF16),
        scratch_shapes=[pltpu.VMEM((cap, D_MODEL), F32)],
        compiler_params=_params(("arbitrary", "arbitrary")),
        name="ffn",
    )(xe, w_eg, w_eu, w_ed)


def _combine_kernel(bsel_s, x1_ref, aff_ref, taut_ref, needt_ref, beqt_ref, bselt_ref, low_ref, spread_ref, gf_ref,
                    ye_hbm, y_ref, buf_ref, xbuf_ref, sem_ref, xsem_ref, *, cap, total, ntile):
    t = pl.program_id(0)
    par = t & 1

    def aligned(e, tile):
        return _floor_pow2(bsel_s[e, tile], ROW_ALIGN)

    def window(e, tile, c):
        start = jnp.minimum(e * cap + aligned(e, tile) + c * SLOT_CHUNK, total - SLOT_CHUNK)
        return pl.multiple_of(start, ROW_ALIGN)

    def first_chunks(tile, buf):
        return [pltpu.make_async_copy(ye_hbm.at[pl.ds(window(e, tile, 0), SLOT_CHUNK)],
                                      buf_ref.at[buf, pl.ds(e * SLOT_CHUNK, SLOT_CHUNK)], sem_ref.at[buf, e])
                for e in range(N_EXPERTS)]

    @pl.when(t == 0)
    def _():
        for cp in first_chunks(0, 0):
            cp.start()

    for cp in first_chunks(jnp.minimum(t + 1, ntile - 1), 1 - par):
        cp.start()

    aff = aff_ref[...]
    k = lax.bitcast_convert_type(aff, I32)
    tau = taut_ref[0:1, :]
    low = low_ref[...]
    eq = k == tau
    eq_cum = _dot(low, _ones_where(eq, BF16)) + beqt_ref[0]
    sel = (k > tau) | (eq & (eq_cum <= needt_ref[0:1, :]))
    slot = jnp.where(sel, _dot(low, _ones_where(sel, BF16)) + (bselt_ref[0] - 1.0), -1.0)

    lane = lax.broadcasted_iota(I32, (1, LANES), 1)
    rel = jnp.zeros((1, LANES), F32)
    for e in range(N_EXPERTS):
        rel = jnp.where(lane == e, (window(e, t, 0) - e * cap).astype(F32), rel)
    d = slot - rel
    d = jnp.where(sel & (d >= 0.0) & (d < float(SLOT_CHUNK)), d, -1.0)
    spread = spread_ref[...]
    wide = lax.broadcasted_iota(I32, (ROUTE_TILE, N_EXPERTS * SLOT_CHUNK), 1)
    hit = _dot(d.astype(BF16), spread) == (wide & (SLOT_CHUNK - 1)).astype(F32)
    onehot_gate = jnp.where(hit, _dot(aff.astype(BF16), spread), 0.0).astype(BF16)
    for cp in first_chunks(t, par):
        cp.wait()
    y_ref[...] = x1_ref[...] + _dot(onehot_gate, buf_ref[par])

    nch = [_cdiv_pow2(bsel_s[e, t + 1] - aligned(e, t), SLOT_CHUNK) for e in range(N_EXPERTS)]

    @pl.when(functools.reduce(jnp.maximum, nch) > 1)
    def _():
        lane64 = lax.broadcasted_iota(I32, (ROUTE_TILE, SLOT_CHUNK), 1).astype(F32)
        for e in range(N_EXPERTS):
            slot_e = slot[:, e:e + 1]

            def extra(c, carry):
                w = window(e, t, c)
                cp = pltpu.make_async_copy(ye_hbm.at[pl.ds(w, SLOT_CHUNK)], xbuf_ref, xsem_ref.at[0])
                cp.start()
                cp.wait()
                first = (aligned(e, t) + c * SLOT_CHUNK).astype(F32)
                hit = (lane64 + (w - e * cap).astype(F32) == slot_e) & (slot_e >= first)
                y_ref[...] += aff[:, e:e + 1] * _dot(_ones_where(hit, BF16), xbuf_ref[...])
                return carry

            lax.fori_loop(1, nch[e], extra, 0)

    acc = y_ref[...]
    ms = jnp.mean(acc * acc, axis=-1, keepdims=True)
    y_ref[...] = acc * lax.rsqrt(ms + EPS) * gf_ref[...]

    @pl.when(t == ntile - 1)
    def _():
        for cp in first_chunks(t, 1 - par):
            cp.wait()


def _combine(bsel_i, x1, aff, tables_t, ye, gf):
    tokens = x1.shape[0]
    cap = CAPACITY_FACTOR * tokens // N_EXPERTS
    ntile = tokens // ROUTE_TILE
    idx = np.arange(ROUTE_TILE)
    low = jnp.asarray(idx[:, None] >= idx[None, :], BF16)
    spread = jnp.asarray(np.arange(LANES)[:, None] == np.arange(N_EXPERTS * SLOT_CHUNK)[None, :] // SLOT_CHUNK, BF16)
    taut, needt, beqt, bselt = tables_t
    rowvec = pl.BlockSpec((8, LANES), lambda t, *_: (0, 0))
    tilevec = pl.BlockSpec((1, 1, LANES), lambda t, *_: (t, 0, 0))
    grid_spec = pltpu.PrefetchScalarGridSpec(
        num_scalar_prefetch=1,
        grid=(ntile,),
        in_specs=[
            pl.BlockSpec((ROUTE_TILE, D_MODEL), lambda t, *_: (t, 0)),
            pl.BlockSpec((ROUTE_TILE, LANES), lambda t, *_: (t, 0)),
            rowvec, rowvec, tilevec, tilevec,
            pl.BlockSpec(low.shape, lambda t, *_: (0, 0)),
            pl.BlockSpec(spread.shape, lambda t, *_: (0, 0)),
            pl.BlockSpec((1, D_MODEL), lambda t, *_: (0, 0)),
            pl.BlockSpec(memory_space=pl.ANY),
        ],
        out_specs=pl.BlockSpec((ROUTE_TILE, D_MODEL), lambda t, *_: (t, 0)),
        scratch_shapes=[
            pltpu.VMEM((2, N_EXPERTS * SLOT_CHUNK, D_MODEL), BF16),
            pltpu.VMEM((SLOT_CHUNK, D_MODEL), BF16),
            pltpu.SemaphoreType.DMA((2, N_EXPERTS)),
            pltpu.SemaphoreType.DMA((1,)),
        ],
    )
    return pl.pallas_call(
        functools.partial(_combine_kernel, cap=cap, total=N_EXPERTS * cap, ntile=ntile),
        grid_spec=grid_spec,
        out_shape=jax.ShapeDtypeStruct((tokens, D_MODEL), F32),
        compiler_params=_params(("arbitrary",)),
        name="combine",
    )(bsel_i, x1, aff, taut, needt, beqt.reshape(LANES, 1, LANES), bselt.reshape(LANES, 1, LANES), low, spread,
      gf, ye)


def _encoder(x, w):
    batch, seq, _ = x.shape
    tokens = batch * seq
    xt = x.reshape(tokens, D_MODEL)
    *qkvs, vr, vi, gates = _in_proj(xt, w["g1"], w["w_in"], w["w_gate"], w["b_gate"], w["cs"], batch, seq)
    outs, lses = [], []
    for g in range(N_GROUPS):
        o, lse = _attention(qkvs[g], w["bias"][g], g)
        outs.append(o)
        lses.append(lse)
    four = _fourier(vr, vi, batch, seq)
    x1, xn, aff, afft = _mix(xt, outs, lses, four, gates, w["w_attn"], w["w_four"], w["w_out"], w["g2"],
                             w["w_router_hi"], w["w_router_lo"])
    tau, need, beq_i, bsel_i, taut, needt, beqt, bselt = _route(afft)
    idx = np.arange(ROUTE_TILE)
    u = jnp.asarray(idx[:, None] <= idx[None, :], BF16)
    xe = _gather(bsel_i, afft, tau, need, beq_i, xn, u)
    ye = _ffn(xe, w["w_eg"], w["w_eu"], w["w_ed"])
    y = _combine(bsel_i, x1, aff, (taut, needt, beqt, bselt), ye, w["gf"])
    return y.reshape(batch, seq, D_MODEL)


def _prepare_weights(rel_bias, norm1_g, w_in, w_attn_br, w_four_br, w_gate, b_gate, w_out,
                     norm2_g, w_router, w_exp_gate, w_exp_up, w_exp_down, final_g):
    c, s = _dft_mats(F_CH)
    starts = [part * ATT_W + g * GROUP_W for g in range(N_GROUPS) for part in range(3)]
    w_in_grouped = jnp.concatenate([w_in[0][:, s0:s0 + GROUP_W] for s0 in starts] + [w_in[0][:, QKV_W:]], axis=1)
    w_router = jnp.pad(w_router[0], ((0, 0), (0, LANES - N_EXPERTS)))
    w_router_hi = w_router.astype(BF16)
    return {
        "g1": norm1_g[0].reshape(1, D_MODEL),
        "w_in": w_in_grouped.astype(BF16),
        "w_gate": w_gate[0].astype(BF16),
        "b_gate": b_gate[0].reshape(1, 2 * D_MODEL),
        "cs": jnp.asarray(np.concatenate([c, s], axis=1), BF16),
        "bias": [_attention_bias(rel_bias, g) for g in range(N_GROUPS)],
        "w_attn": w_attn_br[0].astype(BF16),
        "w_four": w_four_br[0].astype(BF16),
        "w_out": w_out[0].astype(BF16),
        "g2": norm2_g[0].reshape(1, D_MODEL),
        "w_router_hi": w_router_hi,
        "w_router_lo": (w_router - w_router_hi.astype(F32)).astype(BF16),
        "w_eg": w_exp_gate[0],
        "w_eu": w_exp_up[0],
        "w_ed": w_exp_down[0],
        "gf": final_g.reshape(1, D_MODEL),
    }


def kernel(x_prompt, x_sample, rel_bias, norm1_g, w_in, w_attn_br, w_four_br, w_gate, b_gate, w_out,
           norm2_g, w_router, w_exp_gate, w_exp_up, w_exp_down, final_g):
    w = _prepare_weights(rel_bias, norm1_g, w_in, w_attn_br, w_four_br, w_gate, b_gate, w_out,
                         norm2_g, w_router, w_exp_gate, w_exp_up, w_exp_down, final_g)
    return (_encoder(x_prompt, w), _encoder(x_sample, w))
```

```python
import functools
import math

import numpy as np
import jax
import jax.numpy as jnp
from jax import lax
from jax.experimental import pallas as pl
from jax.experimental.pallas import tpu as pltpu

D_MODEL = 1024
HEAD_DIM = 64
HEADS_PER_GROUP = 4
GROUPS = ((128, 1), (512, 4), (2048, 16))
N_GROUPS = len(GROUPS)
GROUP_W = HEADS_PER_GROUP * HEAD_DIM
ATT_W = N_GROUPS * GROUP_W
QKV_W = 3 * ATT_W
F_GROUPS = 6
F_CH = 128
F_W = F_GROUPS * F_CH
NUM_BUCKETS = 32
MAX_DISTANCE = 1024
N_EXPERTS = 16
CAPACITY_FACTOR = 2
D_FF = 2048
EPS = 1e-6
NEG = -1e30

HALF_KEYS = 64
ATT_SUB = 128
ATT_OUT_ROWS = 8192
TOKEN_TILE = 512
ROUTE_TILE = 256
FFT_STEP_ROWS = 1024
FFN_CHUNK = 256
SLOT_CHUNK = 64
COMBINE_TILES = 2
ROW_ALIGN = 16
GATHER_BLOCK = SLOT_CHUNK + ROW_ALIGN
GATHER_STACK = GATHER_BLOCK + ROW_ALIGN
GATHER_PAD = GATHER_BLOCK
LANES = 128
V7X_VMEM_LIMIT = 56 * 1024 * 1024

F32 = jnp.float32
BF16 = jnp.bfloat16
I32 = jnp.int32


def _params(sem):
    return pltpu.CompilerParams(dimension_semantics=sem, vmem_limit_bytes=V7X_VMEM_LIMIT)


def _dot(a, b):
    return jnp.dot(a, b, preferred_element_type=F32)


def _dot_nt(a, b):
    return lax.dot_general(a, b, (((1,), (1,)), ((), ())), preferred_element_type=F32)


def _floor_pow2(x, m):
    return x & ~(m - 1)


def _cdiv_pow2(x, m):
    return (x + (m - 1)) >> (m.bit_length() - 1)


def _ones_where(mask, dtype=F32):
    return jnp.where(mask, jnp.ones((), F32), jnp.zeros((), F32)).astype(dtype)


def _in_proj_kernel(x_ref, g_ref, win_ref, wg_ref, bg_ref, cs_ref, qscale_ref, qkv0_ref, qkv1_ref, qkv2_ref,
                    vr_ref, vi_ref, gates_ref, slab_ref):
    tm = x_ref.shape[0]
    half = tm // 2
    nslab = ATT_W // LANES
    cs = cs_ref[...]
    for h in range(2):
        rows = slice(h * half, (h + 1) * half)
        x = x_ref[rows, :]
        ms = jnp.mean(x * x, axis=-1, keepdims=True)
        xn = (x * lax.rsqrt(ms + EPS) * g_ref[...]).astype(BF16)
        for g, out_ref in enumerate((qkv0_ref, qkv1_ref, qkv2_ref)):
            dil = GROUPS[g][1]
            res = _dot(xn, win_ref[:, g * ATT_W:(g + 1) * ATT_W]) * qscale_ref[...]
            if dil == 1:
                out_ref[0, 0, rows, :] = res.astype(BF16)
                continue
            for j in range(nslab):
                slab_ref[j, rows, :] = res[:, j * LANES:(j + 1) * LANES]
            n = half // dil
            for r in range(dil):
                cls = [slab_ref[j, pl.ds(h * half + r, n, stride=dil), :] for j in range(nslab)]
                out_ref[0, r, h * n:(h + 1) * n, :] = jnp.concatenate(cls, axis=1).astype(BF16)
        u = _dot(xn, win_ref[:, QKV_W:QKV_W + F_W]).astype(BF16)
        for g in range(F_GROUPS):
            a = _dot(u[:, g * F_CH:(g + 1) * F_CH], cs)
            vr_ref[rows, g * F_CH:(g + 1) * F_CH] = a[:, :F_CH].astype(BF16)
            vi_ref[rows, g * F_CH:(g + 1) * F_CH] = (-a[:, F_CH:]).astype(BF16)
        z = _dot(xn, wg_ref[...]) + bg_ref[...]
        gates_ref[rows, :] = (1.0 / (1.0 + jnp.exp(-z))).astype(BF16)


def _class_major_spec(tm, dil, width, per_batch):
    return pl.BlockSpec((1, dil, tm // dil, width), lambda i: (i // per_batch, 0, i % per_batch, 0))


def _in_proj(x, g1, w_in, w_gate, b_gate, cs, batch, seq):
    t = x.shape[0]
    tm = TOKEN_TILE
    per_batch = seq // tm
    const = lambda i: (0, 0)
    row = lambda i: (i, 0)
    qscale = np.ones((1, ATT_W), np.float32)
    qscale[:, :GROUP_W] = 1.0 / math.sqrt(HEAD_DIM)
    return pl.pallas_call(
        _in_proj_kernel,
        grid=(t // tm,),
        in_specs=[
            pl.BlockSpec((tm, D_MODEL), row),
            pl.BlockSpec((1, D_MODEL), const),
            pl.BlockSpec(w_in.shape, const),
            pl.BlockSpec(w_gate.shape, const),
            pl.BlockSpec((1, 2 * D_MODEL), const),
            pl.BlockSpec(cs.shape, const),
            pl.BlockSpec((1, ATT_W), const),
        ],
        out_specs=[_class_major_spec(tm, dil, ATT_W, per_batch) for _, dil in GROUPS] + [
            pl.BlockSpec((tm, F_W), row),
            pl.BlockSpec((tm, F_W), row),
            pl.BlockSpec((tm, 2 * D_MODEL), row),
        ],
        out_shape=[jax.ShapeDtypeStruct((batch, dil, seq // dil, ATT_W), BF16) for _, dil in GROUPS] + [
            jax.ShapeDtypeStruct((t, F_W), BF16),
            jax.ShapeDtypeStruct((t, F_W), BF16),
            jax.ShapeDtypeStruct((t, 2 * D_MODEL), BF16),
        ],
        scratch_shapes=[pltpu.VMEM((ATT_W // LANES, tm, LANES), F32)],
        compiler_params=_params(("parallel",)),
        name="in_proj",
    )(x, g1, w_in, w_gate, b_gate, cs, jnp.asarray(qscale))


def _attention_kernel(q_ref, kp_ref, kc_ref, kn_ref, vp_ref, vc_ref, vn_ref, bias_ref, o_ref, lse_ref, *,
                      tq, length, dil, rc):
    i = pl.program_id(1)
    win = ATT_SUB + 2 * HALF_KEYS
    nsub = tq // ATT_SUB
    lane_head = lax.broadcasted_iota(I32, (ATT_SUB, GROUP_W), 1) // HEAD_DIM
    at_start = (i == 0).astype(I32)
    at_end = (i == length // tq - 1).astype(I32) * 2
    for c, sb in [(c, sb) for c in range(rc) for sb in range(nsub)]:
        r = pl.program_id(2) * rc + c
        if sb == 0:
            kwin = jnp.concatenate([kp_ref[0, c], kc_ref[0, c], kn_ref[0, c]], axis=0)
            vwin = jnp.concatenate([vp_ref[0, c], vc_ref[0, c], vn_ref[0, c]], axis=0)
        off = sb * ATT_SUB
        q = q_ref[0, c, off:off + ATT_SUB, :]
        kw = kwin[off:off + win]
        vw = vwin[off:off + win]
        variant = (at_start if sb == 0 else 0) + (at_end if sb == nsub - 1 else 0)
        qs = jnp.concatenate(
            [jnp.where(lane_head == h, q, jnp.zeros_like(q)) for h in range(HEADS_PER_GROUP)], axis=0)
        s_all = _dot_nt(qs, kw)
        ps, ms, ls = [], [], []
        for h in range(HEADS_PER_GROUP):
            s = s_all[h * ATT_SUB:(h + 1) * ATT_SUB] + bias_ref[variant, h]
            m = jnp.max(s, axis=-1, keepdims=True)
            p = jnp.exp(s - m)
            ls.append(jnp.sum(p, axis=-1, keepdims=True))
            ms.append(m)
            ps.append(p.astype(BF16))
        o_all = _dot(jnp.concatenate(ps, axis=0), vw)
        out = jnp.zeros((ATT_SUB, GROUP_W), F32)
        lse = jnp.zeros((ATT_SUB, GROUP_W), F32)
        for h in range(HEADS_PER_GROUP):
            oh = o_all[h * ATT_SUB:(h + 1) * ATT_SUB] * (1.0 / ls[h])
            out = jnp.where(lane_head == h, oh, out)
            lse = jnp.where(lane_head == h, ms[h] + jnp.log(ls[h]), lse)
        rows = pl.ds(off * dil + r, ATT_SUB, stride=dil) if dil > 1 else pl.ds(off, ATT_SUB)
        for j in range(GROUP_W // LANES):
            o_ref[j, rows, :] = out[:, j * LANES:(j + 1) * LANES]
            lse_ref[j, rows, :] = lse[:, j * LANES:(j + 1) * LANES]


def _attention(qkv, bias, g):
    batch, dil, length, _ = qkv.shape
    tq = min(length, 512, ATT_OUT_ROWS // dil)
    nb = length // tq
    hb = tq // HALF_KEYS
    last_halo = length // HALF_KEYS - 1
    rc = min(dil, max(1, 512 // tq))

    def cur(c):
        return lambda b, i, r: (b, r, i, c)

    def prev(c):
        return lambda b, i, r: (b, r, jnp.maximum(i * hb - 1, 0), c)

    def nxt(c):
        return lambda b, i, r: (b, r, jnp.minimum((i + 1) * hb, last_halo), c)

    blk = lambda rows: (1, rc, rows, GROUP_W)
    out_spec = pl.BlockSpec((GROUP_W // LANES, tq * dil, LANES), lambda b, i, r: (0, b * nb + i, 0))
    return pl.pallas_call(
        functools.partial(_attention_kernel, tq=tq, length=length, dil=dil, rc=rc),
        grid=(batch, nb, dil // rc),
        in_specs=[
            pl.BlockSpec(blk(tq), cur(0)),
            pl.BlockSpec(blk(HALF_KEYS), prev(1)),
            pl.BlockSpec(blk(tq), cur(1)),
            pl.BlockSpec(blk(HALF_KEYS), nxt(1)),
            pl.BlockSpec(blk(HALF_KEYS), prev(2)),
            pl.BlockSpec(blk(tq), cur(2)),
            pl.BlockSpec(blk(HALF_KEYS), nxt(2)),
            pl.BlockSpec(bias.shape, lambda b, i, r: (0, 0, 0, 0)),
        ],
        out_specs=[out_spec] * 2,
        out_shape=[jax.ShapeDtypeStruct((GROUP_W // LANES, batch * dil * length, LANES), F32)] * 2,
        compiler_params=_params(("parallel", "parallel", "arbitrary")),
        name=f"attention_g{g}",
    )(qkv, qkv, qkv, qkv, qkv, qkv, qkv, bias)


def _t5_bucket(rel):
    nb = NUM_BUCKETS // 2
    max_exact = nb // 2
    ret = (rel > 0).astype(np.int32) * nb
    n = np.abs(rel)
    large = max_exact + (np.log(np.maximum(n, max_exact) / max_exact)
                         / np.log(MAX_DISTANCE / max_exact) * (nb - max_exact)).astype(np.int32)
    large = np.minimum(large, nb - 1)
    return (ret + np.where(n < max_exact, n, large)).astype(np.int32)


def _attention_bias(rel_bias, g):
    dil = GROUPS[g][1]
    qi = np.arange(ATT_SUB)[:, None]
    kj = np.arange(ATT_SUB + 2 * HALF_KEYS)[None, :]
    delta = kj - HALF_KEYS - qi
    band = np.abs(delta) <= HALF_KEYS
    bucket = _t5_bucket(dil * delta)
    tab = rel_bias[:, g * HEADS_PER_GROUP:(g + 1) * HEADS_PER_GROUP].astype(F32)
    onehot = jnp.asarray(bucket[..., None] == np.arange(NUM_BUCKETS), F32)
    bias = jnp.einsum("qkb,bh->hqk", onehot, tab, precision=lax.Precision.HIGHEST)
    masks = [band & ((kj >= HALF_KEYS) | ((v & 1) == 0)) & ((kj < ATT_SUB + HALF_KEYS) | ((v & 2) == 0))
             for v in range(4)]
    return jnp.where(jnp.asarray(np.stack(masks))[:, None], bias[None], NEG)


def _dft_mats(n):
    k = np.arange(n)
    ang = 2.0 * np.pi * ((k[:, None] * k[None, :]) % n) / n
    return np.cos(ang), np.sin(ang)


def _fft_stage1_kernel(vr_ref, vi_ref, m1_ref, twc_ref, tws_ref, zr_ref, zi_ref, *, n1, m):
    x = jnp.concatenate([vr_ref[0], vi_ref[0]], axis=0)
    z = _dot(m1_ref[...], x)
    zr, zi = z[:n1], z[n1:]
    twc, tws = twc_ref[0], tws_ref[0]
    for j in range(m):
        c = twc[:, j:j + 1]
        s = tws[:, j:j + 1]
        a = zr[:, j * F_W:(j + 1) * F_W]
        b = zi[:, j * F_W:(j + 1) * F_W]
        zr_ref[0, :, j * F_W:(j + 1) * F_W] = (a * c + b * s).astype(BF16)
        zi_ref[0, :, j * F_W:(j + 1) * F_W] = (b * c - a * s).astype(BF16)


def _fft_stage2_kernel(zr_ref, zi_ref, m2_ref, o_ref, *, kc, scale):
    m2 = m2_ref[...]
    for j in range(kc):
        x = jnp.concatenate([zr_ref[0, j], zi_ref[0, j]], axis=0)
        o_ref[0, :, j * F_W:(j + 1) * F_W] = (_dot(m2, x) * scale).astype(BF16)


def _fourier(vr, vi, batch, seq):
    n2 = LANES
    n1 = seq // n2
    m = min(n2, FFT_STEP_ROWS // n1)
    c1, s1 = _dft_mats(n1)
    m1 = jnp.asarray(np.block([[c1, s1], [-s1, c1]]), BF16)
    c2, s2 = _dft_mats(n2)
    m2 = jnp.asarray(np.concatenate([c2, s2], axis=1), BF16)
    k1 = np.arange(n1)[:, None]
    sv = np.arange(n2)[None, :]
    ang = 2.0 * np.pi * ((k1 * sv) % seq) / seq
    to_blocks = lambda a: jnp.asarray(a.reshape(n1, n2 // m, m).transpose(1, 0, 2), F32)
    twc, tws = to_blocks(np.cos(ang)), to_blocks(np.sin(ang))

    v3 = lambda a: a.reshape(batch, n1, n2 * F_W)
    blk = (1, n1, m * F_W)
    dmap = lambda b, j: (b, 0, j)
    tmap = lambda b, j: (j, 0, 0)
    zr, zi = pl.pallas_call(
        functools.partial(_fft_stage1_kernel, n1=n1, m=m),
        grid=(batch, n2 // m),
        in_specs=[
            pl.BlockSpec(blk, dmap),
            pl.BlockSpec(blk, dmap),
            pl.BlockSpec(m1.shape, lambda b, j: (0, 0)),
            pl.BlockSpec((1, n1, m), tmap),
            pl.BlockSpec((1, n1, m), tmap),
        ],
        out_specs=[pl.BlockSpec(blk, dmap), pl.BlockSpec(blk, dmap)],
        out_shape=[jax.ShapeDtypeStruct((batch, n1, n2 * F_W), BF16)] * 2,
        compiler_params=_params(("parallel", "parallel")),
        name="fft_stage1",
    )(v3(vr), v3(vi), m1, twc, tws)

    kc = min(n1, FFT_STEP_ROWS // n2 * 2)
    v4 = lambda a: a.reshape(batch, n1, n2, F_W)
    zblk = (1, kc, n2, F_W)
    zmap = lambda b, j: (b, j, 0, 0)
    out = pl.pallas_call(
        functools.partial(_fft_stage2_kernel, kc=kc, scale=1.0 / math.sqrt(seq * F_CH)),
        grid=(batch, n1 // kc),
        in_specs=[
            pl.BlockSpec(zblk, zmap),
            pl.BlockSpec(zblk, zmap),
            pl.BlockSpec(m2.shape, lambda b, j: (0, 0)),
        ],
        out_specs=pl.BlockSpec((1, n2, kc * F_W), lambda b, j: (b, 0, j)),
        out_shape=jax.ShapeDtypeStruct((batch, n2, n1 * F_W), BF16),
        compiler_params=_params(("parallel", "parallel")),
        name="fft_stage2",
    )(v4(zr), v4(zi), m2)
    return out.reshape(batch * seq, F_W)


def _mix_kernel(x_ref, o0_ref, o1_ref, o2_ref, l0_ref, l1_ref, l2_ref, four_ref, gates_ref,
                wa_ref, wf_ref, wo_ref, g2_ref, wrh_ref, wrl_ref, x1_ref, xn_ref, aff_ref, afft_ref):
    tm = x_ref.shape[0]
    half = tm // 2
    for rows in (slice(0, half), slice(half, tm)):
        def slabs(ref):
            return jnp.concatenate([ref[j, rows, :] for j in range(GROUP_W // LANES)], axis=1)

        f_br = _dot(four_ref[rows, :], wf_ref[...])
        l0, l1, l2 = slabs(l0_ref), slabs(l1_ref), slabs(l2_ref)
        mx = jnp.maximum(jnp.maximum(l0, l1), l2)
        e0, e1, e2 = jnp.exp(l0 - mx), jnp.exp(l1 - mx), jnp.exp(l2 - mx)
        att = (e0 * slabs(o0_ref) + e1 * slabs(o1_ref) + e2 * slabs(o2_ref)) * (1.0 / (e0 + e1 + e2))
        a_br = _dot(att.astype(BF16), wa_ref[...])
        mix = gates_ref[rows, :D_MODEL] * a_br + gates_ref[rows, D_MODEL:] * f_br
        x1 = x_ref[rows, :] + _dot(mix.astype(BF16), wo_ref[...])
        x1_ref[rows, :] = x1
        ms = jnp.mean(x1 * x1, axis=-1, keepdims=True)
        xn = x1 * lax.rsqrt(ms + EPS) * g2_ref[...]
        xn_ref[rows, :] = xn.astype(BF16)
        xh = xn.astype(BF16)
        xl = (xn - xh.astype(F32)).astype(BF16)
        logits = _dot(xh, wrh_ref[...]) + (_dot(xh, wrl_ref[...]) + _dot(xl, wrh_ref[...]))
        lane = lax.broadcasted_iota(I32, logits.shape, 1)
        logits = jnp.where(lane < N_EXPERTS, logits, NEG)
        p = jnp.exp(logits - jnp.max(logits, axis=-1, keepdims=True))
        aff = p * (1.0 / jnp.sum(p, axis=-1, keepdims=True))
        aff_ref[rows, :] = aff
        afft_ref[:, rows] = aff.T[:N_EXPERTS]


def _mix(x, os_, ls_, four, gates, w_attn, w_four, w_out, g2, w_router_hi, w_router_lo):
    t = x.shape[0]
    tm = TOKEN_TILE
    const = lambda i: (0, 0)
    row = lambda i: (i, 0)
    rows = lambda w: pl.BlockSpec((tm, w), row)
    full = lambda a: pl.BlockSpec(a.shape, const)
    slab = pl.BlockSpec((GROUP_W // LANES, tm, LANES), lambda i: (0, i, 0))
    return pl.pallas_call(
        _mix_kernel,
        grid=(t // tm,),
        in_specs=[rows(D_MODEL)] + [slab] * 6 + [rows(F_W), rows(2 * D_MODEL),
                  full(w_attn), full(w_four), full(w_out), full(g2), full(w_router_hi), full(w_router_lo)],
        out_specs=[rows(D_MODEL), rows(D_MODEL), rows(LANES), pl.BlockSpec((N_EXPERTS, tm), lambda i: (0, i))],
        out_shape=[
            jax.ShapeDtypeStruct((t, D_MODEL), F32),
            jax.ShapeDtypeStruct((t, D_MODEL), BF16),
            jax.ShapeDtypeStruct((t, LANES), F32),
            jax.ShapeDtypeStruct((N_EXPERTS, t), F32),
        ],
        compiler_params=_params(("parallel",)),
        name="mix",
    )(x, *os_, *ls_, four, gates, w_attn, w_four, w_out, g2, w_router_hi, w_router_lo)


def _route_kernel(afft_ref, su_ref, u_ref, tau_ref, need_ref, beq_ref, bsel_ref,
                  taut_ref, needt_ref, beqt_ref, bselt_ref, *, tokens):
    cap = CAPACITY_FACTOR * tokens // N_EXPERTS
    ntile = tokens // ROUTE_TILE
    shape = (N_EXPERTS, LANES)
    lane = lax.broadcasted_iota(I32, shape, 1)

    def keys(start, width):
        return lax.bitcast_convert_type(afft_ref[:, pl.ds(pl.multiple_of(start, LANES), width)], I32)

    span = min(tokens, 16 * LANES)

    def count(pred):
        def body(c, acc):
            hits = _ones_where(pred(keys(c * span, span)))
            for j in range(span // LANES):
                acc = acc + hits[:, j * LANES:(j + 1) * LANES]
            return acc
        acc = lax.fori_loop(0, tokens // span, body, jnp.zeros(shape, F32))
        return jnp.sum(acc, axis=1, keepdims=True)

    def bit_body(i, prefix):
        cand = prefix | lax.shift_left(jnp.ones(shape, I32), jnp.full(shape, 30 - i, I32))
        tot = count(lambda k: k >= cand[:, :1])
        return jnp.where(tot >= cap, cand, prefix)

    tau = lax.fori_loop(0, 31, bit_body, jnp.zeros(shape, I32))
    tau_col = tau[:, :1]
    n_gt = count(lambda k: k > tau_col)
    need = cap - n_gt

    def prefix_over_tiles(tab):
        return _dot(tab.astype(BF16), su_ref[...])

    def at_lane(tab, c):
        return jnp.sum(jnp.where(lane == c, tab, 0.0), axis=1, keepdims=True)

    def eq_body(c, tab):
        k = keys(c * ROUTE_TILE, ROUTE_TILE)
        cnt = jnp.sum(_ones_where(k == tau_col), axis=1, keepdims=True)
        return jnp.where(lane == c, cnt, tab)

    base_eq = prefix_over_tiles(lax.fori_loop(0, ntile, eq_body, jnp.zeros(shape, F32)))

    def sel_body(c, tab):
        k = keys(c * ROUTE_TILE, ROUTE_TILE)
        eq = k == tau_col
        eq_cum = _dot(_ones_where(eq, BF16), u_ref[...]) + at_lane(base_eq, c)
        sel = (k > tau_col) | (eq & (eq_cum <= need))
        cnt = jnp.sum(_ones_where(sel), axis=1, keepdims=True)
        return jnp.where(lane == c, cnt, tab)

    base_sel = prefix_over_tiles(lax.fori_loop(0, ntile, sel_body, jnp.zeros(shape, F32)))

    def transposed(val):
        return jnp.concatenate([val, jnp.zeros((LANES - N_EXPERTS, LANES), val.dtype)], axis=0).T

    tau_ref[...] = tau
    taut_ref[...] = transposed(tau)
    for val, ref, ref_t in ((jnp.broadcast_to(need, shape), need_ref, needt_ref),
                            (base_eq, beq_ref, beqt_ref), (base_sel, bsel_ref, bselt_ref)):
        ref[...] = val.astype(I32)
        ref_t[...] = transposed(val)


def _route(afft):
    tokens = afft.shape[1]
    idx = np.arange(LANES)
    su = jnp.asarray(idx[:, None] < idx[None, :], BF16)
    idx = np.arange(ROUTE_TILE)
    u = jnp.asarray(idx[:, None] <= idx[None, :], BF16)
    full = lambda a: pl.BlockSpec(a.shape, lambda i: (0,) * a.ndim)
    small = pl.BlockSpec((N_EXPERTS, LANES), lambda i: (0, 0))
    smallt = pl.BlockSpec((LANES, LANES), lambda i: (0, 0))
    return pl.pallas_call(
        functools.partial(_route_kernel, tokens=tokens),
        grid=(1,),
        in_specs=[full(afft), full(su), full(u)],
        out_specs=[small] * 4 + [smallt] * 4,
        out_shape=[jax.ShapeDtypeStruct((N_EXPERTS, LANES), I32)] * 4
        + [jax.ShapeDtypeStruct((LANES, LANES), I32)] + [jax.ShapeDtypeStruct((LANES, LANES), F32)] * 3,
        compiler_params=_params(("arbitrary",)),
        name="route",
    )(afft, su, u)


def _gather_kernel(bsel_s, afft_ref, tau_ref, need_ref, beq_ref, x_ref, u_ref, xe_hbm,
                   stage_ref, tail_ref, xbuf_ref, sem_ref, xsem_ref, *, ntile, cap):
    t = pl.program_id(0)
    par = t & 1

    def aligned(e, tile):
        return pl.multiple_of(_floor_pow2(bsel_s[e, tile], ROW_ALIGN), ROW_ALIGN)

    def write(e, tile, buf, first_row=None):
        first_row = aligned(e, tile) if first_row is None else first_row
        return pltpu.make_async_copy(stage_ref.at[buf, e], xe_hbm.at[e, pl.ds(first_row, GATHER_BLOCK)],
                                     sem_ref.at[buf])

    @pl.when(t == 0)
    def _():
        tail_ref[...] = jnp.zeros_like(tail_ref)
        stage_ref[1] = jnp.zeros(stage_ref.shape[1:], BF16)
        for e in range(N_EXPERTS):
            write(e, 0, 1, first_row=cap).start()

    k = lax.bitcast_convert_type(afft_ref[...], I32)
    tau = tau_ref[:, :1]
    lane = lax.broadcasted_iota(I32, (N_EXPERTS, LANES), 1)
    beq = jnp.sum(jnp.where(lane == t, beq_ref[...].astype(F32), 0.0), axis=1, keepdims=True)
    eq = k == tau
    eq_cum = _dot(_ones_where(eq, BF16), u_ref[...]) + beq
    sel = (k > tau) | (eq & (eq_cum <= need_ref[:, :1].astype(F32)))
    rank = jnp.where(sel, _dot(_ones_where(sel, BF16), u_ref[...]) - 1.0, -1e4)

    row = lax.broadcasted_iota(I32, (GATHER_STACK, ROUTE_TILE), 0)
    in_block = row < GATHER_BLOCK
    row_f = row.astype(F32)
    offs, shifts, pieces = [], [], []
    for e in range(N_EXPERTS):
        off = (bsel_s[e, t] - aligned(e, t)).astype(F32)
        shift = _floor_pow2(bsel_s[e, t + 1], ROW_ALIGN) - aligned(e, t)
        target = jnp.where(in_block, row_f, row_f - float(GATHER_BLOCK) + shift.astype(F32))
        pieces.append(_ones_where(rank[e:e + 1, :] + off == target, BF16))
        offs.append(off)
        shifts.append(shift)
    res = _dot(jnp.concatenate(pieces, axis=0), x_ref[...])
    for e in range(N_EXPERTS):
        base = e * GATHER_STACK
        old = tail_ref[e]
        stage_ref[par, e, 0:ROW_ALIGN, :] = (res[base:base + ROW_ALIGN] + old).astype(BF16)
        stage_ref[par, e, ROW_ALIGN:GATHER_BLOCK, :] = res[base + ROW_ALIGN:base + GATHER_BLOCK].astype(BF16)
        tail_ref[e] = res[base + GATHER_BLOCK:base + GATHER_STACK] + jnp.where(shifts[e] == 0, old, 0.0)
    for e in range(N_EXPERTS):
        write(e, jnp.maximum(t - 1, 0), 1 - par).wait()
    for e in range(N_EXPERTS):
        write(e, t, par).start()

    extra = [_cdiv_pow2(jnp.maximum(bsel_s[e, t + 1] - aligned(e, t) - GATHER_BLOCK, 0), SLOT_CHUNK)
             for e in range(N_EXPERTS)]

    @pl.when(functools.reduce(jnp.maximum, extra) > 0)
    def _():
        row64 = lax.broadcasted_iota(I32, (SLOT_CHUNK, ROUTE_TILE), 0).astype(F32)
        for e in range(N_EXPERTS):
            def chunk(c, carry):
                first = GATHER_BLOCK + c * SLOT_CHUNK
                onehot = _ones_where(rank[e:e + 1, :] + offs[e] == row64 + first.astype(F32), BF16)
                xbuf_ref[...] = _dot(onehot, x_ref[...]).astype(BF16)
                dst = pl.multiple_of(aligned(e, t) + first, ROW_ALIGN)
                cp = pltpu.make_async_copy(xbuf_ref, xe_hbm.at[e, pl.ds(dst, SLOT_CHUNK)], xsem_ref.at[0])
                cp.start()
                cp.wait()
                return carry

            lax.fori_loop(0, extra[e], chunk, 0)

    @pl.when(t == ntile - 1)
    def _():
        for e in range(N_EXPERTS):
            write(e, t, par).wait()


def _gather(bsel_i, afft, tau, need, beq_i, xn, u):
    tokens = xn.shape[0]
    cap = CAPACITY_FACTOR * tokens // N_EXPERTS
    ntile = tokens // ROUTE_TILE
    table = pl.BlockSpec((N_EXPERTS, LANES), lambda t, *_: (0, 0))
    grid_spec = pltpu.PrefetchScalarGridSpec(
        num_scalar_prefetch=1,
        grid=(ntile,),
        in_specs=[
            pl.BlockSpec((N_EXPERTS, ROUTE_TILE), lambda t, *_: (0, t)),
            table, table, table,
            pl.BlockSpec((ROUTE_TILE, D_MODEL), lambda t, *_: (t, 0)),
            pl.BlockSpec(u.shape, lambda t, *_: (0, 0)),
        ],
        out_specs=pl.BlockSpec(memory_space=pl.ANY),
        scratch_shapes=[
            pltpu.VMEM((2, N_EXPERTS, GATHER_BLOCK, D_MODEL), BF16),
            pltpu.VMEM((N_EXPERTS, ROW_ALIGN, D_MODEL), F32),
            pltpu.VMEM((SLOT_CHUNK, D_MODEL), BF16),
            pltpu.SemaphoreType.DMA((2,)),
            pltpu.SemaphoreType.DMA((1,)),
        ],
    )
    return pl.pallas_call(
        functools.partial(_gather_kernel, ntile=ntile, cap=cap),
        grid_spec=grid_spec,
        out_shape=jax.ShapeDtypeStruct((N_EXPERTS, cap + GATHER_PAD, D_MODEL), BF16),
        compiler_params=_params(("arbitrary",)),
        name="gather",
    )(bsel_i, afft, tau, need, beq_i, xn, u)


def _ffn_kernel(xe_ref, wg_ref, wu_ref, wd_ref, ye_ref, acc_ref, *, cap, nf, tm):
    f = pl.program_id(1)
    tf = wg_ref.shape[2]
    chunks = [slice(j * FFN_CHUNK, (j + 1) * FFN_CHUNK) for j in range(tf // FFN_CHUNK)]
    cast = {}

    def weight(name, ref, j):
        if (name, j) not in cast:
            cast[name, j] = (ref[0, chunks[j], :] if name == "d" else ref[0, :, chunks[j]]).astype(BF16)
        return cast[name, j]

    @pl.when(f == 0)
    def _():
        acc_ref[...] = jnp.zeros_like(acc_ref)

    for i in range(cap // tm):
        r = slice(i * tm, (i + 1) * tm)
        x = xe_ref[0, r, :]
        y = None
        for j in range(len(chunks)):
            hg = _dot(x, weight("g", wg_ref, j))
            hu = _dot(x, weight("u", wu_ref, j))
            h = (hg * (1.0 / (1.0 + jnp.exp(-hg))) * hu).astype(BF16)
            part = _dot(h, weight("d", wd_ref, j))
            y = part if y is None else y + part
        acc_ref[r, :] += y

    @pl.when(f == nf - 1)
    def _():
        ye_ref[...] = acc_ref[...].astype(BF16)


def _ffn(xe, w_eg, w_eu, w_ed):
    cap = xe.shape[1] - GATHER_PAD
    tf = 512
    nf = D_FF // tf
    tm = min(cap, 1024)
    return pl.pallas_call(
        functools.partial(_ffn_kernel, cap=cap, nf=nf, tm=tm),
        grid=(N_EXPERTS, nf),
        in_specs=[
            pl.BlockSpec((1, cap, D_MODEL), lambda e, f: (e, 0, 0)),
            pl.BlockSpec((1, D_MODEL, tf), lambda e, f: (e, 0, f)),
            pl.BlockSpec((1, D_MODEL, tf), lambda e, f: (e, 0, f)),
            pl.BlockSpec((1, tf, D_MODEL), lambda e, f: (e, f, 0)),
        ],
        out_specs=pl.BlockSpec((cap, D_MODEL), lambda e, f: (e, 0)),
        out_shape=jax.ShapeDtypeStruct((N_EXPERTS * cap, D_MODEL), BF16),
        scratch_shapes=[pltpu.VMEM((cap, D_MODEL), F32)],
        compiler_params=_params(("arbitrary", "arbitrary")),
        name="ffn",
    )(xe, w_eg, w_eu, w_ed)


def _combine_kernel(bsel_s, x1_ref, aff_ref, taut_ref, needt_ref, beqt_ref, bselt_ref, low_ref, spread_ref, gf_ref,
                    ye_hbm, y_ref, buf_ref, xbuf_ref, sem_ref, xsem_ref, *, cap, total, nstep):
    step = pl.program_id(0)
    par = step & 1
    per_tile = N_EXPERTS * SLOT_CHUNK

    def aligned(e, tile):
        return _floor_pow2(bsel_s[e, tile], ROW_ALIGN)

    def window(e, tile, c):
        start = jnp.minimum(e * cap + aligned(e, tile) + c * SLOT_CHUNK, total - SLOT_CHUNK)
        return pl.multiple_of(start, ROW_ALIGN)

    def first_chunks(stp, buf, sub):
        tile = stp * COMBINE_TILES + sub
        return [pltpu.make_async_copy(ye_hbm.at[pl.ds(window(e, tile, 0), SLOT_CHUNK)],
                                      buf_ref.at[buf, pl.ds(sub * per_tile + e * SLOT_CHUNK, SLOT_CHUNK)],
                                      sem_ref.at[buf])
                for e in range(N_EXPERTS)]

    @pl.when(step == 0)
    def _():
        for sub in range(COMBINE_TILES):
            for cp in first_chunks(0, 0, sub):
                cp.start()

    for sub in range(COMBINE_TILES):
        for cp in first_chunks(jnp.minimum(step + 1, nstep - 1), 1 - par, sub):
            cp.start()
    for sub in range(COMBINE_TILES):
        for cp in first_chunks(step, par, sub):
            cp.wait()

    tau = taut_ref[0:1, :]
    low = low_ref[...]
    spread = spread_ref[...]
    lane = lax.broadcasted_iota(I32, (1, LANES), 1)
    wide = lax.broadcasted_iota(I32, (ROUTE_TILE, per_tile), 1)
    in_chunk = (wide & (SLOT_CHUNK - 1)).astype(F32)
    slots, affs = [], []
    for sub in range(COMBINE_TILES):
        tile = step * COMBINE_TILES + sub
        rows = slice(sub * ROUTE_TILE, (sub + 1) * ROUTE_TILE)
        aff = aff_ref[rows, :]
        k = lax.bitcast_convert_type(aff, I32)
        eq = k == tau
        eq_cum = _dot(low, _ones_where(eq, BF16)) + beqt_ref[sub]
        sel = (k > tau) | (eq & (eq_cum <= needt_ref[0:1, :]))
        slot = jnp.where(sel, _dot(low, _ones_where(sel, BF16)) + (bselt_ref[sub] - 1.0), -1.0)

        rel = jnp.zeros((1, LANES), F32)
        for e in range(N_EXPERTS):
            rel = jnp.where(lane == e, (window(e, tile, 0) - e * cap).astype(F32), rel)
        d = slot - rel
        d = jnp.where(sel & (d >= 0.0) & (d < float(SLOT_CHUNK)), d, -1.0)
        hit = _dot(d.astype(BF16), spread) == in_chunk
        onehot_gate = jnp.where(hit, _dot(aff.astype(BF16), spread), 0.0).astype(BF16)
        y_ref[rows, :] = x1_ref[rows, :] + _dot(onehot_gate, buf_ref[par, sub * per_tile:(sub + 1) * per_tile, :])
        slots.append(slot)
        affs.append(aff)

    nch = [[_cdiv_pow2(bsel_s[e, step * COMBINE_TILES + sub + 1] - aligned(e, step * COMBINE_TILES + sub), SLOT_CHUNK)
            for e in range(N_EXPERTS)] for sub in range(COMBINE_TILES)]

    @pl.when(functools.reduce(jnp.maximum, [n for per_sub in nch for n in per_sub]) > 1)
    def _():
        lane64 = lax.broadcasted_iota(I32, (ROUTE_TILE, SLOT_CHUNK), 1).astype(F32)
        for sub in range(COMBINE_TILES):
            tile = step * COMBINE_TILES + sub
            rows = slice(sub * ROUTE_TILE, (sub + 1) * ROUTE_TILE)
            for e in range(N_EXPERTS):
                slot_e = slots[sub][:, e:e + 1]

                def extra(c, carry):
                    w = window(e, tile, c)
                    cp = pltpu.make_async_copy(ye_hbm.at[pl.ds(w, SLOT_CHUNK)], xbuf_ref, xsem_ref.at[0])
                    cp.start()
                    cp.wait()
                    first = (aligned(e, tile) + c * SLOT_CHUNK).astype(F32)
                    hit = (lane64 + (w - e * cap).astype(F32) == slot_e) & (slot_e >= first)
                    y_ref[rows, :] += affs[sub][:, e:e + 1] * _dot(_ones_where(hit, BF16), xbuf_ref[...])
                    return carry

                lax.fori_loop(1, nch[sub][e], extra, 0)

    acc = y_ref[...]
    ms = jnp.mean(acc * acc, axis=-1, keepdims=True)
    y_ref[...] = acc * lax.rsqrt(ms + EPS) * gf_ref[...]

    @pl.when(step == nstep - 1)
    def _():
        for sub in range(COMBINE_TILES):
            for cp in first_chunks(step, 1 - par, sub):
                cp.wait()


def _combine(bsel_i, x1, aff, tables_t, ye, gf):
    tokens = x1.shape[0]
    cap = CAPACITY_FACTOR * tokens // N_EXPERTS
    ntile = tokens // ROUTE_TILE
    idx = np.arange(ROUTE_TILE)
    low = jnp.asarray(idx[:, None] >= idx[None, :], BF16)
    spread = jnp.asarray(np.arange(LANES)[:, None] == np.arange(N_EXPERTS * SLOT_CHUNK)[None, :] // SLOT_CHUNK, BF16)
    taut, needt, beqt, bselt = tables_t
    rows = COMBINE_TILES * ROUTE_TILE
    rowvec = pl.BlockSpec((8, LANES), lambda t, *_: (0, 0))
    tilevec = pl.BlockSpec((COMBINE_TILES, 1, LANES), lambda t, *_: (t, 0, 0))
    grid_spec = pltpu.PrefetchScalarGridSpec(
        num_scalar_prefetch=1,
        grid=(ntile // COMBINE_TILES,),
        in_specs=[
            pl.BlockSpec((rows, D_MODEL), lambda t, *_: (t, 0)),
            pl.BlockSpec((rows, LANES), lambda t, *_: (t, 0)),
            rowvec, rowvec, tilevec, tilevec,
            pl.BlockSpec(low.shape, lambda t, *_: (0, 0)),
            pl.BlockSpec(spread.shape, lambda t, *_: (0, 0)),
            pl.BlockSpec((1, D_MODEL), lambda t, *_: (0, 0)),
            pl.BlockSpec(memory_space=pl.ANY),
        ],
        out_specs=pl.BlockSpec((rows, D_MODEL), lambda t, *_: (t, 0)),
        scratch_shapes=[
            pltpu.VMEM((2, COMBINE_TILES * N_EXPERTS * SLOT_CHUNK, D_MODEL), BF16),
            pltpu.VMEM((SLOT_CHUNK, D_MODEL), BF16),
            pltpu.SemaphoreType.DMA((2,)),
            pltpu.SemaphoreType.DMA((1,)),
        ],
    )
    return pl.pallas_call(
        functools.partial(_combine_kernel, cap=cap, total=N_EXPERTS * cap, nstep=ntile // COMBINE_TILES),
        grid_spec=grid_spec,
        out_shape=jax.ShapeDtypeStruct((tokens, D_MODEL), F32),
        compiler_params=_params(("arbitrary",)),
        name="combine",
    )(bsel_i, x1, aff, taut, needt, beqt.reshape(LANES, 1, LANES), bselt.reshape(LANES, 1, LANES), low, spread,
      gf, ye)


def _encoder(x, w):
    batch, seq, _ = x.shape
    tokens = batch * seq
    xt = x.reshape(tokens, D_MODEL)
    *qkvs, vr, vi, gates = _in_proj(xt, w["g1"], w["w_in"], w["w_gate"], w["b_gate"], w["cs"], batch, seq)
    outs, lses = [], []
    for g in range(N_GROUPS):
        o, lse = _attention(qkvs[g], w["bias"][g], g)
        outs.append(o)
        lses.append(lse)
    four = _fourier(vr, vi, batch, seq)
    x1, xn, aff, afft = _mix(xt, outs, lses, four, gates, w["w_attn"], w["w_four"], w["w_out"], w["g2"],
                             w["w_router_hi"], w["w_router_lo"])
    tau, need, beq_i, bsel_i, taut, needt, beqt, bselt = _route(afft)
    idx = np.arange(ROUTE_TILE)
    u = jnp.asarray(idx[:, None] <= idx[None, :], BF16)
    xe = _gather(bsel_i, afft, tau, need, beq_i, xn, u)
    ye = _ffn(xe, w["w_eg"], w["w_eu"], w["w_ed"])
    y = _combine(bsel_i, x1, aff, (taut, needt, beqt, bselt), ye, w["gf"])
    return y.reshape(batch, seq, D_MODEL)


def _prepare_weights(rel_bias, norm1_g, w_in, w_attn_br, w_four_br, w_gate, b_gate, w_out,
                     norm2_g, w_router, w_exp_gate, w_exp_up, w_exp_down, final_g):
    c, s = _dft_mats(F_CH)
    starts = [part * ATT_W + g * GROUP_W for g in range(N_GROUPS) for part in range(3)]
    w_in_grouped = jnp.concatenate([w_in[0][:, s0:s0 + GROUP_W] for s0 in starts] + [w_in[0][:, QKV_W:]], axis=1)
    w_router = jnp.pad(w_router[0], ((0, 0), (0, LANES - N_EXPERTS)))
    w_router_hi = w_router.astype(BF16)
    return {
        "g1": norm1_g[0].reshape(1, D_MODEL),
        "w_in": w_in_grouped.astype(BF16),
        "w_gate": w_gate[0].astype(BF16),
        "b_gate": b_gate[0].reshape(1, 2 * D_MODEL),
        "cs": jnp.asarray(np.concatenate([c, s], axis=1), BF16),
        "bias": [_attention_bias(rel_bias, g) for g in range(N_GROUPS)],
        "w_attn": w_attn_br[0].astype(BF16),
        "w_four": w_four_br[0].astype(BF16),
        "w_out": w_out[0].astype(BF16),
        "g2": norm2_g[0].reshape(1, D_MODEL),
        "w_router_hi": w_router_hi,
        "w_router_lo": (w_router - w_router_hi.astype(F32)).astype(BF16),
        "w_eg": w_exp_gate[0],
        "w_eu": w_exp_up[0],
        "w_ed": w_exp_down[0],
        "gf": final_g.reshape(1, D_MODEL),
    }


def kernel(x_prompt, x_sample, rel_bias, norm1_g, w_in, w_attn_br, w_four_br, w_gate, b_gate, w_out,
           norm2_g, w_router, w_exp_gate, w_exp_up, w_exp_down, final_g):
    w = _prepare_weights(rel_bias, norm1_g, w_in, w_attn_br, w_four_br, w_gate, b_gate, w_out,
                         norm2_g, w_router, w_exp_gate, w_exp_up, w_exp_down, final_g)
    return (_encoder(x_prompt, w), _encoder(x_sample, w))
```

```python
import functools
import math

import numpy as np
import jax
import jax.numpy as jnp
from jax import lax
from jax.experimental import pallas as pl
from jax.experimental.pallas import tpu as pltpu

D_MODEL = 1024
HEAD_DIM = 64
HEADS_PER_GROUP = 4
GROUPS = ((128, 1), (512, 4), (2048, 16))
N_GROUPS = len(GROUPS)
GROUP_W = HEADS_PER_GROUP * HEAD_DIM
ATT_W = N_GROUPS * GROUP_W
QKV_W = 3 * ATT_W
F_GROUPS = 6
F_CH = 128
F_W = F_GROUPS * F_CH
NUM_BUCKETS = 32
MAX_DISTANCE = 1024
N_EXPERTS = 16
CAPACITY_FACTOR = 2
D_FF = 2048
EPS = 1e-6
NEG = -1e30

HALF_KEYS = 64
ATT_SUB = 128
ATT_OUT_ROWS = 8192
TOKEN_TILE = 512
ROUTE_TILE = 256
FFT_STEP_ROWS = 1024
FFN_CHUNK = 256
SLOT_CHUNK = 64
COMBINE_TILES = 2
ROW_ALIGN = 16
GATHER_BLOCK = SLOT_CHUNK + ROW_ALIGN
GATHER_STACK = GATHER_BLOCK + ROW_ALIGN
GATHER_PAD = GATHER_BLOCK
LANES = 128
V7X_VMEM_LIMIT = 56 * 1024 * 1024

F32 = jnp.float32
BF16 = jnp.bfloat16
I32 = jnp.int32
U32 = jnp.uint32


def _params(sem):
    return pltpu.CompilerParams(dimension_semantics=sem, vmem_limit_bytes=V7X_VMEM_LIMIT)


def _dot(a, b):
    return jnp.dot(a, b, preferred_element_type=F32)


def _dot_nt(a, b):
    return lax.dot_general(a, b, (((1,), (1,)), ((), ())), preferred_element_type=F32)


def _floor_pow2(x, m):
    return x & ~(m - 1)


def _cdiv_pow2(x, m):
    return (x + (m - 1)) >> (m.bit_length() - 1)


def _ones_where(mask, dtype=F32):
    return jnp.where(mask, jnp.ones((), F32), jnp.zeros((), F32)).astype(dtype)


def _in_proj_kernel(x_ref, g_ref, win_ref, wg_ref, bg_ref, cs_ref, qscale_ref, qkv0_ref, qkv1_ref, qkv2_ref,
                    vr_ref, vi_ref, gates_ref, slab_ref):
    tm = x_ref.shape[0]
    half = tm // 2
    nslab = ATT_W // LANES
    cs = cs_ref[...]
    for h in range(2):
        rows = slice(h * half, (h + 1) * half)
        x = x_ref[rows, :]
        ms = jnp.mean(x * x, axis=-1, keepdims=True)
        xn = (x * lax.rsqrt(ms + EPS) * g_ref[...]).astype(BF16)
        for g, out_ref in enumerate((qkv0_ref, qkv1_ref, qkv2_ref)):
            dil = GROUPS[g][1]
            res = _dot(xn, win_ref[:, g * ATT_W:(g + 1) * ATT_W]) * qscale_ref[...]
            if dil == 1:
                out_ref[0, 0, rows, :] = res.astype(BF16)
                continue
            for j in range(nslab):
                slab_ref[j, rows, :] = res[:, j * LANES:(j + 1) * LANES]
            n = half // dil
            for r in range(dil):
                cls = [slab_ref[j, pl.ds(h * half + r, n, stride=dil), :] for j in range(nslab)]
                out_ref[0, r, h * n:(h + 1) * n, :] = jnp.concatenate(cls, axis=1).astype(BF16)
        u = _dot(xn, win_ref[:, QKV_W:QKV_W + F_W]).astype(BF16)
        for g in range(F_GROUPS):
            a = _dot(u[:, g * F_CH:(g + 1) * F_CH], cs)
            vr_ref[rows, g * F_CH:(g + 1) * F_CH] = a[:, :F_CH].astype(BF16)
            vi_ref[rows, g * F_CH:(g + 1) * F_CH] = (-a[:, F_CH:]).astype(BF16)
        z = _dot(xn, wg_ref[...]) + bg_ref[...]
        gates_ref[rows, :] = (1.0 / (1.0 + jnp.exp(-z))).astype(BF16)


def _class_major_spec(tm, dil, width, per_batch):
    return pl.BlockSpec((1, dil, tm // dil, width), lambda i: (i // per_batch, 0, i % per_batch, 0))


def _in_proj(x, g1, w_in, w_gate, b_gate, cs, batch, seq):
    t = x.shape[0]
    tm = TOKEN_TILE
    per_batch = seq // tm
    const = lambda i: (0, 0)
    row = lambda i: (i, 0)
    qscale = np.ones((1, ATT_W), np.float32)
    qscale[:, :GROUP_W] = 1.0 / math.sqrt(HEAD_DIM)
    return pl.pallas_call(
        _in_proj_kernel,
        grid=(t // tm,),
        in_specs=[
            pl.BlockSpec((tm, D_MODEL), row),
            pl.BlockSpec((1, D_MODEL), const),
            pl.BlockSpec(w_in.shape, const),
            pl.BlockSpec(w_gate.shape, const),
            pl.BlockSpec((1, 2 * D_MODEL), const),
            pl.BlockSpec(cs.shape, const),
            pl.BlockSpec((1, ATT_W), const),
        ],
        out_specs=[_class_major_spec(tm, dil, ATT_W, per_batch) for _, dil in GROUPS] + [
            pl.BlockSpec((tm, F_W), row),
            pl.BlockSpec((tm, F_W), row),
            pl.BlockSpec((tm, 2 * D_MODEL), row),
        ],
        out_shape=[jax.ShapeDtypeStruct((batch, dil, seq // dil, ATT_W), BF16) for _, dil in GROUPS] + [
            jax.ShapeDtypeStruct((t, F_W), BF16),
            jax.ShapeDtypeStruct((t, F_W), BF16),
            jax.ShapeDtypeStruct((t, 2 * D_MODEL), BF16),
        ],
        scratch_shapes=[pltpu.VMEM((ATT_W // LANES, tm, LANES), F32)],
        compiler_params=_params(("parallel",)),
        name="in_proj",
    )(x, g1, w_in, w_gate, b_gate, cs, jnp.asarray(qscale))


def _attention_kernel(q_ref, kp_ref, kc_ref, kn_ref, vp_ref, vc_ref, vn_ref, bias_ref, o_ref, lse_ref, *,
                      tq, length, dil, rc):
    i = pl.program_id(1)
    win = ATT_SUB + 2 * HALF_KEYS
    nsub = tq // ATT_SUB
    lane_head = lax.broadcasted_iota(I32, (ATT_SUB, GROUP_W), 1) // HEAD_DIM
    at_start = (i == 0).astype(I32)
    at_end = (i == length // tq - 1).astype(I32) * 2
    for c, sb in [(c, sb) for c in range(rc) for sb in range(nsub)]:
        r = pl.program_id(2) * rc + c
        if sb == 0:
            kwin = jnp.concatenate([kp_ref[0, c], kc_ref[0, c], kn_ref[0, c]], axis=0)
            vwin = jnp.concatenate([vp_ref[0, c], vc_ref[0, c], vn_ref[0, c]], axis=0)
        off = sb * ATT_SUB
        q = q_ref[0, c, off:off + ATT_SUB, :]
        kw = kwin[off:off + win]
        vw = vwin[off:off + win]
        variant = (at_start if sb == 0 else 0) + (at_end if sb == nsub - 1 else 0)
        qs = jnp.concatenate(
            [jnp.where(lane_head == h, q, jnp.zeros_like(q)) for h in range(HEADS_PER_GROUP)], axis=0)
        s_all = _dot_nt(qs, kw)
        ps, ms, ls = [], [], []
        for h in range(HEADS_PER_GROUP):
            s = s_all[h * ATT_SUB:(h + 1) * ATT_SUB] + bias_ref[variant, h]
            m = jnp.max(s, axis=-1, keepdims=True)
            p = jnp.exp(s - m)
            ls.append(jnp.sum(p, axis=-1, keepdims=True))
            ms.append(m)
            ps.append(p.astype(BF16))
        o_all = _dot(jnp.concatenate(ps, axis=0), vw)
        out = jnp.zeros((ATT_SUB, GROUP_W), F32)
        lse = jnp.zeros((ATT_SUB, GROUP_W), F32)
        for h in range(HEADS_PER_GROUP):
            oh = o_all[h * ATT_SUB:(h + 1) * ATT_SUB] * (1.0 / ls[h])
            out = jnp.where(lane_head == h, oh, out)
            lse = jnp.where(lane_head == h, ms[h] + jnp.log(ls[h]), lse)
        rows = pl.ds(off * dil + r, ATT_SUB, stride=dil) if dil > 1 else pl.ds(off, ATT_SUB)
        for j in range(GROUP_W // LANES):
            o_ref[j, rows, :] = out[:, j * LANES:(j + 1) * LANES]
            lse_ref[j, rows, :] = lse[:, j * LANES:(j + 1) * LANES]


def _attention(qkv, bias, g):
    batch, dil, length, _ = qkv.shape
    tq = min(length, 512, ATT_OUT_ROWS // dil)
    nb = length // tq
    hb = tq // HALF_KEYS
    last_halo = length // HALF_KEYS - 1
    rc = min(dil, max(1, 512 // tq))

    def cur(c):
        return lambda b, i, r: (b, r, i, c)

    def prev(c):
        return lambda b, i, r: (b, r, jnp.maximum(i * hb - 1, 0), c)

    def nxt(c):
        return lambda b, i, r: (b, r, jnp.minimum((i + 1) * hb, last_halo), c)

    blk = lambda rows: (1, rc, rows, GROUP_W)
    out_spec = pl.BlockSpec((GROUP_W // LANES, tq * dil, LANES), lambda b, i, r: (0, b * nb + i, 0))
    return pl.pallas_call(
        functools.partial(_attention_kernel, tq=tq, length=length, dil=dil, rc=rc),
        grid=(batch, nb, dil // rc),
        in_specs=[
            pl.BlockSpec(blk(tq), cur(0)),
            pl.BlockSpec(blk(HALF_KEYS), prev(1)),
            pl.BlockSpec(blk(tq), cur(1)),
            pl.BlockSpec(blk(HALF_KEYS), nxt(1)),
            pl.BlockSpec(blk(HALF_KEYS), prev(2)),
            pl.BlockSpec(blk(tq), cur(2)),
            pl.BlockSpec(blk(HALF_KEYS), nxt(2)),
            pl.BlockSpec(bias.shape, lambda b, i, r: (0, 0, 0, 0)),
        ],
        out_specs=[out_spec] * 2,
        out_shape=[jax.ShapeDtypeStruct((GROUP_W // LANES, batch * dil * length, LANES), F32)] * 2,
        compiler_params=_params(("parallel", "parallel", "arbitrary")),
        name=f"attention_g{g}",
    )(qkv, qkv, qkv, qkv, qkv, qkv, qkv, bias)


def _t5_bucket(rel):
    nb = NUM_BUCKETS // 2
    max_exact = nb // 2
    ret = (rel > 0).astype(np.int32) * nb
    n = np.abs(rel)
    large = max_exact + (np.log(np.maximum(n, max_exact) / max_exact)
                         / np.log(MAX_DISTANCE / max_exact) * (nb - max_exact)).astype(np.int32)
    large = np.minimum(large, nb - 1)
    return (ret + np.where(n < max_exact, n, large)).astype(np.int32)


def _attention_bias(rel_bias, g):
    dil = GROUPS[g][1]
    qi = np.arange(ATT_SUB)[:, None]
    kj = np.arange(ATT_SUB + 2 * HALF_KEYS)[None, :]
    delta = kj - HALF_KEYS - qi
    band = np.abs(delta) <= HALF_KEYS
    bucket = _t5_bucket(dil * delta)
    tab = rel_bias[:, g * HEADS_PER_GROUP:(g + 1) * HEADS_PER_GROUP].astype(F32)
    onehot = jnp.asarray(bucket[..., None] == np.arange(NUM_BUCKETS), F32)
    bias = jnp.einsum("qkb,bh->hqk", onehot, tab, precision=lax.Precision.HIGHEST)
    masks = [band & ((kj >= HALF_KEYS) | ((v & 1) == 0)) & ((kj < ATT_SUB + HALF_KEYS) | ((v & 2) == 0))
             for v in range(4)]
    return jnp.where(jnp.asarray(np.stack(masks))[:, None], bias[None], NEG)


def _dft_mats(n):
    k = np.arange(n)
    ang = 2.0 * np.pi * ((k[:, None] * k[None, :]) % n) / n
    return np.cos(ang), np.sin(ang)


def _pack_complex(re, im):
    hi = lax.bitcast_convert_type(re.astype(BF16).astype(F32), U32)
    lo = lax.bitcast_convert_type(im.astype(BF16).astype(F32), U32)
    return hi | lax.shift_right_logical(lo, jnp.full(lo.shape, 16, U32))


def _unpack_complex(word):
    re = lax.bitcast_convert_type(word & jnp.uint32(0xFFFF0000), F32)
    im = lax.bitcast_convert_type(lax.shift_left(word, jnp.full(word.shape, 16, U32)), F32)
    return re.astype(BF16), im.astype(BF16)


def _fft_stage1_kernel(vr_ref, vi_ref, m1_ref, twc_ref, tws_ref, z_ref, *, n1, m):
    x = jnp.concatenate([vr_ref[0], vi_ref[0]], axis=0)
    z = _dot(m1_ref[...], x)
    zr, zi = z[:n1], z[n1:]
    twc, tws = twc_ref[0], tws_ref[0]
    for j in range(m):
        c = twc[:, j:j + 1]
        s = tws[:, j:j + 1]
        a = zr[:, j * F_W:(j + 1) * F_W]
        b = zi[:, j * F_W:(j + 1) * F_W]
        z_ref[0, :, j, :] = _pack_complex(a * c + b * s, b * c - a * s)


def _fft_stage2_kernel(z_ref, m2_ref, o_ref, *, kc, scale):
    m2 = m2_ref[...]
    for j in range(kc):
        x = jnp.concatenate(_unpack_complex(z_ref[0, j]), axis=0)
        o_ref[0, :, j, :] = _dot(m2, x) * scale


def _fourier(vr, vi, batch, seq):
    n2 = LANES
    n1 = seq // n2
    m = min(n2, FFT_STEP_ROWS // n1)
    c1, s1 = _dft_mats(n1)
    m1 = jnp.asarray(np.block([[c1, s1], [-s1, c1]]), BF16)
    c2, s2 = _dft_mats(n2)
    m2 = jnp.asarray(np.concatenate([c2, s2], axis=1), BF16)
    k1 = np.arange(n1)[:, None]
    sv = np.arange(n2)[None, :]
    ang = 2.0 * np.pi * ((k1 * sv) % seq) / seq
    to_blocks = lambda a: jnp.asarray(a.reshape(n1, n2 // m, m).transpose(1, 0, 2), F32)
    twc, tws = to_blocks(np.cos(ang)), to_blocks(np.sin(ang))

    v3 = lambda a: a.reshape(batch, n1, n2 * F_W)
    blk = pl.BlockSpec((1, n1, m * F_W), lambda b, j: (b, 0, j))
    tmap = lambda b, j: (j, 0, 0)
    z = pl.pallas_call(
        functools.partial(_fft_stage1_kernel, n1=n1, m=m),
        grid=(batch, n2 // m),
        in_specs=[
            blk,
            blk,
            pl.BlockSpec(m1.shape, lambda b, j: (0, 0)),
            pl.BlockSpec((1, n1, m), tmap),
            pl.BlockSpec((1, n1, m), tmap),
        ],
        out_specs=pl.BlockSpec((1, n1, m, F_W), lambda b, j: (b, 0, j, 0)),
        out_shape=jax.ShapeDtypeStruct((batch, n1, n2, F_W), U32),
        compiler_params=_params(("parallel", "parallel")),
        name="fft_stage1",
    )(v3(vr), v3(vi), m1, twc, tws)

    kc = min(n1, FFT_STEP_ROWS // n2 * 2)
    out = pl.pallas_call(
        functools.partial(_fft_stage2_kernel, kc=kc, scale=1.0 / math.sqrt(seq * F_CH)),
        grid=(batch, n1 // kc),
        in_specs=[
            pl.BlockSpec((1, kc, n2, F_W), lambda b, j: (b, j, 0, 0)),
            pl.BlockSpec(m2.shape, lambda b, j: (0, 0)),
        ],
        out_specs=pl.BlockSpec((1, n2, kc, F_W), lambda b, j: (b, 0, j, 0)),
        out_shape=jax.ShapeDtypeStruct((batch, n2, n1, F_W), F32),
        compiler_params=_params(("parallel", "parallel")),
        name="fft_stage2",
    )(z, m2)
    return out.reshape(batch * seq, F_W)


def _mix_kernel(x_ref, o0_ref, o1_ref, o2_ref, l0_ref, l1_ref, l2_ref, four_ref, gates_ref,
                wa_ref, wf_ref, wo_ref, g2_ref, wr_ref, x1_ref, xn_ref, aff_ref, afft_ref):
    tm = x_ref.shape[0]
    half = tm // 2
    for rows in (slice(0, half), slice(half, tm)):
        def slabs(ref):
            return jnp.concatenate([ref[j, rows, :] for j in range(GROUP_W // LANES)], axis=1)

        f_br = _dot(four_ref[rows, :].astype(BF16), wf_ref[...])
        l0, l1, l2 = slabs(l0_ref), slabs(l1_ref), slabs(l2_ref)
        mx = jnp.maximum(jnp.maximum(l0, l1), l2)
        e0, e1, e2 = jnp.exp(l0 - mx), jnp.exp(l1 - mx), jnp.exp(l2 - mx)
        att = (e0 * slabs(o0_ref) + e1 * slabs(o1_ref) + e2 * slabs(o2_ref)) * (1.0 / (e0 + e1 + e2))
        a_br = _dot(att.astype(BF16), wa_ref[...])
        mix = gates_ref[rows, :D_MODEL] * a_br + gates_ref[rows, D_MODEL:] * f_br
        x1 = x_ref[rows, :] + _dot(mix.astype(BF16), wo_ref[...])
        x1_ref[rows, :] = x1
        ms = jnp.mean(x1 * x1, axis=-1, keepdims=True)
        xn = x1 * lax.rsqrt(ms + EPS) * g2_ref[...]
        xn_ref[rows, :] = xn.astype(BF16)
        xh = xn.astype(BF16)
        xl = (xn - xh.astype(F32)).astype(BF16)
        both = _dot(xh, wr_ref[...])
        logits = both[:, :LANES] + (both[:, LANES:] + _dot(xl, wr_ref[:, :LANES]))
        lane = lax.broadcasted_iota(I32, logits.shape, 1)
        logits = jnp.where(lane < N_EXPERTS, logits, NEG)
        p = jnp.exp(logits - jnp.max(logits, axis=-1, keepdims=True))
        aff = p * (1.0 / jnp.sum(p, axis=-1, keepdims=True))
        aff_ref[rows, :] = aff
        afft_ref[:, rows] = aff.T[:N_EXPERTS]


def _mix(x, os_, ls_, four, gates, w_attn, w_four, w_out, g2, w_router):
    t = x.shape[0]
    tm = TOKEN_TILE
    const = lambda i: (0, 0)
    row = lambda i: (i, 0)
    rows = lambda w: pl.BlockSpec((tm, w), row)
    full = lambda a: pl.BlockSpec(a.shape, const)
    slab = pl.BlockSpec((GROUP_W // LANES, tm, LANES), lambda i: (0, i, 0))
    return pl.pallas_call(
        _mix_kernel,
        grid=(t // tm,),
        in_specs=[rows(D_MODEL)] + [slab] * 6 + [rows(F_W), rows(2 * D_MODEL),
                  full(w_attn), full(w_four), full(w_out), full(g2), full(w_router)],
        out_specs=[rows(D_MODEL), rows(D_MODEL), rows(LANES), pl.BlockSpec((N_EXPERTS, tm), lambda i: (0, i))],
        out_shape=[
            jax.ShapeDtypeStruct((t, D_MODEL), F32),
            jax.ShapeDtypeStruct((t, D_MODEL), BF16),
            jax.ShapeDtypeStruct((t, LANES), F32),
            jax.ShapeDtypeStruct((N_EXPERTS, t), F32),
        ],
        compiler_params=_params(("parallel",)),
        name="mix",
    )(x, *os_, *ls_, four, gates, w_attn, w_four, w_out, g2, w_router)


def _route_kernel(afft_ref, su_ref, u_ref, tau_ref, need_ref, beq_ref, bsel_ref,
                  taut_ref, needt_ref, beqt_ref, bselt_ref, *, tokens):
    cap = CAPACITY_FACTOR * tokens // N_EXPERTS
    ntile = tokens // ROUTE_TILE
    shape = (N_EXPERTS, LANES)
    lane = lax.broadcasted_iota(I32, shape, 1)

    def keys(start, width):
        return lax.bitcast_convert_type(afft_ref[:, pl.ds(pl.multiple_of(start, LANES), width)], I32)

    span = min(tokens, 16 * LANES)

    def count(pred):
        def body(c, acc):
            hits = _ones_where(pred(keys(c * span, span)))
            for j in range(span // LANES):
                acc = acc + hits[:, j * LANES:(j + 1) * LANES]
            return acc
        acc = lax.fori_loop(0, tokens // span, body, jnp.zeros(shape, F32))
        return jnp.sum(acc, axis=1, keepdims=True)

    def bit_body(i, prefix):
        cand = prefix | lax.shift_left(jnp.ones(shape, I32), jnp.full(shape, 30 - i, I32))
        tot = count(lambda k: k >= cand[:, :1])
        return jnp.where(tot >= cap, cand, prefix)

    tau = lax.fori_loop(0, 31, bit_body, jnp.zeros(shape, I32))
    tau_col = tau[:, :1]
    n_gt = count(lambda k: k > tau_col)
    need = cap - n_gt

    def prefix_over_tiles(tab):
        return _dot(tab.astype(BF16), su_ref[...])

    def at_lane(tab, c):
        return jnp.sum(jnp.where(lane == c, tab, 0.0), axis=1, keepdims=True)

    def eq_body(c, tab):
        k = keys(c * ROUTE_TILE, ROUTE_TILE)
        cnt = jnp.sum(_ones_where(k == tau_col), axis=1, keepdims=True)
        return jnp.where(lane == c, cnt, tab)

    base_eq = prefix_over_tiles(lax.fori_loop(0, ntile, eq_body, jnp.zeros(shape, F32)))

    def sel_body(c, tab):
        k = keys(c * ROUTE_TILE, ROUTE_TILE)
        eq = k == tau_col
        eq_cum = _dot(_ones_where(eq, BF16), u_ref[...]) + at_lane(base_eq, c)
        sel = (k > tau_col) | (eq & (eq_cum <= need))
        cnt = jnp.sum(_ones_where(sel), axis=1, keepdims=True)
        return jnp.where(lane == c, cnt, tab)

    base_sel = prefix_over_tiles(lax.fori_loop(0, ntile, sel_body, jnp.zeros(shape, F32)))

    def transposed(val):
        return jnp.concatenate([val, jnp.zeros((LANES - N_EXPERTS, LANES), val.dtype)], axis=0).T

    tau_ref[...] = tau
    taut_ref[...] = transposed(tau)
    for val, ref, ref_t in ((jnp.broadcast_to(need, shape), need_ref, needt_ref),
                            (base_eq, beq_ref, beqt_ref), (base_sel, bsel_ref, bselt_ref)):
        ref[...] = val.astype(I32)
        ref_t[...] = transposed(val)


def _route(afft):
    tokens = afft.shape[1]
    idx = np.arange(LANES)
    su = jnp.asarray(idx[:, None] < idx[None, :], BF16)
    idx = np.arange(ROUTE_TILE)
    u = jnp.asarray(idx[:, None] <= idx[None, :], BF16)
    full = lambda a: pl.BlockSpec(a.shape, lambda i: (0,) * a.ndim)
    small = pl.BlockSpec((N_EXPERTS, LANES), lambda i: (0, 0))
    smallt = pl.BlockSpec((LANES, LANES), lambda i: (0, 0))
    return pl.pallas_call(
        functools.partial(_route_kernel, tokens=tokens),
        grid=(1,),
        in_specs=[full(afft), full(su), full(u)],
        out_specs=[small] * 4 + [smallt] * 4,
        out_shape=[jax.ShapeDtypeStruct((N_EXPERTS, LANES), I32)] * 4
        + [jax.ShapeDtypeStruct((LANES, LANES), I32)] + [jax.ShapeDtypeStruct((LANES, LANES), F32)] * 3,
        compiler_params=_params(("arbitrary",)),
        name="route",
    )(afft, su, u)


def _gather_kernel(bsel_s, afft_ref, tau_ref, need_ref, beq_ref, x_ref, u_ref, xe_hbm,
                   stage_ref, tail_ref, xbuf_ref, sem_ref, xsem_ref, *, ntile, cap):
    t = pl.program_id(0)
    par = t & 1

    def aligned(e, tile):
        return pl.multiple_of(_floor_pow2(bsel_s[e, tile], ROW_ALIGN), ROW_ALIGN)

    def write(e, tile, buf, first_row=None):
        first_row = aligned(e, tile) if first_row is None else first_row
        return pltpu.make_async_copy(stage_ref.at[buf, e], xe_hbm.at[e, pl.ds(first_row, GATHER_BLOCK)],
                                     sem_ref.at[buf])

    @pl.when(t == 0)
    def _():
        tail_ref[...] = jnp.zeros_like(tail_ref)
        stage_ref[1] = jnp.zeros(stage_ref.shape[1:], BF16)
        for e in range(N_EXPERTS):
            write(e, 0, 1, first_row=cap).start()

    k = lax.bitcast_convert_type(afft_ref[...], I32)
    tau = tau_ref[:, :1]
    lane = lax.broadcasted_iota(I32, (N_EXPERTS, LANES), 1)
    beq = jnp.sum(jnp.where(lane == t, beq_ref[...].astype(F32), 0.0), axis=1, keepdims=True)
    eq = k == tau
    eq_cum = _dot(_ones_where(eq, BF16), u_ref[...]) + beq
    sel = (k > tau) | (eq & (eq_cum <= need_ref[:, :1].astype(F32)))
    rank = jnp.where(sel, _dot(_ones_where(sel, BF16), u_ref[...]) - 1.0, -1e4)

    row = lax.broadcasted_iota(I32, (GATHER_STACK, ROUTE_TILE), 0)
    in_block = row < GATHER_BLOCK
    row_f = row.astype(F32)
    offs, shifts, pieces = [], [], []
    for e in range(N_EXPERTS):
        off = (bsel_s[e, t] - aligned(e, t)).astype(F32)
        shift = _floor_pow2(bsel_s[e, t + 1], ROW_ALIGN) - aligned(e, t)
        target = jnp.where(in_block, row_f, row_f - float(GATHER_BLOCK) + shift.astype(F32))
        pieces.append(_ones_where(rank[e:e + 1, :] + off == target, BF16))
        offs.append(off)
        shifts.append(shift)
    res = _dot(jnp.concatenate(pieces, axis=0), x_ref[...])
    for e in range(N_EXPERTS):
        base = e * GATHER_STACK
        old = tail_ref[e]
        stage_ref[par, e, 0:ROW_ALIGN, :] = (res[base:base + ROW_ALIGN] + old).astype(BF16)
        stage_ref[par, e, ROW_ALIGN:GATHER_BLOCK, :] = res[base + ROW_ALIGN:base + GATHER_BLOCK].astype(BF16)
        tail_ref[e] = res[base + GATHER_BLOCK:base + GATHER_STACK] + jnp.where(shifts[e] == 0, old, 0.0)
    for e in range(N_EXPERTS):
        write(e, jnp.maximum(t - 1, 0), 1 - par).wait()
    for e in range(N_EXPERTS):
        write(e, t, par).start()

    extra = [_cdiv_pow2(jnp.maximum(bsel_s[e, t + 1] - aligned(e, t) - GATHER_BLOCK, 0), SLOT_CHUNK)
             for e in range(N_EXPERTS)]

    @pl.when(functools.reduce(jnp.maximum, extra) > 0)
    def _():
        row64 = lax.broadcasted_iota(I32, (SLOT_CHUNK, ROUTE_TILE), 0).astype(F32)
        for e in range(N_EXPERTS):
            def chunk(c, carry):
                first = GATHER_BLOCK + c * SLOT_CHUNK
                onehot = _ones_where(rank[e:e + 1, :] + offs[e] == row64 + first.astype(F32), BF16)
                xbuf_ref[...] = _dot(onehot, x_ref[...]).astype(BF16)
                dst = pl.multiple_of(aligned(e, t) + first, ROW_ALIGN)
                cp = pltpu.make_async_copy(xbuf_ref, xe_hbm.at[e, pl.ds(dst, SLOT_CHUNK)], xsem_ref.at[0])
                cp.start()
                cp.wait()
                return carry

            lax.fori_loop(0, extra[e], chunk, 0)

    @pl.when(t == ntile - 1)
    def _():
        for e in range(N_EXPERTS):
            write(e, t, par).wait()


def _gather(bsel_i, afft, tau, need, beq_i, xn, u):
    tokens = xn.shape[0]
    cap = CAPACITY_FACTOR * tokens // N_EXPERTS
    ntile = tokens // ROUTE_TILE
    table = pl.BlockSpec((N_EXPERTS, LANES), lambda t, *_: (0, 0))
    grid_spec = pltpu.PrefetchScalarGridSpec(
        num_scalar_prefetch=1,
        grid=(ntile,),
        in_specs=[
            pl.BlockSpec((N_EXPERTS, ROUTE_TILE), lambda t, *_: (0, t)),
            table, table, table,
            pl.BlockSpec((ROUTE_TILE, D_MODEL), lambda t, *_: (t, 0)),
            pl.BlockSpec(u.shape, lambda t, *_: (0, 0)),
        ],
        out_specs=pl.BlockSpec(memory_space=pl.ANY),
        scratch_shapes=[
            pltpu.VMEM((2, N_EXPERTS, GATHER_BLOCK, D_MODEL), BF16),
            pltpu.VMEM((N_EXPERTS, ROW_ALIGN, D_MODEL), F32),
            pltpu.VMEM((SLOT_CHUNK, D_MODEL), BF16),
            pltpu.SemaphoreType.DMA((2,)),
            pltpu.SemaphoreType.DMA((1,)),
        ],
    )
    return pl.pallas_call(
        functools.partial(_gather_kernel, ntile=ntile, cap=cap),
        grid_spec=grid_spec,
        out_shape=jax.ShapeDtypeStruct((N_EXPERTS, cap + GATHER_PAD, D_MODEL), BF16),
        compiler_params=_params(("arbitrary",)),
        name="gather",
    )(bsel_i, afft, tau, need, beq_i, xn, u)


def _ffn_kernel(xe_ref, wg_ref, wu_ref, wd_ref, ye_ref, acc_ref, *, cap, nf, tm):
    f = pl.program_id(1)
    tf = wg_ref.shape[2]
    chunks = [slice(j * FFN_CHUNK, (j + 1) * FFN_CHUNK) for j in range(tf // FFN_CHUNK)]
    cast = {}

    def weight(name, ref, j):
        if (name, j) not in cast:
            cast[name, j] = (ref[0, chunks[j], :] if name == "d" else ref[0, :, chunks[j]]).astype(BF16)
        return cast[name, j]

    @pl.when(f == 0)
    def _():
        acc_ref[...] = jnp.zeros_like(acc_ref)

    for i in range(cap // tm):
        r = slice(i * tm, (i + 1) * tm)
        x = xe_ref[0, r, :]
        y = None
        for j in range(len(chunks)):
            hg = _dot(x, weight("g", wg_ref, j))
            hu = _dot(x, weight("u", wu_ref, j))
            h = (hg * (1.0 / (1.0 + jnp.exp(-hg))) * hu).astype(BF16)
            part = _dot(h, weight("d", wd_ref, j))
            y = part if y is None else y + part
        acc_ref[r, :] += y

    @pl.when(f == nf - 1)
    def _():
        ye_ref[...] = acc_ref[...].astype(BF16)


def _ffn(xe, w_eg, w_eu, w_ed):
    cap = xe.shape[1] - GATHER_PAD
    tf = 512
    nf = D_FF // tf
    tm = min(cap, 1024)
    return pl.pallas_call(
        functools.partial(_ffn_kernel, cap=cap, nf=nf, tm=tm),
        grid=(N_EXPERTS, nf),
        in_specs=[
            pl.BlockSpec((1, cap, D_MODEL), lambda e, f: (e, 0, 0)),
            pl.BlockSpec((1, D_MODEL, tf), lambda e, f: (e, 0, f)),
            pl.BlockSpec((1, D_MODEL, tf), lambda e, f: (e, 0, f)),
            pl.BlockSpec((1, tf, D_MODEL), lambda e, f: (e, f, 0)),
        ],
        out_specs=pl.BlockSpec((cap, D_MODEL), lambda e, f: (e, 0)),
        out_shape=jax.ShapeDtypeStruct((N_EXPERTS * cap, D_MODEL), BF16),
        scratch_shapes=[pltpu.VMEM((cap, D_MODEL), F32)],
        compiler_params=_params(("arbitrary", "arbitrary")),
        name="ffn",
    )(xe, w_eg, w_eu, w_ed)


def _combine_kernel(bsel_s, x1_ref, aff_ref, taut_ref, needt_ref, beqt_ref, bselt_ref, low_ref, spread_ref, gf_ref,
                    ye_hbm, y_ref, buf_ref, xbuf_ref, sem_ref, xsem_ref, *, cap, total, nstep):
    step = pl.program_id(0)
    par = step & 1
    per_tile = N_EXPERTS * SLOT_CHUNK

    def aligned(e, tile):
        return _floor_pow2(bsel_s[e, tile], ROW_ALIGN)

    def window(e, tile, c):
        start = jnp.minimum(e * cap + aligned(e, tile) + c * SLOT_CHUNK, total - SLOT_CHUNK)
        return pl.multiple_of(start, ROW_ALIGN)

    def first_chunks(stp, buf, sub):
        tile = stp * COMBINE_TILES + sub
        return [pltpu.make_async_copy(ye_hbm.at[pl.ds(window(e, tile, 0), SLOT_CHUNK)],
                                      buf_ref.at[buf, pl.ds(sub * per_tile + e * SLOT_CHUNK, SLOT_CHUNK)],
                                      sem_ref.at[buf])
                for e in range(N_EXPERTS)]

    @pl.when(step == 0)
    def _():
        for sub in range(COMBINE_TILES):
            for cp in first_chunks(0, 0, sub):
                cp.start()

    for sub in range(COMBINE_TILES):
        for cp in first_chunks(jnp.minimum(step + 1, nstep - 1), 1 - par, sub):
            cp.start()
    for sub in range(COMBINE_TILES):
        for cp in first_chunks(step, par, sub):
            cp.wait()

    tau = taut_ref[0:1, :]
    low = low_ref[...]
    spread = spread_ref[...]
    lane = lax.broadcasted_iota(I32, (1, LANES), 1)
    wide = lax.broadcasted_iota(I32, (ROUTE_TILE, per_tile), 1)
    in_chunk = (wide & (SLOT_CHUNK - 1)).astype(F32)
    slots, affs = [], []
    for sub in range(COMBINE_TILES):
        tile = step * COMBINE_TILES + sub
        rows = slice(sub * ROUTE_TILE, (sub + 1) * ROUTE_TILE)
        aff = aff_ref[rows, :]
        k = lax.bitcast_convert_type(aff, I32)
        eq = k == tau
        eq_cum = _dot(low, _ones_where(eq, BF16)) + beqt_ref[sub]
        sel = (k > tau) | (eq & (eq_cum <= needt_ref[0:1, :]))
        slot = jnp.where(sel, _dot(low, _ones_where(sel, BF16)) + (bselt_ref[sub] - 1.0), -1.0)

        rel = jnp.zeros((1, LANES), F32)
        for e in range(N_EXPERTS):
            rel = jnp.where(lane == e, (window(e, tile, 0) - e * cap).astype(F32), rel)
        d = slot - rel
        d = jnp.where(sel & (d >= 0.0) & (d < float(SLOT_CHUNK)), d, -1.0)
        hit = _dot(d.astype(BF16), spread) == in_chunk
        onehot_gate = jnp.where(hit, _dot(aff.astype(BF16), spread), 0.0).astype(BF16)
        y_ref[rows, :] = x1_ref[rows, :] + _dot(onehot_gate, buf_ref[par, sub * per_tile:(sub + 1) * per_tile, :])
        slots.append(slot)
        affs.append(aff)

    nch = [[_cdiv_pow2(bsel_s[e, step * COMBINE_TILES + sub + 1] - aligned(e, step * COMBINE_TILES + sub), SLOT_CHUNK)
            for e in range(N_EXPERTS)] for sub in range(COMBINE_TILES)]

    @pl.when(functools.reduce(jnp.maximum, [n for per_sub in nch for n in per_sub]) > 1)
    def _():
        lane64 = lax.broadcasted_iota(I32, (ROUTE_TILE, SLOT_CHUNK), 1).astype(F32)
        for sub in range(COMBINE_TILES):
            tile = step * COMBINE_TILES + sub
            rows = slice(sub * ROUTE_TILE, (sub + 1) * ROUTE_TILE)
            for e in range(N_EXPERTS):
                slot_e = slots[sub][:, e:e + 1]

                def extra(c, carry):
                    w = window(e, tile, c)
                    cp = pltpu.make_async_copy(ye_hbm.at[pl.ds(w, SLOT_CHUNK)], xbuf_ref, xsem_ref.at[0])
                    cp.start()
                    cp.wait()
                    first = (aligned(e, tile) + c * SLOT_CHUNK).astype(F32)
                    hit = (lane64 + (w - e * cap).astype(F32) == slot_e) & (slot_e >= first)
                    y_ref[rows, :] += affs[sub][:, e:e + 1] * _dot(_ones_where(hit, BF16), xbuf_ref[...])
                    return carry

                lax.fori_loop(1, nch[sub][e], extra, 0)

    acc = y_ref[...]
    ms = jnp.mean(acc * acc, axis=-1, keepdims=True)
    y_ref[...] = acc * lax.rsqrt(ms + EPS) * gf_ref[...]

    @pl.when(step == nstep - 1)
    def _():
        for sub in range(COMBINE_TILES):
            for cp in first_chunks(step, 1 - par, sub):
                cp.wait()


def _combine(bsel_i, x1, aff, tables_t, ye, gf):
    tokens = x1.shape[0]
    cap = CAPACITY_FACTOR * tokens // N_EXPERTS
    ntile = tokens // ROUTE_TILE
    idx = np.arange(ROUTE_TILE)
    low = jnp.asarray(idx[:, None] >= idx[None, :], BF16)
    spread = jnp.asarray(np.arange(LANES)[:, None] == np.arange(N_EXPERTS * SLOT_CHUNK)[None, :] // SLOT_CHUNK, BF16)
    taut, needt, beqt, bselt = tables_t
    rows = COMBINE_TILES * ROUTE_TILE
    rowvec = pl.BlockSpec((8, LANES), lambda t, *_: (0, 0))
    tilevec = pl.BlockSpec((COMBINE_TILES, 1, LANES), lambda t, *_: (t, 0, 0))
    grid_spec = pltpu.PrefetchScalarGridSpec(
        num_scalar_prefetch=1,
        grid=(ntile // COMBINE_TILES,),
        in_specs=[
            pl.BlockSpec((rows, D_MODEL), lambda t, *_: (t, 0)),
            pl.BlockSpec((rows, LANES), lambda t, *_: (t, 0)),
            rowvec, rowvec, tilevec, tilevec,
            pl.BlockSpec(low.shape, lambda t, *_: (0, 0)),
            pl.BlockSpec(spread.shape, lambda t, *_: (0, 0)),
            pl.BlockSpec((1, D_MODEL), lambda t, *_: (0, 0)),
            pl.BlockSpec(memory_space=pl.ANY),
        ],
        out_specs=pl.BlockSpec((rows, D_MODEL), lambda t, *_: (t, 0)),
        scratch_shapes=[
            pltpu.VMEM((2, COMBINE_TILES * N_EXPERTS * SLOT_CHUNK, D_MODEL), BF16),
            pltpu.VMEM((SLOT_CHUNK, D_MODEL), BF16),
            pltpu.SemaphoreType.DMA((2,)),
            pltpu.SemaphoreType.DMA((1,)),
        ],
    )
    return pl.pallas_call(
        functools.partial(_combine_kernel, cap=cap, total=N_EXPERTS * cap, nstep=ntile // COMBINE_TILES),
        grid_spec=grid_spec,
        out_shape=jax.ShapeDtypeStruct((tokens, D_MODEL), F32),
        compiler_params=_params(("arbitrary",)),
        name="combine",
    )(bsel_i, x1, aff, taut, needt, beqt.reshape(LANES, 1, LANES), bselt.reshape(LANES, 1, LANES), low, spread,
      gf, ye)


def _encoder(x, w):
    batch, seq, _ = x.shape
    tokens = batch * seq
    xt = x.reshape(tokens, D_MODEL)
    *qkvs, vr, vi, gates = _in_proj(xt, w["g1"], w["w_in"], w["w_gate"], w["b_gate"], w["cs"], batch, seq)
    outs, lses = [], []
    for g in range(N_GROUPS):
        o, lse = _attention(qkvs[g], w["bias"][g], g)
        outs.append(o)
        lses.append(lse)
    four = _fourier(vr, vi, batch, seq)
    x1, xn, aff, afft = _mix(xt, outs, lses, four, gates, w["w_attn"], w["w_four"], w["w_out"], w["g2"],
                             w["w_router"])
    tau, need, beq_i, bsel_i, taut, needt, beqt, bselt = _route(afft)
    idx = np.arange(ROUTE_TILE)
    u = jnp.asarray(idx[:, None] <= idx[None, :], BF16)
    xe = _gather(bsel_i, afft, tau, need, beq_i, xn, u)
    ye = _ffn(xe, w["w_eg"], w["w_eu"], w["w_ed"])
    y = _combine(bsel_i, x1, aff, (taut, needt, beqt, bselt), ye, w["gf"])
    return y.reshape(batch, seq, D_MODEL)


def _prepare_weights(rel_bias, norm1_g, w_in, w_attn_br, w_four_br, w_gate, b_gate, w_out,
                     norm2_g, w_router, w_exp_gate, w_exp_up, w_exp_down, final_g):
    c, s = _dft_mats(F_CH)
    starts = [part * ATT_W + g * GROUP_W for g in range(N_GROUPS) for part in range(3)]
    w_in_grouped = jnp.concatenate([w_in[0][:, s0:s0 + GROUP_W] for s0 in starts] + [w_in[0][:, QKV_W:]], axis=1)
    w_router = jnp.pad(w_router[0], ((0, 0), (0, LANES - N_EXPERTS)))
    w_router_hi = w_router.astype(BF16)
    return {
        "g1": norm1_g[0].reshape(1, D_MODEL),
        "w_in": w_in_grouped.astype(BF16),
        "w_gate": w_gate[0].astype(BF16),
        "b_gate": b_gate[0].reshape(1, 2 * D_MODEL),
        "cs": jnp.asarray(np.concatenate([c, s], axis=1), BF16),
        "bias": [_attention_bias(rel_bias, g) for g in range(N_GROUPS)],
        "w_attn": w_attn_br[0].astype(BF16),
        "w_four": w_four_br[0].astype(BF16),
        "w_out": w_out[0].astype(BF16),
        "g2": norm2_g[0].reshape(1, D_MODEL),
        "w_router": jnp.concatenate([w_router_hi, (w_router - w_router_hi.astype(F32)).astype(BF16)], axis=1),
        "w_eg": w_exp_gate[0],
        "w_eu": w_exp_up[0],
        "w_ed": w_exp_down[0],
        "gf": final_g.reshape(1, D_MODEL),
    }


def kernel(x_prompt, x_sample, rel_bias, norm1_g, w_in, w_attn_br, w_four_br, w_gate, b_gate, w_out,
           norm2_g, w_router, w_exp_gate, w_exp_up, w_exp_down, final_g):
    w = _prepare_weights(rel_bias, norm1_g, w_in, w_attn_br, w_four_br, w_gate, b_gate, w_out,
                         norm2_g, w_router, w_exp_gate, w_exp_up, w_exp_down, final_g)
    return (_encoder(x_prompt, w), _encoder(x_sample, w))
```

```python
import functools
import math

import numpy as np
import jax
import jax.numpy as jnp
from jax import lax
from jax.experimental import pallas as pl
from jax.experimental.pallas import tpu as pltpu

D_MODEL = 1024
HEAD_DIM = 64
HEADS_PER_GROUP = 4
GROUPS = ((128, 1), (512, 4), (2048, 16))
N_GROUPS = len(GROUPS)
GROUP_W = HEADS_PER_GROUP * HEAD_DIM
ATT_W = N_GROUPS * GROUP_W
QKV_W = 3 * ATT_W
F_GROUPS = 6
F_CH = 128
F_W = F_GROUPS * F_CH
NUM_BUCKETS = 32
MAX_DISTANCE = 1024
N_EXPERTS = 16
CAPACITY_FACTOR = 2
D_FF = 2048
EPS = 1e-6
NEG = -1e30

HALF_KEYS = 64
ATT_SUB = 128
ATT_OUT_ROWS = 8192
TOKEN_TILE = 512
ROUTE_TILE = 256
FFT_STEP_ROWS = 1024
FFN_CHUNK = 256
SLOT_CHUNK = 64
COMBINE_TILES = 2
ROW_ALIGN = 16
GATHER_BLOCK = SLOT_CHUNK + ROW_ALIGN
GATHER_STACK = GATHER_BLOCK + ROW_ALIGN
GATHER_PAD = GATHER_BLOCK
LANES = 128
V7X_VMEM_LIMIT = 56 * 1024 * 1024

F32 = jnp.float32
BF16 = jnp.bfloat16
I32 = jnp.int32
U32 = jnp.uint32


def _params(sem):
    return pltpu.CompilerParams(dimension_semantics=sem, vmem_limit_bytes=V7X_VMEM_LIMIT)


def _dot(a, b):
    return jnp.dot(a, b, preferred_element_type=F32)


def _dot_nt(a, b):
    return lax.dot_general(a, b, (((1,), (1,)), ((), ())), preferred_element_type=F32)


def _floor_pow2(x, m):
    return x & ~(m - 1)


def _cdiv_pow2(x, m):
    return (x + (m - 1)) >> (m.bit_length() - 1)


def _ones_where(mask, dtype=F32):
    return jnp.where(mask, jnp.ones((), F32), jnp.zeros((), F32)).astype(dtype)


def _in_proj_kernel(x_ref, g_ref, win_ref, wg_ref, bg_ref, cs_ref, qscale_ref, qkv0_ref, qkv1_ref, qkv2_ref,
                    vr_ref, vi_ref, gates_ref, slab_ref):
    tm = x_ref.shape[0]
    half = tm // 2
    nslab = ATT_W // LANES
    cs = cs_ref[...]
    for h in range(2):
        rows = slice(h * half, (h + 1) * half)
        x = x_ref[rows, :]
        ms = jnp.mean(x * x, axis=-1, keepdims=True)
        xn = (x * lax.rsqrt(ms + EPS) * g_ref[...]).astype(BF16)
        for g, out_ref in enumerate((qkv0_ref, qkv1_ref, qkv2_ref)):
            dil = GROUPS[g][1]
            res = _dot(xn, win_ref[:, g * ATT_W:(g + 1) * ATT_W]) * qscale_ref[...]
            if dil == 1:
                out_ref[0, 0, rows, :] = res.astype(BF16)
                continue
            for j in range(nslab):
                slab_ref[j, rows, :] = res[:, j * LANES:(j + 1) * LANES]
            n = half // dil
            for r in range(dil):
                cls = [slab_ref[j, pl.ds(h * half + r, n, stride=dil), :] for j in range(nslab)]
                out_ref[0, r, h * n:(h + 1) * n, :] = jnp.concatenate(cls, axis=1).astype(BF16)
        u = _dot(xn, win_ref[:, QKV_W:QKV_W + F_W]).astype(BF16)
        for g in range(F_GROUPS):
            a = _dot(u[:, g * F_CH:(g + 1) * F_CH], cs)
            vr_ref[rows, g * F_CH:(g + 1) * F_CH] = a[:, :F_CH].astype(BF16)
            vi_ref[rows, g * F_CH:(g + 1) * F_CH] = (-a[:, F_CH:]).astype(BF16)
        z = _dot(xn, wg_ref[...]) + bg_ref[...]
        gates_ref[rows, :] = (1.0 / (1.0 + jnp.exp(-z))).astype(BF16)


def _class_major_spec(tm, dil, width, per_batch):
    return pl.BlockSpec((1, dil, tm // dil, width), lambda i: (i // per_batch, 0, i % per_batch, 0))


def _in_proj(x, g1, w_in, w_gate, b_gate, cs, batch, seq):
    t = x.shape[0]
    tm = TOKEN_TILE
    per_batch = seq // tm
    const = lambda i: (0, 0)
    row = lambda i: (i, 0)
    qscale = np.ones((1, ATT_W), np.float32)
    qscale[:, :GROUP_W] = 1.0 / math.sqrt(HEAD_DIM)
    return pl.pallas_call(
        _in_proj_kernel,
        grid=(t // tm,),
        in_specs=[
            pl.BlockSpec((tm, D_MODEL), row),
            pl.BlockSpec((1, D_MODEL), const),
            pl.BlockSpec(w_in.shape, const),
            pl.BlockSpec(w_gate.shape, const),
            pl.BlockSpec((1, 2 * D_MODEL), const),
            pl.BlockSpec(cs.shape, const),
            pl.BlockSpec((1, ATT_W), const),
        ],
        out_specs=[_class_major_spec(tm, dil, ATT_W, per_batch) for _, dil in GROUPS] + [
            pl.BlockSpec((tm, F_W), row),
            pl.BlockSpec((tm, F_W), row),
            pl.BlockSpec((tm, 2 * D_MODEL), row),
        ],
        out_shape=[jax.ShapeDtypeStruct((batch, dil, seq // dil, ATT_W), BF16) for _, dil in GROUPS] + [
            jax.ShapeDtypeStruct((t, F_W), BF16),
            jax.ShapeDtypeStruct((t, F_W), BF16),
            jax.ShapeDtypeStruct((t, 2 * D_MODEL), BF16),
        ],
        scratch_shapes=[pltpu.VMEM((ATT_W // LANES, tm, LANES), F32)],
        compiler_params=_params(("parallel",)),
        name="in_proj",
    )(x, g1, w_in, w_gate, b_gate, cs, jnp.asarray(qscale))


def _attention_kernel(q_ref, kp_ref, kc_ref, kn_ref, vp_ref, vc_ref, vn_ref, bias_ref, o_ref, lse_ref, *,
                      tq, length, dil, rc):
    i = pl.program_id(1)
    win = ATT_SUB + 2 * HALF_KEYS
    nsub = tq // ATT_SUB
    lane_head = lax.broadcasted_iota(I32, (ATT_SUB, GROUP_W), 1) // HEAD_DIM
    at_start = (i == 0).astype(I32)
    at_end = (i == length // tq - 1).astype(I32) * 2
    for c, sb in [(c, sb) for c in range(rc) for sb in range(nsub)]:
        r = pl.program_id(2) * rc + c
        if sb == 0:
            kwin = jnp.concatenate([kp_ref[0, c], kc_ref[0, c], kn_ref[0, c]], axis=0)
            vwin = jnp.concatenate([vp_ref[0, c], vc_ref[0, c], vn_ref[0, c]], axis=0)
        off = sb * ATT_SUB
        q = q_ref[0, c, off:off + ATT_SUB, :]
        kw = kwin[off:off + win]
        vw = vwin[off:off + win]
        variant = (at_start if sb == 0 else 0) + (at_end if sb == nsub - 1 else 0)
        qs = jnp.concatenate(
            [jnp.where(lane_head == h, q, jnp.zeros_like(q)) for h in range(HEADS_PER_GROUP)], axis=0)
        s_all = _dot_nt(qs, kw)
        ps, ms, ls = [], [], []
        for h in range(HEADS_PER_GROUP):
            s = s_all[h * ATT_SUB:(h + 1) * ATT_SUB] + bias_ref[variant, h]
            m = jnp.max(s, axis=-1, keepdims=True)
            p = jnp.exp(s - m)
            ls.append(jnp.sum(p, axis=-1, keepdims=True))
            ms.append(m)
            ps.append(p.astype(BF16))
        o_all = _dot(jnp.concatenate(ps, axis=0), vw)
        out = jnp.zeros((ATT_SUB, GROUP_W), F32)
        lse = jnp.zeros((ATT_SUB, GROUP_W), F32)
        for h in range(HEADS_PER_GROUP):
            oh = o_all[h * ATT_SUB:(h + 1) * ATT_SUB] * (1.0 / ls[h])
            out = jnp.where(lane_head == h, oh, out)
            lse = jnp.where(lane_head == h, ms[h] + jnp.log(ls[h]), lse)
        rows = pl.ds(off * dil + r, ATT_SUB, stride=dil) if dil > 1 else pl.ds(off, ATT_SUB)
        for j in range(GROUP_W // LANES):
            o_ref[j, rows, :] = out[:, j * LANES:(j + 1) * LANES]
            lse_ref[j, rows, :] = lse[:, j * LANES:(j + 1) * LANES]


def _attention(qkv, bias, g):
    batch, dil, length, _ = qkv.shape
    tq = min(length, 512, ATT_OUT_ROWS // dil)
    nb = length // tq
    hb = tq // HALF_KEYS
    last_halo = length // HALF_KEYS - 1
    rc = min(dil, max(1, 512 // tq))

    def cur(c):
        return lambda b, i, r: (b, r, i, c)

    def prev(c):
        return lambda b, i, r: (b, r, jnp.maximum(i * hb - 1, 0), c)

    def nxt(c):
        return lambda b, i, r: (b, r, jnp.minimum((i + 1) * hb, last_halo), c)

    blk = lambda rows: (1, rc, rows, GROUP_W)
    out_spec = pl.BlockSpec((GROUP_W // LANES, tq * dil, LANES), lambda b, i, r: (0, b * nb + i, 0))
    return pl.pallas_call(
        functools.partial(_attention_kernel, tq=tq, length=length, dil=dil, rc=rc),
        grid=(batch, nb, dil // rc),
        in_specs=[
            pl.BlockSpec(blk(tq), cur(0)),
            pl.BlockSpec(blk(HALF_KEYS), prev(1)),
            pl.BlockSpec(blk(tq), cur(1)),
            pl.BlockSpec(blk(HALF_KEYS), nxt(1)),
            pl.BlockSpec(blk(HALF_KEYS), prev(2)),
            pl.BlockSpec(blk(tq), cur(2)),
            pl.BlockSpec(blk(HALF_KEYS), nxt(2)),
            pl.BlockSpec(bias.shape, lambda b, i, r: (0, 0, 0, 0)),
        ],
        out_specs=[out_spec] * 2,
        out_shape=[jax.ShapeDtypeStruct((GROUP_W // LANES, batch * dil * length, LANES), F32)] * 2,
        compiler_params=_params(("parallel", "parallel", "arbitrary")),
        name=f"attention_g{g}",
    )(qkv, qkv, qkv, qkv, qkv, qkv, qkv, bias)


def _t5_bucket(rel):
    nb = NUM_BUCKETS // 2
    max_exact = nb // 2
    ret = (rel > 0).astype(np.int32) * nb
    n = np.abs(rel)
    large = max_exact + (np.log(np.maximum(n, max_exact) / max_exact)
                         / np.log(MAX_DISTANCE / max_exact) * (nb - max_exact)).astype(np.int32)
    large = np.minimum(large, nb - 1)
    return (ret + np.where(n < max_exact, n, large)).astype(np.int32)


def _attention_bias(rel_bias, g):
    dil = GROUPS[g][1]
    qi = np.arange(ATT_SUB)[:, None]
    kj = np.arange(ATT_SUB + 2 * HALF_KEYS)[None, :]
    delta = kj - HALF_KEYS - qi
    band = np.abs(delta) <= HALF_KEYS
    bucket = _t5_bucket(dil * delta)
    tab = rel_bias[:, g * HEADS_PER_GROUP:(g + 1) * HEADS_PER_GROUP].astype(F32)
    onehot = jnp.asarray(bucket[..., None] == np.arange(NUM_BUCKETS), F32)
    bias = jnp.einsum("qkb,bh->hqk", onehot, tab, precision=lax.Precision.HIGHEST)
    masks = [band & ((kj >= HALF_KEYS) | ((v & 1) == 0)) & ((kj < ATT_SUB + HALF_KEYS) | ((v & 2) == 0))
             for v in range(4)]
    return jnp.where(jnp.asarray(np.stack(masks))[:, None], bias[None], NEG)


def _dft_mats(n):
    k = np.arange(n)
    ang = 2.0 * np.pi * ((k[:, None] * k[None, :]) % n) / n
    return np.cos(ang), np.sin(ang)


def _pack_bf16_pair(a, b):
    hi = lax.bitcast_convert_type(a.astype(BF16).astype(F32), U32)
    lo = lax.bitcast_convert_type(b.astype(BF16).astype(F32), U32)
    return hi | lax.shift_right_logical(lo, jnp.full(lo.shape, 16, U32))


def _unpack_bf16_pair(word):
    a = lax.bitcast_convert_type(word & jnp.uint32(0xFFFF0000), F32)
    b = lax.bitcast_convert_type(lax.shift_left(word, jnp.full(word.shape, 16, U32)), F32)
    return a.astype(BF16), b.astype(BF16)


def _fft_stage1_kernel(vr_ref, vi_ref, m1_ref, twc_ref, tws_ref, z_ref, *, n1, m):
    x = jnp.concatenate([vr_ref[0], vi_ref[0]], axis=0)
    z = _dot(m1_ref[...], x)
    zr, zi = z[:n1], z[n1:]
    twc, tws = twc_ref[0], tws_ref[0]
    for j in range(m):
        c = twc[:, j:j + 1]
        s = tws[:, j:j + 1]
        a = zr[:, j * F_W:(j + 1) * F_W]
        b = zi[:, j * F_W:(j + 1) * F_W]
        z_ref[0, :, j, :] = _pack_bf16_pair(a * c + b * s, b * c - a * s)


def _fft_stage2_kernel(z_ref, m2_ref, o_ref, *, kc, scale):
    m2 = m2_ref[...]
    for j in range(kc):
        x = jnp.concatenate(_unpack_bf16_pair(z_ref[0, j]), axis=0)
        y = _dot(m2, x) * scale
        o_ref[0, :, j, :] = _pack_bf16_pair(y[:, :F_W // 2], y[:, F_W // 2:])


def _fourier(vr, vi, batch, seq):
    n2 = LANES
    n1 = seq // n2
    m = min(n2, FFT_STEP_ROWS // n1)
    c1, s1 = _dft_mats(n1)
    m1 = jnp.asarray(np.block([[c1, s1], [-s1, c1]]), BF16)
    c2, s2 = _dft_mats(n2)
    m2 = jnp.asarray(np.concatenate([c2, s2], axis=1), BF16)
    k1 = np.arange(n1)[:, None]
    sv = np.arange(n2)[None, :]
    ang = 2.0 * np.pi * ((k1 * sv) % seq) / seq
    to_blocks = lambda a: jnp.asarray(a.reshape(n1, n2 // m, m).transpose(1, 0, 2), F32)
    twc, tws = to_blocks(np.cos(ang)), to_blocks(np.sin(ang))

    v3 = lambda a: a.reshape(batch, n1, n2 * F_W)
    blk = pl.BlockSpec((1, n1, m * F_W), lambda b, j: (b, 0, j))
    tmap = lambda b, j: (j, 0, 0)
    z = pl.pallas_call(
        functools.partial(_fft_stage1_kernel, n1=n1, m=m),
        grid=(batch, n2 // m),
        in_specs=[
            blk,
            blk,
            pl.BlockSpec(m1.shape, lambda b, j: (0, 0)),
            pl.BlockSpec((1, n1, m), tmap),
            pl.BlockSpec((1, n1, m), tmap),
        ],
        out_specs=pl.BlockSpec((1, n1, m, F_W), lambda b, j: (b, 0, j, 0)),
        out_shape=jax.ShapeDtypeStruct((batch, n1, n2, F_W), U32),
        compiler_params=_params(("parallel", "parallel")),
        name="fft_stage1",
    )(v3(vr), v3(vi), m1, twc, tws)

    kc = min(n1, FFT_STEP_ROWS // n2 * 2)
    out = pl.pallas_call(
        functools.partial(_fft_stage2_kernel, kc=kc, scale=1.0 / math.sqrt(seq * F_CH)),
        grid=(batch, n1 // kc),
        in_specs=[
            pl.BlockSpec((1, kc, n2, F_W), lambda b, j: (b, j, 0, 0)),
            pl.BlockSpec(m2.shape, lambda b, j: (0, 0)),
        ],
        out_specs=pl.BlockSpec((1, n2, kc, F_W // 2), lambda b, j: (b, 0, j, 0)),
        out_shape=jax.ShapeDtypeStruct((batch, n2, n1, F_W // 2), U32),
        compiler_params=_params(("parallel", "parallel")),
        name="fft_stage2",
    )(z, m2)
    return out.reshape(batch * seq, F_W // 2)


def _mix_kernel(x_ref, o0_ref, o1_ref, o2_ref, l0_ref, l1_ref, l2_ref, four_ref, gates_ref,
                wa_ref, wf_ref, wo_ref, g2_ref, wr_ref, x1_ref, xn_ref, aff_ref, afft_ref):
    tm = x_ref.shape[0]
    half = tm // 2
    for rows in (slice(0, half), slice(half, tm)):
        def slabs(ref):
            return jnp.concatenate([ref[j, rows, :] for j in range(GROUP_W // LANES)], axis=1)

        f_br = _dot(jnp.concatenate(_unpack_bf16_pair(four_ref[rows, :]), axis=1), wf_ref[...])
        l0, l1, l2 = slabs(l0_ref), slabs(l1_ref), slabs(l2_ref)
        mx = jnp.maximum(jnp.maximum(l0, l1), l2)
        e0, e1, e2 = jnp.exp(l0 - mx), jnp.exp(l1 - mx), jnp.exp(l2 - mx)
        att = (e0 * slabs(o0_ref) + e1 * slabs(o1_ref) + e2 * slabs(o2_ref)) * (1.0 / (e0 + e1 + e2))
        a_br = _dot(att.astype(BF16), wa_ref[...])
        mix = gates_ref[rows, :D_MODEL] * a_br + gates_ref[rows, D_MODEL:] * f_br
        x1 = x_ref[rows, :] + _dot(mix.astype(BF16), wo_ref[...])
        x1_ref[rows, :] = x1
        ms = jnp.mean(x1 * x1, axis=-1, keepdims=True)
        xn = x1 * lax.rsqrt(ms + EPS) * g2_ref[...]
        xn_ref[rows, :] = xn.astype(BF16)
        xh = xn.astype(BF16)
        xl = (xn - xh.astype(F32)).astype(BF16)
        both = _dot(xh, wr_ref[...])
        logits = both[:, :LANES] + (both[:, LANES:] + _dot(xl, wr_ref[:, :LANES]))
        lane = lax.broadcasted_iota(I32, logits.shape, 1)
        logits = jnp.where(lane < N_EXPERTS, logits, NEG)
        p = jnp.exp(logits - jnp.max(logits, axis=-1, keepdims=True))
        aff = p * (1.0 / jnp.sum(p, axis=-1, keepdims=True))
        aff_ref[rows, :] = aff
        afft_ref[:, rows] = aff.T[:N_EXPERTS]


def _mix(x, os_, ls_, four, gates, w_attn, w_four, w_out, g2, w_router):
    t = x.shape[0]
    tm = TOKEN_TILE
    const = lambda i: (0, 0)
    row = lambda i: (i, 0)
    rows = lambda w: pl.BlockSpec((tm, w), row)
    full = lambda a: pl.BlockSpec(a.shape, const)
    slab = pl.BlockSpec((GROUP_W // LANES, tm, LANES), lambda i: (0, i, 0))
    return pl.pallas_call(
        _mix_kernel,
        grid=(t // tm,),
        in_specs=[rows(D_MODEL)] + [slab] * 6 + [rows(F_W // 2), rows(2 * D_MODEL),
                  full(w_attn), full(w_four), full(w_out), full(g2), full(w_router)],
        out_specs=[rows(D_MODEL), rows(D_MODEL), rows(LANES), pl.BlockSpec((N_EXPERTS, tm), lambda i: (0, i))],
        out_shape=[
            jax.ShapeDtypeStruct((t, D_MODEL), F32),
            jax.ShapeDtypeStruct((t, D_MODEL), BF16),
            jax.ShapeDtypeStruct((t, LANES), F32),
            jax.ShapeDtypeStruct((N_EXPERTS, t), F32),
        ],
        compiler_params=_params(("parallel",)),
        name="mix",
    )(x, *os_, *ls_, four, gates, w_attn, w_four, w_out, g2, w_router)


def _route_kernel(afft_ref, su_ref, u_ref, tau_ref, need_ref, beq_ref, bsel_ref,
                  taut_ref, needt_ref, beqt_ref, bselt_ref, *, tokens):
    cap = CAPACITY_FACTOR * tokens // N_EXPERTS
    ntile = tokens // ROUTE_TILE
    shape = (N_EXPERTS, LANES)
    lane = lax.broadcasted_iota(I32, shape, 1)

    def keys(start, width):
        return lax.bitcast_convert_type(afft_ref[:, pl.ds(pl.multiple_of(start, LANES), width)], I32)

    span = min(tokens, 16 * LANES)

    def count(pred):
        def body(c, acc):
            hits = _ones_where(pred(keys(c * span, span)))
            for j in range(span // LANES):
                acc = acc + hits[:, j * LANES:(j + 1) * LANES]
            return acc
        acc = lax.fori_loop(0, tokens // span, body, jnp.zeros(shape, F32))
        return jnp.sum(acc, axis=1, keepdims=True)

    def bit_body(i, prefix):
        cand = prefix | lax.shift_left(jnp.ones(shape, I32), jnp.full(shape, 30 - i, I32))
        tot = count(lambda k: k >= cand[:, :1])
        return jnp.where(tot >= cap, cand, prefix)

    tau = lax.fori_loop(0, 31, bit_body, jnp.zeros(shape, I32))
    tau_col = tau[:, :1]
    n_gt = count(lambda k: k > tau_col)
    need = cap - n_gt

    def prefix_over_tiles(tab):
        return _dot(tab.astype(BF16), su_ref[...])

    def at_lane(tab, c):
        return jnp.sum(jnp.where(lane == c, tab, 0.0), axis=1, keepdims=True)

    def eq_body(c, tab):
        k = keys(c * ROUTE_TILE, ROUTE_TILE)
        cnt = jnp.sum(_ones_where(k == tau_col), axis=1, keepdims=True)
        return jnp.where(lane == c, cnt, tab)

    base_eq = prefix_over_tiles(lax.fori_loop(0, ntile, eq_body, jnp.zeros(shape, F32)))

    def sel_body(c, tab):
        k = keys(c * ROUTE_TILE, ROUTE_TILE)
        eq = k == tau_col
        eq_cum = _dot(_ones_where(eq, BF16), u_ref[...]) + at_lane(base_eq, c)
        sel = (k > tau_col) | (eq & (eq_cum <= need))
        cnt = jnp.sum(_ones_where(sel), axis=1, keepdims=True)
        return jnp.where(lane == c, cnt, tab)

    base_sel = prefix_over_tiles(lax.fori_loop(0, ntile, sel_body, jnp.zeros(shape, F32)))

    def transposed(val):
        return jnp.concatenate([val, jnp.zeros((LANES - N_EXPERTS, LANES), val.dtype)], axis=0).T

    tau_ref[...] = tau
    taut_ref[...] = transposed(tau)
    for val, ref, ref_t in ((jnp.broadcast_to(need, shape), need_ref, needt_ref),
                            (base_eq, beq_ref, beqt_ref), (base_sel, bsel_ref, bselt_ref)):
        ref[...] = val.astype(I32)
        ref_t[...] = transposed(val)


def _route(afft):
    tokens = afft.shape[1]
    idx = np.arange(LANES)
    su = jnp.asarray(idx[:, None] < idx[None, :], BF16)
    idx = np.arange(ROUTE_TILE)
    u = jnp.asarray(idx[:, None] <= idx[None, :], BF16)
    full = lambda a: pl.BlockSpec(a.shape, lambda i: (0,) * a.ndim)
    small = pl.BlockSpec((N_EXPERTS, LANES), lambda i: (0, 0))
    smallt = pl.BlockSpec((LANES, LANES), lambda i: (0, 0))
    return pl.pallas_call(
        functools.partial(_route_kernel, tokens=tokens),
        grid=(1,),
        in_specs=[full(afft), full(su), full(u)],
        out_specs=[small] * 4 + [smallt] * 4,
        out_shape=[jax.ShapeDtypeStruct((N_EXPERTS, LANES), I32)] * 4
        + [jax.ShapeDtypeStruct((LANES, LANES), I32)] + [jax.ShapeDtypeStruct((LANES, LANES), F32)] * 3,
        compiler_params=_params(("arbitrary",)),
        name="route",
    )(afft, su, u)


def _gather_kernel(bsel_s, afft_ref, tau_ref, need_ref, beq_ref, x_ref, u_ref, xe_hbm,
                   stage_ref, tail_ref, xbuf_ref, sem_ref, xsem_ref, *, ntile, cap):
    t = pl.program_id(0)
    par = t & 1

    def aligned(e, tile):
        return pl.multiple_of(_floor_pow2(bsel_s[e, tile], ROW_ALIGN), ROW_ALIGN)

    def write(e, tile, buf, first_row=None):
        first_row = aligned(e, tile) if first_row is None else first_row
        return pltpu.make_async_copy(stage_ref.at[buf, e], xe_hbm.at[e, pl.ds(first_row, GATHER_BLOCK)],
                                     sem_ref.at[buf])

    @pl.when(t == 0)
    def _():
        tail_ref[...] = jnp.zeros_like(tail_ref)
        stage_ref[1] = jnp.zeros(stage_ref.shape[1:], BF16)
        for e in range(N_EXPERTS):
            write(e, 0, 1, first_row=cap).start()

    k = lax.bitcast_convert_type(afft_ref[...], I32)
    tau = tau_ref[:, :1]
    lane = lax.broadcasted_iota(I32, (N_EXPERTS, LANES), 1)
    beq = jnp.sum(jnp.where(lane == t, beq_ref[...].astype(F32), 0.0), axis=1, keepdims=True)
    eq = k == tau
    eq_cum = _dot(_ones_where(eq, BF16), u_ref[...]) + beq
    sel = (k > tau) | (eq & (eq_cum <= need_ref[:, :1].astype(F32)))
    rank = jnp.where(sel, _dot(_ones_where(sel, BF16), u_ref[...]) - 1.0, -1e4)

    row = lax.broadcasted_iota(I32, (GATHER_STACK, ROUTE_TILE), 0)
    in_block = row < GATHER_BLOCK
    row_f = row.astype(F32)
    offs, shifts, pieces = [], [], []
    for e in range(N_EXPERTS):
        off = (bsel_s[e, t] - aligned(e, t)).astype(F32)
        shift = _floor_pow2(bsel_s[e, t + 1], ROW_ALIGN) - aligned(e, t)
        target = jnp.where(in_block, row_f, row_f - float(GATHER_BLOCK) + shift.astype(F32))
        pieces.append(_ones_where(rank[e:e + 1, :] + off == target, BF16))
        offs.append(off)
        shifts.append(shift)
    res = _dot(jnp.concatenate(pieces, axis=0), x_ref[...])
    for e in range(N_EXPERTS):
        base = e * GATHER_STACK
        old = tail_ref[e]
        stage_ref[par, e, 0:ROW_ALIGN, :] = (res[base:base + ROW_ALIGN] + old).astype(BF16)
        stage_ref[par, e, ROW_ALIGN:GATHER_BLOCK, :] = res[base + ROW_ALIGN:base + GATHER_BLOCK].astype(BF16)
        tail_ref[e] = res[base + GATHER_BLOCK:base + GATHER_STACK] + jnp.where(shifts[e] == 0, old, 0.0)
    for e in range(N_EXPERTS):
        write(e, jnp.maximum(t - 1, 0), 1 - par).wait()
    for e in range(N_EXPERTS):
        write(e, t, par).start()

    extra = [_cdiv_pow2(jnp.maximum(bsel_s[e, t + 1] - aligned(e, t) - GATHER_BLOCK, 0), SLOT_CHUNK)
             for e in range(N_EXPERTS)]

    @pl.when(functools.reduce(jnp.maximum, extra) > 0)
    def _():
        row64 = lax.broadcasted_iota(I32, (SLOT_CHUNK, ROUTE_TILE), 0).astype(F32)
        for e in range(N_EXPERTS):
            def chunk(c, carry):
                first = GATHER_BLOCK + c * SLOT_CHUNK
                onehot = _ones_where(rank[e:e + 1, :] + offs[e] == row64 + first.astype(F32), BF16)
                xbuf_ref[...] = _dot(onehot, x_ref[...]).astype(BF16)
                dst = pl.multiple_of(aligned(e, t) + first, ROW_ALIGN)
                cp = pltpu.make_async_copy(xbuf_ref, xe_hbm.at[e, pl.ds(dst, SLOT_CHUNK)], xsem_ref.at[0])
                cp.start()
                cp.wait()
                return carry

            lax.fori_loop(0, extra[e], chunk, 0)

    @pl.when(t == ntile - 1)
    def _():
        for e in range(N_EXPERTS):
            write(e, t, par).wait()


def _gather(bsel_i, afft, tau, need, beq_i, xn, u):
    tokens = xn.shape[0]
    cap = CAPACITY_FACTOR * tokens // N_EXPERTS
    ntile = tokens // ROUTE_TILE
    table = pl.BlockSpec((N_EXPERTS, LANES), lambda t, *_: (0, 0))
    grid_spec = pltpu.PrefetchScalarGridSpec(
        num_scalar_prefetch=1,
        grid=(ntile,),
        in_specs=[
            pl.BlockSpec((N_EXPERTS, ROUTE_TILE), lambda t, *_: (0, t)),
            table, table, table,
            pl.BlockSpec((ROUTE_TILE, D_MODEL), lambda t, *_: (t, 0)),
            pl.BlockSpec(u.shape, lambda t, *_: (0, 0)),
        ],
        out_specs=pl.BlockSpec(memory_space=pl.ANY),
        scratch_shapes=[
            pltpu.VMEM((2, N_EXPERTS, GATHER_BLOCK, D_MODEL), BF16),
            pltpu.VMEM((N_EXPERTS, ROW_ALIGN, D_MODEL), F32),
            pltpu.VMEM((SLOT_CHUNK, D_MODEL), BF16),
            pltpu.SemaphoreType.DMA((2,)),
            pltpu.SemaphoreType.DMA((1,)),
        ],
    )
    return pl.pallas_call(
        functools.partial(_gather_kernel, ntile=ntile, cap=cap),
        grid_spec=grid_spec,
        out_shape=jax.ShapeDtypeStruct((N_EXPERTS, cap + GATHER_PAD, D_MODEL), BF16),
        compiler_params=_params(("arbitrary",)),
        name="gather",
    )(bsel_i, afft, tau, need, beq_i, xn, u)


def _ffn_kernel(xe_ref, wg_ref, wu_ref, wd_ref, ye_ref, acc_ref, *, cap, nf, tm):
    f = pl.program_id(1)
    tf = wg_ref.shape[2]
    chunks = [slice(j * FFN_CHUNK, (j + 1) * FFN_CHUNK) for j in range(tf // FFN_CHUNK)]
    cast = {}

    def weight(name, ref, j):
        if (name, j) not in cast:
            cast[name, j] = (ref[0, chunks[j], :] if name == "d" else ref[0, :, chunks[j]]).astype(BF16)
        return cast[name, j]

    @pl.when(f == 0)
    def _():
        acc_ref[...] = jnp.zeros_like(acc_ref)

    for i in range(cap // tm):
        r = slice(i * tm, (i + 1) * tm)
        x = xe_ref[0, r, :]
        y = None
        for j in range(len(chunks)):
            hg = _dot(x, weight("g", wg_ref, j))
            hu = _dot(x, weight("u", wu_ref, j))
            h = (hg * (1.0 / (1.0 + jnp.exp(-hg))) * hu).astype(BF16)
            part = _dot(h, weight("d", wd_ref, j))
            y = part if y is None else y + part
        acc_ref[r, :] += y

    @pl.when(f == nf - 1)
    def _():
        ye_ref[...] = acc_ref[...].astype(BF16)


def _ffn(xe, w_eg, w_eu, w_ed):
    cap = xe.shape[1] - GATHER_PAD
    tf = 512
    nf = D_FF // tf
    tm = min(cap, 1024)
    return pl.pallas_call(
        functools.partial(_ffn_kernel, cap=cap, nf=nf, tm=tm),
        grid=(N_EXPERTS, nf),
        in_specs=[
            pl.BlockSpec((1, cap, D_MODEL), lambda e, f: (e, 0, 0)),
            pl.BlockSpec((1, D_MODEL, tf), lambda e, f: (e, 0, f)),
            pl.BlockSpec((1, D_MODEL, tf), lambda e, f: (e, 0, f)),
            pl.BlockSpec((1, tf, D_MODEL), lambda e, f: (e, f, 0)),
        ],
        out_specs=pl.BlockSpec((cap, D_MODEL), lambda e, f: (e, 0)),
        out_shape=jax.ShapeDtypeStruct((N_EXPERTS * cap, D_MODEL), BF16),
        scratch_shapes=[pltpu.VMEM((cap, D_MODEL), F32)],
        compiler_params=_params(("arbitrary", "arbitrary")),
        name="ffn",
    )(xe, w_eg, w_eu, w_ed)


def _combine_kernel(bsel_s, x1_ref, aff_ref, taut_ref, needt_ref, beqt_ref, bselt_ref, low_ref, spread_ref, gf_ref,
                    ye_hbm, y_ref, buf_ref, xbuf_ref, sem_ref, xsem_ref, *, cap, total, nstep):
    step = pl.program_id(0)
    par = step & 1
    per_tile = N_EXPERTS * SLOT_CHUNK

    def aligned(e, tile):
        return _floor_pow2(bsel_s[e, tile], ROW_ALIGN)

    def window(e, tile, c):
        start = jnp.minimum(e * cap + aligned(e, tile) + c * SLOT_CHUNK, total - SLOT_CHUNK)
        return pl.multiple_of(start, ROW_ALIGN)

    def first_chunks(stp, buf, sub):
        tile = stp * COMBINE_TILES + sub
        return [pltpu.make_async_copy(ye_hbm.at[pl.ds(window(e, tile, 0), SLOT_CHUNK)],
                                      buf_ref.at[buf, pl.ds(sub * per_tile + e * SLOT_CHUNK, SLOT_CHUNK)],
                                      sem_ref.at[buf])
                for e in range(N_EXPERTS)]

    @pl.when(step == 0)
    def _():
        for sub in range(COMBINE_TILES):
            for cp in first_chunks(0, 0, sub):
                cp.start()

    for sub in range(COMBINE_TILES):
        for cp in first_chunks(jnp.minimum(step + 1, nstep - 1), 1 - par, sub):
            cp.start()
    for sub in range(COMBINE_TILES):
        for cp in first_chunks(step, par, sub):
            cp.wait()

    tau = taut_ref[0:1, :]
    low = low_ref[...]
    spread = spread_ref[...]
    lane = lax.broadcasted_iota(I32, (1, LANES), 1)
    wide = lax.broadcasted_iota(I32, (ROUTE_TILE, per_tile), 1)
    in_chunk = (wide & (SLOT_CHUNK - 1)).astype(F32)
    slots, affs = [], []
    for sub in range(COMBINE_TILES):
        tile = step * COMBINE_TILES + sub
        rows = slice(sub * ROUTE_TILE, (sub + 1) * ROUTE_TILE)
        aff = aff_ref[rows, :]
        k = lax.bitcast_convert_type(aff, I32)
        eq = k == tau
        eq_cum = _dot(low, _ones_where(eq, BF16)) + beqt_ref[sub]
        sel = (k > tau) | (eq & (eq_cum <= needt_ref[0:1, :]))
        slot = jnp.where(sel, _dot(low, _ones_where(sel, BF16)) + (bselt_ref[sub] - 1.0), -1.0)

        rel = jnp.zeros((1, LANES), F32)
        for e in range(N_EXPERTS):
            rel = jnp.where(lane == e, (window(e, tile, 0) - e * cap).astype(F32), rel)
        d = slot - rel
        d = jnp.where(sel & (d >= 0.0) & (d < float(SLOT_CHUNK)), d, -1.0)
        hit = _dot(d.astype(BF16), spread) == in_chunk
        onehot_gate = jnp.where(hit, _dot(aff.astype(BF16), spread), 0.0).astype(BF16)
        y_ref[rows, :] = x1_ref[rows, :] + _dot(onehot_gate, buf_ref[par, sub * per_tile:(sub + 1) * per_tile, :])
        slots.append(slot)
        affs.append(aff)

    nch = [[_cdiv_pow2(bsel_s[e, step * COMBINE_TILES + sub + 1] - aligned(e, step * COMBINE_TILES + sub), SLOT_CHUNK)
            for e in range(N_EXPERTS)] for sub in range(COMBINE_TILES)]

    @pl.when(functools.reduce(jnp.maximum, [n for per_sub in nch for n in per_sub]) > 1)
    def _():
        lane64 = lax.broadcasted_iota(I32, (ROUTE_TILE, SLOT_CHUNK), 1).astype(F32)
        for sub in range(COMBINE_TILES):
            tile = step * COMBINE_TILES + sub
            rows = slice(sub * ROUTE_TILE, (sub + 1) * ROUTE_TILE)
            for e in range(N_EXPERTS):
                slot_e = slots[sub][:, e:e + 1]

                def extra(c, carry):
                    w = window(e, tile, c)
                    cp = pltpu.make_async_copy(ye_hbm.at[pl.ds(w, SLOT_CHUNK)], xbuf_ref, xsem_ref.at[0])
                    cp.start()
                    cp.wait()
                    first = (aligned(e, tile) + c * SLOT_CHUNK).astype(F32)
                    hit = (lane64 + (w - e * cap).astype(F32) == slot_e) & (slot_e >= first)
                    y_ref[rows, :] += affs[sub][:, e:e + 1] * _dot(_ones_where(hit, BF16), xbuf_ref[...])
                    return carry

                lax.fori_loop(1, nch[sub][e], extra, 0)

    acc = y_ref[...]
    ms = jnp.mean(acc * acc, axis=-1, keepdims=True)
    y_ref[...] = acc * lax.rsqrt(ms + EPS) * gf_ref[...]

    @pl.when(step == nstep - 1)
    def _():
        for sub in range(COMBINE_TILES):
            for cp in first_chunks(step, 1 - par, sub):
                cp.wait()


def _combine(bsel_i, x1, aff, tables_t, ye, gf):
    tokens = x1.shape[0]
    cap = CAPACITY_FACTOR * tokens // N_EXPERTS
    ntile = tokens // ROUTE_TILE
    idx = np.arange(ROUTE_TILE)
    low = jnp.asarray(idx[:, None] >= idx[None, :], BF16)
    spread = jnp.asarray(np.arange(LANES)[:, None] == np.arange(N_EXPERTS * SLOT_CHUNK)[None, :] // SLOT_CHUNK, BF16)
    taut, needt, beqt, bselt = tables_t
    rows = COMBINE_TILES * ROUTE_TILE
    rowvec = pl.BlockSpec((8, LANES), lambda t, *_: (0, 0))
    tilevec = pl.BlockSpec((COMBINE_TILES, 1, LANES), lambda t, *_: (t, 0, 0))
    grid_spec = pltpu.PrefetchScalarGridSpec(
        num_scalar_prefetch=1,
        grid=(ntile // COMBINE_TILES,),
        in_specs=[
            pl.BlockSpec((rows, D_MODEL), lambda t, *_: (t, 0)),
            pl.BlockSpec((rows, LANES), lambda t, *_: (t, 0)),
            rowvec, rowvec, tilevec, tilevec,
            pl.BlockSpec(low.shape, lambda t, *_: (0, 0)),
            pl.BlockSpec(spread.shape, lambda t, *_: (0, 0)),
            pl.BlockSpec((1, D_MODEL), lambda t, *_: (0, 0)),
            pl.BlockSpec(memory_space=pl.ANY),
        ],
        out_specs=pl.BlockSpec((rows, D_MODEL), lambda t, *_: (t, 0)),
        scratch_shapes=[
            pltpu.VMEM((2, COMBINE_TILES * N_EXPERTS * SLOT_CHUNK, D_MODEL), BF16),
            pltpu.VMEM((SLOT_CHUNK, D_MODEL), BF16),
            pltpu.SemaphoreType.DMA((2,)),
            pltpu.SemaphoreType.DMA((1,)),
        ],
    )
    return pl.pallas_call(
        functools.partial(_combine_kernel, cap=cap, total=N_EXPERTS * cap, nstep=ntile // COMBINE_TILES),
        grid_spec=grid_spec,
        out_shape=jax.ShapeDtypeStruct((tokens, D_MODEL), F32),
        compiler_params=_params(("arbitrary",)),
        name="combine",
    )(bsel_i, x1, aff, taut, needt, beqt.reshape(LANES, 1, LANES), bselt.reshape(LANES, 1, LANES), low, spread,
      gf, ye)


def _encoder(x, w):
    batch, seq, _ = x.shape
    tokens = batch * seq
    xt = x.reshape(tokens, D_MODEL)
    *qkvs, vr, vi, gates = _in_proj(xt, w["g1"], w["w_in"], w["w_gate"], w["b_gate"], w["cs"], batch, seq)
    outs, lses = [], []
    for g in range(N_GROUPS):
        o, lse = _attention(qkvs[g], w["bias"][g], g)
        outs.append(o)
        lses.append(lse)
    four = _fourier(vr, vi, batch, seq)
    x1, xn, aff, afft = _mix(xt, outs, lses, four, gates, w["w_attn"], w["w_four"], w["w_out"], w["g2"],
                             w["w_router"])
    tau, need, beq_i, bsel_i, taut, needt, beqt, bselt = _route(afft)
    idx = np.arange(ROUTE_TILE)
    u = jnp.asarray(idx[:, None] <= idx[None, :], BF16)
    xe = _gather(bsel_i, afft, tau, need, beq_i, xn, u)
    ye = _ffn(xe, w["w_eg"], w["w_eu"], w["w_ed"])
    y = _combine(bsel_i, x1, aff, (taut, needt, beqt, bselt), ye, w["gf"])
    return y.reshape(batch, seq, D_MODEL)


def _prepare_weights(rel_bias, norm1_g, w_in, w_attn_br, w_four_br, w_gate, b_gate, w_out,
                     norm2_g, w_router, w_exp_gate, w_exp_up, w_exp_down, final_g):
    c, s = _dft_mats(F_CH)
    starts = [part * ATT_W + g * GROUP_W for g in range(N_GROUPS) for part in range(3)]
    w_in_grouped = jnp.concatenate([w_in[0][:, s0:s0 + GROUP_W] for s0 in starts] + [w_in[0][:, QKV_W:]], axis=1)
    w_router = jnp.pad(w_router[0], ((0, 0), (0, LANES - N_EXPERTS)))
    w_router_hi = w_router.astype(BF16)
    return {
        "g1": norm1_g[0].reshape(1, D_MODEL),
        "w_in": w_in_grouped.astype(BF16),
        "w_gate": w_gate[0].astype(BF16),
        "b_gate": b_gate[0].reshape(1, 2 * D_MODEL),
        "cs": jnp.asarray(np.concatenate([c, s], axis=1), BF16),
        "bias": [_attention_bias(rel_bias, g) for g in range(N_GROUPS)],
        "w_attn": w_attn_br[0].astype(BF16),
        "w_four": w_four_br[0].astype(BF16),
        "w_out": w_out[0].astype(BF16),
        "g2": norm2_g[0].reshape(1, D_MODEL),
        "w_router": jnp.concatenate([w_router_hi, (w_router - w_router_hi.astype(F32)).astype(BF16)], axis=1),
        "w_eg": w_exp_gate[0],
        "w_eu": w_exp_up[0],
        "w_ed": w_exp_down[0],
        "gf": final_g.reshape(1, D_MODEL),
    }


def kernel(x_prompt, x_sample, rel_bias, norm1_g, w_in, w_attn_br, w_four_br, w_gate, b_gate, w_out,
           norm2_g, w_router, w_exp_gate, w_exp_up, w_exp_down, final_g):
    w = _prepare_weights(rel_bias, norm1_g, w_in, w_attn_br, w_four_br, w_gate, b_gate, w_out,
                         norm2_g, w_router, w_exp_gate, w_exp_up, w_exp_down, final_g)
    return (_encoder(x_prompt, w), _encoder(x_sample, w))
```

```python
import functools
import math

import numpy as np
import jax
import jax.numpy as jnp
from jax import lax
from jax.experimental import pallas as pl
from jax.experimental.pallas import tpu as pltpu

D_MODEL = 1024
HEAD_DIM = 64
HEADS_PER_GROUP = 4
GROUPS = ((128, 1), (512, 4), (2048, 16))
N_GROUPS = len(GROUPS)
GROUP_W = HEADS_PER_GROUP * HEAD_DIM
ATT_W = N_GROUPS * GROUP_W
QKV_W = 3 * ATT_W
F_GROUPS = 6
F_CH = 128
F_W = F_GROUPS * F_CH
NUM_BUCKETS = 32
MAX_DISTANCE = 1024
N_EXPERTS = 16
CAPACITY_FACTOR = 2
D_FF = 2048
EPS = 1e-6
NEG = -1e30

HALF_KEYS = 64
ATT_SUB = 128
ATT_OUT_ROWS = 8192
TOKEN_TILE = 512
ROUTE_TILE = 256
FFT_STEP_ROWS = 1024
FFN_CHUNK = 256
SLOT_CHUNK = 64
COMBINE_TILES = 2
ROW_ALIGN = 16
GATHER_BLOCK = SLOT_CHUNK + ROW_ALIGN
GATHER_STACK = GATHER_BLOCK + ROW_ALIGN
GATHER_PAD = GATHER_BLOCK
LANES = 128
V7X_VMEM_LIMIT = 56 * 1024 * 1024

F32 = jnp.float32
BF16 = jnp.bfloat16
I32 = jnp.int32
U32 = jnp.uint32


def _params(sem):
    return pltpu.CompilerParams(dimension_semantics=sem, vmem_limit_bytes=V7X_VMEM_LIMIT)


def _dot(a, b):
    return jnp.dot(a, b, preferred_element_type=F32)


def _dot_nt(a, b):
    return lax.dot_general(a, b, (((1,), (1,)), ((), ())), preferred_element_type=F32)


def _floor_pow2(x, m):
    return x & ~(m - 1)


def _cdiv_pow2(x, m):
    return (x + (m - 1)) >> (m.bit_length() - 1)


def _ones_where(mask, dtype=F32):
    return jnp.where(mask, jnp.ones((), F32), jnp.zeros((), F32)).astype(dtype)


def _in_proj_kernel(x_ref, g_ref, win_ref, wg_ref, bg_ref, cs_ref, qscale_ref, qkv0_ref, qkv1_ref, qkv2_ref,
                    vr_ref, vi_ref, gates_ref, slab_ref):
    tm = x_ref.shape[0]
    half = tm // 2
    nslab = ATT_W // LANES
    cs = cs_ref[...]
    for h in range(2):
        rows = slice(h * half, (h + 1) * half)
        x = x_ref[rows, :]
        ms = jnp.mean(x * x, axis=-1, keepdims=True)
        xn = (x * lax.rsqrt(ms + EPS) * g_ref[...]).astype(BF16)
        for g, out_ref in enumerate((qkv0_ref, qkv1_ref, qkv2_ref)):
            dil = GROUPS[g][1]
            res = _dot(xn, win_ref[:, g * ATT_W:(g + 1) * ATT_W]) * qscale_ref[...]
            if dil == 1:
                out_ref[0, 0, rows, :] = res.astype(BF16)
                continue
            for j in range(nslab):
                slab_ref[j, rows, :] = res[:, j * LANES:(j + 1) * LANES]
            n = half // dil
            for r in range(dil):
                cls = [slab_ref[j, pl.ds(h * half + r, n, stride=dil), :] for j in range(nslab)]
                out_ref[0, r, h * n:(h + 1) * n, :] = jnp.concatenate(cls, axis=1).astype(BF16)
        u = _dot(xn, win_ref[:, QKV_W:QKV_W + F_W]).astype(BF16)
        for g in range(F_GROUPS):
            a = _dot(u[:, g * F_CH:(g + 1) * F_CH], cs)
            vr_ref[rows, g * F_CH:(g + 1) * F_CH] = a[:, :F_CH].astype(BF16)
            vi_ref[rows, g * F_CH:(g + 1) * F_CH] = (-a[:, F_CH:]).astype(BF16)
        z = _dot(xn, wg_ref[...]) + bg_ref[...]
        gates_ref[rows, :] = (1.0 / (1.0 + jnp.exp(-z))).astype(BF16)


def _class_major_spec(tm, dil, width, per_batch):
    return pl.BlockSpec((1, dil, tm // dil, width), lambda i: (i // per_batch, 0, i % per_batch, 0))


def _in_proj(x, g1, w_in, w_gate, b_gate, cs, batch, seq):
    t = x.shape[0]
    tm = TOKEN_TILE
    per_batch = seq // tm
    const = lambda i: (0, 0)
    row = lambda i: (i, 0)
    qscale = np.ones((1, ATT_W), np.float32)
    qscale[:, :GROUP_W] = 1.0 / math.sqrt(HEAD_DIM)
    return pl.pallas_call(
        _in_proj_kernel,
        grid=(t // tm,),
        in_specs=[
            pl.BlockSpec((tm, D_MODEL), row),
            pl.BlockSpec((1, D_MODEL), const),
            pl.BlockSpec(w_in.shape, const),
            pl.BlockSpec(w_gate.shape, const),
            pl.BlockSpec((1, 2 * D_MODEL), const),
            pl.BlockSpec(cs.shape, const),
            pl.BlockSpec((1, ATT_W), const),
        ],
        out_specs=[_class_major_spec(tm, dil, ATT_W, per_batch) for _, dil in GROUPS] + [
            pl.BlockSpec((tm, F_W), row),
            pl.BlockSpec((tm, F_W), row),
            pl.BlockSpec((tm, 2 * D_MODEL), row),
        ],
        out_shape=[jax.ShapeDtypeStruct((batch, dil, seq // dil, ATT_W), BF16) for _, dil in GROUPS] + [
            jax.ShapeDtypeStruct((t, F_W), BF16),
            jax.ShapeDtypeStruct((t, F_W), BF16),
            jax.ShapeDtypeStruct((t, 2 * D_MODEL), BF16),
        ],
        scratch_shapes=[pltpu.VMEM((ATT_W // LANES, tm, LANES), F32)],
        compiler_params=_params(("parallel",)),
        name="in_proj",
    )(x, g1, w_in, w_gate, b_gate, cs, jnp.asarray(qscale))


def _attention_kernel(q_ref, kp_ref, kc_ref, kn_ref, vp_ref, vc_ref, vn_ref, bias_ref, o_ref, lse_ref, *,
                      tq, length, dil, rc):
    i = pl.program_id(1)
    win = ATT_SUB + 2 * HALF_KEYS
    nsub = tq // ATT_SUB
    lane_head = lax.broadcasted_iota(I32, (ATT_SUB, GROUP_W), 1) // HEAD_DIM
    at_start = (i == 0).astype(I32)
    at_end = (i == length // tq - 1).astype(I32) * 2
    for c, sb in [(c, sb) for c in range(rc) for sb in range(nsub)]:
        r = pl.program_id(2) * rc + c
        if sb == 0:
            kwin = jnp.concatenate([kp_ref[0, c], kc_ref[0, c], kn_ref[0, c]], axis=0)
            vwin = jnp.concatenate([vp_ref[0, c], vc_ref[0, c], vn_ref[0, c]], axis=0)
        off = sb * ATT_SUB
        q = q_ref[0, c, off:off + ATT_SUB, :]
        kw = kwin[off:off + win]
        vw = vwin[off:off + win]
        variant = (at_start if sb == 0 else 0) + (at_end if sb == nsub - 1 else 0)
        qs = jnp.concatenate(
            [jnp.where(lane_head == h, q, jnp.zeros_like(q)) for h in range(HEADS_PER_GROUP)], axis=0)
        s_all = _dot_nt(qs, kw)
        ps, ms, ls = [], [], []
        for h in range(HEADS_PER_GROUP):
            s = s_all[h * ATT_SUB:(h + 1) * ATT_SUB] + bias_ref[variant, h]
            m = jnp.max(s, axis=-1, keepdims=True)
            p = jnp.exp(s - m)
            ls.append(jnp.sum(p, axis=-1, keepdims=True))
            ms.append(m)
            ps.append(p.astype(BF16))
        o_all = _dot(jnp.concatenate(ps, axis=0), vw)
        out = jnp.zeros((ATT_SUB, GROUP_W), F32)
        lse = jnp.zeros((ATT_SUB, GROUP_W), F32)
        for h in range(HEADS_PER_GROUP):
            oh = o_all[h * ATT_SUB:(h + 1) * ATT_SUB] * (1.0 / ls[h])
            out = jnp.where(lane_head == h, oh, out)
            lse = jnp.where(lane_head == h, ms[h] + jnp.log(ls[h]), lse)
        rows = pl.ds(off * dil + r, ATT_SUB, stride=dil) if dil > 1 else pl.ds(off, ATT_SUB)
        o_ref[rows, :] = _pack_bf16_pair(out[:, :LANES], out[:, LANES:])
        for j in range(GROUP_W // LANES):
            lse_ref[j, rows, :] = lse[:, j * LANES:(j + 1) * LANES]


def _attention(qkv, bias, g):
    batch, dil, length, _ = qkv.shape
    tq = min(length, 512, ATT_OUT_ROWS // dil)
    nb = length // tq
    hb = tq // HALF_KEYS
    last_halo = length // HALF_KEYS - 1
    rc = min(dil, max(1, 512 // tq))

    def cur(c):
        return lambda b, i, r: (b, r, i, c)

    def prev(c):
        return lambda b, i, r: (b, r, jnp.maximum(i * hb - 1, 0), c)

    def nxt(c):
        return lambda b, i, r: (b, r, jnp.minimum((i + 1) * hb, last_halo), c)

    blk = lambda rows: (1, rc, rows, GROUP_W)
    out_spec = pl.BlockSpec((GROUP_W // LANES, tq * dil, LANES), lambda b, i, r: (0, b * nb + i, 0))
    return pl.pallas_call(
        functools.partial(_attention_kernel, tq=tq, length=length, dil=dil, rc=rc),
        grid=(batch, nb, dil // rc),
        in_specs=[
            pl.BlockSpec(blk(tq), cur(0)),
            pl.BlockSpec(blk(HALF_KEYS), prev(1)),
            pl.BlockSpec(blk(tq), cur(1)),
            pl.BlockSpec(blk(HALF_KEYS), nxt(1)),
            pl.BlockSpec(blk(HALF_KEYS), prev(2)),
            pl.BlockSpec(blk(tq), cur(2)),
            pl.BlockSpec(blk(HALF_KEYS), nxt(2)),
            pl.BlockSpec(bias.shape, lambda b, i, r: (0, 0, 0, 0)),
        ],
        out_specs=[pl.BlockSpec((tq * dil, LANES), lambda b, i, r: (b * nb + i, 0)), out_spec],
        out_shape=[jax.ShapeDtypeStruct((batch * dil * length, LANES), U32),
                   jax.ShapeDtypeStruct((GROUP_W // LANES, batch * dil * length, LANES), F32)],
        compiler_params=_params(("parallel", "parallel", "arbitrary")),
        name=f"attention_g{g}",
    )(qkv, qkv, qkv, qkv, qkv, qkv, qkv, bias)


def _t5_bucket(rel):
    nb = NUM_BUCKETS // 2
    max_exact = nb // 2
    ret = (rel > 0).astype(np.int32) * nb
    n = np.abs(rel)
    large = max_exact + (np.log(np.maximum(n, max_exact) / max_exact)
                         / np.log(MAX_DISTANCE / max_exact) * (nb - max_exact)).astype(np.int32)
    large = np.minimum(large, nb - 1)
    return (ret + np.where(n < max_exact, n, large)).astype(np.int32)


def _attention_bias(rel_bias, g):
    dil = GROUPS[g][1]
    qi = np.arange(ATT_SUB)[:, None]
    kj = np.arange(ATT_SUB + 2 * HALF_KEYS)[None, :]
    delta = kj - HALF_KEYS - qi
    band = np.abs(delta) <= HALF_KEYS
    bucket = _t5_bucket(dil * delta)
    tab = rel_bias[:, g * HEADS_PER_GROUP:(g + 1) * HEADS_PER_GROUP].astype(F32)
    onehot = jnp.asarray(bucket[..., None] == np.arange(NUM_BUCKETS), F32)
    bias = jnp.einsum("qkb,bh->hqk", onehot, tab, precision=lax.Precision.HIGHEST)
    masks = [band & ((kj >= HALF_KEYS) | ((v & 1) == 0)) & ((kj < ATT_SUB + HALF_KEYS) | ((v & 2) == 0))
             for v in range(4)]
    return jnp.where(jnp.asarray(np.stack(masks))[:, None], bias[None], NEG)


def _dft_mats(n):
    k = np.arange(n)
    ang = 2.0 * np.pi * ((k[:, None] * k[None, :]) % n) / n
    return np.cos(ang), np.sin(ang)


def _pack_bf16_pair(a, b):
    hi = lax.bitcast_convert_type(a.astype(BF16).astype(F32), U32)
    lo = lax.bitcast_convert_type(b.astype(BF16).astype(F32), U32)
    return hi | lax.shift_right_logical(lo, jnp.full(lo.shape, 16, U32))


def _unpack_bf16_pair(word):
    a = lax.bitcast_convert_type(word & jnp.uint32(0xFFFF0000), F32)
    b = lax.bitcast_convert_type(lax.shift_left(word, jnp.full(word.shape, 16, U32)), F32)
    return a.astype(BF16), b.astype(BF16)


def _fft_stage1_kernel(vr_ref, vi_ref, m1_ref, twc_ref, tws_ref, z_ref, *, n1, m):
    x = jnp.concatenate([vr_ref[0], vi_ref[0]], axis=0)
    z = _dot(m1_ref[...], x)
    zr, zi = z[:n1], z[n1:]
    twc, tws = twc_ref[0], tws_ref[0]
    for j in range(m):
        c = twc[:, j:j + 1]
        s = tws[:, j:j + 1]
        a = zr[:, j * F_W:(j + 1) * F_W]
        b = zi[:, j * F_W:(j + 1) * F_W]
        z_ref[0, :, j, :] = _pack_bf16_pair(a * c + b * s, b * c - a * s)


def _fft_stage2_kernel(z_ref, m2_ref, o_ref, *, kc, scale):
    m2 = m2_ref[...]
    for j in range(kc):
        x = jnp.concatenate(_unpack_bf16_pair(z_ref[0, j]), axis=0)
        y = _dot(m2, x) * scale
        o_ref[0, :, j, :] = _pack_bf16_pair(y[:, :F_W // 2], y[:, F_W // 2:])


def _fourier(vr, vi, batch, seq):
    n2 = LANES
    n1 = seq // n2
    m = min(n2, FFT_STEP_ROWS // n1)
    c1, s1 = _dft_mats(n1)
    m1 = jnp.asarray(np.block([[c1, s1], [-s1, c1]]), BF16)
    c2, s2 = _dft_mats(n2)
    m2 = jnp.asarray(np.concatenate([c2, s2], axis=1), BF16)
    k1 = np.arange(n1)[:, None]
    sv = np.arange(n2)[None, :]
    ang = 2.0 * np.pi * ((k1 * sv) % seq) / seq
    to_blocks = lambda a: jnp.asarray(a.reshape(n1, n2 // m, m).transpose(1, 0, 2), F32)
    twc, tws = to_blocks(np.cos(ang)), to_blocks(np.sin(ang))

    v3 = lambda a: a.reshape(batch, n1, n2 * F_W)
    blk = pl.BlockSpec((1, n1, m * F_W), lambda b, j: (b, 0, j))
    tmap = lambda b, j: (j, 0, 0)
    z = pl.pallas_call(
        functools.partial(_fft_stage1_kernel, n1=n1, m=m),
        grid=(batch, n2 // m),
        in_specs=[
            blk,
            blk,
            pl.BlockSpec(m1.shape, lambda b, j: (0, 0)),
            pl.BlockSpec((1, n1, m), tmap),
            pl.BlockSpec((1, n1, m), tmap),
        ],
        out_specs=pl.BlockSpec((1, n1, m, F_W), lambda b, j: (b, 0, j, 0)),
        out_shape=jax.ShapeDtypeStruct((batch, n1, n2, F_W), U32),
        compiler_params=_params(("parallel", "parallel")),
        name="fft_stage1",
    )(v3(vr), v3(vi), m1, twc, tws)

    kc = min(n1, FFT_STEP_ROWS // n2 * 2)
    out = pl.pallas_call(
        functools.partial(_fft_stage2_kernel, kc=kc, scale=1.0 / math.sqrt(seq * F_CH)),
        grid=(batch, n1 // kc),
        in_specs=[
            pl.BlockSpec((1, kc, n2, F_W), lambda b, j: (b, j, 0, 0)),
            pl.BlockSpec(m2.shape, lambda b, j: (0, 0)),
        ],
        out_specs=pl.BlockSpec((1, n2, kc, F_W // 2), lambda b, j: (b, 0, j, 0)),
        out_shape=jax.ShapeDtypeStruct((batch, n2, n1, F_W // 2), U32),
        compiler_params=_params(("parallel", "parallel")),
        name="fft_stage2",
    )(z, m2)
    return out.reshape(batch * seq, F_W // 2)


def _mix_kernel(x_ref, o0_ref, o1_ref, o2_ref, l0_ref, l1_ref, l2_ref, four_ref, gates_ref,
                wa_ref, wf_ref, wo_ref, g2_ref, wr_ref, x1_ref, xn_ref, aff_ref, afft_ref):
    tm = x_ref.shape[0]
    half = tm // 2
    for rows in (slice(0, half), slice(half, tm)):
        def slabs(ref):
            return jnp.concatenate([ref[j, rows, :] for j in range(GROUP_W // LANES)], axis=1)

        f_br = _dot(jnp.concatenate(_unpack_bf16_pair(four_ref[rows, :]), axis=1), wf_ref[...])
        l0, l1, l2 = slabs(l0_ref), slabs(l1_ref), slabs(l2_ref)
        mx = jnp.maximum(jnp.maximum(l0, l1), l2)
        e0, e1, e2 = jnp.exp(l0 - mx), jnp.exp(l1 - mx), jnp.exp(l2 - mx)
        o0, o1, o2 = (jnp.concatenate(_unpack_bf16_pair(ref[rows, :]), axis=1) for ref in (o0_ref, o1_ref, o2_ref))
        att = (e0 * o0 + e1 * o1 + e2 * o2) * (1.0 / (e0 + e1 + e2))
        a_br = _dot(att.astype(BF16), wa_ref[...])
        mix = gates_ref[rows, :D_MODEL] * a_br + gates_ref[rows, D_MODEL:] * f_br
        x1 = x_ref[rows, :] + _dot(mix.astype(BF16), wo_ref[...])
        x1_ref[rows, :] = x1
        ms = jnp.mean(x1 * x1, axis=-1, keepdims=True)
        xn = x1 * lax.rsqrt(ms + EPS) * g2_ref[...]
        xn_ref[rows, :] = xn.astype(BF16)
        xh = xn.astype(BF16)
        xl = (xn - xh.astype(F32)).astype(BF16)
        both = _dot(xh, wr_ref[...])
        logits = both[:, :LANES] + (both[:, LANES:] + _dot(xl, wr_ref[:, :LANES]))
        lane = lax.broadcasted_iota(I32, logits.shape, 1)
        logits = jnp.where(lane < N_EXPERTS, logits, NEG)
        p = jnp.exp(logits - jnp.max(logits, axis=-1, keepdims=True))
        aff = p * (1.0 / jnp.sum(p, axis=-1, keepdims=True))
        aff_ref[rows, :] = aff
        afft_ref[:, rows] = aff.T[:N_EXPERTS]


def _mix(x, os_, ls_, four, gates, w_attn, w_four, w_out, g2, w_router):
    t = x.shape[0]
    tm = TOKEN_TILE
    const = lambda i: (0, 0)
    row = lambda i: (i, 0)
    rows = lambda w: pl.BlockSpec((tm, w), row)
    full = lambda a: pl.BlockSpec(a.shape, const)
    slab = pl.BlockSpec((GROUP_W // LANES, tm, LANES), lambda i: (0, i, 0))
    return pl.pallas_call(
        _mix_kernel,
        grid=(t // tm,),
        in_specs=[rows(D_MODEL)] + [rows(LANES)] * 3 + [slab] * 3 + [rows(F_W // 2), rows(2 * D_MODEL),
                  full(w_attn), full(w_four), full(w_out), full(g2), full(w_router)],
        out_specs=[rows(D_MODEL), rows(D_MODEL), rows(LANES), pl.BlockSpec((N_EXPERTS, tm), lambda i: (0, i))],
        out_shape=[
            jax.ShapeDtypeStruct((t, D_MODEL), F32),
            jax.ShapeDtypeStruct((t, D_MODEL), BF16),
            jax.ShapeDtypeStruct((t, LANES), F32),
            jax.ShapeDtypeStruct((N_EXPERTS, t), F32),
        ],
        compiler_params=_params(("parallel",)),
        name="mix",
    )(x, *os_, *ls_, four, gates, w_attn, w_four, w_out, g2, w_router)


def _route_kernel(afft_ref, su_ref, u_ref, tau_ref, need_ref, beq_ref, bsel_ref,
                  taut_ref, needt_ref, beqt_ref, bselt_ref, *, tokens):
    cap = CAPACITY_FACTOR * tokens // N_EXPERTS
    ntile = tokens // ROUTE_TILE
    shape = (N_EXPERTS, LANES)
    lane = lax.broadcasted_iota(I32, shape, 1)

    def keys(start, width):
        return lax.bitcast_convert_type(afft_ref[:, pl.ds(pl.multiple_of(start, LANES), width)], I32)

    span = min(tokens, 16 * LANES)

    def count(pred):
        def body(c, acc):
            hits = _ones_where(pred(keys(c * span, span)))
            for j in range(span // LANES):
                acc = acc + hits[:, j * LANES:(j + 1) * LANES]
            return acc
        acc = lax.fori_loop(0, tokens // span, body, jnp.zeros(shape, F32))
        return jnp.sum(acc, axis=1, keepdims=True)

    def bit_body(i, prefix):
        cand = prefix | lax.shift_left(jnp.ones(shape, I32), jnp.full(shape, 30 - i, I32))
        tot = count(lambda k: k >= cand[:, :1])
        return jnp.where(tot >= cap, cand, prefix)

    tau = lax.fori_loop(0, 31, bit_body, jnp.zeros(shape, I32))
    tau_col = tau[:, :1]
    n_gt = count(lambda k: k > tau_col)
    need = cap - n_gt

    def prefix_over_tiles(tab):
        return _dot(tab.astype(BF16), su_ref[...])

    def at_lane(tab, c):
        return jnp.sum(jnp.where(lane == c, tab, 0.0), axis=1, keepdims=True)

    def eq_body(c, tab):
        k = keys(c * ROUTE_TILE, ROUTE_TILE)
        cnt = jnp.sum(_ones_where(k == tau_col), axis=1, keepdims=True)
        return jnp.where(lane == c, cnt, tab)

    base_eq = prefix_over_tiles(lax.fori_loop(0, ntile, eq_body, jnp.zeros(shape, F32)))

    def sel_body(c, tab):
        k = keys(c * ROUTE_TILE, ROUTE_TILE)
        eq = k == tau_col
        eq_cum = _dot(_ones_where(eq, BF16), u_ref[...]) + at_lane(base_eq, c)
        sel = (k > tau_col) | (eq & (eq_cum <= need))
        cnt = jnp.sum(_ones_where(sel), axis=1, keepdims=True)
        return jnp.where(lane == c, cnt, tab)

    base_sel = prefix_over_tiles(lax.fori_loop(0, ntile, sel_body, jnp.zeros(shape, F32)))

    def transposed(val):
        return jnp.concatenate([val, jnp.zeros((LANES - N_EXPERTS, LANES), val.dtype)], axis=0).T

    tau_ref[...] = tau
    taut_ref[...] = transposed(tau)
    for val, ref, ref_t in ((jnp.broadcast_to(need, shape), need_ref, needt_ref),
                            (base_eq, beq_ref, beqt_ref), (base_sel, bsel_ref, bselt_ref)):
        ref[...] = val.astype(I32)
        ref_t[...] = transposed(val)


def _route(afft):
    tokens = afft.shape[1]
    idx = np.arange(LANES)
    su = jnp.asarray(idx[:, None] < idx[None, :], BF16)
    idx = np.arange(ROUTE_TILE)
    u = jnp.asarray(idx[:, None] <= idx[None, :], BF16)
    full = lambda a: pl.BlockSpec(a.shape, lambda i: (0,) * a.ndim)
    small = pl.BlockSpec((N_EXPERTS, LANES), lambda i: (0, 0))
    smallt = pl.BlockSpec((LANES, LANES), lambda i: (0, 0))
    return pl.pallas_call(
        functools.partial(_route_kernel, tokens=tokens),
        grid=(1,),
        in_specs=[full(afft), full(su), full(u)],
        out_specs=[small] * 4 + [smallt] * 4,
        out_shape=[jax.ShapeDtypeStruct((N_EXPERTS, LANES), I32)] * 4
        + [jax.ShapeDtypeStruct((LANES, LANES), I32)] + [jax.ShapeDtypeStruct((LANES, LANES), F32)] * 3,
        compiler_params=_params(("arbitrary",)),
        name="route",
    )(afft, su, u)


def _gather_kernel(bsel_s, afft_ref, tau_ref, need_ref, beq_ref, x_ref, u_ref, xe_hbm,
                   stage_ref, tail_ref, xbuf_ref, sem_ref, xsem_ref, *, ntile, cap):
    t = pl.program_id(0)
    par = t & 1

    def aligned(e, tile):
        return pl.multiple_of(_floor_pow2(bsel_s[e, tile], ROW_ALIGN), ROW_ALIGN)

    def write(e, tile, buf, first_row=None):
        first_row = aligned(e, tile) if first_row is None else first_row
        return pltpu.make_async_copy(stage_ref.at[buf, e], xe_hbm.at[e, pl.ds(first_row, GATHER_BLOCK)],
                                     sem_ref.at[buf])

    @pl.when(t == 0)
    def _():
        tail_ref[...] = jnp.zeros_like(tail_ref)
        stage_ref[1] = jnp.zeros(stage_ref.shape[1:], BF16)
        for e in range(N_EXPERTS):
            write(e, 0, 1, first_row=cap).start()

    k = lax.bitcast_convert_type(afft_ref[...], I32)
    tau = tau_ref[:, :1]
    lane = lax.broadcasted_iota(I32, (N_EXPERTS, LANES), 1)
    beq = jnp.sum(jnp.where(lane == t, beq_ref[...].astype(F32), 0.0), axis=1, keepdims=True)
    eq = k == tau
    eq_cum = _dot(_ones_where(eq, BF16), u_ref[...]) + beq
    sel = (k > tau) | (eq & (eq_cum <= need_ref[:, :1].astype(F32)))
    rank = jnp.where(sel, _dot(_ones_where(sel, BF16), u_ref[...]) - 1.0, -1e4)

    row = lax.broadcasted_iota(I32, (GATHER_STACK, ROUTE_TILE), 0)
    in_block = row < GATHER_BLOCK
    row_f = row.astype(F32)
    offs, shifts, pieces = [], [], []
    for e in range(N_EXPERTS):
        off = (bsel_s[e, t] - aligned(e, t)).astype(F32)
        shift = _floor_pow2(bsel_s[e, t + 1], ROW_ALIGN) - aligned(e, t)
        target = jnp.where(in_block, row_f, row_f - float(GATHER_BLOCK) + shift.astype(F32))
        pieces.append(_ones_where(rank[e:e + 1, :] + off == target, BF16))
        offs.append(off)
        shifts.append(shift)
    res = _dot(jnp.concatenate(pieces, axis=0), x_ref[...])
    for e in range(N_EXPERTS):
        base = e * GATHER_STACK
        old = tail_ref[e]
        stage_ref[par, e, 0:ROW_ALIGN, :] = (res[base:base + ROW_ALIGN] + old).astype(BF16)
        stage_ref[par, e, ROW_ALIGN:GATHER_BLOCK, :] = res[base + ROW_ALIGN:base + GATHER_BLOCK].astype(BF16)
        tail_ref[e] = res[base + GATHER_BLOCK:base + GATHER_STACK] + jnp.where(shifts[e] == 0, old, 0.0)
    for e in range(N_EXPERTS):
        write(e, jnp.maximum(t - 1, 0), 1 - par).wait()
    for e in range(N_EXPERTS):
        write(e, t, par).start()

    extra = [_cdiv_pow2(jnp.maximum(bsel_s[e, t + 1] - aligned(e, t) - GATHER_BLOCK, 0), SLOT_CHUNK)
             for e in range(N_EXPERTS)]

    @pl.when(functools.reduce(jnp.maximum, extra) > 0)
    def _():
        row64 = lax.broadcasted_iota(I32, (SLOT_CHUNK, ROUTE_TILE), 0).astype(F32)
        for e in range(N_EXPERTS):
            def chunk(c, carry):
                first = GATHER_BLOCK + c * SLOT_CHUNK
                onehot = _ones_where(rank[e:e + 1, :] + offs[e] == row64 + first.astype(F32), BF16)
                xbuf_ref[...] = _dot(onehot, x_ref[...]).astype(BF16)
                dst = pl.multiple_of(aligned(e, t) + first, ROW_ALIGN)
                cp = pltpu.make_async_copy(xbuf_ref, xe_hbm.at[e, pl.ds(dst, SLOT_CHUNK)], xsem_ref.at[0])
                cp.start()
                cp.wait()
                return carry

            lax.fori_loop(0, extra[e], chunk, 0)

    @pl.when(t == ntile - 1)
    def _():
        for e in range(N_EXPERTS):
            write(e, t, par).wait()


def _gather(bsel_i, afft, tau, need, beq_i, xn, u):
    tokens = xn.shape[0]
    cap = CAPACITY_FACTOR * tokens // N_EXPERTS
    ntile = tokens // ROUTE_TILE
    table = pl.BlockSpec((N_EXPERTS, LANES), lambda t, *_: (0, 0))
    grid_spec = pltpu.PrefetchScalarGridSpec(
        num_scalar_prefetch=1,
        grid=(ntile,),
        in_specs=[
            pl.BlockSpec((N_EXPERTS, ROUTE_TILE), lambda t, *_: (0, t)),
            table, table, table,
            pl.BlockSpec((ROUTE_TILE, D_MODEL), lambda t, *_: (t, 0)),
            pl.BlockSpec(u.shape, lambda t, *_: (0, 0)),
        ],
        out_specs=pl.BlockSpec(memory_space=pl.ANY),
        scratch_shapes=[
            pltpu.VMEM((2, N_EXPERTS, GATHER_BLOCK, D_MODEL), BF16),
            pltpu.VMEM((N_EXPERTS, ROW_ALIGN, D_MODEL), F32),
            pltpu.VMEM((SLOT_CHUNK, D_MODEL), BF16),
            pltpu.SemaphoreType.DMA((2,)),
            pltpu.SemaphoreType.DMA((1,)),
        ],
    )
    return pl.pallas_call(
        functools.partial(_gather_kernel, ntile=ntile, cap=cap),
        grid_spec=grid_spec,
        out_shape=jax.ShapeDtypeStruct((N_EXPERTS, cap + GATHER_PAD, D_MODEL), BF16),
        compiler_params=_params(("arbitrary",)),
        name="gather",
    )(bsel_i, afft, tau, need, beq_i, xn, u)


def _ffn_kernel(xe_ref, wg_ref, wu_ref, wd_ref, ye_ref, acc_ref, *, cap, nf, tm):
    f = pl.program_id(1)
    tf = wg_ref.shape[2]
    chunks = [slice(j * FFN_CHUNK, (j + 1) * FFN_CHUNK) for j in range(tf // FFN_CHUNK)]
    cast = {}

    def weight(name, ref, j):
        if (name, j) not in cast:
            cast[name, j] = (ref[0, chunks[j], :] if name == "d" else ref[0, :, chunks[j]]).astype(BF16)
        return cast[name, j]

    @pl.when(f == 0)
    def _():
        acc_ref[...] = jnp.zeros_like(acc_ref)

    for i in range(cap // tm):
        r = slice(i * tm, (i + 1) * tm)
        x = xe_ref[0, r, :]
        y = None
        for j in range(len(chunks)):
            hg = _dot(x, weight("g", wg_ref, j))
            hu = _dot(x, weight("u", wu_ref, j))
            h = (hg * (1.0 / (1.0 + jnp.exp(-hg))) * hu).astype(BF16)
            part = _dot(h, weight("d", wd_ref, j))
            y = part if y is None else y + part
        acc_ref[r, :] += y

    @pl.when(f == nf - 1)
    def _():
        ye_ref[...] = acc_ref[...].astype(BF16)


def _ffn(xe, w_eg, w_eu, w_ed):
    cap = xe.shape[1] - GATHER_PAD
    tf = 512
    nf = D_FF // tf
    tm = min(cap, 1024)
    return pl.pallas_call(
        functools.partial(_ffn_kernel, cap=cap, nf=nf, tm=tm),
        grid=(N_EXPERTS, nf),
        in_specs=[
            pl.BlockSpec((1, cap, D_MODEL), lambda e, f: (e, 0, 0)),
            pl.BlockSpec((1, D_MODEL, tf), lambda e, f: (e, 0, f)),
            pl.BlockSpec((1, D_MODEL, tf), lambda e, f: (e, 0, f)),
            pl.BlockSpec((1, tf, D_MODEL), lambda e, f: (e, f, 0)),
        ],
        out_specs=pl.BlockSpec((cap, D_MODEL), lambda e, f: (e, 0)),
        out_shape=jax.ShapeDtypeStruct((N_EXPERTS * cap, D_MODEL), BF16),
        scratch_shapes=[pltpu.VMEM((cap, D_MODEL), F32)],
        compiler_params=_params(("arbitrary", "arbitrary")),
        name="ffn",
    )(xe, w_eg, w_eu, w_ed)


def _combine_kernel(bsel_s, x1_ref, aff_ref, taut_ref, needt_ref, beqt_ref, bselt_ref, low_ref, spread_ref, gf_ref,
                    ye_hbm, y_ref, buf_ref, xbuf_ref, sem_ref, xsem_ref, *, cap, total, nstep):
    step = pl.program_id(0)
    par = step & 1
    per_tile = N_EXPERTS * SLOT_CHUNK

    def aligned(e, tile):
        return _floor_pow2(bsel_s[e, tile], ROW_ALIGN)

    def window(e, tile, c):
        start = jnp.minimum(e * cap + aligned(e, tile) + c * SLOT_CHUNK, total - SLOT_CHUNK)
        return pl.multiple_of(start, ROW_ALIGN)

    def first_chunks(stp, buf, sub):
        tile = stp * COMBINE_TILES + sub
        return [pltpu.make_async_copy(ye_hbm.at[pl.ds(window(e, tile, 0), SLOT_CHUNK)],
                                      buf_ref.at[buf, pl.ds(sub * per_tile + e * SLOT_CHUNK, SLOT_CHUNK)],
                                      sem_ref.at[buf])
                for e in range(N_EXPERTS)]

    @pl.when(step == 0)
    def _():
        for sub in range(COMBINE_TILES):
            for cp in first_chunks(0, 0, sub):
                cp.start()

    for sub in range(COMBINE_TILES):
        for cp in first_chunks(jnp.minimum(step + 1, nstep - 1), 1 - par, sub):
            cp.start()
    for sub in range(COMBINE_TILES):
        for cp in first_chunks(step, par, sub):
            cp.wait()

    tau = taut_ref[0:1, :]
    low = low_ref[...]
    spread = spread_ref[...]
    lane = lax.broadcasted_iota(I32, (1, LANES), 1)
    wide = lax.broadcasted_iota(I32, (ROUTE_TILE, per_tile), 1)
    in_chunk = (wide & (SLOT_CHUNK - 1)).astype(F32)
    slots, affs = [], []
    for sub in range(COMBINE_TILES):
        tile = step * COMBINE_TILES + sub
        rows = slice(sub * ROUTE_TILE, (sub + 1) * ROUTE_TILE)
        aff = aff_ref[rows, :]
        k = lax.bitcast_convert_type(aff, I32)
        eq = k == tau
        eq_cum = _dot(low, _ones_where(eq, BF16)) + beqt_ref[sub]
        sel = (k > tau) | (eq & (eq_cum <= needt_ref[0:1, :]))
        slot = jnp.where(sel, _dot(low, _ones_where(sel, BF16)) + (bselt_ref[sub] - 1.0), -1.0)

        rel = jnp.zeros((1, LANES), F32)
        for e in range(N_EXPERTS):
            rel = jnp.where(lane == e, (window(e, tile, 0) - e * cap).astype(F32), rel)
        d = slot - rel
        d = jnp.where(sel & (d >= 0.0) & (d < float(SLOT_CHUNK)), d, -1.0)
        hit = _dot(d.astype(BF16), spread) == in_chunk
        onehot_gate = jnp.where(hit, _dot(aff.astype(BF16), spread), 0.0).astype(BF16)
        y_ref[rows, :] = x1_ref[rows, :] + _dot(onehot_gate, buf_ref[par, sub * per_tile:(sub + 1) * per_tile, :])
        slots.append(slot)
        affs.append(aff)

    nch = [[_cdiv_pow2(bsel_s[e, step * COMBINE_TILES + sub + 1] - aligned(e, step * COMBINE_TILES + sub), SLOT_CHUNK)
            for e in range(N_EXPERTS)] for sub in range(COMBINE_TILES)]

    @pl.when(functools.reduce(jnp.maximum, [n for per_sub in nch for n in per_sub]) > 1)
    def _():
        lane64 = lax.broadcasted_iota(I32, (ROUTE_TILE, SLOT_CHUNK), 1).astype(F32)
        for sub in range(COMBINE_TILES):
            tile = step * COMBINE_TILES + sub
            rows = slice(sub * ROUTE_TILE, (sub + 1) * ROUTE_TILE)
            for e in range(N_EXPERTS):
                slot_e = slots[sub][:, e:e + 1]

                def extra(c, carry):
                    w = window(e, tile, c)
                    cp = pltpu.make_async_copy(ye_hbm.at[pl.ds(w, SLOT_CHUNK)], xbuf_ref, xsem_ref.at[0])
                    cp.start()
                    cp.wait()
                    first = (aligned(e, tile) + c * SLOT_CHUNK).astype(F32)
                    hit = (lane64 + (w - e * cap).astype(F32) == slot_e) & (slot_e >= first)
                    y_ref[rows, :] += affs[sub][:, e:e + 1] * _dot(_ones_where(hit, BF16), xbuf_ref[...])
                    return carry

                lax.fori_loop(1, nch[sub][e], extra, 0)

    acc = y_ref[...]
    ms = jnp.mean(acc * acc, axis=-1, keepdims=True)
    y_ref[...] = acc * lax.rsqrt(ms + EPS) * gf_ref[...]

    @pl.when(step == nstep - 1)
    def _():
        for sub in range(COMBINE_TILES):
            for cp in first_chunks(step, 1 - par, sub):
                cp.wait()


def _combine(bsel_i, x1, aff, tables_t, ye, gf):
    tokens = x1.shape[0]
    cap = CAPACITY_FACTOR * tokens // N_EXPERTS
    ntile = tokens // ROUTE_TILE
    idx = np.arange(ROUTE_TILE)
    low = jnp.asarray(idx[:, None] >= idx[None, :], BF16)
    spread = jnp.asarray(np.arange(LANES)[:, None] == np.arange(N_EXPERTS * SLOT_CHUNK)[None, :] // SLOT_CHUNK, BF16)
    taut, needt, beqt, bselt = tables_t
    rows = COMBINE_TILES * ROUTE_TILE
    rowvec = pl.BlockSpec((8, LANES), lambda t, *_: (0, 0))
    tilevec = pl.BlockSpec((COMBINE_TILES, 1, LANES), lambda t, *_: (t, 0, 0))
    grid_spec = pltpu.PrefetchScalarGridSpec(
        num_scalar_prefetch=1,
        grid=(ntile // COMBINE_TILES,),
        in_specs=[
            pl.BlockSpec((rows, D_MODEL), lambda t, *_: (t, 0)),
            pl.BlockSpec((rows, LANES), lambda t, *_: (t, 0)),
            rowvec, rowvec, tilevec, tilevec,
            pl.BlockSpec(low.shape, lambda t, *_: (0, 0)),
            pl.BlockSpec(spread.shape, lambda t, *_: (0, 0)),
            pl.BlockSpec((1, D_MODEL), lambda t, *_: (0, 0)),
            pl.BlockSpec(memory_space=pl.ANY),
        ],
        out_specs=pl.BlockSpec((rows, D_MODEL), lambda t, *_: (t, 0)),
        scratch_shapes=[
            pltpu.VMEM((2, COMBINE_TILES * N_EXPERTS * SLOT_CHUNK, D_MODEL), BF16),
            pltpu.VMEM((SLOT_CHUNK, D_MODEL), BF16),
            pltpu.SemaphoreType.DMA((2,)),
            pltpu.SemaphoreType.DMA((1,)),
        ],
    )
    return pl.pallas_call(
        functools.partial(_combine_kernel, cap=cap, total=N_EXPERTS * cap, nstep=ntile // COMBINE_TILES),
        grid_spec=grid_spec,
        out_shape=jax.ShapeDtypeStruct((tokens, D_MODEL), F32),
        compiler_params=_params(("arbitrary",)),
        name="combine",
    )(bsel_i, x1, aff, taut, needt, beqt.reshape(LANES, 1, LANES), bselt.reshape(LANES, 1, LANES), low, spread,
      gf, ye)


def _encoder(x, w):
    batch, seq, _ = x.shape
    tokens = batch * seq
    xt = x.reshape(tokens, D_MODEL)
    *qkvs, vr, vi, gates = _in_proj(xt, w["g1"], w["w_in"], w["w_gate"], w["b_gate"], w["cs"], batch, seq)
    outs, lses = [], []
    for g in range(N_GROUPS):
        o, lse = _attention(qkvs[g], w["bias"][g], g)
        outs.append(o)
        lses.append(lse)
    four = _fourier(vr, vi, batch, seq)
    x1, xn, aff, afft = _mix(xt, outs, lses, four, gates, w["w_attn"], w["w_four"], w["w_out"], w["g2"],
                             w["w_router"])
    tau, need, beq_i, bsel_i, taut, needt, beqt, bselt = _route(afft)
    idx = np.arange(ROUTE_TILE)
    u = jnp.asarray(idx[:, None] <= idx[None, :], BF16)
    xe = _gather(bsel_i, afft, tau, need, beq_i, xn, u)
    ye = _ffn(xe, w["w_eg"], w["w_eu"], w["w_ed"])
    y = _combine(bsel_i, x1, aff, (taut, needt, beqt, bselt), ye, w["gf"])
    return y.reshape(batch, seq, D_MODEL)


def _prepare_weights(rel_bias, norm1_g, w_in, w_attn_br, w_four_br, w_gate, b_gate, w_out,
                     norm2_g, w_router, w_exp_gate, w_exp_up, w_exp_down, final_g):
    c, s = _dft_mats(F_CH)
    starts = [part * ATT_W + g * GROUP_W for g in range(N_GROUPS) for part in range(3)]
    w_in_grouped = jnp.concatenate([w_in[0][:, s0:s0 + GROUP_W] for s0 in starts] + [w_in[0][:, QKV_W:]], axis=1)
    w_router = jnp.pad(w_router[0], ((0, 0), (0, LANES - N_EXPERTS)))
    w_router_hi = w_router.astype(BF16)
    return {
        "g1": norm1_g[0].reshape(1, D_MODEL),
        "w_in": w_in_grouped.astype(BF16),
        "w_gate": w_gate[0].astype(BF16),
        "b_gate": b_gate[0].reshape(1, 2 * D_MODEL),
        "cs": jnp.asarray(np.concatenate([c, s], axis=1), BF16),
        "bias": [_attention_bias(rel_bias, g) for g in range(N_GROUPS)],
        "w_attn": w_attn_br[0].astype(BF16),
        "w_four": w_four_br[0].astype(BF16),
        "w_out": w_out[0].astype(BF16),
        "g2": norm2_g[0].reshape(1, D_MODEL),
        "w_router": jnp.concatenate([w_router_hi, (w_router - w_router_hi.astype(F32)).astype(BF16)], axis=1),
        "w_eg": w_exp_gate[0],
        "w_eu": w_exp_up[0],
        "w_ed": w_exp_down[0],
        "gf": final_g.reshape(1, D_MODEL),
    }


def kernel(x_prompt, x_sample, rel_bias, norm1_g, w_in, w_attn_br, w_four_br, w_gate, b_gate, w_out,
           norm2_g, w_router, w_exp_gate, w_exp_up, w_exp_down, final_g):
    w = _prepare_weights(rel_bias, norm1_g, w_in, w_attn_br, w_four_br, w_gate, b_gate, w_out,
                         norm2_g, w_router, w_exp_gate, w_exp_up, w_exp_down, final_g)
    return (_encoder(x_prompt, w), _encoder(x_sample, w))
```

```python
import functools
import math

import numpy as np
import jax
import jax.numpy as jnp
from jax import lax
from jax.experimental import pallas as pl
from jax.experimental.pallas import tpu as pltpu

D_MODEL = 1024
HEAD_DIM = 64
HEADS_PER_GROUP = 4
GROUPS = ((128, 1), (512, 4), (2048, 16))
N_GROUPS = len(GROUPS)
GROUP_W = HEADS_PER_GROUP * HEAD_DIM
ATT_W = N_GROUPS * GROUP_W
QKV_W = 3 * ATT_W
F_GROUPS = 6
F_CH = 128
F_W = F_GROUPS * F_CH
NUM_BUCKETS = 32
MAX_DISTANCE = 1024
N_EXPERTS = 16
CAPACITY_FACTOR = 2
D_FF = 2048
EPS = 1e-6
NEG = -1e30

HALF_KEYS = 64
ATT_SUB = 128
ATT_OUT_ROWS = 8192
TOKEN_TILE = 512
ROUTE_TILE = 256
FFT_STEP_ROWS = 1024
FFN_CHUNK = 256
SLOT_CHUNK = 64
COMBINE_TILES = 2
ROW_ALIGN = 16
GATHER_BLOCK = SLOT_CHUNK + ROW_ALIGN
GATHER_STACK = GATHER_BLOCK + ROW_ALIGN
GATHER_PAD = GATHER_BLOCK
LANES = 128
V7X_VMEM_LIMIT = 56 * 1024 * 1024

F32 = jnp.float32
BF16 = jnp.bfloat16
I32 = jnp.int32
U32 = jnp.uint32


def _params(sem):
    return pltpu.CompilerParams(dimension_semantics=sem, vmem_limit_bytes=V7X_VMEM_LIMIT)


def _dot(a, b):
    return jnp.dot(a, b, preferred_element_type=F32)


def _dot_nt(a, b):
    return lax.dot_general(a, b, (((1,), (1,)), ((), ())), preferred_element_type=F32)


def _floor_pow2(x, m):
    return x & ~(m - 1)


def _cdiv_pow2(x, m):
    return (x + (m - 1)) >> (m.bit_length() - 1)


def _ones_where(mask, dtype=F32):
    return jnp.where(mask, jnp.ones((), F32), jnp.zeros((), F32)).astype(dtype)


def _in_proj_kernel(x_ref, g_ref, win_ref, wg_ref, bg_ref, cs_ref, qscale_ref, qkv0_ref, qkv1_ref, qkv2_ref,
                    vr_ref, vi_ref, gates_ref, slab_ref):
    tm = x_ref.shape[0]
    half = tm // 2
    nslab = ATT_W // LANES
    cs = cs_ref[...]
    for h in range(2):
        rows = slice(h * half, (h + 1) * half)
        x = x_ref[rows, :]
        ms = jnp.mean(x * x, axis=-1, keepdims=True)
        xn = (x * lax.rsqrt(ms + EPS) * g_ref[...]).astype(BF16)
        for g, out_ref in enumerate((qkv0_ref, qkv1_ref, qkv2_ref)):
            dil = GROUPS[g][1]
            res = _dot(xn, win_ref[:, g * ATT_W:(g + 1) * ATT_W]) * qscale_ref[...]
            if dil == 1:
                out_ref[0, 0, rows, :] = res.astype(BF16)
                continue
            for j in range(nslab):
                slab_ref[j, rows, :] = res[:, j * LANES:(j + 1) * LANES]
            n = half // dil
            for r in range(dil):
                cls = [slab_ref[j, pl.ds(h * half + r, n, stride=dil), :] for j in range(nslab)]
                out_ref[0, r, h * n:(h + 1) * n, :] = jnp.concatenate(cls, axis=1).astype(BF16)
        u = _dot(xn, win_ref[:, QKV_W:QKV_W + F_W]).astype(BF16)
        for g in range(F_GROUPS):
            a = _dot(u[:, g * F_CH:(g + 1) * F_CH], cs)
            vr_ref[rows, g * F_CH:(g + 1) * F_CH] = a[:, :F_CH].astype(BF16)
            vi_ref[rows, g * F_CH:(g + 1) * F_CH] = (-a[:, F_CH:]).astype(BF16)
        z = _dot(xn, wg_ref[...]) + bg_ref[...]
        gates_ref[rows, :] = (1.0 / (1.0 + jnp.exp(-z))).astype(BF16)


def _class_major_spec(tm, dil, width, per_batch):
    return pl.BlockSpec((1, dil, tm // dil, width), lambda i: (i // per_batch, 0, i % per_batch, 0))


def _in_proj(x, g1, w_in, w_gate, b_gate, cs, batch, seq):
    t = x.shape[0]
    tm = TOKEN_TILE
    per_batch = seq // tm
    const = lambda i: (0, 0)
    row = lambda i: (i, 0)
    qscale = np.ones((1, ATT_W), np.float32)
    qscale[:, :GROUP_W] = 1.0 / math.sqrt(HEAD_DIM)
    return pl.pallas_call(
        _in_proj_kernel,
        grid=(t // tm,),
        in_specs=[
            pl.BlockSpec((tm, D_MODEL), row),
            pl.BlockSpec((1, D_MODEL), const),
            pl.BlockSpec(w_in.shape, const),
            pl.BlockSpec(w_gate.shape, const),
            pl.BlockSpec((1, 2 * D_MODEL), const),
            pl.BlockSpec(cs.shape, const),
            pl.BlockSpec((1, ATT_W), const),
        ],
        out_specs=[_class_major_spec(tm, dil, ATT_W, per_batch) for _, dil in GROUPS] + [
            pl.BlockSpec((tm, F_W), row),
            pl.BlockSpec((tm, F_W), row),
            pl.BlockSpec((tm, 2 * D_MODEL), row),
        ],
        out_shape=[jax.ShapeDtypeStruct((batch, dil, seq // dil, ATT_W), BF16) for _, dil in GROUPS] + [
            jax.ShapeDtypeStruct((t, F_W), BF16),
            jax.ShapeDtypeStruct((t, F_W), BF16),
            jax.ShapeDtypeStruct((t, 2 * D_MODEL), BF16),
        ],
        scratch_shapes=[pltpu.VMEM((ATT_W // LANES, tm, LANES), F32)],
        compiler_params=_params(("parallel",)),
        name="in_proj",
    )(x, g1, w_in, w_gate, b_gate, cs, jnp.asarray(qscale))


def _attention_kernel(q_ref, kp_ref, kc_ref, kn_ref, vp_ref, vc_ref, vn_ref, bias_ref, o_ref, lse_ref, *,
                      tq, length, dil, rc):
    i = pl.program_id(1)
    win = ATT_SUB + 2 * HALF_KEYS
    nsub = tq // ATT_SUB
    lane_head = lax.broadcasted_iota(I32, (ATT_SUB, GROUP_W), 1) // HEAD_DIM
    at_start = (i == 0).astype(I32)
    at_end = (i == length // tq - 1).astype(I32) * 2
    for c, sb in [(c, sb) for c in range(rc) for sb in range(nsub)]:
        r = pl.program_id(2) * rc + c
        if sb == 0:
            kwin = jnp.concatenate([kp_ref[0, c], kc_ref[0, c], kn_ref[0, c]], axis=0)
            vwin = jnp.concatenate([vp_ref[0, c], vc_ref[0, c], vn_ref[0, c]], axis=0)
        off = sb * ATT_SUB
        q = q_ref[0, c, off:off + ATT_SUB, :]
        kw = kwin[off:off + win]
        vw = vwin[off:off + win]
        variant = (at_start if sb == 0 else 0) + (at_end if sb == nsub - 1 else 0)
        qs = jnp.concatenate(
            [jnp.where(lane_head == h, q, jnp.zeros_like(q)) for h in range(HEADS_PER_GROUP)], axis=0)
        s_all = _dot_nt(qs, kw)
        ps, ms, ls = [], [], []
        for h in range(HEADS_PER_GROUP):
            s = s_all[h * ATT_SUB:(h + 1) * ATT_SUB] + bias_ref[variant, h]
            m = jnp.max(s, axis=-1, keepdims=True)
            p = jnp.exp(s - m)
            ls.append(jnp.sum(p, axis=-1, keepdims=True))
            ms.append(m)
            ps.append(p.astype(BF16))
        o_all = _dot(jnp.concatenate(ps, axis=0), vw)
        out = jnp.zeros((ATT_SUB, GROUP_W), F32)
        lse = jnp.zeros((ATT_SUB, GROUP_W), F32)
        for h in range(HEADS_PER_GROUP):
            oh = o_all[h * ATT_SUB:(h + 1) * ATT_SUB] * (1.0 / ls[h])
            out = jnp.where(lane_head == h, oh, out)
            lse = jnp.where(lane_head == h, ms[h] + jnp.log(ls[h]), lse)
        rows = pl.ds(off * dil + r, ATT_SUB, stride=dil) if dil > 1 else pl.ds(off, ATT_SUB)
        o_ref[rows, :] = _pack_bf16_pair(out[:, :LANES], out[:, LANES:])
        for j in range(GROUP_W // LANES):
            lse_ref[j, rows, :] = lse[:, j * LANES:(j + 1) * LANES]


def _attention(qkv, bias, g):
    batch, dil, length, _ = qkv.shape
    tq = min(length, 512, ATT_OUT_ROWS // dil)
    nb = length // tq
    hb = tq // HALF_KEYS
    last_halo = length // HALF_KEYS - 1
    rc = min(dil, max(1, 512 // tq))

    def cur(c):
        return lambda b, i, r: (b, r, i, c)

    def prev(c):
        return lambda b, i, r: (b, r, jnp.maximum(i * hb - 1, 0), c)

    def nxt(c):
        return lambda b, i, r: (b, r, jnp.minimum((i + 1) * hb, last_halo), c)

    blk = lambda rows: (1, rc, rows, GROUP_W)
    out_spec = pl.BlockSpec((GROUP_W // LANES, tq * dil, LANES), lambda b, i, r: (0, b * nb + i, 0))
    return pl.pallas_call(
        functools.partial(_attention_kernel, tq=tq, length=length, dil=dil, rc=rc),
        grid=(batch, nb, dil // rc),
        in_specs=[
            pl.BlockSpec(blk(tq), cur(0)),
            pl.BlockSpec(blk(HALF_KEYS), prev(1)),
            pl.BlockSpec(blk(tq), cur(1)),
            pl.BlockSpec(blk(HALF_KEYS), nxt(1)),
            pl.BlockSpec(blk(HALF_KEYS), prev(2)),
            pl.BlockSpec(blk(tq), cur(2)),
            pl.BlockSpec(blk(HALF_KEYS), nxt(2)),
            pl.BlockSpec(bias.shape, lambda b, i, r: (0, 0, 0, 0)),
        ],
        out_specs=[pl.BlockSpec((tq * dil, LANES), lambda b, i, r: (b * nb + i, 0)), out_spec],
        out_shape=[jax.ShapeDtypeStruct((batch * dil * length, LANES), U32),
                   jax.ShapeDtypeStruct((GROUP_W // LANES, batch * dil * length, LANES), F32)],
        compiler_params=_params(("parallel", "parallel", "arbitrary")),
        name=f"attention_g{g}",
    )(qkv, qkv, qkv, qkv, qkv, qkv, qkv, bias)


def _t5_bucket(rel):
    nb = NUM_BUCKETS // 2
    max_exact = nb // 2
    ret = (rel > 0).astype(np.int32) * nb
    n = np.abs(rel)
    large = max_exact + (np.log(np.maximum(n, max_exact) / max_exact)
                         / np.log(MAX_DISTANCE / max_exact) * (nb - max_exact)).astype(np.int32)
    large = np.minimum(large, nb - 1)
    return (ret + np.where(n < max_exact, n, large)).astype(np.int32)


def _attention_bias(rel_bias, g):
    dil = GROUPS[g][1]
    qi = np.arange(ATT_SUB)[:, None]
    kj = np.arange(ATT_SUB + 2 * HALF_KEYS)[None, :]
    delta = kj - HALF_KEYS - qi
    band = np.abs(delta) <= HALF_KEYS
    bucket = _t5_bucket(dil * delta)
    tab = rel_bias[:, g * HEADS_PER_GROUP:(g + 1) * HEADS_PER_GROUP].astype(F32)
    onehot = jnp.asarray(bucket[..., None] == np.arange(NUM_BUCKETS), F32)
    bias = jnp.einsum("qkb,bh->hqk", onehot, tab, precision=lax.Precision.HIGHEST)
    masks = [band & ((kj >= HALF_KEYS) | ((v & 1) == 0)) & ((kj < ATT_SUB + HALF_KEYS) | ((v & 2) == 0))
             for v in range(4)]
    return jnp.where(jnp.asarray(np.stack(masks))[:, None], bias[None], NEG)


def _dft_mats(n):
    k = np.arange(n)
    ang = 2.0 * np.pi * ((k[:, None] * k[None, :]) % n) / n
    return np.cos(ang), np.sin(ang)


def _pack_bf16_pair(a, b):
    hi = lax.bitcast_convert_type(a.astype(BF16).astype(F32), U32)
    lo = lax.bitcast_convert_type(b.astype(BF16).astype(F32), U32)
    return hi | lax.shift_right_logical(lo, jnp.full(lo.shape, 16, U32))


def _unpack_bf16_pair(word):
    a = lax.bitcast_convert_type(word & jnp.uint32(0xFFFF0000), F32)
    b = lax.bitcast_convert_type(lax.shift_left(word, jnp.full(word.shape, 16, U32)), F32)
    return a.astype(BF16), b.astype(BF16)


def _fft_stage1_kernel(vr_ref, vi_ref, m1_ref, twc_ref, tws_ref, z_ref, *, n1, m):
    x = jnp.concatenate([vr_ref[0], vi_ref[0]], axis=0)
    z = _dot(m1_ref[...], x)
    zr, zi = z[:n1], z[n1:]
    twc, tws = twc_ref[0], tws_ref[0]
    for j in range(m):
        c = twc[:, j:j + 1]
        s = tws[:, j:j + 1]
        a = zr[:, j * F_W:(j + 1) * F_W]
        b = zi[:, j * F_W:(j + 1) * F_W]
        z_ref[0, :, j, :] = _pack_bf16_pair(a * c + b * s, b * c - a * s)


def _fft_stage2_kernel(z_ref, m2_ref, o_ref, *, kc, scale):
    m2 = m2_ref[...]
    for j in range(kc):
        x = jnp.concatenate(_unpack_bf16_pair(z_ref[0, j]), axis=0)
        y = _dot(m2, x) * scale
        o_ref[0, :, j, :] = _pack_bf16_pair(y[:, :F_W // 2], y[:, F_W // 2:])


def _fourier(vr, vi, batch, seq):
    n2 = LANES
    n1 = seq // n2
    m = min(n2, FFT_STEP_ROWS // n1)
    c1, s1 = _dft_mats(n1)
    m1 = jnp.asarray(np.block([[c1, s1], [-s1, c1]]), BF16)
    c2, s2 = _dft_mats(n2)
    m2 = jnp.asarray(np.concatenate([c2, s2], axis=1), BF16)
    k1 = np.arange(n1)[:, None]
    sv = np.arange(n2)[None, :]
    ang = 2.0 * np.pi * ((k1 * sv) % seq) / seq
    to_blocks = lambda a: jnp.asarray(a.reshape(n1, n2 // m, m).transpose(1, 0, 2), F32)
    twc, tws = to_blocks(np.cos(ang)), to_blocks(np.sin(ang))

    v3 = lambda a: a.reshape(batch, n1, n2 * F_W)
    blk = pl.BlockSpec((1, n1, m * F_W), lambda b, j: (b, 0, j))
    tmap = lambda b, j: (j, 0, 0)
    z = pl.pallas_call(
        functools.partial(_fft_stage1_kernel, n1=n1, m=m),
        grid=(batch, n2 // m),
        in_specs=[
            blk,
            blk,
            pl.BlockSpec(m1.shape, lambda b, j: (0, 0)),
            pl.BlockSpec((1, n1, m), tmap),
            pl.BlockSpec((1, n1, m), tmap),
        ],
        out_specs=pl.BlockSpec((1, n1, m, F_W), lambda b, j: (b, 0, j, 0)),
        out_shape=jax.ShapeDtypeStruct((batch, n1, n2, F_W), U32),
        compiler_params=_params(("parallel", "parallel")),
        name="fft_stage1",
    )(v3(vr), v3(vi), m1, twc, tws)

    kc = min(n1, FFT_STEP_ROWS // n2 * 2)
    out = pl.pallas_call(
        functools.partial(_fft_stage2_kernel, kc=kc, scale=1.0 / math.sqrt(seq * F_CH)),
        grid=(batch, n1 // kc),
        in_specs=[
            pl.BlockSpec((1, kc, n2, F_W), lambda b, j: (b, j, 0, 0)),
            pl.BlockSpec(m2.shape, lambda b, j: (0, 0)),
        ],
        out_specs=pl.BlockSpec((1, n2, kc, F_W // 2), lambda b, j: (b, 0, j, 0)),
        out_shape=jax.ShapeDtypeStruct((batch, n2, n1, F_W // 2), U32),
        compiler_params=_params(("parallel", "parallel")),
        name="fft_stage2",
    )(z, m2)
    return out.reshape(batch * seq, F_W // 2)


def _mix_kernel(x_ref, o0_ref, o1_ref, o2_ref, l0_ref, l1_ref, l2_ref, four_ref, gates_ref,
                wa_ref, wf_ref, wo_ref, g2_ref, wr_ref, x1_ref, xn_ref, aff_ref, afft_ref):
    tm = x_ref.shape[0]
    half = tm // 2
    for rows in (slice(0, half), slice(half, tm)):
        def slabs(ref):
            return jnp.concatenate([ref[j, rows, :] for j in range(GROUP_W // LANES)], axis=1)

        f_br = _dot(jnp.concatenate(_unpack_bf16_pair(four_ref[rows, :]), axis=1), wf_ref[...])
        l0, l1, l2 = slabs(l0_ref), slabs(l1_ref), slabs(l2_ref)
        mx = jnp.maximum(jnp.maximum(l0, l1), l2)
        e0, e1, e2 = jnp.exp(l0 - mx), jnp.exp(l1 - mx), jnp.exp(l2 - mx)
        o0, o1, o2 = (jnp.concatenate(_unpack_bf16_pair(ref[rows, :]), axis=1) for ref in (o0_ref, o1_ref, o2_ref))
        att = (e0 * o0 + e1 * o1 + e2 * o2) * (1.0 / (e0 + e1 + e2))
        a_br = _dot(att.astype(BF16), wa_ref[...])
        mix = gates_ref[rows, :D_MODEL] * a_br + gates_ref[rows, D_MODEL:] * f_br
        x1 = x_ref[rows, :] + _dot(mix.astype(BF16), wo_ref[...])
        x1_ref[rows, :] = x1
        ms = jnp.mean(x1 * x1, axis=-1, keepdims=True)
        xn = x1 * lax.rsqrt(ms + EPS) * g2_ref[...]
        xn_ref[rows, :] = xn.astype(BF16)
        xh = xn.astype(BF16)
        xl = (xn - xh.astype(F32)).astype(BF16)
        both = _dot(xh, wr_ref[...])
        logits = both[:, :LANES] + (both[:, LANES:] + _dot(xl, wr_ref[:, :LANES]))
        lane = lax.broadcasted_iota(I32, logits.shape, 1)
        logits = jnp.where(lane < N_EXPERTS, logits, NEG)
        p = jnp.exp(logits - jnp.max(logits, axis=-1, keepdims=True))
        aff = p * (1.0 / jnp.sum(p, axis=-1, keepdims=True))
        aff_ref[rows, :] = aff
        afft_ref[:, rows] = aff.T[:N_EXPERTS]


def _mix(x, os_, ls_, four, gates, w_attn, w_four, w_out, g2, w_router):
    t = x.shape[0]
    tm = TOKEN_TILE
    const = lambda i: (0, 0)
    row = lambda i: (i, 0)
    rows = lambda w: pl.BlockSpec((tm, w), row)
    full = lambda a: pl.BlockSpec(a.shape, const)
    slab = pl.BlockSpec((GROUP_W // LANES, tm, LANES), lambda i: (0, i, 0))
    return pl.pallas_call(
        _mix_kernel,
        grid=(t // tm,),
        in_specs=[rows(D_MODEL)] + [rows(LANES)] * 3 + [slab] * 3 + [rows(F_W // 2), rows(2 * D_MODEL),
                  full(w_attn), full(w_four), full(w_out), full(g2), full(w_router)],
        out_specs=[rows(D_MODEL), rows(D_MODEL), rows(LANES), pl.BlockSpec((N_EXPERTS, tm), lambda i: (0, i))],
        out_shape=[
            jax.ShapeDtypeStruct((t, D_MODEL), F32),
            jax.ShapeDtypeStruct((t, D_MODEL), BF16),
            jax.ShapeDtypeStruct((t, LANES), F32),
            jax.ShapeDtypeStruct((N_EXPERTS, t), F32),
        ],
        compiler_params=_params(("parallel",)),
        name="mix",
    )(x, *os_, *ls_, four, gates, w_attn, w_four, w_out, g2, w_router)


def _route_kernel(afft_ref, su_ref, u_ref, tau_ref, need_ref, beq_ref, bsel_ref,
                  taut_ref, needt_ref, beqt_ref, bselt_ref, *, tokens):
    cap = CAPACITY_FACTOR * tokens // N_EXPERTS
    ntile = tokens // ROUTE_TILE
    shape = (N_EXPERTS, LANES)
    lane = lax.broadcasted_iota(I32, shape, 1)

    def keys(start, width):
        return lax.bitcast_convert_type(afft_ref[:, pl.ds(pl.multiple_of(start, LANES), width)], I32)

    span = min(tokens, 16 * LANES)

    def count(pred):
        def body(c, acc):
            hits = _ones_where(pred(keys(c * span, span)))
            for j in range(span // LANES):
                acc = acc + hits[:, j * LANES:(j + 1) * LANES]
            return acc
        acc = lax.fori_loop(0, tokens // span, body, jnp.zeros(shape, F32), unroll=True)
        return jnp.sum(acc, axis=1, keepdims=True)

    def bit_body(i, prefix):
        cand = prefix | lax.shift_left(jnp.ones(shape, I32), jnp.full(shape, 30 - i, I32))
        tot = count(lambda k: k >= cand[:, :1])
        return jnp.where(tot >= cap, cand, prefix)

    tau = lax.fori_loop(0, 31, bit_body, jnp.zeros(shape, I32))
    tau_col = tau[:, :1]
    n_gt = count(lambda k: k > tau_col)
    need = cap - n_gt

    def prefix_over_tiles(tab):
        return _dot(tab.astype(BF16), su_ref[...])

    def at_lane(tab, c):
        return jnp.sum(jnp.where(lane == c, tab, 0.0), axis=1, keepdims=True)

    def eq_body(c, tab):
        k = keys(c * ROUTE_TILE, ROUTE_TILE)
        cnt = jnp.sum(_ones_where(k == tau_col), axis=1, keepdims=True)
        return jnp.where(lane == c, cnt, tab)

    unroll = 8 if ntile % 8 == 0 else 1
    base_eq = prefix_over_tiles(lax.fori_loop(0, ntile, eq_body, jnp.zeros(shape, F32), unroll=unroll))

    def sel_body(c, tab):
        k = keys(c * ROUTE_TILE, ROUTE_TILE)
        eq = k == tau_col
        eq_cum = _dot(_ones_where(eq, BF16), u_ref[...]) + at_lane(base_eq, c)
        sel = (k > tau_col) | (eq & (eq_cum <= need))
        cnt = jnp.sum(_ones_where(sel), axis=1, keepdims=True)
        return jnp.where(lane == c, cnt, tab)

    base_sel = prefix_over_tiles(lax.fori_loop(0, ntile, sel_body, jnp.zeros(shape, F32), unroll=unroll))

    def transposed(val):
        return jnp.concatenate([val, jnp.zeros((LANES - N_EXPERTS, LANES), val.dtype)], axis=0).T

    tau_ref[...] = tau
    taut_ref[...] = transposed(tau)
    for val, ref, ref_t in ((jnp.broadcast_to(need, shape), need_ref, needt_ref),
                            (base_eq, beq_ref, beqt_ref), (base_sel, bsel_ref, bselt_ref)):
        ref[...] = val.astype(I32)
        ref_t[...] = transposed(val)


def _route(afft):
    tokens = afft.shape[1]
    idx = np.arange(LANES)
    su = jnp.asarray(idx[:, None] < idx[None, :], BF16)
    idx = np.arange(ROUTE_TILE)
    u = jnp.asarray(idx[:, None] <= idx[None, :], BF16)
    full = lambda a: pl.BlockSpec(a.shape, lambda i: (0,) * a.ndim)
    small = pl.BlockSpec((N_EXPERTS, LANES), lambda i: (0, 0))
    smallt = pl.BlockSpec((LANES, LANES), lambda i: (0, 0))
    return pl.pallas_call(
        functools.partial(_route_kernel, tokens=tokens),
        grid=(1,),
        in_specs=[full(afft), full(su), full(u)],
        out_specs=[small] * 4 + [smallt] * 4,
        out_shape=[jax.ShapeDtypeStruct((N_EXPERTS, LANES), I32)] * 4
        + [jax.ShapeDtypeStruct((LANES, LANES), I32)] + [jax.ShapeDtypeStruct((LANES, LANES), F32)] * 3,
        compiler_params=_params(("arbitrary",)),
        name="route",
    )(afft, su, u)


def _gather_kernel(bsel_s, afft_ref, tau_ref, need_ref, beq_ref, x_ref, u_ref, xe_hbm,
                   stage_ref, tail_ref, xbuf_ref, sem_ref, xsem_ref, *, ntile, cap):
    t = pl.program_id(0)
    par = t & 1

    def aligned(e, tile):
        return pl.multiple_of(_floor_pow2(bsel_s[e, tile], ROW_ALIGN), ROW_ALIGN)

    def write(e, tile, buf, first_row=None):
        first_row = aligned(e, tile) if first_row is None else first_row
        return pltpu.make_async_copy(stage_ref.at[buf, e], xe_hbm.at[e, pl.ds(first_row, GATHER_BLOCK)],
                                     sem_ref.at[buf])

    @pl.when(t == 0)
    def _():
        tail_ref[...] = jnp.zeros_like(tail_ref)
        stage_ref[1] = jnp.zeros(stage_ref.shape[1:], BF16)
        for e in range(N_EXPERTS):
            write(e, 0, 1, first_row=cap).start()

    k = lax.bitcast_convert_type(afft_ref[...], I32)
    tau = tau_ref[:, :1]
    lane = lax.broadcasted_iota(I32, (N_EXPERTS, LANES), 1)
    beq = jnp.sum(jnp.where(lane == t, beq_ref[...].astype(F32), 0.0), axis=1, keepdims=True)
    eq = k == tau
    eq_cum = _dot(_ones_where(eq, BF16), u_ref[...]) + beq
    sel = (k > tau) | (eq & (eq_cum <= need_ref[:, :1].astype(F32)))
    rank = jnp.where(sel, _dot(_ones_where(sel, BF16), u_ref[...]) - 1.0, -1e4)

    row = lax.broadcasted_iota(I32, (GATHER_STACK, ROUTE_TILE), 0)
    in_block = row < GATHER_BLOCK
    row_f = row.astype(F32)
    offs, shifts, pieces = [], [], []
    for e in range(N_EXPERTS):
        off = (bsel_s[e, t] - aligned(e, t)).astype(F32)
        shift = _floor_pow2(bsel_s[e, t + 1], ROW_ALIGN) - aligned(e, t)
        target = jnp.where(in_block, row_f, row_f - float(GATHER_BLOCK) + shift.astype(F32))
        pieces.append(_ones_where(rank[e:e + 1, :] + off == target, BF16))
        offs.append(off)
        shifts.append(shift)
    res = _dot(jnp.concatenate(pieces, axis=0), x_ref[...])
    for e in range(N_EXPERTS):
        base = e * GATHER_STACK
        old = tail_ref[e]
        stage_ref[par, e, 0:ROW_ALIGN, :] = (res[base:base + ROW_ALIGN] + old).astype(BF16)
        stage_ref[par, e, ROW_ALIGN:GATHER_BLOCK, :] = res[base + ROW_ALIGN:base + GATHER_BLOCK].astype(BF16)
        tail_ref[e] = res[base + GATHER_BLOCK:base + GATHER_STACK] + jnp.where(shifts[e] == 0, old, 0.0)
    for e in range(N_EXPERTS):
        write(e, jnp.maximum(t - 1, 0), 1 - par).wait()
    for e in range(N_EXPERTS):
        write(e, t, par).start()

    extra = [_cdiv_pow2(jnp.maximum(bsel_s[e, t + 1] - aligned(e, t) - GATHER_BLOCK, 0), SLOT_CHUNK)
             for e in range(N_EXPERTS)]

    @pl.when(functools.reduce(jnp.maximum, extra) > 0)
    def _():
        row64 = lax.broadcasted_iota(I32, (SLOT_CHUNK, ROUTE_TILE), 0).astype(F32)
        for e in range(N_EXPERTS):
            def chunk(c, carry):
                first = GATHER_BLOCK + c * SLOT_CHUNK
                onehot = _ones_where(rank[e:e + 1, :] + offs[e] == row64 + first.astype(F32), BF16)
                xbuf_ref[...] = _dot(onehot, x_ref[...]).astype(BF16)
                dst = pl.multiple_of(aligned(e, t) + first, ROW_ALIGN)
                cp = pltpu.make_async_copy(xbuf_ref, xe_hbm.at[e, pl.ds(dst, SLOT_CHUNK)], xsem_ref.at[0])
                cp.start()
                cp.wait()
                return carry

            lax.fori_loop(0, extra[e], chunk, 0)

    @pl.when(t == ntile - 1)
    def _():
        for e in range(N_EXPERTS):
            write(e, t, par).wait()


def _gather(bsel_i, afft, tau, need, beq_i, xn, u):
    tokens = xn.shape[0]
    cap = CAPACITY_FACTOR * tokens // N_EXPERTS
    ntile = tokens // ROUTE_TILE
    table = pl.BlockSpec((N_EXPERTS, LANES), lambda t, *_: (0, 0))
    grid_spec = pltpu.PrefetchScalarGridSpec(
        num_scalar_prefetch=1,
        grid=(ntile,),
        in_specs=[
            pl.BlockSpec((N_EXPERTS, ROUTE_TILE), lambda t, *_: (0, t)),
            table, table, table,
            pl.BlockSpec((ROUTE_TILE, D_MODEL), lambda t, *_: (t, 0)),
            pl.BlockSpec(u.shape, lambda t, *_: (0, 0)),
        ],
        out_specs=pl.BlockSpec(memory_space=pl.ANY),
        scratch_shapes=[
            pltpu.VMEM((2, N_EXPERTS, GATHER_BLOCK, D_MODEL), BF16),
            pltpu.VMEM((N_EXPERTS, ROW_ALIGN, D_MODEL), F32),
            pltpu.VMEM((SLOT_CHUNK, D_MODEL), BF16),
            pltpu.SemaphoreType.DMA((2,)),
            pltpu.SemaphoreType.DMA((1,)),
        ],
    )
    return pl.pallas_call(
        functools.partial(_gather_kernel, ntile=ntile, cap=cap),
        grid_spec=grid_spec,
        out_shape=jax.ShapeDtypeStruct((N_EXPERTS, cap + GATHER_PAD, D_MODEL), BF16),
        compiler_params=_params(("arbitrary",)),
        name="gather",
    )(bsel_i, afft, tau, need, beq_i, xn, u)


def _ffn_kernel(xe_ref, wg_ref, wu_ref, wd_ref, ye_ref, acc_ref, *, cap, nf, tm):
    f = pl.program_id(1)
    tf = wg_ref.shape[2]
    chunks = [slice(j * FFN_CHUNK, (j + 1) * FFN_CHUNK) for j in range(tf // FFN_CHUNK)]
    cast = {}

    def weight(name, ref, j):
        if (name, j) not in cast:
            cast[name, j] = (ref[0, chunks[j], :] if name == "d" else ref[0, :, chunks[j]]).astype(BF16)
        return cast[name, j]

    @pl.when(f == 0)
    def _():
        acc_ref[...] = jnp.zeros_like(acc_ref)

    for i in range(cap // tm):
        r = slice(i * tm, (i + 1) * tm)
        x = xe_ref[0, r, :]
        y = None
        for j in range(len(chunks)):
            hg = _dot(x, weight("g", wg_ref, j))
            hu = _dot(x, weight("u", wu_ref, j))
            h = (hg * (1.0 / (1.0 + jnp.exp(-hg))) * hu).astype(BF16)
            part = _dot(h, weight("d", wd_ref, j))
            y = part if y is None else y + part
        acc_ref[r, :] += y

    @pl.when(f == nf - 1)
    def _():
        ye_ref[...] = acc_ref[...].astype(BF16)


def _ffn(xe, w_eg, w_eu, w_ed):
    cap = xe.shape[1] - GATHER_PAD
    tf = 512
    nf = D_FF // tf
    tm = min(cap, 1024)
    return pl.pallas_call(
        functools.partial(_ffn_kernel, cap=cap, nf=nf, tm=tm),
        grid=(N_EXPERTS, nf),
        in_specs=[
            pl.BlockSpec((1, cap, D_MODEL), lambda e, f: (e, 0, 0)),
            pl.BlockSpec((1, D_MODEL, tf), lambda e, f: (e, 0, f)),
            pl.BlockSpec((1, D_MODEL, tf), lambda e, f: (e, 0, f)),
            pl.BlockSpec((1, tf, D_MODEL), lambda e, f: (e, f, 0)),
        ],
        out_specs=pl.BlockSpec((cap, D_MODEL), lambda e, f: (e, 0)),
        out_shape=jax.ShapeDtypeStruct((N_EXPERTS * cap, D_MODEL), BF16),
        scratch_shapes=[pltpu.VMEM((cap, D_MODEL), F32)],
        compiler_params=_params(("arbitrary", "arbitrary")),
        name="ffn",
    )(xe, w_eg, w_eu, w_ed)


def _combine_kernel(bsel_s, x1_ref, aff_ref, taut_ref, needt_ref, beqt_ref, bselt_ref, low_ref, spread_ref, gf_ref,
                    ye_hbm, y_ref, buf_ref, xbuf_ref, sem_ref, xsem_ref, *, cap, total, nstep):
    step = pl.program_id(0)
    par = step & 1
    per_tile = N_EXPERTS * SLOT_CHUNK

    def aligned(e, tile):
        return _floor_pow2(bsel_s[e, tile], ROW_ALIGN)

    def window(e, tile, c):
        start = jnp.minimum(e * cap + aligned(e, tile) + c * SLOT_CHUNK, total - SLOT_CHUNK)
        return pl.multiple_of(start, ROW_ALIGN)

    def first_chunks(stp, buf, sub):
        tile = stp * COMBINE_TILES + sub
        return [pltpu.make_async_copy(ye_hbm.at[pl.ds(window(e, tile, 0), SLOT_CHUNK)],
                                      buf_ref.at[buf, pl.ds(sub * per_tile + e * SLOT_CHUNK, SLOT_CHUNK)],
                                      sem_ref.at[buf])
                for e in range(N_EXPERTS)]

    @pl.when(step == 0)
    def _():
        for sub in range(COMBINE_TILES):
            for cp in first_chunks(0, 0, sub):
                cp.start()

    for sub in range(COMBINE_TILES):
        for cp in first_chunks(jnp.minimum(step + 1, nstep - 1), 1 - par, sub):
            cp.start()
    for sub in range(COMBINE_TILES):
        for cp in first_chunks(step, par, sub):
            cp.wait()

    tau = taut_ref[0:1, :]
    low = low_ref[...]
    spread = spread_ref[...]
    lane = lax.broadcasted_iota(I32, (1, LANES), 1)
    wide = lax.broadcasted_iota(I32, (ROUTE_TILE, per_tile), 1)
    in_chunk = (wide & (SLOT_CHUNK - 1)).astype(F32)
    slots, affs = [], []
    for sub in range(COMBINE_TILES):
        tile = step * COMBINE_TILES + sub
        rows = slice(sub * ROUTE_TILE, (sub + 1) * ROUTE_TILE)
        aff = aff_ref[rows, :]
        k = lax.bitcast_convert_type(aff, I32)
        eq = k == tau
        eq_cum = _dot(low, _ones_where(eq, BF16)) + beqt_ref[sub]
        sel = (k > tau) | (eq & (eq_cum <= needt_ref[0:1, :]))
        slot = jnp.where(sel, _dot(low, _ones_where(sel, BF16)) + (bselt_ref[sub] - 1.0), -1.0)

        rel = jnp.zeros((1, LANES), F32)
        for e in range(N_EXPERTS):
            rel = jnp.where(lane == e, (window(e, tile, 0) - e * cap).astype(F32), rel)
        d = slot - rel
        d = jnp.where(sel & (d >= 0.0) & (d < float(SLOT_CHUNK)), d, -1.0)
        hit = _dot(d.astype(BF16), spread) == in_chunk
        onehot_gate = jnp.where(hit, _dot(aff.astype(BF16), spread), 0.0).astype(BF16)
        y_ref[rows, :] = x1_ref[rows, :] + _dot(onehot_gate, buf_ref[par, sub * per_tile:(sub + 1) * per_tile, :])
        slots.append(slot)
        affs.append(aff)

    nch = [[_cdiv_pow2(bsel_s[e, step * COMBINE_TILES + sub + 1] - aligned(e, step * COMBINE_TILES + sub), SLOT_CHUNK)
            for e in range(N_EXPERTS)] for sub in range(COMBINE_TILES)]

    @pl.when(functools.reduce(jnp.maximum, [n for per_sub in nch for n in per_sub]) > 1)
    def _():
        lane64 = lax.broadcasted_iota(I32, (ROUTE_TILE, SLOT_CHUNK), 1).astype(F32)
        for sub in range(COMBINE_TILES):
            tile = step * COMBINE_TILES + sub
            rows = slice(sub * ROUTE_TILE, (sub + 1) * ROUTE_TILE)
            for e in range(N_EXPERTS):
                slot_e = slots[sub][:, e:e + 1]

                def extra(c, carry):
                    w = window(e, tile, c)
                    cp = pltpu.make_async_copy(ye_hbm.at[pl.ds(w, SLOT_CHUNK)], xbuf_ref, xsem_ref.at[0])
                    cp.start()
                    cp.wait()
                    first = (aligned(e, tile) + c * SLOT_CHUNK).astype(F32)
                    hit = (lane64 + (w - e * cap).astype(F32) == slot_e) & (slot_e >= first)
                    y_ref[rows, :] += affs[sub][:, e:e + 1] * _dot(_ones_where(hit, BF16), xbuf_ref[...])
                    return carry

                lax.fori_loop(1, nch[sub][e], extra, 0)

    acc = y_ref[...]
    ms = jnp.mean(acc * acc, axis=-1, keepdims=True)
    y_ref[...] = acc * lax.rsqrt(ms + EPS) * gf_ref[...]

    @pl.when(step == nstep - 1)
    def _():
        for sub in range(COMBINE_TILES):
            for cp in first_chunks(step, 1 - par, sub):
                cp.wait()


def _combine(bsel_i, x1, aff, tables_t, ye, gf):
    tokens = x1.shape[0]
    cap = CAPACITY_FACTOR * tokens // N_EXPERTS
    ntile = tokens // ROUTE_TILE
    idx = np.arange(ROUTE_TILE)
    low = jnp.asarray(idx[:, None] >= idx[None, :], BF16)
    spread = jnp.asarray(np.arange(LANES)[:, None] == np.arange(N_EXPERTS * SLOT_CHUNK)[None, :] // SLOT_CHUNK, BF16)
    taut, needt, beqt, bselt = tables_t
    rows = COMBINE_TILES * ROUTE_TILE
    rowvec = pl.BlockSpec((8, LANES), lambda t, *_: (0, 0))
    tilevec = pl.BlockSpec((COMBINE_TILES, 1, LANES), lambda t, *_: (t, 0, 0))
    grid_spec = pltpu.PrefetchScalarGridSpec(
        num_scalar_prefetch=1,
        grid=(ntile // COMBINE_TILES,),
        in_specs=[
            pl.BlockSpec((rows, D_MODEL), lambda t, *_: (t, 0)),
            pl.BlockSpec((rows, LANES), lambda t, *_: (t, 0)),
            rowvec, rowvec, tilevec, tilevec,
            pl.BlockSpec(low.shape, lambda t, *_: (0, 0)),
            pl.BlockSpec(spread.shape, lambda t, *_: (0, 0)),
            pl.BlockSpec((1, D_MODEL), lambda t, *_: (0, 0)),
            pl.BlockSpec(memory_space=pl.ANY),
        ],
        out_specs=pl.BlockSpec((rows, D_MODEL), lambda t, *_: (t, 0)),
        scratch_shapes=[
            pltpu.VMEM((2, COMBINE_TILES * N_EXPERTS * SLOT_CHUNK, D_MODEL), BF16),
            pltpu.VMEM((SLOT_CHUNK, D_MODEL), BF16),
            pltpu.SemaphoreType.DMA((2,)),
            pltpu.SemaphoreType.DMA((1,)),
        ],
    )
    return pl.pallas_call(
        functools.partial(_combine_kernel, cap=cap, total=N_EXPERTS * cap, nstep=ntile // COMBINE_TILES),
        grid_spec=grid_spec,
        out_shape=jax.ShapeDtypeStruct((tokens, D_MODEL), F32),
        compiler_params=_params(("arbitrary",)),
        name="combine",
    )(bsel_i, x1, aff, taut, needt, beqt.reshape(LANES, 1, LANES), bselt.reshape(LANES, 1, LANES), low, spread,
      gf, ye)


def _encoder(x, w):
    batch, seq, _ = x.shape
    tokens = batch * seq
    xt = x.reshape(tokens, D_MODEL)
    *qkvs, vr, vi, gates = _in_proj(xt, w["g1"], w["w_in"], w["w_gate"], w["b_gate"], w["cs"], batch, seq)
    outs, lses = [], []
    for g in range(N_GROUPS):
        o, lse = _attention(qkvs[g], w["bias"][g], g)
        outs.append(o)
        lses.append(lse)
    four = _fourier(vr, vi, batch, seq)
    x1, xn, aff, afft = _mix(xt, outs, lses, four, gates, w["w_attn"], w["w_four"], w["w_out"], w["g2"],
                             w["w_router"])
    tau, need, beq_i, bsel_i, taut, needt, beqt, bselt = _route(afft)
    idx = np.arange(ROUTE_TILE)
    u = jnp.asarray(idx[:, None] <= idx[None, :], BF16)
    xe = _gather(bsel_i, afft, tau, need, beq_i, xn, u)
    ye = _ffn(xe, w["w_eg"], w["w_eu"], w["w_ed"])
    y = _combine(bsel_i, x1, aff, (taut, needt, beqt, bselt), ye, w["gf"])
    return y.reshape(batch, seq, D_MODEL)


def _prepare_weights(rel_bias, norm1_g, w_in, w_attn_br, w_four_br, w_gate, b_gate, w_out,
                     norm2_g, w_router, w_exp_gate, w_exp_up, w_exp_down, final_g):
    c, s = _dft_mats(F_CH)
    starts = [part * ATT_W + g * GROUP_W for g in range(N_GROUPS) for part in range(3)]
    w_in_grouped = jnp.concatenate([w_in[0][:, s0:s0 + GROUP_W] for s0 in starts] + [w_in[0][:, QKV_W:]], axis=1)
    w_router = jnp.pad(w_router[0], ((0, 0), (0, LANES - N_EXPERTS)))
    w_router_hi = w_router.astype(BF16)
    return {
        "g1": norm1_g[0].reshape(1, D_MODEL),
        "w_in": w_in_grouped.astype(BF16),
        "w_gate": w_gate[0].astype(BF16),
        "b_gate": b_gate[0].reshape(1, 2 * D_MODEL),
        "cs": jnp.asarray(np.concatenate([c, s], axis=1), BF16),
        "bias": [_attention_bias(rel_bias, g) for g in range(N_GROUPS)],
        "w_attn": w_attn_br[0].astype(BF16),
        "w_four": w_four_br[0].astype(BF16),
        "w_out": w_out[0].astype(BF16),
        "g2": norm2_g[0].reshape(1, D_MODEL),
        "w_router": jnp.concatenate([w_router_hi, (w_router - w_router_hi.astype(F32)).astype(BF16)], axis=1),
        "w_eg": w_exp_gate[0],
        "w_eu": w_exp_up[0],
        "w_ed": w_exp_down[0],
        "gf": final_g.reshape(1, D_MODEL),
    }


def kernel(x_prompt, x_sample, rel_bias, norm1_g, w_in, w_attn_br, w_four_br, w_gate, b_gate, w_out,
           norm2_g, w_router, w_exp_gate, w_exp_up, w_exp_down, final_g):
    w = _prepare_weights(rel_bias, norm1_g, w_in, w_attn_br, w_four_br, w_gate, b_gate, w_out,
                         norm2_g, w_router, w_exp_gate, w_exp_up, w_exp_down, final_g)
    return (_encoder(x_prompt, w), _encoder(x_sample, w))
```

```python
import functools
import math

import numpy as np
import jax
import jax.numpy as jnp
from jax import lax
from jax.experimental import pallas as pl
from jax.experimental.pallas import tpu as pltpu

D_MODEL = 1024
HEAD_DIM = 64
HEADS_PER_GROUP = 4
GROUPS = ((128, 1), (512, 4), (2048, 16))
N_GROUPS = len(GROUPS)
GROUP_W = HEADS_PER_GROUP * HEAD_DIM
ATT_W = N_GROUPS * GROUP_W
QKV_W = 3 * ATT_W
F_GROUPS = 6
F_CH = 128
F_W = F_GROUPS * F_CH
NUM_BUCKETS = 32
MAX_DISTANCE = 1024
N_EXPERTS = 16
CAPACITY_FACTOR = 2
D_FF = 2048
EPS = 1e-6
NEG = -1e30

HALF_KEYS = 64
ATT_SUB = 128
ATT_OUT_ROWS = 8192
TOKEN_TILE = 512
ROUTE_TILE = 256
FFT_STEP_ROWS = 1024
FFN_CHUNK = 256
SLOT_CHUNK = 64
COMBINE_TILES = 2
ROW_ALIGN = 16
GATHER_BLOCK = SLOT_CHUNK + ROW_ALIGN
GATHER_STACK = GATHER_BLOCK + ROW_ALIGN
GATHER_PAD = GATHER_BLOCK
LANES = 128
V7X_VMEM_LIMIT = 56 * 1024 * 1024

F32 = jnp.float32
BF16 = jnp.bfloat16
I32 = jnp.int32
U32 = jnp.uint32


def _params(sem):
    return pltpu.CompilerParams(dimension_semantics=sem, vmem_limit_bytes=V7X_VMEM_LIMIT)


def _dot(a, b):
    return jnp.dot(a, b, preferred_element_type=F32)


def _dot_nt(a, b):
    return lax.dot_general(a, b, (((1,), (1,)), ((), ())), preferred_element_type=F32)


def _floor_pow2(x, m):
    return x & ~(m - 1)


def _cdiv_pow2(x, m):
    return (x + (m - 1)) >> (m.bit_length() - 1)


def _ones_where(mask, dtype=F32):
    return jnp.where(mask, jnp.ones((), F32), jnp.zeros((), F32)).astype(dtype)


def _in_proj_kernel(x_ref, g_ref, win_ref, wg_ref, bg_ref, cs_ref, qscale_ref, qkv0_ref, qkv1_ref, qkv2_ref,
                    vr_ref, vi_ref, gates_ref, slab_ref):
    tm = x_ref.shape[0]
    half = tm // 2
    nslab = ATT_W // LANES
    cs = cs_ref[...]
    for h in range(2):
        rows = slice(h * half, (h + 1) * half)
        x = x_ref[rows, :]
        ms = jnp.mean(x * x, axis=-1, keepdims=True)
        xn = (x * lax.rsqrt(ms + EPS) * g_ref[...]).astype(BF16)
        for g, out_ref in enumerate((qkv0_ref, qkv1_ref, qkv2_ref)):
            dil = GROUPS[g][1]
            res = _dot(xn, win_ref[:, g * ATT_W:(g + 1) * ATT_W]) * qscale_ref[...]
            if dil == 1:
                out_ref[0, 0, rows, :] = res.astype(BF16)
                continue
            for j in range(nslab):
                slab_ref[j, rows, :] = res[:, j * LANES:(j + 1) * LANES]
            n = half // dil
            for r in range(dil):
                cls = [slab_ref[j, pl.ds(h * half + r, n, stride=dil), :] for j in range(nslab)]
                out_ref[0, r, h * n:(h + 1) * n, :] = jnp.concatenate(cls, axis=1).astype(BF16)
        u = _dot(xn, win_ref[:, QKV_W:QKV_W + F_W]).astype(BF16)
        for g in range(F_GROUPS):
            a = _dot(u[:, g * F_CH:(g + 1) * F_CH], cs)
            vr_ref[rows, g * F_CH:(g + 1) * F_CH] = a[:, :F_CH].astype(BF16)
            vi_ref[rows, g * F_CH:(g + 1) * F_CH] = (-a[:, F_CH:]).astype(BF16)
        z = _dot(xn, wg_ref[...]) + bg_ref[...]
        gates_ref[rows, :] = (1.0 / (1.0 + jnp.exp(-z))).astype(BF16)


def _class_major_spec(tm, dil, width, per_batch):
    return pl.BlockSpec((1, dil, tm // dil, width), lambda i: (i // per_batch, 0, i % per_batch, 0))


def _in_proj(x, g1, w_in, w_gate, b_gate, cs, batch, seq):
    t = x.shape[0]
    tm = TOKEN_TILE
    per_batch = seq // tm
    const = lambda i: (0, 0)
    row = lambda i: (i, 0)
    qscale = np.ones((1, ATT_W), np.float32)
    qscale[:, :GROUP_W] = 1.0 / math.sqrt(HEAD_DIM)
    return pl.pallas_call(
        _in_proj_kernel,
        grid=(t // tm,),
        in_specs=[
            pl.BlockSpec((tm, D_MODEL), row),
            pl.BlockSpec((1, D_MODEL), const),
            pl.BlockSpec(w_in.shape, const),
            pl.BlockSpec(w_gate.shape, const),
            pl.BlockSpec((1, 2 * D_MODEL), const),
            pl.BlockSpec(cs.shape, const),
            pl.BlockSpec((1, ATT_W), const),
        ],
        out_specs=[_class_major_spec(tm, dil, ATT_W, per_batch) for _, dil in GROUPS] + [
            pl.BlockSpec((tm, F_W), row),
            pl.BlockSpec((tm, F_W), row),
            pl.BlockSpec((tm, 2 * D_MODEL), row),
        ],
        out_shape=[jax.ShapeDtypeStruct((batch, dil, seq // dil, ATT_W), BF16) for _, dil in GROUPS] + [
            jax.ShapeDtypeStruct((t, F_W), BF16),
            jax.ShapeDtypeStruct((t, F_W), BF16),
            jax.ShapeDtypeStruct((t, 2 * D_MODEL), BF16),
        ],
        scratch_shapes=[pltpu.VMEM((ATT_W // LANES, tm, LANES), F32)],
        compiler_params=_params(("parallel",)),
        name="in_proj",
    )(x, g1, w_in, w_gate, b_gate, cs, jnp.asarray(qscale))


def _attention_kernel(q_ref, kp_ref, kc_ref, kn_ref, vp_ref, vc_ref, vn_ref, bias_ref, o_ref, lse_ref, *,
                      tq, length, dil, rc):
    i = pl.program_id(1)
    win = ATT_SUB + 2 * HALF_KEYS
    nsub = tq // ATT_SUB
    lane_head = lax.broadcasted_iota(I32, (ATT_SUB, GROUP_W), 1) // HEAD_DIM
    lane_slot = lax.broadcasted_iota(I32, (ATT_SUB, LANES), 1) // (LANES // HEADS_PER_GROUP)
    at_start = (i == 0).astype(I32)
    at_end = (i == length // tq - 1).astype(I32) * 2
    for c, sb in [(c, sb) for c in range(rc) for sb in range(nsub)]:
        r = pl.program_id(2) * rc + c
        if sb == 0:
            kwin = jnp.concatenate([kp_ref[0, c], kc_ref[0, c], kn_ref[0, c]], axis=0)
            vwin = jnp.concatenate([vp_ref[0, c], vc_ref[0, c], vn_ref[0, c]], axis=0)
        off = sb * ATT_SUB
        q = q_ref[0, c, off:off + ATT_SUB, :]
        kw = kwin[off:off + win]
        vw = vwin[off:off + win]
        variant = (at_start if sb == 0 else 0) + (at_end if sb == nsub - 1 else 0)
        qs = jnp.concatenate(
            [jnp.where(lane_head == h, q, jnp.zeros_like(q)) for h in range(HEADS_PER_GROUP)], axis=0)
        s_all = _dot_nt(qs, kw)
        ps, ms, ls = [], [], []
        for h in range(HEADS_PER_GROUP):
            s = s_all[h * ATT_SUB:(h + 1) * ATT_SUB] + bias_ref[variant, h]
            m = jnp.max(s, axis=-1, keepdims=True)
            p = jnp.exp(s - m)
            ls.append(jnp.sum(p, axis=-1, keepdims=True))
            ms.append(m)
            ps.append(p.astype(BF16))
        o_all = _dot(jnp.concatenate(ps, axis=0), vw)
        out = jnp.zeros((ATT_SUB, GROUP_W), F32)
        lse = jnp.zeros((ATT_SUB, LANES), F32)
        for h in range(HEADS_PER_GROUP):
            oh = o_all[h * ATT_SUB:(h + 1) * ATT_SUB] * (1.0 / ls[h])
            out = jnp.where(lane_head == h, oh, out)
            lse = jnp.where(lane_slot == h, ms[h] + jnp.log(ls[h]), lse)
        rows = pl.ds(off * dil + r, ATT_SUB, stride=dil) if dil > 1 else pl.ds(off, ATT_SUB)
        o_ref[rows, :] = _pack_bf16_pair(out[:, :LANES], out[:, LANES:])
        lse_ref[rows, :] = lse


def _attention(qkv, bias, g):
    batch, dil, length, _ = qkv.shape
    tq = min(length, 512, ATT_OUT_ROWS // dil)
    nb = length // tq
    hb = tq // HALF_KEYS
    last_halo = length // HALF_KEYS - 1
    rc = min(dil, max(1, 512 // tq))

    def cur(c):
        return lambda b, i, r: (b, r, i, c)

    def prev(c):
        return lambda b, i, r: (b, r, jnp.maximum(i * hb - 1, 0), c)

    def nxt(c):
        return lambda b, i, r: (b, r, jnp.minimum((i + 1) * hb, last_halo), c)

    blk = lambda rows: (1, rc, rows, GROUP_W)
    out_spec = pl.BlockSpec((tq * dil, LANES), lambda b, i, r: (b * nb + i, 0))
    return pl.pallas_call(
        functools.partial(_attention_kernel, tq=tq, length=length, dil=dil, rc=rc),
        grid=(batch, nb, dil // rc),
        in_specs=[
            pl.BlockSpec(blk(tq), cur(0)),
            pl.BlockSpec(blk(HALF_KEYS), prev(1)),
            pl.BlockSpec(blk(tq), cur(1)),
            pl.BlockSpec(blk(HALF_KEYS), nxt(1)),
            pl.BlockSpec(blk(HALF_KEYS), prev(2)),
            pl.BlockSpec(blk(tq), cur(2)),
            pl.BlockSpec(blk(HALF_KEYS), nxt(2)),
            pl.BlockSpec(bias.shape, lambda b, i, r: (0, 0, 0, 0)),
        ],
        out_specs=[out_spec, out_spec],
        out_shape=[jax.ShapeDtypeStruct((batch * dil * length, LANES), U32),
                   jax.ShapeDtypeStruct((batch * dil * length, LANES), F32)],
        compiler_params=_params(("parallel", "parallel", "arbitrary")),
        name=f"attention_g{g}",
    )(qkv, qkv, qkv, qkv, qkv, qkv, qkv, bias)


def _t5_bucket(rel):
    nb = NUM_BUCKETS // 2
    max_exact = nb // 2
    ret = (rel > 0).astype(np.int32) * nb
    n = np.abs(rel)
    large = max_exact + (np.log(np.maximum(n, max_exact) / max_exact)
                         / np.log(MAX_DISTANCE / max_exact) * (nb - max_exact)).astype(np.int32)
    large = np.minimum(large, nb - 1)
    return (ret + np.where(n < max_exact, n, large)).astype(np.int32)


def _attention_bias(rel_bias, g):
    dil = GROUPS[g][1]
    qi = np.arange(ATT_SUB)[:, None]
    kj = np.arange(ATT_SUB + 2 * HALF_KEYS)[None, :]
    delta = kj - HALF_KEYS - qi
    band = np.abs(delta) <= HALF_KEYS
    bucket = _t5_bucket(dil * delta)
    tab = rel_bias[:, g * HEADS_PER_GROUP:(g + 1) * HEADS_PER_GROUP].astype(F32)
    onehot = jnp.asarray(bucket[..., None] == np.arange(NUM_BUCKETS), F32)
    bias = jnp.einsum("qkb,bh->hqk", onehot, tab, precision=lax.Precision.HIGHEST)
    masks = [band & ((kj >= HALF_KEYS) | ((v & 1) == 0)) & ((kj < ATT_SUB + HALF_KEYS) | ((v & 2) == 0))
             for v in range(4)]
    return jnp.where(jnp.asarray(np.stack(masks))[:, None], bias[None], NEG)


def _dft_mats(n):
    k = np.arange(n)
    ang = 2.0 * np.pi * ((k[:, None] * k[None, :]) % n) / n
    return np.cos(ang), np.sin(ang)


def _pack_bf16_pair(a, b):
    hi = lax.bitcast_convert_type(a.astype(BF16).astype(F32), U32)
    lo = lax.bitcast_convert_type(b.astype(BF16).astype(F32), U32)
    return hi | lax.shift_right_logical(lo, jnp.full(lo.shape, 16, U32))


def _unpack_bf16_pair(word):
    a = lax.bitcast_convert_type(word & jnp.uint32(0xFFFF0000), F32)
    b = lax.bitcast_convert_type(lax.shift_left(word, jnp.full(word.shape, 16, U32)), F32)
    return a.astype(BF16), b.astype(BF16)


def _fft_stage1_kernel(vr_ref, vi_ref, m1_ref, twc_ref, tws_ref, z_ref, *, n1, m):
    x = jnp.concatenate([vr_ref[0], vi_ref[0]], axis=0)
    z = _dot(m1_ref[...], x)
    zr, zi = z[:n1], z[n1:]
    twc, tws = twc_ref[0], tws_ref[0]
    for j in range(m):
        c = twc[:, j:j + 1]
        s = tws[:, j:j + 1]
        a = zr[:, j * F_W:(j + 1) * F_W]
        b = zi[:, j * F_W:(j + 1) * F_W]
        z_ref[0, :, j, :] = _pack_bf16_pair(a * c + b * s, b * c - a * s)


def _fft_stage2_kernel(z_ref, m2_ref, o_ref, *, kc, scale):
    m2 = m2_ref[...]
    for j in range(kc):
        x = jnp.concatenate(_unpack_bf16_pair(z_ref[0, j]), axis=0)
        y = _dot(m2, x) * scale
        o_ref[0, :, j, :] = _pack_bf16_pair(y[:, :F_W // 2], y[:, F_W // 2:])


def _fourier(vr, vi, batch, seq):
    n2 = LANES
    n1 = seq // n2
    m = min(n2, FFT_STEP_ROWS // n1)
    c1, s1 = _dft_mats(n1)
    m1 = jnp.asarray(np.block([[c1, s1], [-s1, c1]]), BF16)
    c2, s2 = _dft_mats(n2)
    m2 = jnp.asarray(np.concatenate([c2, s2], axis=1), BF16)
    k1 = np.arange(n1)[:, None]
    sv = np.arange(n2)[None, :]
    ang = 2.0 * np.pi * ((k1 * sv) % seq) / seq
    to_blocks = lambda a: jnp.asarray(a.reshape(n1, n2 // m, m).transpose(1, 0, 2), F32)
    twc, tws = to_blocks(np.cos(ang)), to_blocks(np.sin(ang))

    v3 = lambda a: a.reshape(batch, n1, n2 * F_W)
    blk = pl.BlockSpec((1, n1, m * F_W), lambda b, j: (b, 0, j))
    tmap = lambda b, j: (j, 0, 0)
    z = pl.pallas_call(
        functools.partial(_fft_stage1_kernel, n1=n1, m=m),
        grid=(batch, n2 // m),
        in_specs=[
            blk,
            blk,
            pl.BlockSpec(m1.shape, lambda b, j: (0, 0)),
            pl.BlockSpec((1, n1, m), tmap),
            pl.BlockSpec((1, n1, m), tmap),
        ],
        out_specs=pl.BlockSpec((1, n1, m, F_W), lambda b, j: (b, 0, j, 0)),
        out_shape=jax.ShapeDtypeStruct((batch, n1, n2, F_W), U32),
        compiler_params=_params(("parallel", "parallel")),
        name="fft_stage1",
    )(v3(vr), v3(vi), m1, twc, tws)

    kc = min(n1, FFT_STEP_ROWS // n2 * 2)
    out = pl.pallas_call(
        functools.partial(_fft_stage2_kernel, kc=kc, scale=1.0 / math.sqrt(seq * F_CH)),
        grid=(batch, n1 // kc),
        in_specs=[
            pl.BlockSpec((1, kc, n2, F_W), lambda b, j: (b, j, 0, 0)),
            pl.BlockSpec(m2.shape, lambda b, j: (0, 0)),
        ],
        out_specs=pl.BlockSpec((1, n2, kc, F_W // 2), lambda b, j: (b, 0, j, 0)),
        out_shape=jax.ShapeDtypeStruct((batch, n2, n1, F_W // 2), U32),
        compiler_params=_params(("parallel", "parallel")),
        name="fft_stage2",
    )(z, m2)
    return out.reshape(batch * seq, F_W // 2)


def _mix_kernel(x_ref, o0_ref, o1_ref, o2_ref, l0_ref, l1_ref, l2_ref, four_ref, gates_ref,
                wa_ref, wf_ref, wo_ref, g2_ref, wr_ref, hs_ref, x1_ref, xn_ref, aff_ref, afft_ref):
    tm = x_ref.shape[0]
    half = tm // 2
    for rows in (slice(0, half), slice(half, tm)):
        f_br = _dot(jnp.concatenate(_unpack_bf16_pair(four_ref[rows, :]), axis=1), wf_ref[...])
        l0, l1, l2 = l0_ref[rows, :], l1_ref[rows, :], l2_ref[rows, :]
        mx = jnp.maximum(jnp.maximum(l0, l1), l2)
        e0, e1, e2 = jnp.exp(l0 - mx), jnp.exp(l1 - mx), jnp.exp(l2 - mx)
        inv = 1.0 / (e0 + e1 + e2)
        w0, w1, w2 = (_dot((e * inv).astype(BF16), hs_ref[...]) for e in (e0, e1, e2))
        o0, o1, o2 = (jnp.concatenate(_unpack_bf16_pair(ref[rows, :]), axis=1) for ref in (o0_ref, o1_ref, o2_ref))
        att = w0 * o0 + w1 * o1 + w2 * o2
        a_br = _dot(att.astype(BF16), wa_ref[...])
        mix = gates_ref[rows, :D_MODEL] * a_br + gates_ref[rows, D_MODEL:] * f_br
        x1 = x_ref[rows, :] + _dot(mix.astype(BF16), wo_ref[...])
        x1_ref[rows, :] = x1
        ms = jnp.mean(x1 * x1, axis=-1, keepdims=True)
        xn = x1 * lax.rsqrt(ms + EPS) * g2_ref[...]
        xn_ref[rows, :] = xn.astype(BF16)
        xh = xn.astype(BF16)
        xl = (xn - xh.astype(F32)).astype(BF16)
        both = _dot(xh, wr_ref[...])
        logits = both[:, :LANES] + (both[:, LANES:] + _dot(xl, wr_ref[:, :LANES]))
        lane = lax.broadcasted_iota(I32, logits.shape, 1)
        logits = jnp.where(lane < N_EXPERTS, logits, NEG)
        p = jnp.exp(logits - jnp.max(logits, axis=-1, keepdims=True))
        aff = p * (1.0 / jnp.sum(p, axis=-1, keepdims=True))
        aff_ref[rows, :] = aff
        afft_ref[:, rows] = aff.T[:N_EXPERTS]


def _mix(x, os_, ls_, four, gates, w_attn, w_four, w_out, g2, w_router):
    t = x.shape[0]
    tm = TOKEN_TILE
    const = lambda i: (0, 0)
    row = lambda i: (i, 0)
    rows = lambda w: pl.BlockSpec((tm, w), row)
    full = lambda a: pl.BlockSpec(a.shape, const)
    slot = LANES // HEADS_PER_GROUP
    head_spread = jnp.asarray(np.arange(LANES)[:, None] == slot * (np.arange(GROUP_W)[None, :] // HEAD_DIM), BF16)
    return pl.pallas_call(
        _mix_kernel,
        grid=(t // tm,),
        in_specs=[rows(D_MODEL)] + [rows(LANES)] * 6 + [rows(F_W // 2), rows(2 * D_MODEL),
                  full(w_attn), full(w_four), full(w_out), full(g2), full(w_router), full(head_spread)],
        out_specs=[rows(D_MODEL), rows(D_MODEL), rows(LANES), pl.BlockSpec((N_EXPERTS, tm), lambda i: (0, i))],
        out_shape=[
            jax.ShapeDtypeStruct((t, D_MODEL), F32),
            jax.ShapeDtypeStruct((t, D_MODEL), BF16),
            jax.ShapeDtypeStruct((t, LANES), F32),
            jax.ShapeDtypeStruct((N_EXPERTS, t), F32),
        ],
        compiler_params=_params(("parallel",)),
        name="mix",
    )(x, *os_, *ls_, four, gates, w_attn, w_four, w_out, g2, w_router, head_spread)


def _route_kernel(afft_ref, su_ref, u_ref, tau_ref, need_ref, beq_ref, bsel_ref,
                  taut_ref, needt_ref, beqt_ref, bselt_ref, *, tokens):
    cap = CAPACITY_FACTOR * tokens // N_EXPERTS
    ntile = tokens // ROUTE_TILE
    shape = (N_EXPERTS, LANES)
    lane = lax.broadcasted_iota(I32, shape, 1)

    def keys(start, width):
        return lax.bitcast_convert_type(afft_ref[:, pl.ds(pl.multiple_of(start, LANES), width)], I32)

    span = min(tokens, 16 * LANES)

    def count(pred):
        def body(c, acc):
            hits = _ones_where(pred(keys(c * span, span)))
            for j in range(span // LANES):
                acc = acc + hits[:, j * LANES:(j + 1) * LANES]
            return acc
        acc = lax.fori_loop(0, tokens // span, body, jnp.zeros(shape, F32), unroll=True)
        return jnp.sum(acc, axis=1, keepdims=True)

    def bit_body(i, prefix):
        cand = prefix | lax.shift_left(jnp.ones(shape, I32), jnp.full(shape, 30 - i, I32))
        tot = count(lambda k: k >= cand[:, :1])
        return jnp.where(tot >= cap, cand, prefix)

    tau = lax.fori_loop(0, 31, bit_body, jnp.zeros(shape, I32))
    tau_col = tau[:, :1]
    n_gt = count(lambda k: k > tau_col)
    need = cap - n_gt

    def prefix_over_tiles(tab):
        return _dot(tab.astype(BF16), su_ref[...])

    def at_lane(tab, c):
        return jnp.sum(jnp.where(lane == c, tab, 0.0), axis=1, keepdims=True)

    def eq_body(c, tab):
        k = keys(c * ROUTE_TILE, ROUTE_TILE)
        cnt = jnp.sum(_ones_where(k == tau_col), axis=1, keepdims=True)
        return jnp.where(lane == c, cnt, tab)

    unroll = 8 if ntile % 8 == 0 else 1
    base_eq = prefix_over_tiles(lax.fori_loop(0, ntile, eq_body, jnp.zeros(shape, F32), unroll=unroll))

    def sel_body(c, tab):
        k = keys(c * ROUTE_TILE, ROUTE_TILE)
        eq = k == tau_col
        eq_cum = _dot(_ones_where(eq, BF16), u_ref[...]) + at_lane(base_eq, c)
        sel = (k > tau_col) | (eq & (eq_cum <= need))
        cnt = jnp.sum(_ones_where(sel), axis=1, keepdims=True)
        return jnp.where(lane == c, cnt, tab)

    base_sel = prefix_over_tiles(lax.fori_loop(0, ntile, sel_body, jnp.zeros(shape, F32), unroll=unroll))

    def transposed(val):
        return jnp.concatenate([val, jnp.zeros((LANES - N_EXPERTS, LANES), val.dtype)], axis=0).T

    tau_ref[...] = tau
    taut_ref[...] = transposed(tau)
    for val, ref, ref_t in ((jnp.broadcast_to(need, shape), need_ref, needt_ref),
                            (base_eq, beq_ref, beqt_ref), (base_sel, bsel_ref, bselt_ref)):
        ref[...] = val.astype(I32)
        ref_t[...] = transposed(val)


def _route(afft):
    tokens = afft.shape[1]
    idx = np.arange(LANES)
    su = jnp.asarray(idx[:, None] < idx[None, :], BF16)
    idx = np.arange(ROUTE_TILE)
    u = jnp.asarray(idx[:, None] <= idx[None, :], BF16)
    full = lambda a: pl.BlockSpec(a.shape, lambda i: (0,) * a.ndim)
    small = pl.BlockSpec((N_EXPERTS, LANES), lambda i: (0, 0))
    smallt = pl.BlockSpec((LANES, LANES), lambda i: (0, 0))
    return pl.pallas_call(
        functools.partial(_route_kernel, tokens=tokens),
        grid=(1,),
        in_specs=[full(afft), full(su), full(u)],
        out_specs=[small] * 4 + [smallt] * 4,
        out_shape=[jax.ShapeDtypeStruct((N_EXPERTS, LANES), I32)] * 4
        + [jax.ShapeDtypeStruct((LANES, LANES), I32)] + [jax.ShapeDtypeStruct((LANES, LANES), F32)] * 3,
        compiler_params=_params(("arbitrary",)),
        name="route",
    )(afft, su, u)


def _gather_kernel(bsel_s, afft_ref, tau_ref, need_ref, beq_ref, x_ref, u_ref, xe_hbm,
                   stage_ref, tail_ref, xbuf_ref, sem_ref, xsem_ref, *, ntile, cap):
    t = pl.program_id(0)
    par = t & 1

    def aligned(e, tile):
        return pl.multiple_of(_floor_pow2(bsel_s[e, tile], ROW_ALIGN), ROW_ALIGN)

    def write(e, tile, buf, first_row=None):
        first_row = aligned(e, tile) if first_row is None else first_row
        return pltpu.make_async_copy(stage_ref.at[buf, e], xe_hbm.at[e, pl.ds(first_row, GATHER_BLOCK)],
                                     sem_ref.at[buf])

    @pl.when(t == 0)
    def _():
        tail_ref[...] = jnp.zeros_like(tail_ref)
        stage_ref[1] = jnp.zeros(stage_ref.shape[1:], BF16)
        for e in range(N_EXPERTS):
            write(e, 0, 1, first_row=cap).start()

    k = lax.bitcast_convert_type(afft_ref[...], I32)
    tau = tau_ref[:, :1]
    lane = lax.broadcasted_iota(I32, (N_EXPERTS, LANES), 1)
    beq = jnp.sum(jnp.where(lane == t, beq_ref[...].astype(F32), 0.0), axis=1, keepdims=True)
    eq = k == tau
    eq_cum = _dot(_ones_where(eq, BF16), u_ref[...]) + beq
    sel = (k > tau) | (eq & (eq_cum <= need_ref[:, :1].astype(F32)))
    rank = jnp.where(sel, _dot(_ones_where(sel, BF16), u_ref[...]) - 1.0, -1e4)

    row = lax.broadcasted_iota(I32, (GATHER_STACK, ROUTE_TILE), 0)
    in_block = row < GATHER_BLOCK
    row_f = row.astype(F32)
    offs, shifts, pieces = [], [], []
    for e in range(N_EXPERTS):
        off = (bsel_s[e, t] - aligned(e, t)).astype(F32)
        shift = _floor_pow2(bsel_s[e, t + 1], ROW_ALIGN) - aligned(e, t)
        target = jnp.where(in_block, row_f, row_f - float(GATHER_BLOCK) + shift.astype(F32))
        pieces.append(_ones_where(rank[e:e + 1, :] + off == target, BF16))
        offs.append(off)
        shifts.append(shift)
    res = _dot(jnp.concatenate(pieces, axis=0), x_ref[...])
    for e in range(N_EXPERTS):
        base = e * GATHER_STACK
        old = tail_ref[e]
        stage_ref[par, e, 0:ROW_ALIGN, :] = (res[base:base + ROW_ALIGN] + old).astype(BF16)
        stage_ref[par, e, ROW_ALIGN:GATHER_BLOCK, :] = res[base + ROW_ALIGN:base + GATHER_BLOCK].astype(BF16)
        tail_ref[e] = res[base + GATHER_BLOCK:base + GATHER_STACK] + jnp.where(shifts[e] == 0, old, 0.0)
    for e in range(N_EXPERTS):
        write(e, jnp.maximum(t - 1, 0), 1 - par).wait()
    for e in range(N_EXPERTS):
        write(e, t, par).start()

    extra = [_cdiv_pow2(jnp.maximum(bsel_s[e, t + 1] - aligned(e, t) - GATHER_BLOCK, 0), SLOT_CHUNK)
             for e in range(N_EXPERTS)]

    @pl.when(functools.reduce(jnp.maximum, extra) > 0)
    def _():
        row64 = lax.broadcasted_iota(I32, (SLOT_CHUNK, ROUTE_TILE), 0).astype(F32)
        for e in range(N_EXPERTS):
            def chunk(c, carry):
                first = GATHER_BLOCK + c * SLOT_CHUNK
                onehot = _ones_where(rank[e:e + 1, :] + offs[e] == row64 + first.astype(F32), BF16)
                xbuf_ref[...] = _dot(onehot, x_ref[...]).astype(BF16)
                dst = pl.multiple_of(aligned(e, t) + first, ROW_ALIGN)
                cp = pltpu.make_async_copy(xbuf_ref, xe_hbm.at[e, pl.ds(dst, SLOT_CHUNK)], xsem_ref.at[0])
                cp.start()
                cp.wait()
                return carry

            lax.fori_loop(0, extra[e], chunk, 0)

    @pl.when(t == ntile - 1)
    def _():
        for e in range(N_EXPERTS):
            write(e, t, par).wait()


def _gather(bsel_i, afft, tau, need, beq_i, xn, u):
    tokens = xn.shape[0]
    cap = CAPACITY_FACTOR * tokens // N_EXPERTS
    ntile = tokens // ROUTE_TILE
    table = pl.BlockSpec((N_EXPERTS, LANES), lambda t, *_: (0, 0))
    grid_spec = pltpu.PrefetchScalarGridSpec(
        num_scalar_prefetch=1,
        grid=(ntile,),
        in_specs=[
            pl.BlockSpec((N_EXPERTS, ROUTE_TILE), lambda t, *_: (0, t)),
            table, table, table,
            pl.BlockSpec((ROUTE_TILE, D_MODEL), lambda t, *_: (t, 0)),
            pl.BlockSpec(u.shape, lambda t, *_: (0, 0)),
        ],
        out_specs=pl.BlockSpec(memory_space=pl.ANY),
        scratch_shapes=[
            pltpu.VMEM((2, N_EXPERTS, GATHER_BLOCK, D_MODEL), BF16),
            pltpu.VMEM((N_EXPERTS, ROW_ALIGN, D_MODEL), F32),
            pltpu.VMEM((SLOT_CHUNK, D_MODEL), BF16),
            pltpu.SemaphoreType.DMA((2,)),
            pltpu.SemaphoreType.DMA((1,)),
        ],
    )
    return pl.pallas_call(
        functools.partial(_gather_kernel, ntile=ntile, cap=cap),
        grid_spec=grid_spec,
        out_shape=jax.ShapeDtypeStruct((N_EXPERTS, cap + GATHER_PAD, D_MODEL), BF16),
        compiler_params=_params(("arbitrary",)),
        name="gather",
    )(bsel_i, afft, tau, need, beq_i, xn, u)


def _ffn_kernel(xe_ref, wg_ref, wu_ref, wd_ref, ye_ref, acc_ref, *, cap, nf, tm):
    f = pl.program_id(1)
    tf = wg_ref.shape[2]
    chunks = [slice(j * FFN_CHUNK, (j + 1) * FFN_CHUNK) for j in range(tf // FFN_CHUNK)]
    cast = {}

    def weight(name, ref, j):
        if (name, j) not in cast:
            cast[name, j] = (ref[0, chunks[j], :] if name == "d" else ref[0, :, chunks[j]]).astype(BF16)
        return cast[name, j]

    @pl.when(f == 0)
    def _():
        acc_ref[...] = jnp.zeros_like(acc_ref)

    for i in range(cap // tm):
        r = slice(i * tm, (i + 1) * tm)
        x = xe_ref[0, r, :]
        y = None
        for j in range(len(chunks)):
            hg = _dot(x, weight("g", wg_ref, j))
            hu = _dot(x, weight("u", wu_ref, j))
            h = (hg * (1.0 / (1.0 + jnp.exp(-hg))) * hu).astype(BF16)
            part = _dot(h, weight("d", wd_ref, j))
            y = part if y is None else y + part
        acc_ref[r, :] += y

    @pl.when(f == nf - 1)
    def _():
        ye_ref[...] = acc_ref[...].astype(BF16)


def _ffn(xe, w_eg, w_eu, w_ed):
    cap = xe.shape[1] - GATHER_PAD
    tf = 512
    nf = D_FF // tf
    tm = min(cap, 1024)
    return pl.pallas_call(
        functools.partial(_ffn_kernel, cap=cap, nf=nf, tm=tm),
        grid=(N_EXPERTS, nf),
        in_specs=[
            pl.BlockSpec((1, cap, D_MODEL), lambda e, f: (e, 0, 0)),
            pl.BlockSpec((1, D_MODEL, tf), lambda e, f: (e, 0, f)),
            pl.BlockSpec((1, D_MODEL, tf), lambda e, f: (e, 0, f)),
            pl.BlockSpec((1, tf, D_MODEL), lambda e, f: (e, f, 0)),
        ],
        out_specs=pl.BlockSpec((cap, D_MODEL), lambda e, f: (e, 0)),
        out_shape=jax.ShapeDtypeStruct((N_EXPERTS * cap, D_MODEL), BF16),
        scratch_shapes=[pltpu.VMEM((cap, D_MODEL), F32)],
        compiler_params=_params(("arbitrary", "arbitrary")),
        name="ffn",
    )(xe, w_eg, w_eu, w_ed)


def _combine_kernel(bsel_s, x1_ref, aff_ref, taut_ref, needt_ref, beqt_ref, bselt_ref, low_ref, spread_ref, gf_ref,
                    ye_hbm, y_ref, buf_ref, xbuf_ref, sem_ref, xsem_ref, *, cap, total, nstep):
    step = pl.program_id(0)
    par = step & 1
    per_tile = N_EXPERTS * SLOT_CHUNK

    def aligned(e, tile):
        return _floor_pow2(bsel_s[e, tile], ROW_ALIGN)

    def window(e, tile, c):
        start = jnp.minimum(e * cap + aligned(e, tile) + c * SLOT_CHUNK, total - SLOT_CHUNK)
        return pl.multiple_of(start, ROW_ALIGN)

    def first_chunks(stp, buf, sub):
        tile = stp * COMBINE_TILES + sub
        return [pltpu.make_async_copy(ye_hbm.at[pl.ds(window(e, tile, 0), SLOT_CHUNK)],
                                      buf_ref.at[buf, pl.ds(sub * per_tile + e * SLOT_CHUNK, SLOT_CHUNK)],
                                      sem_ref.at[buf])
                for e in range(N_EXPERTS)]

    @pl.when(step == 0)
    def _():
        for sub in range(COMBINE_TILES):
            for cp in first_chunks(0, 0, sub):
                cp.start()

    for sub in range(COMBINE_TILES):
        for cp in first_chunks(jnp.minimum(step + 1, nstep - 1), 1 - par, sub):
            cp.start()
    for sub in range(COMBINE_TILES):
        for cp in first_chunks(step, par, sub):
            cp.wait()

    tau = taut_ref[0:1, :]
    low = low_ref[...]
    spread = spread_ref[...]
    lane = lax.broadcasted_iota(I32, (1, LANES), 1)
    wide = lax.broadcasted_iota(I32, (ROUTE_TILE, per_tile), 1)
    in_chunk = (wide & (SLOT_CHUNK - 1)).astype(F32)
    slots, affs = [], []
    for sub in range(COMBINE_TILES):
        tile = step * COMBINE_TILES + sub
        rows = slice(sub * ROUTE_TILE, (sub + 1) * ROUTE_TILE)
        aff = aff_ref[rows, :]
        k = lax.bitcast_convert_type(aff, I32)
        eq = k == tau
        eq_cum = _dot(low, _ones_where(eq, BF16)) + beqt_ref[sub]
        sel = (k > tau) | (eq & (eq_cum <= needt_ref[0:1, :]))
        slot = jnp.where(sel, _dot(low, _ones_where(sel, BF16)) + (bselt_ref[sub] - 1.0), -1.0)

        rel = jnp.zeros((1, LANES), F32)
        for e in range(N_EXPERTS):
            rel = jnp.where(lane == e, (window(e, tile, 0) - e * cap).astype(F32), rel)
        d = slot - rel
        d = jnp.where(sel & (d >= 0.0) & (d < float(SLOT_CHUNK)), d, -1.0)
        hit = _dot(d.astype(BF16), spread) == in_chunk
        onehot_gate = jnp.where(hit, _dot(aff.astype(BF16), spread), 0.0).astype(BF16)
        y_ref[rows, :] = x1_ref[rows, :] + _dot(onehot_gate, buf_ref[par, sub * per_tile:(sub + 1) * per_tile, :])
        slots.append(slot)
        affs.append(aff)

    nch = [[_cdiv_pow2(bsel_s[e, step * COMBINE_TILES + sub + 1] - aligned(e, step * COMBINE_TILES + sub), SLOT_CHUNK)
            for e in range(N_EXPERTS)] for sub in range(COMBINE_TILES)]

    @pl.when(functools.reduce(jnp.maximum, [n for per_sub in nch for n in per_sub]) > 1)
    def _():
        lane64 = lax.broadcasted_iota(I32, (ROUTE_TILE, SLOT_CHUNK), 1).astype(F32)
        for sub in range(COMBINE_TILES):
            tile = step * COMBINE_TILES + sub
            rows = slice(sub * ROUTE_TILE, (sub + 1) * ROUTE_TILE)
            for e in range(N_EXPERTS):
                slot_e = slots[sub][:, e:e + 1]

                def extra(c, carry):
                    w = window(e, tile, c)
                    cp = pltpu.make_async_copy(ye_hbm.at[pl.ds(w, SLOT_CHUNK)], xbuf_ref, xsem_ref.at[0])
                    cp.start()
                    cp.wait()
                    first = (aligned(e, tile) + c * SLOT_CHUNK).astype(F32)
                    hit = (lane64 + (w - e * cap).astype(F32) == slot_e) & (slot_e >= first)
                    y_ref[rows, :] += affs[sub][:, e:e + 1] * _dot(_ones_where(hit, BF16), xbuf_ref[...])
                    return carry

                lax.fori_loop(1, nch[sub][e], extra, 0)

    acc = y_ref[...]
    ms = jnp.mean(acc * acc, axis=-1, keepdims=True)
    y_ref[...] = acc * lax.rsqrt(ms + EPS) * gf_ref[...]

    @pl.when(step == nstep - 1)
    def _():
        for sub in range(COMBINE_TILES):
            for cp in first_chunks(step, 1 - par, sub):
                cp.wait()


def _combine(bsel_i, x1, aff, tables_t, ye, gf):
    tokens = x1.shape[0]
    cap = CAPACITY_FACTOR * tokens // N_EXPERTS
    ntile = tokens // ROUTE_TILE
    idx = np.arange(ROUTE_TILE)
    low = jnp.asarray(idx[:, None] >= idx[None, :], BF16)
    spread = jnp.asarray(np.arange(LANES)[:, None] == np.arange(N_EXPERTS * SLOT_CHUNK)[None, :] // SLOT_CHUNK, BF16)
    taut, needt, beqt, bselt = tables_t
    rows = COMBINE_TILES * ROUTE_TILE
    rowvec = pl.BlockSpec((8, LANES), lambda t, *_: (0, 0))
    tilevec = pl.BlockSpec((COMBINE_TILES, 1, LANES), lambda t, *_: (t, 0, 0))
    grid_spec = pltpu.PrefetchScalarGridSpec(
        num_scalar_prefetch=1,
        grid=(ntile // COMBINE_TILES,),
        in_specs=[
            pl.BlockSpec((rows, D_MODEL), lambda t, *_: (t, 0)),
            pl.BlockSpec((rows, LANES), lambda t, *_: (t, 0)),
            rowvec, rowvec, tilevec, tilevec,
            pl.BlockSpec(low.shape, lambda t, *_: (0, 0)),
            pl.BlockSpec(spread.shape, lambda t, *_: (0, 0)),
            pl.BlockSpec((1, D_MODEL), lambda t, *_: (0, 0)),
            pl.BlockSpec(memory_space=pl.ANY),
        ],
        out_specs=pl.BlockSpec((rows, D_MODEL), lambda t, *_: (t, 0)),
        scratch_shapes=[
            pltpu.VMEM((2, COMBINE_TILES * N_EXPERTS * SLOT_CHUNK, D_MODEL), BF16),
            pltpu.VMEM((SLOT_CHUNK, D_MODEL), BF16),
            pltpu.SemaphoreType.DMA((2,)),
            pltpu.SemaphoreType.DMA((1,)),
        ],
    )
    return pl.pallas_call(
        functools.partial(_combine_kernel, cap=cap, total=N_EXPERTS * cap, nstep=ntile // COMBINE_TILES),
        grid_spec=grid_spec,
        out_shape=jax.ShapeDtypeStruct((tokens, D_MODEL), F32),
        compiler_params=_params(("arbitrary",)),
        name="combine",
    )(bsel_i, x1, aff, taut, needt, beqt.reshape(LANES, 1, LANES), bselt.reshape(LANES, 1, LANES), low, spread,
      gf, ye)


def _encoder(x, w):
    batch, seq, _ = x.shape
    tokens = batch * seq
    xt = x.reshape(tokens, D_MODEL)
    *qkvs, vr, vi, gates = _in_proj(xt, w["g1"], w["w_in"], w["w_gate"], w["b_gate"], w["cs"], batch, seq)
    outs, lses = [], []
    for g in range(N_GROUPS):
        o, lse = _attention(qkvs[g], w["bias"][g], g)
        outs.append(o)
        lses.append(lse)
    four = _fourier(vr, vi, batch, seq)
    x1, xn, aff, afft = _mix(xt, outs, lses, four, gates, w["w_attn"], w["w_four"], w["w_out"], w["g2"],
                             w["w_router"])
    tau, need, beq_i, bsel_i, taut, needt, beqt, bselt = _route(afft)
    idx = np.arange(ROUTE_TILE)
    u = jnp.asarray(idx[:, None] <= idx[None, :], BF16)
    xe = _gather(bsel_i, afft, tau, need, beq_i, xn, u)
    ye = _ffn(xe, w["w_eg"], w["w_eu"], w["w_ed"])
    y = _combine(bsel_i, x1, aff, (taut, needt, beqt, bselt), ye, w["gf"])
    return y.reshape(batch, seq, D_MODEL)


def _prepare_weights(rel_bias, norm1_g, w_in, w_attn_br, w_four_br, w_gate, b_gate, w_out,
                     norm2_g, w_router, w_exp_gate, w_exp_up, w_exp_down, final_g):
    c, s = _dft_mats(F_CH)
    qkv_cols = w_in[0][:, :QKV_W].reshape(D_MODEL, 3, N_GROUPS, GROUP_W).transpose(0, 2, 1, 3)
    w_in_grouped = jnp.concatenate([qkv_cols.reshape(D_MODEL, QKV_W), w_in[0][:, QKV_W:]], axis=1)
    w_router = jnp.pad(w_router[0], ((0, 0), (0, LANES - N_EXPERTS)))
    w_router_hi = w_router.astype(BF16)
    return {
        "g1": norm1_g[0].reshape(1, D_MODEL),
        "w_in": w_in_grouped.astype(BF16),
        "w_gate": w_gate[0].astype(BF16),
        "b_gate": b_gate[0].reshape(1, 2 * D_MODEL),
        "cs": jnp.asarray(np.concatenate([c, s], axis=1), BF16),
        "bias": [_attention_bias(rel_bias, g) for g in range(N_GROUPS)],
        "w_attn": w_attn_br[0].astype(BF16),
        "w_four": w_four_br[0].astype(BF16),
        "w_out": w_out[0].astype(BF16),
        "g2": norm2_g[0].reshape(1, D_MODEL),
        "w_router": jnp.concatenate([w_router_hi, (w_router - w_router_hi.astype(F32)).astype(BF16)], axis=1),
        "w_eg": w_exp_gate[0],
        "w_eu": w_exp_up[0],
        "w_ed": w_exp_down[0],
        "gf": final_g.reshape(1, D_MODEL),
    }


def kernel(x_prompt, x_sample, rel_bias, norm1_g, w_in, w_attn_br, w_four_br, w_gate, b_gate, w_out,
           norm2_g, w_router, w_exp_gate, w_exp_up, w_exp_down, final_g):
    w = _prepare_weights(rel_bias, norm1_g, w_in, w_attn_br, w_four_br, w_gate, b_gate, w_out,
                         norm2_g, w_router, w_exp_gate, w_exp_up, w_exp_down, final_g)
    return (_encoder(x_prompt, w), _encoder(x_sample, w))
```

```python
import functools
import math

import numpy as np
import jax
import jax.numpy as jnp
from jax import lax
from jax.experimental import pallas as pl
from jax.experimental.pallas import tpu as pltpu

D_MODEL = 1024
HEAD_DIM = 64
HEADS_PER_GROUP = 4
GROUPS = ((128, 1), (512, 4), (2048, 16))
N_GROUPS = len(GROUPS)
GROUP_W = HEADS_PER_GROUP * HEAD_DIM
ATT_W = N_GROUPS * GROUP_W
QKV_W = 3 * ATT_W
F_GROUPS = 6
F_CH = 128
F_W = F_GROUPS * F_CH
NUM_BUCKETS = 32
MAX_DISTANCE = 1024
N_EXPERTS = 16
CAPACITY_FACTOR = 2
D_FF = 2048
EPS = 1e-6
NEG = -1e30

HALF_KEYS = 64
ATT_SUB = 128
ATT_OUT_ROWS = 8192
TOKEN_TILE = 512
ROUTE_TILE = 256
FFT_STEP_ROWS = 1024
FFN_CHUNK = 256
SLOT_CHUNK = 64
COMBINE_TILES = 2
ROW_ALIGN = 16
GATHER_BLOCK = SLOT_CHUNK + ROW_ALIGN
GATHER_STACK = GATHER_BLOCK + ROW_ALIGN
GATHER_PAD = GATHER_BLOCK
LANES = 128
V7X_VMEM_LIMIT = 56 * 1024 * 1024

F32 = jnp.float32
BF16 = jnp.bfloat16
I32 = jnp.int32
U32 = jnp.uint32


def _params(sem):
    return pltpu.CompilerParams(dimension_semantics=sem, vmem_limit_bytes=V7X_VMEM_LIMIT)


def _dot(a, b):
    return jnp.dot(a, b, preferred_element_type=F32)


def _dot_nt(a, b):
    return lax.dot_general(a, b, (((1,), (1,)), ((), ())), preferred_element_type=F32)


def _floor_pow2(x, m):
    return x & ~(m - 1)


def _cdiv_pow2(x, m):
    return (x + (m - 1)) >> (m.bit_length() - 1)


def _ones_where(mask, dtype=F32):
    return jnp.where(mask, jnp.ones((), F32), jnp.zeros((), F32)).astype(dtype)


def _in_proj_kernel(x_ref, g_ref, win_ref, wg_ref, bg_ref, cs_ref, qscale_ref, qkv0_ref, qkv1_ref, qkv2_ref,
                    vr_ref, vi_ref, gates_ref, slab_ref):
    tm = x_ref.shape[0]
    half = tm // 2
    nslab = ATT_W // LANES
    cs = cs_ref[...]
    for h in range(2):
        rows = slice(h * half, (h + 1) * half)
        x = x_ref[rows, :]
        ms = jnp.mean(x * x, axis=-1, keepdims=True)
        xn = (x * lax.rsqrt(ms + EPS) * g_ref[...]).astype(BF16)
        for g, out_ref in enumerate((qkv0_ref, qkv1_ref, qkv2_ref)):
            dil = GROUPS[g][1]
            res = _dot(xn, win_ref[:, g * ATT_W:(g + 1) * ATT_W]) * qscale_ref[...]
            if dil == 1:
                out_ref[0, 0, rows, :] = res.astype(BF16)
                continue
            for j in range(nslab):
                slab_ref[j, rows, :] = res[:, j * LANES:(j + 1) * LANES]
            n = half // dil
            for r in range(dil):
                cls = [slab_ref[j, pl.ds(h * half + r, n, stride=dil), :] for j in range(nslab)]
                out_ref[0, r, h * n:(h + 1) * n, :] = jnp.concatenate(cls, axis=1).astype(BF16)
        u = _dot(xn, win_ref[:, QKV_W:QKV_W + F_W]).astype(BF16)
        for g in range(F_GROUPS):
            a = _dot(u[:, g * F_CH:(g + 1) * F_CH], cs)
            vr_ref[rows, g * F_CH:(g + 1) * F_CH] = a[:, :F_CH].astype(BF16)
            vi_ref[rows, g * F_CH:(g + 1) * F_CH] = (-a[:, F_CH:]).astype(BF16)
        z = _dot(xn, wg_ref[...]) + bg_ref[...]
        gates_ref[rows, :] = (1.0 / (1.0 + jnp.exp(-z))).astype(BF16)


def _class_major_spec(tm, dil, width, per_batch):
    return pl.BlockSpec((1, dil, tm // dil, width), lambda i: (i // per_batch, 0, i % per_batch, 0))


def _in_proj(x, g1, w_in, w_gate, b_gate, cs, batch, seq):
    t = x.shape[0]
    tm = TOKEN_TILE
    per_batch = seq // tm
    const = lambda i: (0, 0)
    row = lambda i: (i, 0)
    qscale = np.ones((1, ATT_W), np.float32)
    qscale[:, :GROUP_W] = 1.0 / math.sqrt(HEAD_DIM)
    return pl.pallas_call(
        _in_proj_kernel,
        grid=(t // tm,),
        in_specs=[
            pl.BlockSpec((tm, D_MODEL), row),
            pl.BlockSpec((1, D_MODEL), const),
            pl.BlockSpec(w_in.shape, const),
            pl.BlockSpec(w_gate.shape, const),
            pl.BlockSpec((1, 2 * D_MODEL), const),
            pl.BlockSpec(cs.shape, const),
            pl.BlockSpec((1, ATT_W), const),
        ],
        out_specs=[_class_major_spec(tm, dil, ATT_W, per_batch) for _, dil in GROUPS] + [
            pl.BlockSpec((tm, F_W), row),
            pl.BlockSpec((tm, F_W), row),
            pl.BlockSpec((tm, 2 * D_MODEL), row),
        ],
        out_shape=[jax.ShapeDtypeStruct((batch, dil, seq // dil, ATT_W), BF16) for _, dil in GROUPS] + [
            jax.ShapeDtypeStruct((t, F_W), BF16),
            jax.ShapeDtypeStruct((t, F_W), BF16),
            jax.ShapeDtypeStruct((t, 2 * D_MODEL), BF16),
        ],
        scratch_shapes=[pltpu.VMEM((ATT_W // LANES, tm, LANES), F32)],
        compiler_params=_params(("parallel",)),
        name="in_proj",
    )(x, g1, w_in, w_gate, b_gate, cs, jnp.asarray(qscale))


def _attention_kernel(q_ref, kp_ref, kc_ref, kn_ref, vp_ref, vc_ref, vn_ref, bias_ref, o_ref, lse_ref, *,
                      tq, length, dil, rc):
    i = pl.program_id(1)
    win = ATT_SUB + 2 * HALF_KEYS
    nsub = tq // ATT_SUB
    lane_head = lax.broadcasted_iota(I32, (ATT_SUB, GROUP_W), 1) // HEAD_DIM
    lane_slot = lax.broadcasted_iota(I32, (ATT_SUB, LANES), 1) // (LANES // HEADS_PER_GROUP)
    at_start = (i == 0).astype(I32)
    at_end = (i == length // tq - 1).astype(I32) * 2
    for c, sb in [(c, sb) for c in range(rc) for sb in range(nsub)]:
        r = pl.program_id(2) * rc + c
        if sb == 0:
            kwin = jnp.concatenate([kp_ref[0, c], kc_ref[0, c], kn_ref[0, c]], axis=0)
            vwin = jnp.concatenate([vp_ref[0, c], vc_ref[0, c], vn_ref[0, c]], axis=0)
        off = sb * ATT_SUB
        q = q_ref[0, c, off:off + ATT_SUB, :]
        kw = kwin[off:off + win]
        vw = vwin[off:off + win]
        variant = (at_start if sb == 0 else 0) + (at_end if sb == nsub - 1 else 0)
        qs = jnp.concatenate(
            [jnp.where(lane_head == h, q, jnp.zeros_like(q)) for h in range(HEADS_PER_GROUP)], axis=0)
        s_all = _dot_nt(qs, kw)
        ps, ms, ls = [], [], []
        for h in range(HEADS_PER_GROUP):
            s = s_all[h * ATT_SUB:(h + 1) * ATT_SUB] + bias_ref[variant, h]
            m = jnp.max(s, axis=-1, keepdims=True)
            p = jnp.exp(s - m)
            ls.append(jnp.sum(p, axis=-1, keepdims=True))
            ms.append(m)
            ps.append(p.astype(BF16))
        o_all = _dot(jnp.concatenate(ps, axis=0), vw)
        out = jnp.zeros((ATT_SUB, GROUP_W), F32)
        lse = jnp.zeros((ATT_SUB, LANES), F32)
        for h in range(HEADS_PER_GROUP):
            oh = o_all[h * ATT_SUB:(h + 1) * ATT_SUB] * (1.0 / ls[h])
            out = jnp.where(lane_head == h, oh, out)
            lse = jnp.where(lane_slot == h, ms[h] + jnp.log(ls[h]), lse)
        rows = pl.ds(off * dil + r, ATT_SUB, stride=dil) if dil > 1 else pl.ds(off, ATT_SUB)
        o_ref[rows, :] = _pack_bf16_pair(out[:, :LANES], out[:, LANES:])
        lse_ref[rows, :] = lse


def _attention(qkv, bias, g):
    batch, dil, length, _ = qkv.shape
    tq = min(length, 512, ATT_OUT_ROWS // dil)
    nb = length // tq
    hb = tq // HALF_KEYS
    last_halo = length // HALF_KEYS - 1
    rc = min(dil, max(1, 512 // tq))

    def cur(c):
        return lambda b, i, r: (b, r, i, c)

    def prev(c):
        return lambda b, i, r: (b, r, jnp.maximum(i * hb - 1, 0), c)

    def nxt(c):
        return lambda b, i, r: (b, r, jnp.minimum((i + 1) * hb, last_halo), c)

    blk = lambda rows: (1, rc, rows, GROUP_W)
    out_spec = pl.BlockSpec((tq * dil, LANES), lambda b, i, r: (b * nb + i, 0))
    return pl.pallas_call(
        functools.partial(_attention_kernel, tq=tq, length=length, dil=dil, rc=rc),
        grid=(batch, nb, dil // rc),
        in_specs=[
            pl.BlockSpec(blk(tq), cur(0)),
            pl.BlockSpec(blk(HALF_KEYS), prev(1)),
            pl.BlockSpec(blk(tq), cur(1)),
            pl.BlockSpec(blk(HALF_KEYS), nxt(1)),
            pl.BlockSpec(blk(HALF_KEYS), prev(2)),
            pl.BlockSpec(blk(tq), cur(2)),
            pl.BlockSpec(blk(HALF_KEYS), nxt(2)),
            pl.BlockSpec(bias.shape, lambda b, i, r: (0, 0, 0, 0)),
        ],
        out_specs=[out_spec, out_spec],
        out_shape=[jax.ShapeDtypeStruct((batch * dil * length, LANES), U32),
                   jax.ShapeDtypeStruct((batch * dil * length, LANES), F32)],
        compiler_params=_params(("parallel", "parallel", "arbitrary")),
        name=f"attention_g{g}",
    )(qkv, qkv, qkv, qkv, qkv, qkv, qkv, bias)


def _t5_bucket(rel):
    nb = NUM_BUCKETS // 2
    max_exact = nb // 2
    ret = (rel > 0).astype(np.int32) * nb
    n = np.abs(rel)
    large = max_exact + (np.log(np.maximum(n, max_exact) / max_exact)
                         / np.log(MAX_DISTANCE / max_exact) * (nb - max_exact)).astype(np.int32)
    large = np.minimum(large, nb - 1)
    return (ret + np.where(n < max_exact, n, large)).astype(np.int32)


def _attention_bias(rel_bias, g):
    dil = GROUPS[g][1]
    qi = np.arange(ATT_SUB)[:, None]
    kj = np.arange(ATT_SUB + 2 * HALF_KEYS)[None, :]
    delta = kj - HALF_KEYS - qi
    band = np.abs(delta) <= HALF_KEYS
    bucket = _t5_bucket(dil * delta)
    tab = rel_bias[:, g * HEADS_PER_GROUP:(g + 1) * HEADS_PER_GROUP].astype(F32)
    onehot = jnp.asarray(bucket[..., None] == np.arange(NUM_BUCKETS), F32)
    bias = jnp.einsum("qkb,bh->hqk", onehot, tab, precision=lax.Precision.HIGHEST)
    masks = [band & ((kj >= HALF_KEYS) | ((v & 1) == 0)) & ((kj < ATT_SUB + HALF_KEYS) | ((v & 2) == 0))
             for v in range(4)]
    return jnp.where(jnp.asarray(np.stack(masks))[:, None], bias[None], NEG)


def _dft_mats(n):
    k = np.arange(n)
    ang = 2.0 * np.pi * ((k[:, None] * k[None, :]) % n) / n
    return np.cos(ang), np.sin(ang)


def _pack_bf16_pair(a, b):
    hi = lax.bitcast_convert_type(a.astype(BF16).astype(F32), U32)
    lo = lax.bitcast_convert_type(b.astype(BF16).astype(F32), U32)
    return hi | lax.shift_right_logical(lo, jnp.full(lo.shape, 16, U32))


def _unpack_bf16_pair(word):
    a = lax.bitcast_convert_type(word & jnp.uint32(0xFFFF0000), F32)
    b = lax.bitcast_convert_type(lax.shift_left(word, jnp.full(word.shape, 16, U32)), F32)
    return a.astype(BF16), b.astype(BF16)


def _fft_stage1_kernel(vr_ref, vi_ref, m1_ref, twc_ref, tws_ref, z_ref, *, n1, m):
    x = jnp.concatenate([vr_ref[0], vi_ref[0]], axis=0)
    z = _dot(m1_ref[...], x)
    zr, zi = z[:n1], z[n1:]
    twc, tws = twc_ref[0], tws_ref[0]
    for j in range(m):
        c = twc[:, j:j + 1]
        s = tws[:, j:j + 1]
        a = zr[:, j * F_W:(j + 1) * F_W]
        b = zi[:, j * F_W:(j + 1) * F_W]
        z_ref[0, :, j, :] = _pack_bf16_pair(a * c + b * s, b * c - a * s)


def _fft_stage2_kernel(z_ref, m2_ref, o_ref, *, kc, scale):
    m2 = m2_ref[...]
    for j in range(kc):
        x = jnp.concatenate(_unpack_bf16_pair(z_ref[0, j]), axis=0)
        y = _dot(m2, x) * scale
        o_ref[0, :, j, :] = _pack_bf16_pair(y[:, :F_W // 2], y[:, F_W // 2:])


def _fourier(vr, vi, batch, seq):
    n2 = LANES
    n1 = seq // n2
    m = min(n2, FFT_STEP_ROWS // n1)
    c1, s1 = _dft_mats(n1)
    m1 = jnp.asarray(np.block([[c1, s1], [-s1, c1]]), BF16)
    c2, s2 = _dft_mats(n2)
    m2 = jnp.asarray(np.concatenate([c2, s2], axis=1), BF16)
    k1 = np.arange(n1)[:, None]
    sv = np.arange(n2)[None, :]
    ang = 2.0 * np.pi * ((k1 * sv) % seq) / seq
    to_blocks = lambda a: jnp.asarray(a.reshape(n1, n2 // m, m).transpose(1, 0, 2), F32)
    twc, tws = to_blocks(np.cos(ang)), to_blocks(np.sin(ang))

    v3 = lambda a: a.reshape(batch, n1, n2 * F_W)
    blk = pl.BlockSpec((1, n1, m * F_W), lambda b, j: (b, 0, j))
    tmap = lambda b, j: (j, 0, 0)
    z = pl.pallas_call(
        functools.partial(_fft_stage1_kernel, n1=n1, m=m),
        grid=(batch, n2 // m),
        in_specs=[
            blk,
            blk,
            pl.BlockSpec(m1.shape, lambda b, j: (0, 0)),
            pl.BlockSpec((1, n1, m), tmap),
            pl.BlockSpec((1, n1, m), tmap),
        ],
        out_specs=pl.BlockSpec((1, n1, m, F_W), lambda b, j: (b, 0, j, 0)),
        out_shape=jax.ShapeDtypeStruct((batch, n1, n2, F_W), U32),
        compiler_params=_params(("parallel", "parallel")),
        name="fft_stage1",
    )(v3(vr), v3(vi), m1, twc, tws)

    kc = min(n1, FFT_STEP_ROWS // n2 * 2)
    out = pl.pallas_call(
        functools.partial(_fft_stage2_kernel, kc=kc, scale=1.0 / math.sqrt(seq * F_CH)),
        grid=(batch, n1 // kc),
        in_specs=[
            pl.BlockSpec((1, kc, n2, F_W), lambda b, j: (b, j, 0, 0)),
            pl.BlockSpec(m2.shape, lambda b, j: (0, 0)),
        ],
        out_specs=pl.BlockSpec((1, n2, kc, F_W // 2), lambda b, j: (b, 0, j, 0)),
        out_shape=jax.ShapeDtypeStruct((batch, n2, n1, F_W // 2), U32),
        compiler_params=_params(("parallel", "parallel")),
        name="fft_stage2",
    )(z, m2)
    return out.reshape(batch * seq, F_W // 2)


def _mix_kernel(x_ref, o0_ref, o1_ref, o2_ref, l0_ref, l1_ref, l2_ref, four_ref, gates_ref,
                wa_ref, wf_ref, wo_ref, g2_ref, wr_ref, hs_ref, x1_ref, xn_ref, aff_ref, afft_ref):
    tm = x_ref.shape[0]
    half = tm // 2
    for rows in (slice(0, half), slice(half, tm)):
        f_br = _dot(jnp.concatenate(_unpack_bf16_pair(four_ref[rows, :]), axis=1), wf_ref[...])
        l0, l1, l2 = l0_ref[rows, :], l1_ref[rows, :], l2_ref[rows, :]
        mx = jnp.maximum(jnp.maximum(l0, l1), l2)
        e0, e1, e2 = jnp.exp(l0 - mx), jnp.exp(l1 - mx), jnp.exp(l2 - mx)
        inv = 1.0 / (e0 + e1 + e2)
        w0, w1, w2 = (_dot((e * inv).astype(BF16), hs_ref[...]) for e in (e0, e1, e2))
        o0, o1, o2 = (jnp.concatenate(_unpack_bf16_pair(ref[rows, :]), axis=1) for ref in (o0_ref, o1_ref, o2_ref))
        att = w0 * o0 + w1 * o1 + w2 * o2
        a_br = _dot(att.astype(BF16), wa_ref[...])
        mix = gates_ref[rows, :D_MODEL] * a_br + gates_ref[rows, D_MODEL:] * f_br
        x1 = x_ref[rows, :] + _dot(mix.astype(BF16), wo_ref[...])
        x1_ref[rows, :] = x1
        ms = jnp.mean(x1 * x1, axis=-1, keepdims=True)
        xn = x1 * lax.rsqrt(ms + EPS) * g2_ref[...]
        xn_ref[rows, :] = xn.astype(BF16)
        xh = xn.astype(BF16)
        xl = (xn - xh.astype(F32)).astype(BF16)
        both = _dot(xh, wr_ref[...])
        logits = both[:, :LANES] + (both[:, LANES:] + _dot(xl, wr_ref[:, :LANES]))
        lane = lax.broadcasted_iota(I32, logits.shape, 1)
        logits = jnp.where(lane < N_EXPERTS, logits, NEG)
        p = jnp.exp(logits - jnp.max(logits, axis=-1, keepdims=True))
        aff = p * (1.0 / jnp.sum(p, axis=-1, keepdims=True))
        aff_ref[rows, :] = aff
        afft_ref[:, rows] = aff.T[:N_EXPERTS]


def _mix(x, os_, ls_, four, gates, w_attn, w_four, w_out, g2, w_router):
    t = x.shape[0]
    tm = TOKEN_TILE
    const = lambda i: (0, 0)
    row = lambda i: (i, 0)
    rows = lambda w: pl.BlockSpec((tm, w), row)
    full = lambda a: pl.BlockSpec(a.shape, const)
    slot = LANES // HEADS_PER_GROUP
    head_spread = jnp.asarray(np.arange(LANES)[:, None] == slot * (np.arange(GROUP_W)[None, :] // HEAD_DIM), BF16)
    return pl.pallas_call(
        _mix_kernel,
        grid=(t // tm,),
        in_specs=[rows(D_MODEL)] + [rows(LANES)] * 6 + [rows(F_W // 2), rows(2 * D_MODEL),
                  full(w_attn), full(w_four), full(w_out), full(g2), full(w_router), full(head_spread)],
        out_specs=[rows(D_MODEL), rows(D_MODEL), rows(LANES), pl.BlockSpec((N_EXPERTS, tm), lambda i: (0, i))],
        out_shape=[
            jax.ShapeDtypeStruct((t, D_MODEL), F32),
            jax.ShapeDtypeStruct((t, D_MODEL), BF16),
            jax.ShapeDtypeStruct((t, LANES), F32),
            jax.ShapeDtypeStruct((N_EXPERTS, t), F32),
        ],
        compiler_params=_params(("parallel",)),
        name="mix",
    )(x, *os_, *ls_, four, gates, w_attn, w_four, w_out, g2, w_router, head_spread)


def _route_kernel(afft_ref, su_ref, u_ref, tau_ref, need_ref, beq_ref, bsel_ref,
                  taut_ref, needt_ref, beqt_ref, bselt_ref, *, tokens):
    cap = CAPACITY_FACTOR * tokens // N_EXPERTS
    ntile = tokens // ROUTE_TILE
    shape = (N_EXPERTS, LANES)
    lane = lax.broadcasted_iota(I32, shape, 1)

    def keys(start, width):
        return lax.bitcast_convert_type(afft_ref[:, pl.ds(pl.multiple_of(start, LANES), width)], I32)

    span = min(tokens, 16 * LANES)

    def count(pred):
        def body(c, acc):
            hits = _ones_where(pred(keys(c * span, span)))
            for j in range(span // LANES):
                acc = acc + hits[:, j * LANES:(j + 1) * LANES]
            return acc
        acc = lax.fori_loop(0, tokens // span, body, jnp.zeros(shape, F32), unroll=True)
        return jnp.sum(acc, axis=1, keepdims=True)

    def bit_body(i, prefix):
        cand = prefix | lax.shift_left(jnp.ones(shape, I32), jnp.full(shape, 30 - i, I32))
        tot = count(lambda k: k >= cand[:, :1])
        return jnp.where(tot >= cap, cand, prefix)

    tau = lax.fori_loop(0, 31, bit_body, jnp.zeros(shape, I32))
    tau_col = tau[:, :1]
    n_gt = count(lambda k: k > tau_col)
    need = cap - n_gt

    def prefix_over_tiles(tab):
        return _dot(tab.astype(BF16), su_ref[...])

    def at_lane(tab, c):
        return jnp.sum(jnp.where(lane == c, tab, 0.0), axis=1, keepdims=True)

    def eq_body(c, tab):
        k = keys(c * ROUTE_TILE, ROUTE_TILE)
        cnt = jnp.sum(_ones_where(k == tau_col), axis=1, keepdims=True)
        return jnp.where(lane == c, cnt, tab)

    unroll = 8 if ntile % 8 == 0 else 1
    base_eq = prefix_over_tiles(lax.fori_loop(0, ntile, eq_body, jnp.zeros(shape, F32), unroll=unroll))

    def sel_body(c, tab):
        k = keys(c * ROUTE_TILE, ROUTE_TILE)
        eq = k == tau_col
        eq_cum = _dot(_ones_where(eq, BF16), u_ref[...]) + at_lane(base_eq, c)
        sel = (k > tau_col) | (eq & (eq_cum <= need))
        cnt = jnp.sum(_ones_where(sel), axis=1, keepdims=True)
        return jnp.where(lane == c, cnt, tab)

    base_sel = prefix_over_tiles(lax.fori_loop(0, ntile, sel_body, jnp.zeros(shape, F32), unroll=unroll))

    def transposed(val):
        return jnp.concatenate([val, jnp.zeros((LANES - N_EXPERTS, LANES), val.dtype)], axis=0).T

    tau_ref[...] = tau
    taut_ref[...] = transposed(tau)
    for val, ref, ref_t in ((jnp.broadcast_to(need, shape), need_ref, needt_ref),
                            (base_eq, beq_ref, beqt_ref), (base_sel, bsel_ref, bselt_ref)):
        ref[...] = val.astype(I32)
        ref_t[...] = transposed(val)


def _route(afft):
    tokens = afft.shape[1]
    idx = np.arange(LANES)
    su = jnp.asarray(idx[:, None] < idx[None, :], BF16)
    idx = np.arange(ROUTE_TILE)
    u = jnp.asarray(idx[:, None] <= idx[None, :], BF16)
    full = lambda a: pl.BlockSpec(a.shape, lambda i: (0,) * a.ndim)
    small = pl.BlockSpec((N_EXPERTS, LANES), lambda i: (0, 0))
    smallt = pl.BlockSpec((LANES, LANES), lambda i: (0, 0))
    return pl.pallas_call(
        functools.partial(_route_kernel, tokens=tokens),
        grid=(1,),
        in_specs=[full(afft), full(su), full(u)],
        out_specs=[small] * 4 + [smallt] * 4,
        out_shape=[jax.ShapeDtypeStruct((N_EXPERTS, LANES), I32)] * 4
        + [jax.ShapeDtypeStruct((LANES, LANES), I32)] + [jax.ShapeDtypeStruct((LANES, LANES), F32)] * 3,
        compiler_params=_params(("arbitrary",)),
        name="route",
    )(afft, su, u)


def _gather_kernel(bsel_s, afft_ref, tau_ref, need_ref, beq_ref, x_ref, u_ref, xe_hbm,
                   stage_ref, tail_ref, xbuf_ref, sem_ref, xsem_ref, *, ntile, cap):
    t = pl.program_id(0)
    par = t & 1

    def aligned(e, tile):
        return pl.multiple_of(_floor_pow2(bsel_s[e, tile], ROW_ALIGN), ROW_ALIGN)

    def write(e, tile, buf, first_row=None):
        first_row = aligned(e, tile) if first_row is None else first_row
        return pltpu.make_async_copy(stage_ref.at[buf, e], xe_hbm.at[e, pl.ds(first_row, GATHER_BLOCK)],
                                     sem_ref.at[buf])

    @pl.when(t == 0)
    def _():
        tail_ref[...] = jnp.zeros_like(tail_ref)
        stage_ref[1] = jnp.zeros(stage_ref.shape[1:], BF16)
        for e in range(N_EXPERTS):
            write(e, 0, 1, first_row=cap).start()

    k = lax.bitcast_convert_type(afft_ref[...], I32)
    tau = tau_ref[:, :1]
    lane = lax.broadcasted_iota(I32, (N_EXPERTS, LANES), 1)
    beq = jnp.sum(jnp.where(lane == t, beq_ref[...].astype(F32), 0.0), axis=1, keepdims=True)
    eq = k == tau
    eq_cum = _dot(_ones_where(eq, BF16), u_ref[...]) + beq
    sel = (k > tau) | (eq & (eq_cum <= need_ref[:, :1].astype(F32)))
    rank = jnp.where(sel, _dot(_ones_where(sel, BF16), u_ref[...]) - 1.0, -1e4)

    row = lax.broadcasted_iota(I32, (GATHER_STACK, ROUTE_TILE), 0)
    in_block = row < GATHER_BLOCK
    row_f = row.astype(F32)
    offs, shifts, pieces = [], [], []
    for e in range(N_EXPERTS):
        off = (bsel_s[e, t] - aligned(e, t)).astype(F32)
        shift = _floor_pow2(bsel_s[e, t + 1], ROW_ALIGN) - aligned(e, t)
        target = jnp.where(in_block, row_f, row_f - float(GATHER_BLOCK) + shift.astype(F32))
        pieces.append(_ones_where(rank[e:e + 1, :] + off == target, BF16))
        offs.append(off)
        shifts.append(shift)
    res = _dot(jnp.concatenate(pieces, axis=0), x_ref[...])
    for e in range(N_EXPERTS):
        base = e * GATHER_STACK
        old = tail_ref[e]
        stage_ref[par, e, 0:ROW_ALIGN, :] = (res[base:base + ROW_ALIGN] + old).astype(BF16)
        stage_ref[par, e, ROW_ALIGN:GATHER_BLOCK, :] = res[base + ROW_ALIGN:base + GATHER_BLOCK].astype(BF16)
        tail_ref[e] = res[base + GATHER_BLOCK:base + GATHER_STACK] + jnp.where(shifts[e] == 0, old, 0.0)
    for e in range(N_EXPERTS):
        write(e, jnp.maximum(t - 1, 0), 1 - par).wait()
    for e in range(N_EXPERTS):
        write(e, t, par).start()

    extra = [_cdiv_pow2(jnp.maximum(bsel_s[e, t + 1] - aligned(e, t) - GATHER_BLOCK, 0), SLOT_CHUNK)
             for e in range(N_EXPERTS)]

    @pl.when(functools.reduce(jnp.maximum, extra) > 0)
    def _():
        row64 = lax.broadcasted_iota(I32, (SLOT_CHUNK, ROUTE_TILE), 0).astype(F32)
        for e in range(N_EXPERTS):
            def chunk(c, carry):
                first = GATHER_BLOCK + c * SLOT_CHUNK
                onehot = _ones_where(rank[e:e + 1, :] + offs[e] == row64 + first.astype(F32), BF16)
                xbuf_ref[...] = _dot(onehot, x_ref[...]).astype(BF16)
                dst = pl.multiple_of(aligned(e, t) + first, ROW_ALIGN)
                cp = pltpu.make_async_copy(xbuf_ref, xe_hbm.at[e, pl.ds(dst, SLOT_CHUNK)], xsem_ref.at[0])
                cp.start()
                cp.wait()
                return carry

            lax.fori_loop(0, extra[e], chunk, 0)

    @pl.when(t == ntile - 1)
    def _():
        for e in range(N_EXPERTS):
            write(e, t, par).wait()


def _gather(bsel_i, afft, tau, need, beq_i, xn, u):
    tokens = xn.shape[0]
    cap = CAPACITY_FACTOR * tokens // N_EXPERTS
    ntile = tokens // ROUTE_TILE
    table = pl.BlockSpec((N_EXPERTS, LANES), lambda t, *_: (0, 0))
    grid_spec = pltpu.PrefetchScalarGridSpec(
        num_scalar_prefetch=1,
        grid=(ntile,),
        in_specs=[
            pl.BlockSpec((N_EXPERTS, ROUTE_TILE), lambda t, *_: (0, t)),
            table, table, table,
            pl.BlockSpec((ROUTE_TILE, D_MODEL), lambda t, *_: (t, 0)),
            pl.BlockSpec(u.shape, lambda t, *_: (0, 0)),
        ],
        out_specs=pl.BlockSpec(memory_space=pl.ANY),
        scratch_shapes=[
            pltpu.VMEM((2, N_EXPERTS, GATHER_BLOCK, D_MODEL), BF16),
            pltpu.VMEM((N_EXPERTS, ROW_ALIGN, D_MODEL), F32),
            pltpu.VMEM((SLOT_CHUNK, D_MODEL), BF16),
            pltpu.SemaphoreType.DMA((2,)),
            pltpu.SemaphoreType.DMA((1,)),
        ],
    )
    return pl.pallas_call(
        functools.partial(_gather_kernel, ntile=ntile, cap=cap),
        grid_spec=grid_spec,
        out_shape=jax.ShapeDtypeStruct((N_EXPERTS, cap + GATHER_PAD, D_MODEL), BF16),
        compiler_params=_params(("arbitrary",)),
        name="gather",
    )(bsel_i, afft, tau, need, beq_i, xn, u)


def _ffn_kernel(xe_ref, wg_ref, wu_ref, wd_ref, ye_ref, acc_ref, *, cap, nf, tm):
    f = pl.program_id(1)
    tf = wg_ref.shape[2]
    chunks = [slice(j * FFN_CHUNK, (j + 1) * FFN_CHUNK) for j in range(tf // FFN_CHUNK)]
    cast = {}

    def weight(name, ref, j):
        if (name, j) not in cast:
            cast[name, j] = (ref[0, chunks[j], :] if name == "d" else ref[0, :, chunks[j]]).astype(BF16)
        return cast[name, j]

    @pl.when(f == 0)
    def _():
        acc_ref[...] = jnp.zeros_like(acc_ref)

    for i in range(cap // tm):
        r = slice(i * tm, (i + 1) * tm)
        x = xe_ref[0, r, :]
        y = None
        for j in range(len(chunks)):
            hg = _dot(x, weight("g", wg_ref, j))
            hu = _dot(x, weight("u", wu_ref, j))
            h = (hg * (1.0 / (1.0 + jnp.exp(-hg))) * hu).astype(BF16)
            part = _dot(h, weight("d", wd_ref, j))
            y = part if y is None else y + part
        acc_ref[r, :] += y

    @pl.when(f == nf - 1)
    def _():
        ye_ref[...] = acc_ref[...].astype(BF16)


def _ffn(xe, w_eg, w_eu, w_ed):
    cap = xe.shape[1] - GATHER_PAD
    tf = 512
    nf = D_FF // tf
    tm = min(cap, 1024)
    return pl.pallas_call(
        functools.partial(_ffn_kernel, cap=cap, nf=nf, tm=tm),
        grid=(N_EXPERTS, nf),
        in_specs=[
            pl.BlockSpec((1, cap, D_MODEL), lambda e, f: (e, 0, 0)),
            pl.BlockSpec((1, D_MODEL, tf), lambda e, f: (e, 0, f)),
            pl.BlockSpec((1, D_MODEL, tf), lambda e, f: (e, 0, f)),
            pl.BlockSpec((1, tf, D_MODEL), lambda e, f: (e, f, 0)),
        ],
        out_specs=pl.BlockSpec((cap, D_MODEL), lambda e, f: (e, 0)),
        out_shape=jax.ShapeDtypeStruct((N_EXPERTS * cap, D_MODEL), BF16),
        scratch_shapes=[pltpu.VMEM((cap, D_MODEL), F32)],
        compiler_params=_params(("arbitrary", "arbitrary")),
        name="ffn",
    )(xe, w_eg, w_eu, w_ed)


def _combine_kernel(bsel_s, x1_ref, aff_ref, taut_ref, needt_ref, beqt_ref, bselt_ref, low_ref, spread_ref, gf_ref,
                    ye_hbm, y_ref, buf_ref, xbuf_ref, sem_ref, xsem_ref, *, cap, total, nstep):
    step = pl.program_id(0)
    par = step & 1
    per_tile = N_EXPERTS * SLOT_CHUNK

    def aligned(e, tile):
        return _floor_pow2(bsel_s[e, tile], ROW_ALIGN)

    def window(e, tile, c):
        start = jnp.minimum(e * cap + aligned(e, tile) + c * SLOT_CHUNK, total - SLOT_CHUNK)
        return pl.multiple_of(start, ROW_ALIGN)

    def first_chunks(stp, buf, sub):
        tile = stp * COMBINE_TILES + sub
        return [pltpu.make_async_copy(ye_hbm.at[pl.ds(window(e, tile, 0), SLOT_CHUNK)],
                                      buf_ref.at[buf, pl.ds(sub * per_tile + e * SLOT_CHUNK, SLOT_CHUNK)],
                                      sem_ref.at[buf])
                for e in range(N_EXPERTS)]

    @pl.when(step == 0)
    def _():
        for sub in range(COMBINE_TILES):
            for cp in first_chunks(0, 0, sub):
                cp.start()

    for sub in range(COMBINE_TILES):
        for cp in first_chunks(jnp.minimum(step + 1, nstep - 1), 1 - par, sub):
            cp.start()
    for sub in range(COMBINE_TILES):
        for cp in first_chunks(step, par, sub):
            cp.wait()

    tau = taut_ref[0:1, :]
    low = low_ref[...]
    spread = spread_ref[...]
    lane = lax.broadcasted_iota(I32, (1, LANES), 1)
    wide = lax.broadcasted_iota(I32, (ROUTE_TILE, per_tile), 1)
    in_chunk = (wide & (SLOT_CHUNK - 1)).astype(F32)
    slots, affs = [], []
    for sub in range(COMBINE_TILES):
        tile = step * COMBINE_TILES + sub
        rows = slice(sub * ROUTE_TILE, (sub + 1) * ROUTE_TILE)
        aff = aff_ref[rows, :]
        k = lax.bitcast_convert_type(aff, I32)
        eq = k == tau
        eq_cum = _dot(low, _ones_where(eq, BF16)) + beqt_ref[sub]
        sel = (k > tau) | (eq & (eq_cum <= needt_ref[0:1, :]))
        slot = jnp.where(sel, _dot(low, _ones_where(sel, BF16)) + (bselt_ref[sub] - 1.0), -1.0)

        rel = jnp.zeros((1, LANES), F32)
        for e in range(N_EXPERTS):
            rel = jnp.where(lane == e, (window(e, tile, 0) - e * cap).astype(F32), rel)
        d = slot - rel
        d = jnp.where(sel & (d >= 0.0) & (d < float(SLOT_CHUNK)), d, -1.0)
        hit = _dot(d.astype(BF16), spread) == in_chunk
        onehot_gate = jnp.where(hit, _dot(aff.astype(BF16), spread), 0.0).astype(BF16)
        y_ref[rows, :] = x1_ref[rows, :] + _dot(onehot_gate, buf_ref[par, sub * per_tile:(sub + 1) * per_tile, :])
        slots.append(slot)
        affs.append(aff)

    nch = [[_cdiv_pow2(bsel_s[e, step * COMBINE_TILES + sub + 1] - aligned(e, step * COMBINE_TILES + sub), SLOT_CHUNK)
            for e in range(N_EXPERTS)] for sub in range(COMBINE_TILES)]

    @pl.when(functools.reduce(jnp.maximum, [n for per_sub in nch for n in per_sub]) > 1)
    def _():
        lane64 = lax.broadcasted_iota(I32, (ROUTE_TILE, SLOT_CHUNK), 1).astype(F32)
        for sub in range(COMBINE_TILES):
            tile = step * COMBINE_TILES + sub
            rows = slice(sub * ROUTE_TILE, (sub + 1) * ROUTE_TILE)
            for e in range(N_EXPERTS):
                slot_e = slots[sub][:, e:e + 1]

                def extra(c, carry):
                    w = window(e, tile, c)
                    cp = pltpu.make_async_copy(ye_hbm.at[pl.ds(w, SLOT_CHUNK)], xbuf_ref, xsem_ref.at[0])
                    cp.start()
                    cp.wait()
                    first = (aligned(e, tile) + c * SLOT_CHUNK).astype(F32)
                    hit = (lane64 + (w - e * cap).astype(F32) == slot_e) & (slot_e >= first)
                    y_ref[rows, :] += affs[sub][:, e:e + 1] * _dot(_ones_where(hit, BF16), xbuf_ref[...])
                    return carry

                lax.fori_loop(1, nch[sub][e], extra, 0)

    acc = y_ref[...]
    ms = jnp.mean(acc * acc, axis=-1, keepdims=True)
    y_ref[...] = acc * lax.rsqrt(ms + EPS) * gf_ref[...]

    @pl.when(step == nstep - 1)
    def _():
        for sub in range(COMBINE_TILES):
            for cp in first_chunks(step, 1 - par, sub):
                cp.wait()


def _combine(bsel_i, x1, aff, tables_t, ye, gf):
    tokens = x1.shape[0]
    cap = CAPACITY_FACTOR * tokens // N_EXPERTS
    ntile = tokens // ROUTE_TILE
    idx = np.arange(ROUTE_TILE)
    low = jnp.asarray(idx[:, None] >= idx[None, :], BF16)
    spread = jnp.asarray(np.arange(LANES)[:, None] == np.arange(N_EXPERTS * SLOT_CHUNK)[None, :] // SLOT_CHUNK, BF16)
    taut, needt, beqt, bselt = tables_t
    rows = COMBINE_TILES * ROUTE_TILE
    rowvec = pl.BlockSpec((8, LANES), lambda t, *_: (0, 0))
    tilevec = pl.BlockSpec((COMBINE_TILES, 1, LANES), lambda t, *_: (t, 0, 0))
    grid_spec = pltpu.PrefetchScalarGridSpec(
        num_scalar_prefetch=1,
        grid=(ntile // COMBINE_TILES,),
        in_specs=[
            pl.BlockSpec((rows, D_MODEL), lambda t, *_: (t, 0)),
            pl.BlockSpec((rows, LANES), lambda t, *_: (t, 0)),
            rowvec, rowvec, tilevec, tilevec,
            pl.BlockSpec(low.shape, lambda t, *_: (0, 0)),
            pl.BlockSpec(spread.shape, lambda t, *_: (0, 0)),
            pl.BlockSpec((1, D_MODEL), lambda t, *_: (0, 0)),
            pl.BlockSpec(memory_space=pl.ANY),
        ],
        out_specs=pl.BlockSpec((rows, D_MODEL), lambda t, *_: (t, 0)),
        scratch_shapes=[
            pltpu.VMEM((2, COMBINE_TILES * N_EXPERTS * SLOT_CHUNK, D_MODEL), BF16),
            pltpu.VMEM((SLOT_CHUNK, D_MODEL), BF16),
            pltpu.SemaphoreType.DMA((2,)),
            pltpu.SemaphoreType.DMA((1,)),
        ],
    )
    return pl.pallas_call(
        functools.partial(_combine_kernel, cap=cap, total=N_EXPERTS * cap, nstep=ntile // COMBINE_TILES),
        grid_spec=grid_spec,
        out_shape=jax.ShapeDtypeStruct((tokens, D_MODEL), F32),
        compiler_params=_params(("arbitrary",)),
        name="combine",
    )(bsel_i, x1, aff, taut, needt, beqt.reshape(LANES, 1, LANES), bselt.reshape(LANES, 1, LANES), low, spread,
      gf, ye)


def _encoder(x, w):
    batch, seq, width = x.shape
    tokens = batch * seq
    max_dil = max(dil for _, dil in GROUPS)
    assert width == D_MODEL and x.dtype == F32
    assert seq % (ATT_SUB * max_dil) == 0 and seq % (LANES * ROW_ALIGN) == 0 and seq % TOKEN_TILE == 0
    assert LANES % min(LANES, FFT_STEP_ROWS // (seq // LANES)) == 0
    assert tokens % (ROUTE_TILE * COMBINE_TILES * 8) == 0 and tokens // ROUTE_TILE < LANES
    assert (CAPACITY_FACTOR * tokens) % (N_EXPERTS * ROW_ALIGN) == 0
    xt = x.reshape(tokens, D_MODEL)
    *qkvs, vr, vi, gates = _in_proj(xt, w["g1"], w["w_in"], w["w_gate"], w["b_gate"], w["cs"], batch, seq)
    outs, lses = [], []
    for g in range(N_GROUPS):
        o, lse = _attention(qkvs[g], w["bias"][g], g)
        outs.append(o)
        lses.append(lse)
    four = _fourier(vr, vi, batch, seq)
    x1, xn, aff, afft = _mix(xt, outs, lses, four, gates, w["w_attn"], w["w_four"], w["w_out"], w["g2"],
                             w["w_router"])
    tau, need, beq_i, bsel_i, taut, needt, beqt, bselt = _route(afft)
    idx = np.arange(ROUTE_TILE)
    u = jnp.asarray(idx[:, None] <= idx[None, :], BF16)
    xe = _gather(bsel_i, afft, tau, need, beq_i, xn, u)
    ye = _ffn(xe, w["w_eg"], w["w_eu"], w["w_ed"])
    y = _combine(bsel_i, x1, aff, (taut, needt, beqt, bselt), ye, w["gf"])
    return y.reshape(batch, seq, D_MODEL)


def _prepare_weights(rel_bias, norm1_g, w_in, w_attn_br, w_four_br, w_gate, b_gate, w_out,
                     norm2_g, w_router, w_exp_gate, w_exp_up, w_exp_down, final_g):
    c, s = _dft_mats(F_CH)
    qkv_cols = w_in[0][:, :QKV_W].reshape(D_MODEL, 3, N_GROUPS, GROUP_W).transpose(0, 2, 1, 3)
    w_in_grouped = jnp.concatenate([qkv_cols.reshape(D_MODEL, QKV_W), w_in[0][:, QKV_W:]], axis=1)
    w_router = jnp.pad(w_router[0], ((0, 0), (0, LANES - N_EXPERTS)))
    w_router_hi = w_router.astype(BF16)
    return {
        "g1": norm1_g[0].reshape(1, D_MODEL),
        "w_in": w_in_grouped.astype(BF16),
        "w_gate": w_gate[0].astype(BF16),
        "b_gate": b_gate[0].reshape(1, 2 * D_MODEL),
        "cs": jnp.asarray(np.concatenate([c, s], axis=1), BF16),
        "bias": [_attention_bias(rel_bias, g) for g in range(N_GROUPS)],
        "w_attn": w_attn_br[0].astype(BF16),
        "w_four": w_four_br[0].astype(BF16),
        "w_out": w_out[0].astype(BF16),
        "g2": norm2_g[0].reshape(1, D_MODEL),
        "w_router": jnp.concatenate([w_router_hi, (w_router - w_router_hi.astype(F32)).astype(BF16)], axis=1),
        "w_eg": w_exp_gate[0],
        "w_eu": w_exp_up[0],
        "w_ed": w_exp_down[0],
        "gf": final_g.reshape(1, D_MODEL),
    }


def kernel(x_prompt, x_sample, rel_bias, norm1_g, w_in, w_attn_br, w_four_br, w_gate, b_gate, w_out,
           norm2_g, w_router, w_exp_gate, w_exp_up, w_exp_down, final_g):
    w = _prepare_weights(rel_bias, norm1_g, w_in, w_attn_br, w_four_br, w_gate, b_gate, w_out,
                         norm2_g, w_router, w_exp_gate, w_exp_up, w_exp_down, final_g)
    return (_encoder(x_prompt, w), _encoder(x_sample, w))
```

```python
import functools
import math

import numpy as np
import jax
import jax.numpy as jnp
from jax import lax
from jax.experimental import pallas as pl
from jax.experimental.pallas import tpu as pltpu

D_MODEL = 1024
HEAD_DIM = 64
HEADS_PER_GROUP = 4
GROUPS = ((128, 1), (512, 4), (2048, 16))
N_GROUPS = len(GROUPS)
GROUP_W = HEADS_PER_GROUP * HEAD_DIM
ATT_W = N_GROUPS * GROUP_W
QKV_W = 3 * ATT_W
F_GROUPS = 6
F_CH = 128
F_W = F_GROUPS * F_CH
NUM_BUCKETS = 32
MAX_DISTANCE = 1024
N_EXPERTS = 16
CAPACITY_FACTOR = 2
D_FF = 2048
EPS = 1e-6
NEG = -1e30

HALF_KEYS = 64
ATT_SUB = 128
ATT_STEP_ROWS = 1024
ATT_OUT_ROWS = 8192
TOKEN_TILE = 512
ROUTE_TILE = 256
FFT_STEP_ROWS = 1024
FFN_CHUNK = 256
SLOT_CHUNK = 64
COMBINE_TILES = 2
ROW_ALIGN = 16
GATHER_BLOCK = SLOT_CHUNK + ROW_ALIGN
GATHER_STACK = GATHER_BLOCK + ROW_ALIGN
GATHER_PAD = GATHER_BLOCK
LANES = 128
V7X_VMEM_LIMIT = 56 * 1024 * 1024

F32 = jnp.float32
BF16 = jnp.bfloat16
I32 = jnp.int32
U32 = jnp.uint32


def _params(sem):
    return pltpu.CompilerParams(dimension_semantics=sem, vmem_limit_bytes=V7X_VMEM_LIMIT)


def _dot(a, b):
    return jnp.dot(a, b, preferred_element_type=F32)


def _dot_nt(a, b):
    return lax.dot_general(a, b, (((1,), (1,)), ((), ())), preferred_element_type=F32)


def _floor_pow2(x, m):
    return x & ~(m - 1)


def _cdiv_pow2(x, m):
    return (x + (m - 1)) >> (m.bit_length() - 1)


def _ones_where(mask, dtype=F32):
    return jnp.where(mask, jnp.ones((), F32), jnp.zeros((), F32)).astype(dtype)


def _in_proj_kernel(x_ref, g_ref, win_ref, wg_ref, bg_ref, cs_ref, qscale_ref, qkv0_ref, qkv1_ref, qkv2_ref,
                    vr_ref, vi_ref, gates_ref, slab_ref):
    tm = x_ref.shape[0]
    half = tm // 2
    nslab = ATT_W // LANES
    cs = cs_ref[...]
    for h in range(2):
        rows = slice(h * half, (h + 1) * half)
        x = x_ref[rows, :]
        ms = jnp.mean(x * x, axis=-1, keepdims=True)
        xn = (x * lax.rsqrt(ms + EPS) * g_ref[...]).astype(BF16)
        for g, out_ref in enumerate((qkv0_ref, qkv1_ref, qkv2_ref)):
            dil = GROUPS[g][1]
            res = _dot(xn, win_ref[:, g * ATT_W:(g + 1) * ATT_W]) * qscale_ref[...]
            if dil == 1:
                out_ref[0, 0, rows, :] = res.astype(BF16)
                continue
            for j in range(nslab):
                slab_ref[j, rows, :] = res[:, j * LANES:(j + 1) * LANES]
            n = half // dil
            for r in range(dil):
                cls = [slab_ref[j, pl.ds(h * half + r, n, stride=dil), :] for j in range(nslab)]
                out_ref[0, r, h * n:(h + 1) * n, :] = jnp.concatenate(cls, axis=1).astype(BF16)
        u = _dot(xn, win_ref[:, QKV_W:QKV_W + F_W]).astype(BF16)
        for g in range(F_GROUPS):
            a = _dot(u[:, g * F_CH:(g + 1) * F_CH], cs)
            vr_ref[rows, g * F_CH:(g + 1) * F_CH] = a[:, :F_CH].astype(BF16)
            vi_ref[rows, g * F_CH:(g + 1) * F_CH] = (-a[:, F_CH:]).astype(BF16)
        z = _dot(xn, wg_ref[...]) + bg_ref[...]
        gates_ref[rows, :] = (1.0 / (1.0 + jnp.exp(-z))).astype(BF16)


def _class_major_spec(tm, dil, width, per_batch):
    return pl.BlockSpec((1, dil, tm // dil, width), lambda i: (i // per_batch, 0, i % per_batch, 0))


def _in_proj(x, g1, w_in, w_gate, b_gate, cs, batch, seq):
    t = x.shape[0]
    tm = TOKEN_TILE
    per_batch = seq // tm
    const = lambda i: (0, 0)
    row = lambda i: (i, 0)
    qscale = np.ones((1, ATT_W), np.float32)
    qscale[:, :GROUP_W] = 1.0 / math.sqrt(HEAD_DIM)
    return pl.pallas_call(
        _in_proj_kernel,
        grid=(t // tm,),
        in_specs=[
            pl.BlockSpec((tm, D_MODEL), row),
            pl.BlockSpec((1, D_MODEL), const),
            pl.BlockSpec(w_in.shape, const),
            pl.BlockSpec(w_gate.shape, const),
            pl.BlockSpec((1, 2 * D_MODEL), const),
            pl.BlockSpec(cs.shape, const),
            pl.BlockSpec((1, ATT_W), const),
        ],
        out_specs=[_class_major_spec(tm, dil, ATT_W, per_batch) for _, dil in GROUPS] + [
            pl.BlockSpec((tm, F_W), row),
            pl.BlockSpec((tm, F_W), row),
            pl.BlockSpec((tm, 2 * D_MODEL), row),
        ],
        out_shape=[jax.ShapeDtypeStruct((batch, dil, seq // dil, ATT_W), BF16) for _, dil in GROUPS] + [
            jax.ShapeDtypeStruct((t, F_W), BF16),
            jax.ShapeDtypeStruct((t, F_W), BF16),
            jax.ShapeDtypeStruct((t, 2 * D_MODEL), BF16),
        ],
        scratch_shapes=[pltpu.VMEM((ATT_W // LANES, tm, LANES), F32)],
        compiler_params=_params(("parallel",)),
        name="in_proj",
    )(x, g1, w_in, w_gate, b_gate, cs, jnp.asarray(qscale))


def _attention_kernel(q_ref, kp_ref, kc_ref, kn_ref, vp_ref, vc_ref, vn_ref, bias_ref, o_ref, lse_ref, *,
                      tq, length, dil, rc):
    i = pl.program_id(1)
    win = ATT_SUB + 2 * HALF_KEYS
    nsub = tq // ATT_SUB
    lane_head = lax.broadcasted_iota(I32, (ATT_SUB, GROUP_W), 1) // HEAD_DIM
    lane_slot = lax.broadcasted_iota(I32, (ATT_SUB, LANES), 1) // (LANES // HEADS_PER_GROUP)
    at_start = (i == 0).astype(I32)
    at_end = (i == length // tq - 1).astype(I32) * 2
    for c, sb in [(c, sb) for c in range(rc) for sb in range(nsub)]:
        r = pl.program_id(2) * rc + c
        if sb == 0:
            kwin = jnp.concatenate([kp_ref[0, c], kc_ref[0, c], kn_ref[0, c]], axis=0)
            vwin = jnp.concatenate([vp_ref[0, c], vc_ref[0, c], vn_ref[0, c]], axis=0)
        off = sb * ATT_SUB
        q = q_ref[0, c, off:off + ATT_SUB, :]
        kw = kwin[off:off + win]
        vw = vwin[off:off + win]
        variant = (at_start if sb == 0 else 0) + (at_end if sb == nsub - 1 else 0)
        qs = jnp.concatenate(
            [jnp.where(lane_head == h, q, jnp.zeros_like(q)) for h in range(HEADS_PER_GROUP)], axis=0)
        s_all = _dot_nt(qs, kw)
        ps, ms, ls = [], [], []
        for h in range(HEADS_PER_GROUP):
            s = s_all[h * ATT_SUB:(h + 1) * ATT_SUB] + bias_ref[variant, h]
            m = jnp.max(s, axis=-1, keepdims=True)
            p = jnp.exp(s - m)
            ls.append(jnp.sum(p, axis=-1, keepdims=True))
            ms.append(m)
            ps.append(p.astype(BF16))
        o_all = _dot(jnp.concatenate(ps, axis=0), vw)
        out = jnp.zeros((ATT_SUB, GROUP_W), F32)
        lse = jnp.zeros((ATT_SUB, LANES), F32)
        for h in range(HEADS_PER_GROUP):
            oh = o_all[h * ATT_SUB:(h + 1) * ATT_SUB] * (1.0 / ls[h])
            out = jnp.where(lane_head == h, oh, out)
            lse = jnp.where(lane_slot == h, ms[h] + jnp.log(ls[h]), lse)
        rows = pl.ds(off * dil + r, ATT_SUB, stride=dil) if dil > 1 else pl.ds(off, ATT_SUB)
        o_ref[rows, :] = _pack_bf16_pair(out[:, :LANES], out[:, LANES:])
        lse_ref[rows, :] = lse


def _attention(qkv, bias, g):
    batch, dil, length, _ = qkv.shape
    tq = min(length, ATT_STEP_ROWS, ATT_OUT_ROWS // dil)
    nb = length // tq
    hb = tq // HALF_KEYS
    last_halo = length // HALF_KEYS - 1
    rc = min(dil, max(1, ATT_STEP_ROWS // tq))

    def cur(c):
        return lambda b, i, r: (b, r, i, c)

    def prev(c):
        return lambda b, i, r: (b, r, jnp.maximum(i * hb - 1, 0), c)

    def nxt(c):
        return lambda b, i, r: (b, r, jnp.minimum((i + 1) * hb, last_halo), c)

    blk = lambda rows: (1, rc, rows, GROUP_W)
    out_spec = pl.BlockSpec((tq * dil, LANES), lambda b, i, r: (b * nb + i, 0))
    return pl.pallas_call(
        functools.partial(_attention_kernel, tq=tq, length=length, dil=dil, rc=rc),
        grid=(batch, nb, dil // rc),
        in_specs=[
            pl.BlockSpec(blk(tq), cur(0)),
            pl.BlockSpec(blk(HALF_KEYS), prev(1)),
            pl.BlockSpec(blk(tq), cur(1)),
            pl.BlockSpec(blk(HALF_KEYS), nxt(1)),
            pl.BlockSpec(blk(HALF_KEYS), prev(2)),
            pl.BlockSpec(blk(tq), cur(2)),
            pl.BlockSpec(blk(HALF_KEYS), nxt(2)),
            pl.BlockSpec(bias.shape, lambda b, i, r: (0, 0, 0, 0)),
        ],
        out_specs=[out_spec, out_spec],
        out_shape=[jax.ShapeDtypeStruct((batch * dil * length, LANES), U32),
                   jax.ShapeDtypeStruct((batch * dil * length, LANES), F32)],
        compiler_params=_params(("parallel", "parallel", "arbitrary")),
        name=f"attention_g{g}",
    )(qkv, qkv, qkv, qkv, qkv, qkv, qkv, bias)


def _t5_bucket(rel):
    nb = NUM_BUCKETS // 2
    max_exact = nb // 2
    ret = (rel > 0).astype(np.int32) * nb
    n = np.abs(rel)
    large = max_exact + (np.log(np.maximum(n, max_exact) / max_exact)
                         / np.log(MAX_DISTANCE / max_exact) * (nb - max_exact)).astype(np.int32)
    large = np.minimum(large, nb - 1)
    return (ret + np.where(n < max_exact, n, large)).astype(np.int32)


def _attention_bias(rel_bias, g):
    dil = GROUPS[g][1]
    qi = np.arange(ATT_SUB)[:, None]
    kj = np.arange(ATT_SUB + 2 * HALF_KEYS)[None, :]
    delta = kj - HALF_KEYS - qi
    band = np.abs(delta) <= HALF_KEYS
    bucket = _t5_bucket(dil * delta)
    tab = rel_bias[:, g * HEADS_PER_GROUP:(g + 1) * HEADS_PER_GROUP].astype(F32)
    onehot = jnp.asarray(bucket[..., None] == np.arange(NUM_BUCKETS), F32)
    bias = jnp.einsum("qkb,bh->hqk", onehot, tab, precision=lax.Precision.HIGHEST)
    masks = [band & ((kj >= HALF_KEYS) | ((v & 1) == 0)) & ((kj < ATT_SUB + HALF_KEYS) | ((v & 2) == 0))
             for v in range(4)]
    return jnp.where(jnp.asarray(np.stack(masks))[:, None], bias[None], NEG)


def _dft_mats(n):
    k = np.arange(n)
    ang = 2.0 * np.pi * ((k[:, None] * k[None, :]) % n) / n
    return np.cos(ang), np.sin(ang)


def _pack_bf16_pair(a, b):
    hi = lax.bitcast_convert_type(a.astype(BF16).astype(F32), U32)
    lo = lax.bitcast_convert_type(b.astype(BF16).astype(F32), U32)
    return hi | lax.shift_right_logical(lo, jnp.full(lo.shape, 16, U32))


def _unpack_bf16_pair(word):
    a = lax.bitcast_convert_type(word & jnp.uint32(0xFFFF0000), F32)
    b = lax.bitcast_convert_type(lax.shift_left(word, jnp.full(word.shape, 16, U32)), F32)
    return a.astype(BF16), b.astype(BF16)


def _fft_stage1_kernel(vr_ref, vi_ref, m1_ref, twc_ref, tws_ref, z_ref, *, n1, m):
    x = jnp.concatenate([vr_ref[0], vi_ref[0]], axis=0)
    z = _dot(m1_ref[...], x)
    zr, zi = z[:n1], z[n1:]
    twc, tws = twc_ref[0], tws_ref[0]
    for j in range(m):
        c = twc[:, j:j + 1]
        s = tws[:, j:j + 1]
        a = zr[:, j * F_W:(j + 1) * F_W]
        b = zi[:, j * F_W:(j + 1) * F_W]
        z_ref[0, :, j, :] = _pack_bf16_pair(a * c + b * s, b * c - a * s)


def _fft_stage2_kernel(z_ref, m2_ref, o_ref, *, kc, scale):
    m2 = m2_ref[...]
    for j in range(kc):
        x = jnp.concatenate(_unpack_bf16_pair(z_ref[0, j]), axis=0)
        y = _dot(m2, x) * scale
        o_ref[0, :, j, :] = _pack_bf16_pair(y[:, :F_W // 2], y[:, F_W // 2:])


def _fourier(vr, vi, batch, seq):
    n2 = LANES
    n1 = seq // n2
    m = min(n2, FFT_STEP_ROWS // n1)
    c1, s1 = _dft_mats(n1)
    m1 = jnp.asarray(np.block([[c1, s1], [-s1, c1]]), BF16)
    c2, s2 = _dft_mats(n2)
    m2 = jnp.asarray(np.concatenate([c2, s2], axis=1), BF16)
    k1 = np.arange(n1)[:, None]
    sv = np.arange(n2)[None, :]
    ang = 2.0 * np.pi * ((k1 * sv) % seq) / seq
    to_blocks = lambda a: jnp.asarray(a.reshape(n1, n2 // m, m).transpose(1, 0, 2), F32)
    twc, tws = to_blocks(np.cos(ang)), to_blocks(np.sin(ang))

    v3 = lambda a: a.reshape(batch, n1, n2 * F_W)
    blk = pl.BlockSpec((1, n1, m * F_W), lambda b, j: (b, 0, j))
    tmap = lambda b, j: (j, 0, 0)
    z = pl.pallas_call(
        functools.partial(_fft_stage1_kernel, n1=n1, m=m),
        grid=(batch, n2 // m),
        in_specs=[
            blk,
            blk,
            pl.BlockSpec(m1.shape, lambda b, j: (0, 0)),
            pl.BlockSpec((1, n1, m), tmap),
            pl.BlockSpec((1, n1, m), tmap),
        ],
        out_specs=pl.BlockSpec((1, n1, m, F_W), lambda b, j: (b, 0, j, 0)),
        out_shape=jax.ShapeDtypeStruct((batch, n1, n2, F_W), U32),
        compiler_params=_params(("parallel", "parallel")),
        name="fft_stage1",
    )(v3(vr), v3(vi), m1, twc, tws)

    kc = min(n1, FFT_STEP_ROWS // n2 * 2)
    out = pl.pallas_call(
        functools.partial(_fft_stage2_kernel, kc=kc, scale=1.0 / math.sqrt(seq * F_CH)),
        grid=(batch, n1 // kc),
        in_specs=[
            pl.BlockSpec((1, kc, n2, F_W), lambda b, j: (b, j, 0, 0)),
            pl.BlockSpec(m2.shape, lambda b, j: (0, 0)),
        ],
        out_specs=pl.BlockSpec((1, n2, kc, F_W // 2), lambda b, j: (b, 0, j, 0)),
        out_shape=jax.ShapeDtypeStruct((batch, n2, n1, F_W // 2), U32),
        compiler_params=_params(("parallel", "parallel")),
        name="fft_stage2",
    )(z, m2)
    return out.reshape(batch * seq, F_W // 2)


def _mix_kernel(x_ref, o0_ref, o1_ref, o2_ref, l0_ref, l1_ref, l2_ref, four_ref, gates_ref,
                wa_ref, wf_ref, wo_ref, g2_ref, wr_ref, hs_ref, x1_ref, xn_ref, aff_ref, afft_ref):
    tm = x_ref.shape[0]
    half = tm // 2
    for rows in (slice(0, half), slice(half, tm)):
        f_br = _dot(jnp.concatenate(_unpack_bf16_pair(four_ref[rows, :]), axis=1), wf_ref[...])
        l0, l1, l2 = l0_ref[rows, :], l1_ref[rows, :], l2_ref[rows, :]
        mx = jnp.maximum(jnp.maximum(l0, l1), l2)
        e0, e1, e2 = jnp.exp(l0 - mx), jnp.exp(l1 - mx), jnp.exp(l2 - mx)
        inv = 1.0 / (e0 + e1 + e2)
        w0, w1, w2 = (_dot((e * inv).astype(BF16), hs_ref[...]) for e in (e0, e1, e2))
        o0, o1, o2 = (jnp.concatenate(_unpack_bf16_pair(ref[rows, :]), axis=1) for ref in (o0_ref, o1_ref, o2_ref))
        att = w0 * o0 + w1 * o1 + w2 * o2
        a_br = _dot(att.astype(BF16), wa_ref[...])
        mix = gates_ref[rows, :D_MODEL] * a_br + gates_ref[rows, D_MODEL:] * f_br
        x1 = x_ref[rows, :] + _dot(mix.astype(BF16), wo_ref[...])
        x1_ref[rows, :] = x1
        ms = jnp.mean(x1 * x1, axis=-1, keepdims=True)
        xn = x1 * lax.rsqrt(ms + EPS) * g2_ref[...]
        xn_ref[rows, :] = xn.astype(BF16)
        xh = xn.astype(BF16)
        xl = (xn - xh.astype(F32)).astype(BF16)
        both = _dot(xh, wr_ref[...])
        logits = both[:, :LANES] + (both[:, LANES:] + _dot(xl, wr_ref[:, :LANES]))
        lane = lax.broadcasted_iota(I32, logits.shape, 1)
        logits = jnp.where(lane < N_EXPERTS, logits, NEG)
        p = jnp.exp(logits - jnp.max(logits, axis=-1, keepdims=True))
        aff = p * (1.0 / jnp.sum(p, axis=-1, keepdims=True))
        aff_ref[rows, :] = aff
        afft_ref[:, rows] = aff.T[:N_EXPERTS]


def _mix(x, os_, ls_, four, gates, w_attn, w_four, w_out, g2, w_router):
    t = x.shape[0]
    tm = TOKEN_TILE
    const = lambda i: (0, 0)
    row = lambda i: (i, 0)
    rows = lambda w: pl.BlockSpec((tm, w), row)
    full = lambda a: pl.BlockSpec(a.shape, const)
    slot = LANES // HEADS_PER_GROUP
    head_spread = jnp.asarray(np.arange(LANES)[:, None] == slot * (np.arange(GROUP_W)[None, :] // HEAD_DIM), BF16)
    return pl.pallas_call(
        _mix_kernel,
        grid=(t // tm,),
        in_specs=[rows(D_MODEL)] + [rows(LANES)] * 6 + [rows(F_W // 2), rows(2 * D_MODEL),
                  full(w_attn), full(w_four), full(w_out), full(g2), full(w_router), full(head_spread)],
        out_specs=[rows(D_MODEL), rows(D_MODEL), rows(LANES), pl.BlockSpec((N_EXPERTS, tm), lambda i: (0, i))],
        out_shape=[
            jax.ShapeDtypeStruct((t, D_MODEL), F32),
            jax.ShapeDtypeStruct((t, D_MODEL), BF16),
            jax.ShapeDtypeStruct((t, LANES), F32),
            jax.ShapeDtypeStruct((N_EXPERTS, t), F32),
        ],
        compiler_params=_params(("parallel",)),
        name="mix",
    )(x, *os_, *ls_, four, gates, w_attn, w_four, w_out, g2, w_router, head_spread)


def _route_kernel(afft_ref, su_ref, u_ref, tau_ref, need_ref, beq_ref, bsel_ref,
                  taut_ref, needt_ref, beqt_ref, bselt_ref, *, tokens):
    cap = CAPACITY_FACTOR * tokens // N_EXPERTS
    ntile = tokens // ROUTE_TILE
    shape = (N_EXPERTS, LANES)
    lane = lax.broadcasted_iota(I32, shape, 1)

    def keys(start, width):
        return lax.bitcast_convert_type(afft_ref[:, pl.ds(pl.multiple_of(start, LANES), width)], I32)

    span = min(tokens, 16 * LANES)

    def count(pred):
        def body(c, acc):
            hits = _ones_where(pred(keys(c * span, span)))
            for j in range(span // LANES):
                acc = acc + hits[:, j * LANES:(j + 1) * LANES]
            return acc
        acc = lax.fori_loop(0, tokens // span, body, jnp.zeros(shape, F32), unroll=True)
        return jnp.sum(acc, axis=1, keepdims=True)

    def bit_body(i, prefix):
        cand = prefix | lax.shift_left(jnp.ones(shape, I32), jnp.full(shape, 30 - i, I32))
        tot = count(lambda k: k >= cand[:, :1])
        return jnp.where(tot >= cap, cand, prefix)

    tau = lax.fori_loop(0, 31, bit_body, jnp.zeros(shape, I32))
    tau_col = tau[:, :1]
    n_gt = count(lambda k: k > tau_col)
    need = cap - n_gt

    def prefix_over_tiles(tab):
        return _dot(tab.astype(BF16), su_ref[...])

    def at_lane(tab, c):
        return jnp.sum(jnp.where(lane == c, tab, 0.0), axis=1, keepdims=True)

    def eq_body(c, tab):
        k = keys(c * ROUTE_TILE, ROUTE_TILE)
        cnt = jnp.sum(_ones_where(k == tau_col), axis=1, keepdims=True)
        return jnp.where(lane == c, cnt, tab)

    unroll = 8 if ntile % 8 == 0 else 1
    base_eq = prefix_over_tiles(lax.fori_loop(0, ntile, eq_body, jnp.zeros(shape, F32), unroll=unroll))

    def sel_body(c, tab):
        k = keys(c * ROUTE_TILE, ROUTE_TILE)
        eq = k == tau_col
        eq_cum = _dot(_ones_where(eq, BF16), u_ref[...]) + at_lane(base_eq, c)
        sel = (k > tau_col) | (eq & (eq_cum <= need))
        cnt = jnp.sum(_ones_where(sel), axis=1, keepdims=True)
        return jnp.where(lane == c, cnt, tab)

    base_sel = prefix_over_tiles(lax.fori_loop(0, ntile, sel_body, jnp.zeros(shape, F32), unroll=unroll))

    def transposed(val):
        return jnp.concatenate([val, jnp.zeros((LANES - N_EXPERTS, LANES), val.dtype)], axis=0).T

    tau_ref[...] = tau
    taut_ref[...] = transposed(tau)
    for val, ref, ref_t in ((jnp.broadcast_to(need, shape), need_ref, needt_ref),
                            (base_eq, beq_ref, beqt_ref), (base_sel, bsel_ref, bselt_ref)):
        ref[...] = val.astype(I32)
        ref_t[...] = transposed(val)


def _route(afft):
    tokens = afft.shape[1]
    idx = np.arange(LANES)
    su = jnp.asarray(idx[:, None] < idx[None, :], BF16)
    idx = np.arange(ROUTE_TILE)
    u = jnp.asarray(idx[:, None] <= idx[None, :], BF16)
    full = lambda a: pl.BlockSpec(a.shape, lambda i: (0,) * a.ndim)
    small = pl.BlockSpec((N_EXPERTS, LANES), lambda i: (0, 0))
    smallt = pl.BlockSpec((LANES, LANES), lambda i: (0, 0))
    return pl.pallas_call(
        functools.partial(_route_kernel, tokens=tokens),
        grid=(1,),
        in_specs=[full(afft), full(su), full(u)],
        out_specs=[small] * 4 + [smallt] * 4,
        out_shape=[jax.ShapeDtypeStruct((N_EXPERTS, LANES), I32)] * 4
        + [jax.ShapeDtypeStruct((LANES, LANES), I32)] + [jax.ShapeDtypeStruct((LANES, LANES), F32)] * 3,
        compiler_params=_params(("arbitrary",)),
        name="route",
    )(afft, su, u)


def _gather_kernel(bsel_s, afft_ref, tau_ref, need_ref, beq_ref, x_ref, u_ref, xe_hbm,
                   stage_ref, tail_ref, xbuf_ref, sem_ref, xsem_ref, *, ntile, cap):
    t = pl.program_id(0)
    par = t & 1

    def aligned(e, tile):
        return pl.multiple_of(_floor_pow2(bsel_s[e, tile], ROW_ALIGN), ROW_ALIGN)

    def write(e, tile, buf, first_row=None):
        first_row = aligned(e, tile) if first_row is None else first_row
        return pltpu.make_async_copy(stage_ref.at[buf, e], xe_hbm.at[e, pl.ds(first_row, GATHER_BLOCK)],
                                     sem_ref.at[buf])

    @pl.when(t == 0)
    def _():
        tail_ref[...] = jnp.zeros_like(tail_ref)
        stage_ref[1] = jnp.zeros(stage_ref.shape[1:], BF16)
        for e in range(N_EXPERTS):
            write(e, 0, 1, first_row=cap).start()

    k = lax.bitcast_convert_type(afft_ref[...], I32)
    tau = tau_ref[:, :1]
    lane = lax.broadcasted_iota(I32, (N_EXPERTS, LANES), 1)
    beq = jnp.sum(jnp.where(lane == t, beq_ref[...].astype(F32), 0.0), axis=1, keepdims=True)
    eq = k == tau
    eq_cum = _dot(_ones_where(eq, BF16), u_ref[...]) + beq
    sel = (k > tau) | (eq & (eq_cum <= need_ref[:, :1].astype(F32)))
    rank = jnp.where(sel, _dot(_ones_where(sel, BF16), u_ref[...]) - 1.0, -1e4)

    row = lax.broadcasted_iota(I32, (GATHER_STACK, ROUTE_TILE), 0)
    in_block = row < GATHER_BLOCK
    row_f = row.astype(F32)
    offs, shifts, pieces = [], [], []
    for e in range(N_EXPERTS):
        off = (bsel_s[e, t] - aligned(e, t)).astype(F32)
        shift = _floor_pow2(bsel_s[e, t + 1], ROW_ALIGN) - aligned(e, t)
        target = jnp.where(in_block, row_f, row_f - float(GATHER_BLOCK) + shift.astype(F32))
        pieces.append(_ones_where(rank[e:e + 1, :] + off == target, BF16))
        offs.append(off)
        shifts.append(shift)
    res = _dot(jnp.concatenate(pieces, axis=0), x_ref[...])
    for e in range(N_EXPERTS):
        base = e * GATHER_STACK
        old = tail_ref[e]
        stage_ref[par, e, 0:ROW_ALIGN, :] = (res[base:base + ROW_ALIGN] + old).astype(BF16)
        stage_ref[par, e, ROW_ALIGN:GATHER_BLOCK, :] = res[base + ROW_ALIGN:base + GATHER_BLOCK].astype(BF16)
        tail_ref[e] = res[base + GATHER_BLOCK:base + GATHER_STACK] + jnp.where(shifts[e] == 0, old, 0.0)
    for e in range(N_EXPERTS):
        write(e, jnp.maximum(t - 1, 0), 1 - par).wait()
    for e in range(N_EXPERTS):
        write(e, t, par).start()

    extra = [_cdiv_pow2(jnp.maximum(bsel_s[e, t + 1] - aligned(e, t) - GATHER_BLOCK, 0), SLOT_CHUNK)
             for e in range(N_EXPERTS)]

    @pl.when(functools.reduce(jnp.maximum, extra) > 0)
    def _():
        row64 = lax.broadcasted_iota(I32, (SLOT_CHUNK, ROUTE_TILE), 0).astype(F32)
        for e in range(N_EXPERTS):
            def chunk(c, carry):
                first = GATHER_BLOCK + c * SLOT_CHUNK
                onehot = _ones_where(rank[e:e + 1, :] + offs[e] == row64 + first.astype(F32), BF16)
                xbuf_ref[...] = _dot(onehot, x_ref[...]).astype(BF16)
                dst = pl.multiple_of(aligned(e, t) + first, ROW_ALIGN)
                cp = pltpu.make_async_copy(xbuf_ref, xe_hbm.at[e, pl.ds(dst, SLOT_CHUNK)], xsem_ref.at[0])
                cp.start()
                cp.wait()
                return carry

            lax.fori_loop(0, extra[e], chunk, 0)

    @pl.when(t == ntile - 1)
    def _():
        for e in range(N_EXPERTS):
            write(e, t, par).wait()


def _gather(bsel_i, afft, tau, need, beq_i, xn, u):
    tokens = xn.shape[0]
    cap = CAPACITY_FACTOR * tokens // N_EXPERTS
    ntile = tokens // ROUTE_TILE
    table = pl.BlockSpec((N_EXPERTS, LANES), lambda t, *_: (0, 0))
    grid_spec = pltpu.PrefetchScalarGridSpec(
        num_scalar_prefetch=1,
        grid=(ntile,),
        in_specs=[
            pl.BlockSpec((N_EXPERTS, ROUTE_TILE), lambda t, *_: (0, t)),
            table, table, table,
            pl.BlockSpec((ROUTE_TILE, D_MODEL), lambda t, *_: (t, 0)),
            pl.BlockSpec(u.shape, lambda t, *_: (0, 0)),
        ],
        out_specs=pl.BlockSpec(memory_space=pl.ANY),
        scratch_shapes=[
            pltpu.VMEM((2, N_EXPERTS, GATHER_BLOCK, D_MODEL), BF16),
            pltpu.VMEM((N_EXPERTS, ROW_ALIGN, D_MODEL), F32),
            pltpu.VMEM((SLOT_CHUNK, D_MODEL), BF16),
            pltpu.SemaphoreType.DMA((2,)),
            pltpu.SemaphoreType.DMA((1,)),
        ],
    )
    return pl.pallas_call(
        functools.partial(_gather_kernel, ntile=ntile, cap=cap),
        grid_spec=grid_spec,
        out_shape=jax.ShapeDtypeStruct((N_EXPERTS, cap + GATHER_PAD, D_MODEL), BF16),
        compiler_params=_params(("arbitrary",)),
        name="gather",
    )(bsel_i, afft, tau, need, beq_i, xn, u)


def _ffn_kernel(xe_ref, wg_ref, wu_ref, wd_ref, ye_ref, acc_ref, *, cap, nf, tm):
    f = pl.program_id(1)
    tf = wg_ref.shape[2]
    chunks = [slice(j * FFN_CHUNK, (j + 1) * FFN_CHUNK) for j in range(tf // FFN_CHUNK)]
    cast = {}

    def weight(name, ref, j):
        if (name, j) not in cast:
            cast[name, j] = (ref[0, chunks[j], :] if name == "d" else ref[0, :, chunks[j]]).astype(BF16)
        return cast[name, j]

    @pl.when(f == 0)
    def _():
        acc_ref[...] = jnp.zeros_like(acc_ref)

    for i in range(cap // tm):
        r = slice(i * tm, (i + 1) * tm)
        x = xe_ref[0, r, :]
        y = None
        for j in range(len(chunks)):
            hg = _dot(x, weight("g", wg_ref, j))
            hu = _dot(x, weight("u", wu_ref, j))
            h = (hg * (1.0 / (1.0 + jnp.exp(-hg))) * hu).astype(BF16)
            part = _dot(h, weight("d", wd_ref, j))
            y = part if y is None else y + part
        acc_ref[r, :] += y

    @pl.when(f == nf - 1)
    def _():
        ye_ref[...] = acc_ref[...].astype(BF16)


def _ffn(xe, w_eg, w_eu, w_ed):
    cap = xe.shape[1] - GATHER_PAD
    tf = 512
    nf = D_FF // tf
    tm = min(cap, 1024)
    return pl.pallas_call(
        functools.partial(_ffn_kernel, cap=cap, nf=nf, tm=tm),
        grid=(N_EXPERTS, nf),
        in_specs=[
            pl.BlockSpec((1, cap, D_MODEL), lambda e, f: (e, 0, 0)),
            pl.BlockSpec((1, D_MODEL, tf), lambda e, f: (e, 0, f)),
            pl.BlockSpec((1, D_MODEL, tf), lambda e, f: (e, 0, f)),
            pl.BlockSpec((1, tf, D_MODEL), lambda e, f: (e, f, 0)),
        ],
        out_specs=pl.BlockSpec((cap, D_MODEL), lambda e, f: (e, 0)),
        out_shape=jax.ShapeDtypeStruct((N_EXPERTS * cap, D_MODEL), BF16),
        scratch_shapes=[pltpu.VMEM((cap, D_MODEL), F32)],
        compiler_params=_params(("arbitrary", "arbitrary")),
        name="ffn",
    )(xe, w_eg, w_eu, w_ed)


def _combine_kernel(bsel_s, x1_ref, aff_ref, taut_ref, needt_ref, beqt_ref, bselt_ref, low_ref, spread_ref, gf_ref,
                    ye_hbm, y_ref, buf_ref, xbuf_ref, sem_ref, xsem_ref, *, cap, total, nstep):
    step = pl.program_id(0)
    par = step & 1
    per_tile = N_EXPERTS * SLOT_CHUNK

    def aligned(e, tile):
        return _floor_pow2(bsel_s[e, tile], ROW_ALIGN)

    def window(e, tile, c):
        start = jnp.minimum(e * cap + aligned(e, tile) + c * SLOT_CHUNK, total - SLOT_CHUNK)
        return pl.multiple_of(start, ROW_ALIGN)

    def first_chunks(stp, buf, sub):
        tile = stp * COMBINE_TILES + sub
        return [pltpu.make_async_copy(ye_hbm.at[pl.ds(window(e, tile, 0), SLOT_CHUNK)],
                                      buf_ref.at[buf, pl.ds(sub * per_tile + e * SLOT_CHUNK, SLOT_CHUNK)],
                                      sem_ref.at[buf])
                for e in range(N_EXPERTS)]

    @pl.when(step == 0)
    def _():
        for sub in range(COMBINE_TILES):
            for cp in first_chunks(0, 0, sub):
                cp.start()

    for sub in range(COMBINE_TILES):
        for cp in first_chunks(jnp.minimum(step + 1, nstep - 1), 1 - par, sub):
            cp.start()
    for sub in range(COMBINE_TILES):
        for cp in first_chunks(step, par, sub):
            cp.wait()

    tau = taut_ref[0:1, :]
    low = low_ref[...]
    spread = spread_ref[...]
    lane = lax.broadcasted_iota(I32, (1, LANES), 1)
    wide = lax.broadcasted_iota(I32, (ROUTE_TILE, per_tile), 1)
    in_chunk = (wide & (SLOT_CHUNK - 1)).astype(F32)
    slots, affs = [], []
    for sub in range(COMBINE_TILES):
        tile = step * COMBINE_TILES + sub
        rows = slice(sub * ROUTE_TILE, (sub + 1) * ROUTE_TILE)
        aff = aff_ref[rows, :]
        k = lax.bitcast_convert_type(aff, I32)
        eq = k == tau
        eq_cum = _dot(low, _ones_where(eq, BF16)) + beqt_ref[sub]
        sel = (k > tau) | (eq & (eq_cum <= needt_ref[0:1, :]))
        slot = jnp.where(sel, _dot(low, _ones_where(sel, BF16)) + (bselt_ref[sub] - 1.0), -1.0)

        rel = jnp.zeros((1, LANES), F32)
        for e in range(N_EXPERTS):
            rel = jnp.where(lane == e, (window(e, tile, 0) - e * cap).astype(F32), rel)
        d = slot - rel
        d = jnp.where(sel & (d >= 0.0) & (d < float(SLOT_CHUNK)), d, -1.0)
        hit = _dot(d.astype(BF16), spread) == in_chunk
        onehot_gate = jnp.where(hit, _dot(aff.astype(BF16), spread), 0.0).astype(BF16)
        y_ref[rows, :] = x1_ref[rows, :] + _dot(onehot_gate, buf_ref[par, sub * per_tile:(sub + 1) * per_tile, :])
        slots.append(slot)
        affs.append(aff)

    nch = [[_cdiv_pow2(bsel_s[e, step * COMBINE_TILES + sub + 1] - aligned(e, step * COMBINE_TILES + sub), SLOT_CHUNK)
            for e in range(N_EXPERTS)] for sub in range(COMBINE_TILES)]

    @pl.when(functools.reduce(jnp.maximum, [n for per_sub in nch for n in per_sub]) > 1)
    def _():
        lane64 = lax.broadcasted_iota(I32, (ROUTE_TILE, SLOT_CHUNK), 1).astype(F32)
        for sub in range(COMBINE_TILES):
            tile = step * COMBINE_TILES + sub
            rows = slice(sub * ROUTE_TILE, (sub + 1) * ROUTE_TILE)
            for e in range(N_EXPERTS):
                slot_e = slots[sub][:, e:e + 1]

                def extra(c, carry):
                    w = window(e, tile, c)
                    cp = pltpu.make_async_copy(ye_hbm.at[pl.ds(w, SLOT_CHUNK)], xbuf_ref, xsem_ref.at[0])
                    cp.start()
                    cp.wait()
                    first = (aligned(e, tile) + c * SLOT_CHUNK).astype(F32)
                    hit = (lane64 + (w - e * cap).astype(F32) == slot_e) & (slot_e >= first)
                    y_ref[rows, :] += affs[sub][:, e:e + 1] * _dot(_ones_where(hit, BF16), xbuf_ref[...])
                    return carry

                lax.fori_loop(1, nch[sub][e], extra, 0)

    acc = y_ref[...]
    ms = jnp.mean(acc * acc, axis=-1, keepdims=True)
    y_ref[...] = acc * lax.rsqrt(ms + EPS) * gf_ref[...]

    @pl.when(step == nstep - 1)
    def _():
        for sub in range(COMBINE_TILES):
            for cp in first_chunks(step, 1 - par, sub):
                cp.wait()


def _combine(bsel_i, x1, aff, tables_t, ye, gf):
    tokens = x1.shape[0]
    cap = CAPACITY_FACTOR * tokens // N_EXPERTS
    ntile = tokens // ROUTE_TILE
    idx = np.arange(ROUTE_TILE)
    low = jnp.asarray(idx[:, None] >= idx[None, :], BF16)
    spread = jnp.asarray(np.arange(LANES)[:, None] == np.arange(N_EXPERTS * SLOT_CHUNK)[None, :] // SLOT_CHUNK, BF16)
    taut, needt, beqt, bselt = tables_t
    rows = COMBINE_TILES * ROUTE_TILE
    rowvec = pl.BlockSpec((8, LANES), lambda t, *_: (0, 0))
    tilevec = pl.BlockSpec((COMBINE_TILES, 1, LANES), lambda t, *_: (t, 0, 0))
    grid_spec = pltpu.PrefetchScalarGridSpec(
        num_scalar_prefetch=1,
        grid=(ntile // COMBINE_TILES,),
        in_specs=[
            pl.BlockSpec((rows, D_MODEL), lambda t, *_: (t, 0)),
            pl.BlockSpec((rows, LANES), lambda t, *_: (t, 0)),
            rowvec, rowvec, tilevec, tilevec,
            pl.BlockSpec(low.shape, lambda t, *_: (0, 0)),
            pl.BlockSpec(spread.shape, lambda t, *_: (0, 0)),
            pl.BlockSpec((1, D_MODEL), lambda t, *_: (0, 0)),
            pl.BlockSpec(memory_space=pl.ANY),
        ],
        out_specs=pl.BlockSpec((rows, D_MODEL), lambda t, *_: (t, 0)),
        scratch_shapes=[
            pltpu.VMEM((2, COMBINE_TILES * N_EXPERTS * SLOT_CHUNK, D_MODEL), BF16),
            pltpu.VMEM((SLOT_CHUNK, D_MODEL), BF16),
            pltpu.SemaphoreType.DMA((2,)),
            pltpu.SemaphoreType.DMA((1,)),
        ],
    )
    return pl.pallas_call(
        functools.partial(_combine_kernel, cap=cap, total=N_EXPERTS * cap, nstep=ntile // COMBINE_TILES),
        grid_spec=grid_spec,
        out_shape=jax.ShapeDtypeStruct((tokens, D_MODEL), F32),
        compiler_params=_params(("arbitrary",)),
        name="combine",
    )(bsel_i, x1, aff, taut, needt, beqt.reshape(LANES, 1, LANES), bselt.reshape(LANES, 1, LANES), low, spread,
      gf, ye)


def _encoder(x, w):
    batch, seq, width = x.shape
    tokens = batch * seq
    max_dil = max(dil for _, dil in GROUPS)
    assert width == D_MODEL and x.dtype == F32
    assert seq % (ATT_SUB * max_dil) == 0 and seq % (LANES * ROW_ALIGN) == 0 and seq % TOKEN_TILE == 0
    assert LANES % min(LANES, FFT_STEP_ROWS // (seq // LANES)) == 0
    assert tokens % (ROUTE_TILE * COMBINE_TILES * 8) == 0 and tokens // ROUTE_TILE < LANES
    assert (CAPACITY_FACTOR * tokens) % (N_EXPERTS * ROW_ALIGN) == 0
    xt = x.reshape(tokens, D_MODEL)
    *qkvs, vr, vi, gates = _in_proj(xt, w["g1"], w["w_in"], w["w_gate"], w["b_gate"], w["cs"], batch, seq)
    outs, lses = [], []
    for g in range(N_GROUPS):
        o, lse = _attention(qkvs[g], w["bias"][g], g)
        outs.append(o)
        lses.append(lse)
    four = _fourier(vr, vi, batch, seq)
    x1, xn, aff, afft = _mix(xt, outs, lses, four, gates, w["w_attn"], w["w_four"], w["w_out"], w["g2"],
                             w["w_router"])
    tau, need, beq_i, bsel_i, taut, needt, beqt, bselt = _route(afft)
    idx = np.arange(ROUTE_TILE)
    u = jnp.asarray(idx[:, None] <= idx[None, :], BF16)
    xe = _gather(bsel_i, afft, tau, need, beq_i, xn, u)
    ye = _ffn(xe, w["w_eg"], w["w_eu"], w["w_ed"])
    y = _combine(bsel_i, x1, aff, (taut, needt, beqt, bselt), ye, w["gf"])
    return y.reshape(batch, seq, D_MODEL)


def _prepare_weights(rel_bias, norm1_g, w_in, w_attn_br, w_four_br, w_gate, b_gate, w_out,
                     norm2_g, w_router, w_exp_gate, w_exp_up, w_exp_down, final_g):
    c, s = _dft_mats(F_CH)
    qkv_cols = w_in[0][:, :QKV_W].reshape(D_MODEL, 3, N_GROUPS, GROUP_W).transpose(0, 2, 1, 3)
    w_in_grouped = jnp.concatenate([qkv_cols.reshape(D_MODEL, QKV_W), w_in[0][:, QKV_W:]], axis=1)
    w_router = jnp.pad(w_router[0], ((0, 0), (0, LANES - N_EXPERTS)))
    w_router_hi = w_router.astype(BF16)
    return {
        "g1": norm1_g[0].reshape(1, D_MODEL),
        "w_in": w_in_grouped.astype(BF16),
        "w_gate": w_gate[0].astype(BF16),
        "b_gate": b_gate[0].reshape(1, 2 * D_MODEL),
        "cs": jnp.asarray(np.concatenate([c, s], axis=1), BF16),
        "bias": [_attention_bias(rel_bias, g) for g in range(N_GROUPS)],
        "w_attn": w_attn_br[0].astype(BF16),
        "w_four": w_four_br[0].astype(BF16),
        "w_out": w_out[0].astype(BF16),
        "g2": norm2_g[0].reshape(1, D_MODEL),
        "w_router": jnp.concatenate([w_router_hi, (w_router - w_router_hi.astype(F32)).astype(BF16)], axis=1),
        "w_eg": w_exp_gate[0],
        "w_eu": w_exp_up[0],
        "w_ed": w_exp_down[0],
        "gf": final_g.reshape(1, D_MODEL),
    }


def kernel(x_prompt, x_sample, rel_bias, norm1_g, w_in, w_attn_br, w_four_br, w_gate, b_gate, w_out,
           norm2_g, w_router, w_exp_gate, w_exp_up, w_exp_down, final_g):
    w = _prepare_weights(rel_bias, norm1_g, w_in, w_attn_br, w_four_br, w_gate, b_gate, w_out,
                         norm2_g, w_router, w_exp_gate, w_exp_up, w_exp_down, final_g)
    return (_encoder(x_prompt, w), _encoder(x_sample, w))
```

```python
import functools
import math

import numpy as np
import jax
import jax.numpy as jnp
from jax import lax
from jax.experimental import pallas as pl
from jax.experimental.pallas import tpu as pltpu

D_MODEL = 1024
HEAD_DIM = 64
HEADS_PER_GROUP = 4
GROUPS = ((128, 1), (512, 4), (2048, 16))
N_GROUPS = len(GROUPS)
GROUP_W = HEADS_PER_GROUP * HEAD_DIM
ATT_W = N_GROUPS * GROUP_W
QKV_W = 3 * ATT_W
F_GROUPS = 6
F_CH = 128
F_W = F_GROUPS * F_CH
NUM_BUCKETS = 32
MAX_DISTANCE = 1024
N_EXPERTS = 16
CAPACITY_FACTOR = 2
D_FF = 2048
EPS = 1e-6
NEG = -1e30

HALF_KEYS = 64
ATT_SUB = 128
ATT_STEP_ROWS = 2048
ATT_OUT_ROWS = 8192
TOKEN_TILE = 512
ROUTE_TILE = 256
FFT_STEP_ROWS = 1024
FFN_CHUNK = 256
SLOT_CHUNK = 64
COMBINE_TILES = 4
ROW_ALIGN = 16
GATHER_BLOCK = SLOT_CHUNK + ROW_ALIGN
GATHER_STACK = GATHER_BLOCK + ROW_ALIGN
GATHER_PAD = GATHER_BLOCK
LANES = 128
V7X_VMEM_LIMIT = 56 * 1024 * 1024

F32 = jnp.float32
BF16 = jnp.bfloat16
I32 = jnp.int32
U32 = jnp.uint32


def _params(sem):
    return pltpu.CompilerParams(dimension_semantics=sem, vmem_limit_bytes=V7X_VMEM_LIMIT)


def _dot(a, b):
    return jnp.dot(a, b, preferred_element_type=F32)


def _dot_nt(a, b):
    return lax.dot_general(a, b, (((1,), (1,)), ((), ())), preferred_element_type=F32)


def _floor_pow2(x, m):
    return x & ~(m - 1)


def _cdiv_pow2(x, m):
    return (x + (m - 1)) >> (m.bit_length() - 1)


def _ones_where(mask, dtype=F32):
    return jnp.where(mask, jnp.ones((), F32), jnp.zeros((), F32)).astype(dtype)


def _in_proj_kernel(x_ref, g_ref, win_ref, wg_ref, bg_ref, cs_ref, qscale_ref, qkv0_ref, qkv1_ref, qkv2_ref,
                    vr_ref, vi_ref, gates_ref, slab_ref):
    tm = x_ref.shape[0]
    half = tm // 2
    nslab = ATT_W // LANES
    cs = cs_ref[...]
    for h in range(2):
        rows = slice(h * half, (h + 1) * half)
        x = x_ref[rows, :]
        ms = jnp.mean(x * x, axis=-1, keepdims=True)
        xn = (x * lax.rsqrt(ms + EPS) * g_ref[...]).astype(BF16)
        for g, out_ref in enumerate((qkv0_ref, qkv1_ref, qkv2_ref)):
            dil = GROUPS[g][1]
            res = _dot(xn, win_ref[:, g * ATT_W:(g + 1) * ATT_W]) * qscale_ref[...]
            if dil == 1:
                out_ref[0, 0, rows, :] = res.astype(BF16)
                continue
            for j in range(nslab):
                slab_ref[j, rows, :] = res[:, j * LANES:(j + 1) * LANES]
            n = half // dil
            for r in range(dil):
                cls = [slab_ref[j, pl.ds(h * half + r, n, stride=dil), :] for j in range(nslab)]
                out_ref[0, r, h * n:(h + 1) * n, :] = jnp.concatenate(cls, axis=1).astype(BF16)
        u = _dot(xn, win_ref[:, QKV_W:QKV_W + F_W]).astype(BF16)
        for g in range(F_GROUPS):
            a = _dot(u[:, g * F_CH:(g + 1) * F_CH], cs)
            vr_ref[rows, g * F_CH:(g + 1) * F_CH] = a[:, :F_CH].astype(BF16)
            vi_ref[rows, g * F_CH:(g + 1) * F_CH] = (-a[:, F_CH:]).astype(BF16)
        z = _dot(xn, wg_ref[...]) + bg_ref[...]
        gates_ref[rows, :] = (1.0 / (1.0 + jnp.exp(-z))).astype(BF16)


def _class_major_spec(tm, dil, width, per_batch):
    return pl.BlockSpec((1, dil, tm // dil, width), lambda i: (i // per_batch, 0, i % per_batch, 0))


def _in_proj(x, g1, w_in, w_gate, b_gate, cs, batch, seq):
    t = x.shape[0]
    tm = TOKEN_TILE
    per_batch = seq // tm
    const = lambda i: (0, 0)
    row = lambda i: (i, 0)
    qscale = np.ones((1, ATT_W), np.float32)
    qscale[:, :GROUP_W] = 1.0 / math.sqrt(HEAD_DIM)
    return pl.pallas_call(
        _in_proj_kernel,
        grid=(t // tm,),
        in_specs=[
            pl.BlockSpec((tm, D_MODEL), row),
            pl.BlockSpec((1, D_MODEL), const),
            pl.BlockSpec(w_in.shape, const),
            pl.BlockSpec(w_gate.shape, const),
            pl.BlockSpec((1, 2 * D_MODEL), const),
            pl.BlockSpec(cs.shape, const),
            pl.BlockSpec((1, ATT_W), const),
        ],
        out_specs=[_class_major_spec(tm, dil, ATT_W, per_batch) for _, dil in GROUPS] + [
            pl.BlockSpec((tm, F_W), row),
            pl.BlockSpec((tm, F_W), row),
            pl.BlockSpec((tm, 2 * D_MODEL), row),
        ],
        out_shape=[jax.ShapeDtypeStruct((batch, dil, seq // dil, ATT_W), BF16) for _, dil in GROUPS] + [
            jax.ShapeDtypeStruct((t, F_W), BF16),
            jax.ShapeDtypeStruct((t, F_W), BF16),
            jax.ShapeDtypeStruct((t, 2 * D_MODEL), BF16),
        ],
        scratch_shapes=[pltpu.VMEM((ATT_W // LANES, tm, LANES), F32)],
        compiler_params=_params(("parallel",)),
        name="in_proj",
    )(x, g1, w_in, w_gate, b_gate, cs, jnp.asarray(qscale))


def _attention_kernel(q_ref, kp_ref, kc_ref, kn_ref, vp_ref, vc_ref, vn_ref, bias_ref, o_ref, lse_ref, *,
                      tq, length, dil, rc):
    i = pl.program_id(1)
    win = ATT_SUB + 2 * HALF_KEYS
    nsub = tq // ATT_SUB
    lane_head = lax.broadcasted_iota(I32, (ATT_SUB, GROUP_W), 1) // HEAD_DIM
    lane_slot = lax.broadcasted_iota(I32, (ATT_SUB, LANES), 1) // (LANES // HEADS_PER_GROUP)
    at_start = (i == 0).astype(I32)
    at_end = (i == length // tq - 1).astype(I32) * 2
    for c, sb in [(c, sb) for c in range(rc) for sb in range(nsub)]:
        r = pl.program_id(2) * rc + c
        if sb == 0:
            kwin = jnp.concatenate([kp_ref[0, c], kc_ref[0, c], kn_ref[0, c]], axis=0)
            vwin = jnp.concatenate([vp_ref[0, c], vc_ref[0, c], vn_ref[0, c]], axis=0)
        off = sb * ATT_SUB
        q = q_ref[0, c, off:off + ATT_SUB, :]
        kw = kwin[off:off + win]
        vw = vwin[off:off + win]
        variant = (at_start if sb == 0 else 0) + (at_end if sb == nsub - 1 else 0)
        qs = jnp.concatenate(
            [jnp.where(lane_head == h, q, jnp.zeros_like(q)) for h in range(HEADS_PER_GROUP)], axis=0)
        s_all = _dot_nt(qs, kw)
        ps, ms, ls = [], [], []
        for h in range(HEADS_PER_GROUP):
            s = s_all[h * ATT_SUB:(h + 1) * ATT_SUB] + bias_ref[variant, h]
            m = jnp.max(s, axis=-1, keepdims=True)
            p = jnp.exp(s - m)
            ls.append(jnp.sum(p, axis=-1, keepdims=True))
            ms.append(m)
            ps.append(p.astype(BF16))
        o_all = _dot(jnp.concatenate(ps, axis=0), vw)
        out = jnp.zeros((ATT_SUB, GROUP_W), F32)
        lse = jnp.zeros((ATT_SUB, LANES), F32)
        for h in range(HEADS_PER_GROUP):
            oh = o_all[h * ATT_SUB:(h + 1) * ATT_SUB] * (1.0 / ls[h])
            out = jnp.where(lane_head == h, oh, out)
            lse = jnp.where(lane_slot == h, ms[h] + jnp.log(ls[h]), lse)
        rows = pl.ds(off * dil + r, ATT_SUB, stride=dil) if dil > 1 else pl.ds(off, ATT_SUB)
        o_ref[rows, :] = _pack_bf16_pair(out[:, :LANES], out[:, LANES:])
        lse_ref[rows, :] = lse


def _attention(qkv, bias, g):
    batch, dil, length, _ = qkv.shape
    tq = min(length, ATT_STEP_ROWS, ATT_OUT_ROWS // dil)
    nb = length // tq
    hb = tq // HALF_KEYS
    last_halo = length // HALF_KEYS - 1
    rc = min(dil, max(1, ATT_STEP_ROWS // tq))

    def cur(c):
        return lambda b, i, r: (b, r, i, c)

    def prev(c):
        return lambda b, i, r: (b, r, jnp.maximum(i * hb - 1, 0), c)

    def nxt(c):
        return lambda b, i, r: (b, r, jnp.minimum((i + 1) * hb, last_halo), c)

    blk = lambda rows: (1, rc, rows, GROUP_W)
    out_spec = pl.BlockSpec((tq * dil, LANES), lambda b, i, r: (b * nb + i, 0))
    return pl.pallas_call(
        functools.partial(_attention_kernel, tq=tq, length=length, dil=dil, rc=rc),
        grid=(batch, nb, dil // rc),
        in_specs=[
            pl.BlockSpec(blk(tq), cur(0)),
            pl.BlockSpec(blk(HALF_KEYS), prev(1)),
            pl.BlockSpec(blk(tq), cur(1)),
            pl.BlockSpec(blk(HALF_KEYS), nxt(1)),
            pl.BlockSpec(blk(HALF_KEYS), prev(2)),
            pl.BlockSpec(blk(tq), cur(2)),
            pl.BlockSpec(blk(HALF_KEYS), nxt(2)),
            pl.BlockSpec(bias.shape, lambda b, i, r: (0, 0, 0, 0)),
        ],
        out_specs=[out_spec, out_spec],
        out_shape=[jax.ShapeDtypeStruct((batch * dil * length, LANES), U32),
                   jax.ShapeDtypeStruct((batch * dil * length, LANES), F32)],
        compiler_params=_params(("parallel", "parallel", "arbitrary")),
        name=f"attention_g{g}",
    )(qkv, qkv, qkv, qkv, qkv, qkv, qkv, bias)


def _t5_bucket(rel):
    nb = NUM_BUCKETS // 2
    max_exact = nb // 2
    ret = (rel > 0).astype(np.int32) * nb
    n = np.abs(rel)
    large = max_exact + (np.log(np.maximum(n, max_exact) / max_exact)
                         / np.log(MAX_DISTANCE / max_exact) * (nb - max_exact)).astype(np.int32)
    large = np.minimum(large, nb - 1)
    return (ret + np.where(n < max_exact, n, large)).astype(np.int32)


def _attention_bias(rel_bias, g):
    dil = GROUPS[g][1]
    qi = np.arange(ATT_SUB)[:, None]
    kj = np.arange(ATT_SUB + 2 * HALF_KEYS)[None, :]
    delta = kj - HALF_KEYS - qi
    band = np.abs(delta) <= HALF_KEYS
    bucket = _t5_bucket(dil * delta)
    tab = rel_bias[:, g * HEADS_PER_GROUP:(g + 1) * HEADS_PER_GROUP].astype(F32)
    onehot = jnp.asarray(bucket[..., None] == np.arange(NUM_BUCKETS), F32)
    bias = jnp.einsum("qkb,bh->hqk", onehot, tab, precision=lax.Precision.HIGHEST)
    masks = [band & ((kj >= HALF_KEYS) | ((v & 1) == 0)) & ((kj < ATT_SUB + HALF_KEYS) | ((v & 2) == 0))
             for v in range(4)]
    return jnp.where(jnp.asarray(np.stack(masks))[:, None], bias[None], NEG)


def _dft_mats(n):
    k = np.arange(n)
    ang = 2.0 * np.pi * ((k[:, None] * k[None, :]) % n) / n
    return np.cos(ang), np.sin(ang)


def _pack_bf16_pair(a, b):
    hi = lax.bitcast_convert_type(a.astype(BF16).astype(F32), U32)
    lo = lax.bitcast_convert_type(b.astype(BF16).astype(F32), U32)
    return hi | lax.shift_right_logical(lo, jnp.full(lo.shape, 16, U32))


def _unpack_bf16_pair(word):
    a = lax.bitcast_convert_type(word & jnp.uint32(0xFFFF0000), F32)
    b = lax.bitcast_convert_type(lax.shift_left(word, jnp.full(word.shape, 16, U32)), F32)
    return a.astype(BF16), b.astype(BF16)


def _fft_stage1_kernel(vr_ref, vi_ref, m1_ref, twc_ref, tws_ref, z_ref, *, n1, m):
    x = jnp.concatenate([vr_ref[0], vi_ref[0]], axis=0)
    z = _dot(m1_ref[...], x)
    zr, zi = z[:n1], z[n1:]
    twc, tws = twc_ref[0], tws_ref[0]
    for j in range(m):
        c = twc[:, j:j + 1]
        s = tws[:, j:j + 1]
        a = zr[:, j * F_W:(j + 1) * F_W]
        b = zi[:, j * F_W:(j + 1) * F_W]
        z_ref[0, :, j, :] = _pack_bf16_pair(a * c + b * s, b * c - a * s)


def _fft_stage2_kernel(z_ref, m2_ref, o_ref, *, kc, scale):
    m2 = m2_ref[...]
    for j in range(kc):
        x = jnp.concatenate(_unpack_bf16_pair(z_ref[0, j]), axis=0)
        y = _dot(m2, x) * scale
        o_ref[0, :, j, :] = _pack_bf16_pair(y[:, :F_W // 2], y[:, F_W // 2:])


def _fourier(vr, vi, batch, seq):
    n2 = LANES
    n1 = seq // n2
    m = min(n2, FFT_STEP_ROWS // n1)
    c1, s1 = _dft_mats(n1)
    m1 = jnp.asarray(np.block([[c1, s1], [-s1, c1]]), BF16)
    c2, s2 = _dft_mats(n2)
    m2 = jnp.asarray(np.concatenate([c2, s2], axis=1), BF16)
    k1 = np.arange(n1)[:, None]
    sv = np.arange(n2)[None, :]
    ang = 2.0 * np.pi * ((k1 * sv) % seq) / seq
    to_blocks = lambda a: jnp.asarray(a.reshape(n1, n2 // m, m).transpose(1, 0, 2), F32)
    twc, tws = to_blocks(np.cos(ang)), to_blocks(np.sin(ang))

    v3 = lambda a: a.reshape(batch, n1, n2 * F_W)
    blk = pl.BlockSpec((1, n1, m * F_W), lambda b, j: (b, 0, j))
    tmap = lambda b, j: (j, 0, 0)
    z = pl.pallas_call(
        functools.partial(_fft_stage1_kernel, n1=n1, m=m),
        grid=(batch, n2 // m),
        in_specs=[
            blk,
            blk,
            pl.BlockSpec(m1.shape, lambda b, j: (0, 0)),
            pl.BlockSpec((1, n1, m), tmap),
            pl.BlockSpec((1, n1, m), tmap),
        ],
        out_specs=pl.BlockSpec((1, n1, m, F_W), lambda b, j: (b, 0, j, 0)),
        out_shape=jax.ShapeDtypeStruct((batch, n1, n2, F_W), U32),
        compiler_params=_params(("parallel", "parallel")),
        name="fft_stage1",
    )(v3(vr), v3(vi), m1, twc, tws)

    kc = min(n1, FFT_STEP_ROWS // n2 * 2)
    out = pl.pallas_call(
        functools.partial(_fft_stage2_kernel, kc=kc, scale=1.0 / math.sqrt(seq * F_CH)),
        grid=(batch, n1 // kc),
        in_specs=[
            pl.BlockSpec((1, kc, n2, F_W), lambda b, j: (b, j, 0, 0)),
            pl.BlockSpec(m2.shape, lambda b, j: (0, 0)),
        ],
        out_specs=pl.BlockSpec((1, n2, kc, F_W // 2), lambda b, j: (b, 0, j, 0)),
        out_shape=jax.ShapeDtypeStruct((batch, n2, n1, F_W // 2), U32),
        compiler_params=_params(("parallel", "parallel")),
        name="fft_stage2",
    )(z, m2)
    return out.reshape(batch * seq, F_W // 2)


def _mix_kernel(x_ref, o0_ref, o1_ref, o2_ref, l0_ref, l1_ref, l2_ref, four_ref, gates_ref,
                wa_ref, wf_ref, wo_ref, g2_ref, wr_ref, hs_ref, x1_ref, xn_ref, aff_ref, afft_ref):
    tm = x_ref.shape[0]
    half = tm // 2
    for rows in (slice(0, half), slice(half, tm)):
        f_br = _dot(jnp.concatenate(_unpack_bf16_pair(four_ref[rows, :]), axis=1), wf_ref[...])
        l0, l1, l2 = l0_ref[rows, :], l1_ref[rows, :], l2_ref[rows, :]
        mx = jnp.maximum(jnp.maximum(l0, l1), l2)
        e0, e1, e2 = jnp.exp(l0 - mx), jnp.exp(l1 - mx), jnp.exp(l2 - mx)
        inv = 1.0 / (e0 + e1 + e2)
        w0, w1, w2 = (_dot((e * inv).astype(BF16), hs_ref[...]) for e in (e0, e1, e2))
        o0, o1, o2 = (jnp.concatenate(_unpack_bf16_pair(ref[rows, :]), axis=1) for ref in (o0_ref, o1_ref, o2_ref))
        att = w0 * o0 + w1 * o1 + w2 * o2
        a_br = _dot(att.astype(BF16), wa_ref[...])
        mix = gates_ref[rows, :D_MODEL] * a_br + gates_ref[rows, D_MODEL:] * f_br
        x1 = x_ref[rows, :] + _dot(mix.astype(BF16), wo_ref[...])
        x1_ref[rows, :] = x1
        ms = jnp.mean(x1 * x1, axis=-1, keepdims=True)
        xn = x1 * lax.rsqrt(ms + EPS) * g2_ref[...]
        xn_ref[rows, :] = xn.astype(BF16)
        xh = xn.astype(BF16)
        xl = (xn - xh.astype(F32)).astype(BF16)
        both = _dot(xh, wr_ref[...])
        logits = both[:, :LANES] + (both[:, LANES:] + _dot(xl, wr_ref[:, :LANES]))
        lane = lax.broadcasted_iota(I32, logits.shape, 1)
        logits = jnp.where(lane < N_EXPERTS, logits, NEG)
        p = jnp.exp(logits - jnp.max(logits, axis=-1, keepdims=True))
        aff = p * (1.0 / jnp.sum(p, axis=-1, keepdims=True))
        aff_ref[rows, :] = aff
        afft_ref[:, rows] = aff.T[:N_EXPERTS]


def _mix(x, os_, ls_, four, gates, w_attn, w_four, w_out, g2, w_router):
    t = x.shape[0]
    tm = TOKEN_TILE
    const = lambda i: (0, 0)
    row = lambda i: (i, 0)
    rows = lambda w: pl.BlockSpec((tm, w), row)
    full = lambda a: pl.BlockSpec(a.shape, const)
    slot = LANES // HEADS_PER_GROUP
    head_spread = jnp.asarray(np.arange(LANES)[:, None] == slot * (np.arange(GROUP_W)[None, :] // HEAD_DIM), BF16)
    return pl.pallas_call(
        _mix_kernel,
        grid=(t // tm,),
        in_specs=[rows(D_MODEL)] + [rows(LANES)] * 6 + [rows(F_W // 2), rows(2 * D_MODEL),
                  full(w_attn), full(w_four), full(w_out), full(g2), full(w_router), full(head_spread)],
        out_specs=[rows(D_MODEL), rows(D_MODEL), rows(LANES), pl.BlockSpec((N_EXPERTS, tm), lambda i: (0, i))],
        out_shape=[
            jax.ShapeDtypeStruct((t, D_MODEL), F32),
            jax.ShapeDtypeStruct((t, D_MODEL), BF16),
            jax.ShapeDtypeStruct((t, LANES), F32),
            jax.ShapeDtypeStruct((N_EXPERTS, t), F32),
        ],
        compiler_params=_params(("parallel",)),
        name="mix",
    )(x, *os_, *ls_, four, gates, w_attn, w_four, w_out, g2, w_router, head_spread)


def _route_kernel(afft_ref, su_ref, u_ref, tau_ref, need_ref, beq_ref, bsel_ref,
                  taut_ref, needt_ref, beqt_ref, bselt_ref, *, tokens):
    cap = CAPACITY_FACTOR * tokens // N_EXPERTS
    ntile = tokens // ROUTE_TILE
    shape = (N_EXPERTS, LANES)
    lane = lax.broadcasted_iota(I32, shape, 1)

    def keys(start, width):
        return lax.bitcast_convert_type(afft_ref[:, pl.ds(pl.multiple_of(start, LANES), width)], I32)

    span = min(tokens, 16 * LANES)

    def count(pred):
        def body(c, acc):
            hits = _ones_where(pred(keys(c * span, span)))
            for j in range(span // LANES):
                acc = acc + hits[:, j * LANES:(j + 1) * LANES]
            return acc
        acc = lax.fori_loop(0, tokens // span, body, jnp.zeros(shape, F32), unroll=True)
        return jnp.sum(acc, axis=1, keepdims=True)

    def bit_body(i, prefix):
        cand = prefix | lax.shift_left(jnp.ones(shape, I32), jnp.full(shape, 30 - i, I32))
        tot = count(lambda k: k >= cand[:, :1])
        return jnp.where(tot >= cap, cand, prefix)

    tau = lax.fori_loop(0, 31, bit_body, jnp.zeros(shape, I32))
    tau_col = tau[:, :1]
    n_gt = count(lambda k: k > tau_col)
    need = cap - n_gt

    def prefix_over_tiles(tab):
        return _dot(tab.astype(BF16), su_ref[...])

    def at_lane(tab, c):
        return jnp.sum(jnp.where(lane == c, tab, 0.0), axis=1, keepdims=True)

    def eq_body(c, tab):
        k = keys(c * ROUTE_TILE, ROUTE_TILE)
        cnt = jnp.sum(_ones_where(k == tau_col), axis=1, keepdims=True)
        return jnp.where(lane == c, cnt, tab)

    unroll = 8 if ntile % 8 == 0 else 1
    base_eq = prefix_over_tiles(lax.fori_loop(0, ntile, eq_body, jnp.zeros(shape, F32), unroll=unroll))

    def sel_body(c, tab):
        k = keys(c * ROUTE_TILE, ROUTE_TILE)
        eq = k == tau_col
        eq_cum = _dot(_ones_where(eq, BF16), u_ref[...]) + at_lane(base_eq, c)
        sel = (k > tau_col) | (eq & (eq_cum <= need))
        cnt = jnp.sum(_ones_where(sel), axis=1, keepdims=True)
        return jnp.where(lane == c, cnt, tab)

    base_sel = prefix_over_tiles(lax.fori_loop(0, ntile, sel_body, jnp.zeros(shape, F32), unroll=unroll))

    def transposed(val):
        return jnp.concatenate([val, jnp.zeros((LANES - N_EXPERTS, LANES), val.dtype)], axis=0).T

    tau_ref[...] = tau
    taut_ref[...] = transposed(tau)
    for val, ref, ref_t in ((jnp.broadcast_to(need, shape), need_ref, needt_ref),
                            (base_eq, beq_ref, beqt_ref), (base_sel, bsel_ref, bselt_ref)):
        ref[...] = val.astype(I32)
        ref_t[...] = transposed(val)


def _route(afft):
    tokens = afft.shape[1]
    idx = np.arange(LANES)
    su = jnp.asarray(idx[:, None] < idx[None, :], BF16)
    idx = np.arange(ROUTE_TILE)
    u = jnp.asarray(idx[:, None] <= idx[None, :], BF16)
    full = lambda a: pl.BlockSpec(a.shape, lambda i: (0,) * a.ndim)
    small = pl.BlockSpec((N_EXPERTS, LANES), lambda i: (0, 0))
    smallt = pl.BlockSpec((LANES, LANES), lambda i: (0, 0))
    return pl.pallas_call(
        functools.partial(_route_kernel, tokens=tokens),
        grid=(1,),
        in_specs=[full(afft), full(su), full(u)],
        out_specs=[small] * 4 + [smallt] * 4,
        out_shape=[jax.ShapeDtypeStruct((N_EXPERTS, LANES), I32)] * 4
        + [jax.ShapeDtypeStruct((LANES, LANES), I32)] + [jax.ShapeDtypeStruct((LANES, LANES), F32)] * 3,
        compiler_params=_params(("arbitrary",)),
        name="route",
    )(afft, su, u)


def _gather_kernel(bsel_s, afft_ref, tau_ref, need_ref, beq_ref, x_ref, u_ref, xe_hbm,
                   stage_ref, tail_ref, xbuf_ref, sem_ref, xsem_ref, *, ntile, cap):
    t = pl.program_id(0)
    par = t & 1

    def aligned(e, tile):
        return pl.multiple_of(_floor_pow2(bsel_s[e, tile], ROW_ALIGN), ROW_ALIGN)

    def write(e, tile, buf, first_row=None):
        first_row = aligned(e, tile) if first_row is None else first_row
        return pltpu.make_async_copy(stage_ref.at[buf, e], xe_hbm.at[e, pl.ds(first_row, GATHER_BLOCK)],
                                     sem_ref.at[buf])

    @pl.when(t == 0)
    def _():
        tail_ref[...] = jnp.zeros_like(tail_ref)
        stage_ref[1] = jnp.zeros(stage_ref.shape[1:], BF16)
        for e in range(N_EXPERTS):
            write(e, 0, 1, first_row=cap).start()

    k = lax.bitcast_convert_type(afft_ref[...], I32)
    tau = tau_ref[:, :1]
    lane = lax.broadcasted_iota(I32, (N_EXPERTS, LANES), 1)
    beq = jnp.sum(jnp.where(lane == t, beq_ref[...].astype(F32), 0.0), axis=1, keepdims=True)
    eq = k == tau
    eq_cum = _dot(_ones_where(eq, BF16), u_ref[...]) + beq
    sel = (k > tau) | (eq & (eq_cum <= need_ref[:, :1].astype(F32)))
    rank = jnp.where(sel, _dot(_ones_where(sel, BF16), u_ref[...]) - 1.0, -1e4)

    row = lax.broadcasted_iota(I32, (GATHER_STACK, ROUTE_TILE), 0)
    in_block = row < GATHER_BLOCK
    row_f = row.astype(F32)
    offs, shifts, pieces = [], [], []
    for e in range(N_EXPERTS):
        off = (bsel_s[e, t] - aligned(e, t)).astype(F32)
        shift = _floor_pow2(bsel_s[e, t + 1], ROW_ALIGN) - aligned(e, t)
        target = jnp.where(in_block, row_f, row_f - float(GATHER_BLOCK) + shift.astype(F32))
        pieces.append(_ones_where(rank[e:e + 1, :] + off == target, BF16))
        offs.append(off)
        shifts.append(shift)
    res = _dot(jnp.concatenate(pieces, axis=0), x_ref[...])
    for e in range(N_EXPERTS):
        base = e * GATHER_STACK
        old = tail_ref[e]
        stage_ref[par, e, 0:ROW_ALIGN, :] = (res[base:base + ROW_ALIGN] + old).astype(BF16)
        stage_ref[par, e, ROW_ALIGN:GATHER_BLOCK, :] = res[base + ROW_ALIGN:base + GATHER_BLOCK].astype(BF16)
        tail_ref[e] = res[base + GATHER_BLOCK:base + GATHER_STACK] + jnp.where(shifts[e] == 0, old, 0.0)
    for e in range(N_EXPERTS):
        write(e, jnp.maximum(t - 1, 0), 1 - par).wait()
    for e in range(N_EXPERTS):
        write(e, t, par).start()

    extra = [_cdiv_pow2(jnp.maximum(bsel_s[e, t + 1] - aligned(e, t) - GATHER_BLOCK, 0), SLOT_CHUNK)
             for e in range(N_EXPERTS)]

    @pl.when(functools.reduce(jnp.maximum, extra) > 0)
    def _():
        row64 = lax.broadcasted_iota(I32, (SLOT_CHUNK, ROUTE_TILE), 0).astype(F32)
        for e in range(N_EXPERTS):
            def chunk(c, carry):
                first = GATHER_BLOCK + c * SLOT_CHUNK
                onehot = _ones_where(rank[e:e + 1, :] + offs[e] == row64 + first.astype(F32), BF16)
                xbuf_ref[...] = _dot(onehot, x_ref[...]).astype(BF16)
                dst = pl.multiple_of(aligned(e, t) + first, ROW_ALIGN)
                cp = pltpu.make_async_copy(xbuf_ref, xe_hbm.at[e, pl.ds(dst, SLOT_CHUNK)], xsem_ref.at[0])
                cp.start()
                cp.wait()
                return carry

            lax.fori_loop(0, extra[e], chunk, 0)

    @pl.when(t == ntile - 1)
    def _():
        for e in range(N_EXPERTS):
            write(e, t, par).wait()


def _gather(bsel_i, afft, tau, need, beq_i, xn, u):
    tokens = xn.shape[0]
    cap = CAPACITY_FACTOR * tokens // N_EXPERTS
    ntile = tokens // ROUTE_TILE
    table = pl.BlockSpec((N_EXPERTS, LANES), lambda t, *_: (0, 0))
    grid_spec = pltpu.PrefetchScalarGridSpec(
        num_scalar_prefetch=1,
        grid=(ntile,),
        in_specs=[
            pl.BlockSpec((N_EXPERTS, ROUTE_TILE), lambda t, *_: (0, t)),
            table, table, table,
            pl.BlockSpec((ROUTE_TILE, D_MODEL), lambda t, *_: (t, 0)),
            pl.BlockSpec(u.shape, lambda t, *_: (0, 0)),
        ],
        out_specs=pl.BlockSpec(memory_space=pl.ANY),
        scratch_shapes=[
            pltpu.VMEM((2, N_EXPERTS, GATHER_BLOCK, D_MODEL), BF16),
            pltpu.VMEM((N_EXPERTS, ROW_ALIGN, D_MODEL), F32),
            pltpu.VMEM((SLOT_CHUNK, D_MODEL), BF16),
            pltpu.SemaphoreType.DMA((2,)),
            pltpu.SemaphoreType.DMA((1,)),
        ],
    )
    return pl.pallas_call(
        functools.partial(_gather_kernel, ntile=ntile, cap=cap),
        grid_spec=grid_spec,
        out_shape=jax.ShapeDtypeStruct((N_EXPERTS, cap + GATHER_PAD, D_MODEL), BF16),
        compiler_params=_params(("arbitrary",)),
        name="gather",
    )(bsel_i, afft, tau, need, beq_i, xn, u)


def _ffn_kernel(xe_ref, wg_ref, wu_ref, wd_ref, ye_ref, acc_ref, *, cap, nf, tm):
    f = pl.program_id(1)
    tf = wg_ref.shape[2]
    chunks = [slice(j * FFN_CHUNK, (j + 1) * FFN_CHUNK) for j in range(tf // FFN_CHUNK)]
    cast = {}

    def weight(name, ref, j):
        if (name, j) not in cast:
            cast[name, j] = (ref[0, chunks[j], :] if name == "d" else ref[0, :, chunks[j]]).astype(BF16)
        return cast[name, j]

    @pl.when(f == 0)
    def _():
        acc_ref[...] = jnp.zeros_like(acc_ref)

    for i in range(cap // tm):
        r = slice(i * tm, (i + 1) * tm)
        x = xe_ref[0, r, :]
        y = None
        for j in range(len(chunks)):
            hg = _dot(x, weight("g", wg_ref, j))
            hu = _dot(x, weight("u", wu_ref, j))
            h = (hg * (1.0 / (1.0 + jnp.exp(-hg))) * hu).astype(BF16)
            part = _dot(h, weight("d", wd_ref, j))
            y = part if y is None else y + part
        acc_ref[r, :] += y

    @pl.when(f == nf - 1)
    def _():
        ye_ref[...] = acc_ref[...].astype(BF16)


def _ffn(xe, w_eg, w_eu, w_ed):
    cap = xe.shape[1] - GATHER_PAD
    tf = 512
    nf = D_FF // tf
    tm = min(cap, 1024)
    return pl.pallas_call(
        functools.partial(_ffn_kernel, cap=cap, nf=nf, tm=tm),
        grid=(N_EXPERTS, nf),
        in_specs=[
            pl.BlockSpec((1, cap, D_MODEL), lambda e, f: (e, 0, 0)),
            pl.BlockSpec((1, D_MODEL, tf), lambda e, f: (e, 0, f)),
            pl.BlockSpec((1, D_MODEL, tf), lambda e, f: (e, 0, f)),
            pl.BlockSpec((1, tf, D_MODEL), lambda e, f: (e, f, 0)),
        ],
        out_specs=pl.BlockSpec((cap, D_MODEL), lambda e, f: (e, 0)),
        out_shape=jax.ShapeDtypeStruct((N_EXPERTS * cap, D_MODEL), BF16),
        scratch_shapes=[pltpu.VMEM((cap, D_MODEL), F32)],
        compiler_params=_params(("arbitrary", "arbitrary")),
        name="ffn",
    )(xe, w_eg, w_eu, w_ed)


def _combine_kernel(bsel_s, x1_ref, aff_ref, taut_ref, needt_ref, beqt_ref, bselt_ref, low_ref, spread_ref, gf_ref,
                    ye_hbm, y_ref, buf_ref, xbuf_ref, sem_ref, xsem_ref, *, cap, total, nstep):
    step = pl.program_id(0)
    par = step & 1
    per_tile = N_EXPERTS * SLOT_CHUNK

    def aligned(e, tile):
        return _floor_pow2(bsel_s[e, tile], ROW_ALIGN)

    def window(e, tile, c):
        start = jnp.minimum(e * cap + aligned(e, tile) + c * SLOT_CHUNK, total - SLOT_CHUNK)
        return pl.multiple_of(start, ROW_ALIGN)

    def first_chunks(stp, buf, sub):
        tile = stp * COMBINE_TILES + sub
        return [pltpu.make_async_copy(ye_hbm.at[pl.ds(window(e, tile, 0), SLOT_CHUNK)],
                                      buf_ref.at[buf, pl.ds(sub * per_tile + e * SLOT_CHUNK, SLOT_CHUNK)],
                                      sem_ref.at[buf])
                for e in range(N_EXPERTS)]

    @pl.when(step == 0)
    def _():
        for sub in range(COMBINE_TILES):
            for cp in first_chunks(0, 0, sub):
                cp.start()

    for sub in range(COMBINE_TILES):
        for cp in first_chunks(jnp.minimum(step + 1, nstep - 1), 1 - par, sub):
            cp.start()
    for sub in range(COMBINE_TILES):
        for cp in first_chunks(step, par, sub):
            cp.wait()

    tau = taut_ref[0:1, :]
    low = low_ref[...]
    spread = spread_ref[...]
    lane = lax.broadcasted_iota(I32, (1, LANES), 1)
    wide = lax.broadcasted_iota(I32, (ROUTE_TILE, per_tile), 1)
    in_chunk = (wide & (SLOT_CHUNK - 1)).astype(F32)
    slots, affs = [], []
    for sub in range(COMBINE_TILES):
        tile = step * COMBINE_TILES + sub
        rows = slice(sub * ROUTE_TILE, (sub + 1) * ROUTE_TILE)
        aff = aff_ref[rows, :]
        k = lax.bitcast_convert_type(aff, I32)
        eq = k == tau
        eq_cum = _dot(low, _ones_where(eq, BF16)) + beqt_ref[sub]
        sel = (k > tau) | (eq & (eq_cum <= needt_ref[0:1, :]))
        slot = jnp.where(sel, _dot(low, _ones_where(sel, BF16)) + (bselt_ref[sub] - 1.0), -1.0)

        rel = jnp.zeros((1, LANES), F32)
        for e in range(N_EXPERTS):
            rel = jnp.where(lane == e, (window(e, tile, 0) - e * cap).astype(F32), rel)
        d = slot - rel
        d = jnp.where(sel & (d >= 0.0) & (d < float(SLOT_CHUNK)), d, -1.0)
        hit = _dot(d.astype(BF16), spread) == in_chunk
        onehot_gate = jnp.where(hit, _dot(aff.astype(BF16), spread), 0.0).astype(BF16)
        y_ref[rows, :] = x1_ref[rows, :] + _dot(onehot_gate, buf_ref[par, sub * per_tile:(sub + 1) * per_tile, :])
        slots.append(slot)
        affs.append(aff)

    nch = [[_cdiv_pow2(bsel_s[e, step * COMBINE_TILES + sub + 1] - aligned(e, step * COMBINE_TILES + sub), SLOT_CHUNK)
            for e in range(N_EXPERTS)] for sub in range(COMBINE_TILES)]

    @pl.when(functools.reduce(jnp.maximum, [n for per_sub in nch for n in per_sub]) > 1)
    def _():
        lane64 = lax.broadcasted_iota(I32, (ROUTE_TILE, SLOT_CHUNK), 1).astype(F32)
        for sub in range(COMBINE_TILES):
            tile = step * COMBINE_TILES + sub
            rows = slice(sub * ROUTE_TILE, (sub + 1) * ROUTE_TILE)
            for e in range(N_EXPERTS):
                slot_e = slots[sub][:, e:e + 1]

                def extra(c, carry):
                    w = window(e, tile, c)
                    cp = pltpu.make_async_copy(ye_hbm.at[pl.ds(w, SLOT_CHUNK)], xbuf_ref, xsem_ref.at[0])
                    cp.start()
                    cp.wait()
                    first = (aligned(e, tile) + c * SLOT_CHUNK).astype(F32)
                    hit = (lane64 + (w - e * cap).astype(F32) == slot_e) & (slot_e >= first)
                    y_ref[rows, :] += affs[sub][:, e:e + 1] * _dot(_ones_where(hit, BF16), xbuf_ref[...])
                    return carry

                lax.fori_loop(1, nch[sub][e], extra, 0)

    acc = y_ref[...]
    ms = jnp.mean(acc * acc, axis=-1, keepdims=True)
    y_ref[...] = acc * lax.rsqrt(ms + EPS) * gf_ref[...]

    @pl.when(step == nstep - 1)
    def _():
        for sub in range(COMBINE_TILES):
            for cp in first_chunks(step, 1 - par, sub):
                cp.wait()


def _combine(bsel_i, x1, aff, tables_t, ye, gf):
    tokens = x1.shape[0]
    cap = CAPACITY_FACTOR * tokens // N_EXPERTS
    ntile = tokens // ROUTE_TILE
    idx = np.arange(ROUTE_TILE)
    low = jnp.asarray(idx[:, None] >= idx[None, :], BF16)
    spread = jnp.asarray(np.arange(LANES)[:, None] == np.arange(N_EXPERTS * SLOT_CHUNK)[None, :] // SLOT_CHUNK, BF16)
    taut, needt, beqt, bselt = tables_t
    rows = COMBINE_TILES * ROUTE_TILE
    rowvec = pl.BlockSpec((8, LANES), lambda t, *_: (0, 0))
    tilevec = pl.BlockSpec((COMBINE_TILES, 1, LANES), lambda t, *_: (t, 0, 0))
    grid_spec = pltpu.PrefetchScalarGridSpec(
        num_scalar_prefetch=1,
        grid=(ntile // COMBINE_TILES,),
        in_specs=[
            pl.BlockSpec((rows, D_MODEL), lambda t, *_: (t, 0)),
            pl.BlockSpec((rows, LANES), lambda t, *_: (t, 0)),
            rowvec, rowvec, tilevec, tilevec,
            pl.BlockSpec(low.shape, lambda t, *_: (0, 0)),
            pl.BlockSpec(spread.shape, lambda t, *_: (0, 0)),
            pl.BlockSpec((1, D_MODEL), lambda t, *_: (0, 0)),
            pl.BlockSpec(memory_space=pl.ANY),
        ],
        out_specs=pl.BlockSpec((rows, D_MODEL), lambda t, *_: (t, 0)),
        scratch_shapes=[
            pltpu.VMEM((2, COMBINE_TILES * N_EXPERTS * SLOT_CHUNK, D_MODEL), BF16),
            pltpu.VMEM((SLOT_CHUNK, D_MODEL), BF16),
            pltpu.SemaphoreType.DMA((2,)),
            pltpu.SemaphoreType.DMA((1,)),
        ],
    )
    return pl.pallas_call(
        functools.partial(_combine_kernel, cap=cap, total=N_EXPERTS * cap, nstep=ntile // COMBINE_TILES),
        grid_spec=grid_spec,
        out_shape=jax.ShapeDtypeStruct((tokens, D_MODEL), F32),
        compiler_params=_params(("arbitrary",)),
        name="combine",
    )(bsel_i, x1, aff, taut, needt, beqt.reshape(LANES, 1, LANES), bselt.reshape(LANES, 1, LANES), low, spread,
      gf, ye)


def _encoder(x, w):
    batch, seq, width = x.shape
    tokens = batch * seq
    max_dil = max(dil for _, dil in GROUPS)
    assert width == D_MODEL and x.dtype == F32
    assert seq % (ATT_SUB * max_dil) == 0 and seq % (LANES * ROW_ALIGN) == 0 and seq % TOKEN_TILE == 0
    assert LANES % min(LANES, FFT_STEP_ROWS // (seq // LANES)) == 0
    assert tokens % (ROUTE_TILE * COMBINE_TILES) == 0 and tokens % (16 * LANES) == 0 and tokens // ROUTE_TILE < LANES
    assert (CAPACITY_FACTOR * tokens) % (N_EXPERTS * ROW_ALIGN) == 0
    xt = x.reshape(tokens, D_MODEL)
    *qkvs, vr, vi, gates = _in_proj(xt, w["g1"], w["w_in"], w["w_gate"], w["b_gate"], w["cs"], batch, seq)
    outs, lses = [], []
    for g in range(N_GROUPS):
        o, lse = _attention(qkvs[g], w["bias"][g], g)
        outs.append(o)
        lses.append(lse)
    four = _fourier(vr, vi, batch, seq)
    x1, xn, aff, afft = _mix(xt, outs, lses, four, gates, w["w_attn"], w["w_four"], w["w_out"], w["g2"],
                             w["w_router"])
    tau, need, beq_i, bsel_i, taut, needt, beqt, bselt = _route(afft)
    idx = np.arange(ROUTE_TILE)
    u = jnp.asarray(idx[:, None] <= idx[None, :], BF16)
    xe = _gather(bsel_i, afft, tau, need, beq_i, xn, u)
    ye = _ffn(xe, w["w_eg"], w["w_eu"], w["w_ed"])
    y = _combine(bsel_i, x1, aff, (taut, needt, beqt, bselt), ye, w["gf"])
    return y.reshape(batch, seq, D_MODEL)


def _prepare_weights(rel_bias, norm1_g, w_in, w_attn_br, w_four_br, w_gate, b_gate, w_out,
                     norm2_g, w_router, w_exp_gate, w_exp_up, w_exp_down, final_g):
    c, s = _dft_mats(F_CH)
    qkv_cols = w_in[0][:, :QKV_W].reshape(D_MODEL, 3, N_GROUPS, GROUP_W).transpose(0, 2, 1, 3)
    w_in_grouped = jnp.concatenate([qkv_cols.reshape(D_MODEL, QKV_W), w_in[0][:, QKV_W:]], axis=1)
    w_router = jnp.pad(w_router[0], ((0, 0), (0, LANES - N_EXPERTS)))
    w_router_hi = w_router.astype(BF16)
    return {
        "g1": norm1_g[0].reshape(1, D_MODEL),
        "w_in": w_in_grouped.astype(BF16),
        "w_gate": w_gate[0].astype(BF16),
        "b_gate": b_gate[0].reshape(1, 2 * D_MODEL),
        "cs": jnp.asarray(np.concatenate([c, s], axis=1), BF16),
        "bias": [_attention_bias(rel_bias, g) for g in range(N_GROUPS)],
        "w_attn": w_attn_br[0].astype(BF16),
        "w_four": w_four_br[0].astype(BF16),
        "w_out": w_out[0].astype(BF16),
        "g2": norm2_g[0].reshape(1, D_MODEL),
        "w_router": jnp.concatenate([w_router_hi, (w_router - w_router_hi.astype(F32)).astype(BF16)], axis=1),
        "w_eg": w_exp_gate[0],
        "w_eu": w_exp_up[0],
        "w_ed": w_exp_down[0],
        "gf": final_g.reshape(1, D_MODEL),
    }


def kernel(x_prompt, x_sample, rel_bias, norm1_g, w_in, w_attn_br, w_four_br, w_gate, b_gate, w_out,
           norm2_g, w_router, w_exp_gate, w_exp_up, w_exp_down, final_g):
    w = _prepare_weights(rel_bias, norm1_g, w_in, w_attn_br, w_four_br, w_gate, b_gate, w_out,
                         norm2_g, w_router, w_exp_gate, w_exp_up, w_exp_down, final_g)
    return (_encoder(x_prompt, w), _encoder(x_sample, w))
```

```python
import functools
import math

import numpy as np
import jax
import jax.numpy as jnp
from jax import lax
from jax.experimental import pallas as pl
from jax.experimental.pallas import tpu as pltpu

D_MODEL = 1024
HEAD_DIM = 64
HEADS_PER_GROUP = 4
GROUPS = ((128, 1), (512, 4), (2048, 16))
N_GROUPS = len(GROUPS)
GROUP_W = HEADS_PER_GROUP * HEAD_DIM
ATT_W = N_GROUPS * GROUP_W
QKV_W = 3 * ATT_W
F_GROUPS = 6
F_CH = 128
F_W = F_GROUPS * F_CH
NUM_BUCKETS = 32
MAX_DISTANCE = 1024
N_EXPERTS = 16
CAPACITY_FACTOR = 2
D_FF = 2048
EPS = 1e-6
NEG = -1e30

HALF_KEYS = 64
ATT_SUB = 128
ATT_STEP_ROWS = 2048
ATT_OUT_ROWS = 8192
TOKEN_TILE = 512
ROUTE_TILE = 256
FFT_STEP_ROWS = 1024
FFN_CHUNK = 256
SLOT_CHUNK = 64
COMBINE_TILES = 4
ROW_ALIGN = 16
GATHER_BLOCK = SLOT_CHUNK + ROW_ALIGN
GATHER_STACK = GATHER_BLOCK + ROW_ALIGN
GATHER_PAD = GATHER_BLOCK
LANES = 128
V7X_VMEM_LIMIT = 56 * 1024 * 1024

F32 = jnp.float32
BF16 = jnp.bfloat16
I32 = jnp.int32
U32 = jnp.uint32


def _params(sem):
    return pltpu.CompilerParams(dimension_semantics=sem, vmem_limit_bytes=V7X_VMEM_LIMIT)


def _dot(a, b):
    return jnp.dot(a, b, preferred_element_type=F32)


def _dot_nt(a, b):
    return lax.dot_general(a, b, (((1,), (1,)), ((), ())), preferred_element_type=F32)


def _floor_pow2(x, m):
    return x & ~(m - 1)


def _cdiv_pow2(x, m):
    return (x + (m - 1)) >> (m.bit_length() - 1)


def _ones_where(mask, dtype=F32):
    return jnp.where(mask, jnp.ones((), F32), jnp.zeros((), F32)).astype(dtype)


def _in_proj_kernel(x_ref, g_ref, win_ref, wg_ref, bg_ref, cs_ref, qscale_ref, qkv0_ref, qkv1_ref, qkv2_ref,
                    vr_ref, vi_ref, gates_ref, slab_ref):
    tm = x_ref.shape[0]
    half = tm // 2
    nslab = ATT_W // LANES
    cs = cs_ref[...]
    for h in range(2):
        rows = slice(h * half, (h + 1) * half)
        x = x_ref[rows, :]
        ms = jnp.mean(x * x, axis=-1, keepdims=True)
        xn = (x * lax.rsqrt(ms + EPS) * g_ref[...]).astype(BF16)
        for g, out_ref in enumerate((qkv0_ref, qkv1_ref, qkv2_ref)):
            dil = GROUPS[g][1]
            res = _dot(xn, win_ref[:, g * ATT_W:(g + 1) * ATT_W]) * qscale_ref[...]
            if dil == 1:
                out_ref[0, 0, rows, :] = res.astype(BF16)
                continue
            for j in range(nslab):
                slab_ref[j, rows, :] = res[:, j * LANES:(j + 1) * LANES]
            n = half // dil
            for r in range(dil):
                cls = [slab_ref[j, pl.ds(h * half + r, n, stride=dil), :] for j in range(nslab)]
                out_ref[0, r, h * n:(h + 1) * n, :] = jnp.concatenate(cls, axis=1).astype(BF16)
        u = _dot(xn, win_ref[:, QKV_W:QKV_W + F_W]).astype(BF16)
        for g in range(F_GROUPS):
            a = _dot(u[:, g * F_CH:(g + 1) * F_CH], cs)
            vr_ref[rows, g * F_CH:(g + 1) * F_CH] = a[:, :F_CH].astype(BF16)
            vi_ref[rows, g * F_CH:(g + 1) * F_CH] = (-a[:, F_CH:]).astype(BF16)
        z = _dot(xn, wg_ref[...]) + bg_ref[...]
        gates_ref[rows, :] = (1.0 / (1.0 + jnp.exp(-z))).astype(BF16)


def _class_major_spec(tm, dil, width, per_batch):
    return pl.BlockSpec((1, dil, tm // dil, width), lambda i: (i // per_batch, 0, i % per_batch, 0))


def _in_proj(x, g1, w_in, w_gate, b_gate, cs, batch, seq):
    t = x.shape[0]
    tm = TOKEN_TILE
    per_batch = seq // tm
    const = lambda i: (0, 0)
    row = lambda i: (i, 0)
    qscale = np.ones((1, ATT_W), np.float32)
    qscale[:, :GROUP_W] = 1.0 / math.sqrt(HEAD_DIM)
    return pl.pallas_call(
        _in_proj_kernel,
        grid=(t // tm,),
        in_specs=[
            pl.BlockSpec((tm, D_MODEL), row),
            pl.BlockSpec((1, D_MODEL), const),
            pl.BlockSpec(w_in.shape, const),
            pl.BlockSpec(w_gate.shape, const),
            pl.BlockSpec((1, 2 * D_MODEL), const),
            pl.BlockSpec(cs.shape, const),
            pl.BlockSpec((1, ATT_W), const),
        ],
        out_specs=[_class_major_spec(tm, dil, ATT_W, per_batch) for _, dil in GROUPS] + [
            pl.BlockSpec((tm, F_W), row),
            pl.BlockSpec((tm, F_W), row),
            pl.BlockSpec((tm, 2 * D_MODEL), row),
        ],
        out_shape=[jax.ShapeDtypeStruct((batch, dil, seq // dil, ATT_W), BF16) for _, dil in GROUPS] + [
            jax.ShapeDtypeStruct((t, F_W), BF16),
            jax.ShapeDtypeStruct((t, F_W), BF16),
            jax.ShapeDtypeStruct((t, 2 * D_MODEL), BF16),
        ],
        scratch_shapes=[pltpu.VMEM((ATT_W // LANES, tm, LANES), F32)],
        compiler_params=_params(("parallel",)),
        name="in_proj",
    )(x, g1, w_in, w_gate, b_gate, cs, jnp.asarray(qscale))


def _attention_kernel(q_ref, kp_ref, kc_ref, kn_ref, vp_ref, vc_ref, vn_ref, bias_ref, o_ref, lse_ref, *,
                      tq, length, dil, rc):
    i = pl.program_id(1)
    win = ATT_SUB + 2 * HALF_KEYS
    nsub = tq // ATT_SUB
    lane_head = lax.broadcasted_iota(I32, (ATT_SUB, GROUP_W), 1) // HEAD_DIM
    lane_slot = lax.broadcasted_iota(I32, (ATT_SUB, LANES), 1) // (LANES // HEADS_PER_GROUP)
    at_start = (i == 0).astype(I32)
    at_end = (i == length // tq - 1).astype(I32) * 2
    for c, sb in [(c, sb) for c in range(rc) for sb in range(nsub)]:
        r = pl.program_id(2) * rc + c
        if sb == 0:
            kwin = jnp.concatenate([kp_ref[0, c], kc_ref[0, c], kn_ref[0, c]], axis=0)
            vwin = jnp.concatenate([vp_ref[0, c], vc_ref[0, c], vn_ref[0, c]], axis=0)
        off = sb * ATT_SUB
        q = q_ref[0, c, off:off + ATT_SUB, :]
        kw = kwin[off:off + win]
        vw = vwin[off:off + win]
        variant = (at_start if sb == 0 else 0) + (at_end if sb == nsub - 1 else 0)
        qs = jnp.concatenate(
            [jnp.where(lane_head == h, q, jnp.zeros_like(q)) for h in range(HEADS_PER_GROUP)], axis=0)
        s_all = _dot_nt(qs, kw)
        ps, ms, ls = [], [], []
        for h in range(HEADS_PER_GROUP):
            s = s_all[h * ATT_SUB:(h + 1) * ATT_SUB] + bias_ref[variant, h]
            m = jnp.max(s, axis=-1, keepdims=True)
            p = jnp.exp(s - m)
            ls.append(jnp.sum(p, axis=-1, keepdims=True))
            ms.append(m)
            ps.append(p.astype(BF16))
        o_all = _dot(jnp.concatenate(ps, axis=0), vw)
        out = jnp.zeros((ATT_SUB, GROUP_W), F32)
        lse = jnp.zeros((ATT_SUB, LANES), F32)
        for h in range(HEADS_PER_GROUP):
            oh = o_all[h * ATT_SUB:(h + 1) * ATT_SUB] * (1.0 / ls[h])
            out = jnp.where(lane_head == h, oh, out)
            lse = jnp.where(lane_slot == h, ms[h] + jnp.log(ls[h]), lse)
        rows = pl.ds(off * dil + r, ATT_SUB, stride=dil) if dil > 1 else pl.ds(off, ATT_SUB)
        o_ref[rows, :] = _pack_bf16_pair(out[:, :LANES], out[:, LANES:])
        lse_ref[rows, :] = lse


def _attention(qkv, bias, g):
    batch, dil, length, _ = qkv.shape
    tq = min(length, ATT_STEP_ROWS, ATT_OUT_ROWS // dil)
    nb = length // tq
    hb = tq // HALF_KEYS
    last_halo = length // HALF_KEYS - 1
    rc = min(dil, max(1, ATT_STEP_ROWS // tq))

    def cur(c):
        return lambda b, i, r: (b, r, i, c)

    def prev(c):
        return lambda b, i, r: (b, r, jnp.maximum(i * hb - 1, 0), c)

    def nxt(c):
        return lambda b, i, r: (b, r, jnp.minimum((i + 1) * hb, last_halo), c)

    blk = lambda rows: (1, rc, rows, GROUP_W)
    out_spec = pl.BlockSpec((tq * dil, LANES), lambda b, i, r: (b * nb + i, 0))
    return pl.pallas_call(
        functools.partial(_attention_kernel, tq=tq, length=length, dil=dil, rc=rc),
        grid=(batch, nb, dil // rc),
        in_specs=[
            pl.BlockSpec(blk(tq), cur(0)),
            pl.BlockSpec(blk(HALF_KEYS), prev(1)),
            pl.BlockSpec(blk(tq), cur(1)),
            pl.BlockSpec(blk(HALF_KEYS), nxt(1)),
            pl.BlockSpec(blk(HALF_KEYS), prev(2)),
            pl.BlockSpec(blk(tq), cur(2)),
            pl.BlockSpec(blk(HALF_KEYS), nxt(2)),
            pl.BlockSpec(bias.shape, lambda b, i, r: (0, 0, 0, 0)),
        ],
        out_specs=[out_spec, out_spec],
        out_shape=[jax.ShapeDtypeStruct((batch * dil * length, LANES), U32),
                   jax.ShapeDtypeStruct((batch * dil * length, LANES), F32)],
        compiler_params=_params(("parallel", "parallel", "arbitrary")),
        name=f"attention_g{g}",
    )(qkv, qkv, qkv, qkv, qkv, qkv, qkv, bias)


def _t5_bucket(rel):
    nb = NUM_BUCKETS // 2
    max_exact = nb // 2
    ret = (rel > 0).astype(np.int32) * nb
    n = np.abs(rel)
    large = max_exact + (np.log(np.maximum(n, max_exact) / max_exact)
                         / np.log(MAX_DISTANCE / max_exact) * (nb - max_exact)).astype(np.int32)
    large = np.minimum(large, nb - 1)
    return (ret + np.where(n < max_exact, n, large)).astype(np.int32)


def _attention_bias(rel_bias, g):
    dil = GROUPS[g][1]
    qi = np.arange(ATT_SUB)[:, None]
    kj = np.arange(ATT_SUB + 2 * HALF_KEYS)[None, :]
    delta = kj - HALF_KEYS - qi
    band = np.abs(delta) <= HALF_KEYS
    bucket = _t5_bucket(dil * delta)
    tab = rel_bias[:, g * HEADS_PER_GROUP:(g + 1) * HEADS_PER_GROUP].astype(F32)
    onehot = jnp.asarray(bucket[..., None] == np.arange(NUM_BUCKETS), F32)
    bias = jnp.einsum("qkb,bh->hqk", onehot, tab, precision=lax.Precision.HIGHEST)
    masks = [band & ((kj >= HALF_KEYS) | ((v & 1) == 0)) & ((kj < ATT_SUB + HALF_KEYS) | ((v & 2) == 0))
             for v in range(4)]
    return jnp.where(jnp.asarray(np.stack(masks))[:, None], bias[None], NEG)


def _dft_mats(n):
    k = np.arange(n)
    ang = 2.0 * np.pi * ((k[:, None] * k[None, :]) % n) / n
    return np.cos(ang), np.sin(ang)


def _pack_bf16_pair(a, b):
    hi = lax.bitcast_convert_type(a.astype(BF16).astype(F32), U32)
    lo = lax.bitcast_convert_type(b.astype(BF16).astype(F32), U32)
    return hi | lax.shift_right_logical(lo, jnp.full(lo.shape, 16, U32))


def _unpack_bf16_pair(word):
    a = lax.bitcast_convert_type(word & jnp.uint32(0xFFFF0000), F32)
    b = lax.bitcast_convert_type(lax.shift_left(word, jnp.full(word.shape, 16, U32)), F32)
    return a.astype(BF16), b.astype(BF16)


def _fft_stage1_kernel(vr_ref, vi_ref, m1_ref, twc_ref, tws_ref, z_ref, *, n1, m):
    x = jnp.concatenate([vr_ref[0], vi_ref[0]], axis=0)
    z = _dot(m1_ref[...], x)
    zr, zi = z[:n1], z[n1:]
    twc, tws = twc_ref[0], tws_ref[0]
    for j in range(m):
        c = twc[:, j:j + 1]
        s = tws[:, j:j + 1]
        a = zr[:, j * F_W:(j + 1) * F_W]
        b = zi[:, j * F_W:(j + 1) * F_W]
        z_ref[0, :, j, :] = _pack_bf16_pair(a * c + b * s, b * c - a * s)


def _fft_stage2_kernel(z_ref, m2_ref, o_ref, *, kc, scale):
    m2 = m2_ref[...]
    for j in range(kc):
        x = jnp.concatenate(_unpack_bf16_pair(z_ref[0, j]), axis=0)
        y = _dot(m2, x) * scale
        o_ref[0, :, j, :] = _pack_bf16_pair(y[:, :F_W // 2], y[:, F_W // 2:])


def _fourier(vr, vi, batch, seq):
    n2 = LANES
    n1 = seq // n2
    m = min(n2, FFT_STEP_ROWS // n1)
    c1, s1 = _dft_mats(n1)
    m1 = jnp.asarray(np.block([[c1, s1], [-s1, c1]]), BF16)
    c2, s2 = _dft_mats(n2)
    m2 = jnp.asarray(np.concatenate([c2, s2], axis=1), BF16)
    k1 = np.arange(n1)[:, None]
    sv = np.arange(n2)[None, :]
    ang = 2.0 * np.pi * ((k1 * sv) % seq) / seq
    to_blocks = lambda a: jnp.asarray(a.reshape(n1, n2 // m, m).transpose(1, 0, 2), F32)
    twc, tws = to_blocks(np.cos(ang)), to_blocks(np.sin(ang))

    v3 = lambda a: a.reshape(batch, n1, n2 * F_W)
    blk = pl.BlockSpec((1, n1, m * F_W), lambda b, j: (b, 0, j))
    tmap = lambda b, j: (j, 0, 0)
    z = pl.pallas_call(
        functools.partial(_fft_stage1_kernel, n1=n1, m=m),
        grid=(batch, n2 // m),
        in_specs=[
            blk,
            blk,
            pl.BlockSpec(m1.shape, lambda b, j: (0, 0)),
            pl.BlockSpec((1, n1, m), tmap),
            pl.BlockSpec((1, n1, m), tmap),
        ],
        out_specs=pl.BlockSpec((1, n1, m, F_W), lambda b, j: (b, 0, j, 0)),
        out_shape=jax.ShapeDtypeStruct((batch, n1, n2, F_W), U32),
        compiler_params=_params(("parallel", "parallel")),
        name="fft_stage1",
    )(v3(vr), v3(vi), m1, twc, tws)

    kc = min(n1, FFT_STEP_ROWS // n2 * 2)
    out = pl.pallas_call(
        functools.partial(_fft_stage2_kernel, kc=kc, scale=1.0 / math.sqrt(seq * F_CH)),
        grid=(batch, n1 // kc),
        in_specs=[
            pl.BlockSpec((1, kc, n2, F_W), lambda b, j: (b, j, 0, 0)),
            pl.BlockSpec(m2.shape, lambda b, j: (0, 0)),
        ],
        out_specs=pl.BlockSpec((1, n2, kc, F_W // 2), lambda b, j: (b, 0, j, 0)),
        out_shape=jax.ShapeDtypeStruct((batch, n2, n1, F_W // 2), U32),
        compiler_params=_params(("parallel", "parallel")),
        name="fft_stage2",
    )(z, m2)
    return out.reshape(batch * seq, F_W // 2)


def _mix_kernel(x_ref, o0_ref, o1_ref, o2_ref, l0_ref, l1_ref, l2_ref, four_ref, gates_ref,
                wa_ref, wf_ref, wo_ref, g2_ref, wr_ref, hs_ref, x1_ref, xn_ref, aff_ref, afft_ref):
    tm = x_ref.shape[0]
    half = tm // 2
    for rows in (slice(0, half), slice(half, tm)):
        f_br = _dot(jnp.concatenate(_unpack_bf16_pair(four_ref[rows, :]), axis=1), wf_ref[...])
        l0, l1, l2 = l0_ref[rows, :], l1_ref[rows, :], l2_ref[rows, :]
        mx = jnp.maximum(jnp.maximum(l0, l1), l2)
        e0, e1, e2 = jnp.exp(l0 - mx), jnp.exp(l1 - mx), jnp.exp(l2 - mx)
        inv = 1.0 / (e0 + e1 + e2)
        w0, w1, w2 = (_dot((e * inv).astype(BF16), hs_ref[...]) for e in (e0, e1, e2))
        o0, o1, o2 = (jnp.concatenate(_unpack_bf16_pair(ref[rows, :]), axis=1) for ref in (o0_ref, o1_ref, o2_ref))
        att = w0 * o0 + w1 * o1 + w2 * o2
        a_br = _dot(att.astype(BF16), wa_ref[...])
        mix = gates_ref[rows, :D_MODEL] * a_br + gates_ref[rows, D_MODEL:] * f_br
        x1 = x_ref[rows, :] + _dot(mix.astype(BF16), wo_ref[...])
        x1_ref[rows, :] = x1
        ms = jnp.mean(x1 * x1, axis=-1, keepdims=True)
        xn = x1 * lax.rsqrt(ms + EPS) * g2_ref[...]
        xn_ref[rows, :] = xn.astype(BF16)
        xh = xn.astype(BF16)
        xl = (xn - xh.astype(F32)).astype(BF16)
        both = _dot(xh, wr_ref[...])
        logits = both[:, :LANES] + (both[:, LANES:] + _dot(xl, wr_ref[:, :LANES]))
        lane = lax.broadcasted_iota(I32, logits.shape, 1)
        logits = jnp.where(lane < N_EXPERTS, logits, NEG)
        p = jnp.exp(logits - jnp.max(logits, axis=-1, keepdims=True))
        aff = p * (1.0 / jnp.sum(p, axis=-1, keepdims=True))
        aff_ref[rows, :] = aff
        afft_ref[:, rows] = aff.T[:N_EXPERTS]


def _mix(x, os_, ls_, four, gates, w_attn, w_four, w_out, g2, w_router):
    t = x.shape[0]
    tm = TOKEN_TILE
    const = lambda i: (0, 0)
    row = lambda i: (i, 0)
    rows = lambda w: pl.BlockSpec((tm, w), row)
    full = lambda a: pl.BlockSpec(a.shape, const)
    slot = LANES // HEADS_PER_GROUP
    head_spread = jnp.asarray(np.arange(LANES)[:, None] == slot * (np.arange(GROUP_W)[None, :] // HEAD_DIM), BF16)
    return pl.pallas_call(
        _mix_kernel,
        grid=(t // tm,),
        in_specs=[rows(D_MODEL)] + [rows(LANES)] * 6 + [rows(F_W // 2), rows(2 * D_MODEL),
                  full(w_attn), full(w_four), full(w_out), full(g2), full(w_router), full(head_spread)],
        out_specs=[rows(D_MODEL), rows(D_MODEL), rows(LANES), pl.BlockSpec((N_EXPERTS, tm), lambda i: (0, i))],
        out_shape=[
            jax.ShapeDtypeStruct((t, D_MODEL), F32),
            jax.ShapeDtypeStruct((t, D_MODEL), BF16),
            jax.ShapeDtypeStruct((t, LANES), F32),
            jax.ShapeDtypeStruct((N_EXPERTS, t), F32),
        ],
        compiler_params=_params(("parallel",)),
        name="mix",
    )(x, *os_, *ls_, four, gates, w_attn, w_four, w_out, g2, w_router, head_spread)


def _route_kernel(afft_ref, su_ref, u_ref, tau_ref, need_ref, beq_ref, bsel_ref,
                  taut_ref, needt_ref, beqt_ref, bselt_ref, *, tokens):
    cap = CAPACITY_FACTOR * tokens // N_EXPERTS
    ntile = tokens // ROUTE_TILE
    shape = (N_EXPERTS, LANES)
    lane = lax.broadcasted_iota(I32, shape, 1)

    def keys(start, width):
        return lax.bitcast_convert_type(afft_ref[:, pl.ds(pl.multiple_of(start, LANES), width)], I32)

    span = min(tokens, 16 * LANES)

    def count(pred):
        def body(c, acc):
            hits = _ones_where(pred(keys(c * span, span)))
            for j in range(span // LANES):
                acc = acc + hits[:, j * LANES:(j + 1) * LANES]
            return acc
        acc = lax.fori_loop(0, tokens // span, body, jnp.zeros(shape, F32), unroll=True)
        return jnp.sum(acc, axis=1, keepdims=True)

    def bit_body(i, prefix):
        cand = prefix | lax.shift_left(jnp.ones(shape, I32), jnp.full(shape, 30 - i, I32))
        tot = count(lambda k: k >= cand[:, :1])
        return jnp.where(tot >= cap, cand, prefix)

    tau = lax.fori_loop(0, 31, bit_body, jnp.zeros(shape, I32))
    tau_col = tau[:, :1]
    n_gt = count(lambda k: k > tau_col)
    need = cap - n_gt

    def prefix_over_tiles(tab):
        return _dot(tab.astype(BF16), su_ref[...])

    def at_lane(tab, c):
        return jnp.sum(jnp.where(lane == c, tab, 0.0), axis=1, keepdims=True)

    def eq_body(c, tab):
        k = keys(c * ROUTE_TILE, ROUTE_TILE)
        cnt = jnp.sum(_ones_where(k == tau_col), axis=1, keepdims=True)
        return jnp.where(lane == c, cnt, tab)

    unroll = 8 if ntile % 8 == 0 else 1
    base_eq = prefix_over_tiles(lax.fori_loop(0, ntile, eq_body, jnp.zeros(shape, F32), unroll=unroll))

    def sel_body(c, tab):
        k = keys(c * ROUTE_TILE, ROUTE_TILE)
        eq = k == tau_col
        eq_cum = _dot(_ones_where(eq, BF16), u_ref[...]) + at_lane(base_eq, c)
        sel = (k > tau_col) | (eq & (eq_cum <= need))
        cnt = jnp.sum(_ones_where(sel), axis=1, keepdims=True)
        return jnp.where(lane == c, cnt, tab)

    base_sel = prefix_over_tiles(lax.fori_loop(0, ntile, sel_body, jnp.zeros(shape, F32), unroll=unroll))

    def transposed(val):
        return jnp.concatenate([val, jnp.zeros((LANES - N_EXPERTS, LANES), val.dtype)], axis=0).T

    tau_ref[...] = tau
    taut_ref[...] = transposed(tau)
    for val, ref, ref_t in ((jnp.broadcast_to(need, shape), need_ref, needt_ref),
                            (base_eq, beq_ref, beqt_ref), (base_sel, bsel_ref, bselt_ref)):
        ref[...] = val.astype(I32)
        ref_t[...] = transposed(val)


def _route(afft):
    tokens = afft.shape[1]
    idx = np.arange(LANES)
    su = jnp.asarray(idx[:, None] < idx[None, :], BF16)
    idx = np.arange(ROUTE_TILE)
    u = jnp.asarray(idx[:, None] <= idx[None, :], BF16)
    full = lambda a: pl.BlockSpec(a.shape, lambda i: (0,) * a.ndim)
    small = pl.BlockSpec((N_EXPERTS, LANES), lambda i: (0, 0))
    smallt = pl.BlockSpec((LANES, LANES), lambda i: (0, 0))
    return pl.pallas_call(
        functools.partial(_route_kernel, tokens=tokens),
        grid=(1,),
        in_specs=[full(afft), full(su), full(u)],
        out_specs=[small] * 4 + [smallt] * 4,
        out_shape=[jax.ShapeDtypeStruct((N_EXPERTS, LANES), I32)] * 4
        + [jax.ShapeDtypeStruct((LANES, LANES), I32)] + [jax.ShapeDtypeStruct((LANES, LANES), F32)] * 3,
        compiler_params=_params(("arbitrary",)),
        name="route",
    )(afft, su, u)


def _gather_kernel(bsel_s, afft_ref, tau_ref, need_ref, beq_ref, x_ref, u_ref, xe_hbm,
                   stage_ref, tail_ref, xbuf_ref, sem_ref, xsem_ref, *, ntile, cap):
    t = pl.program_id(0)
    par = t & 1

    def aligned(e, tile):
        return pl.multiple_of(_floor_pow2(bsel_s[e, tile], ROW_ALIGN), ROW_ALIGN)

    def write(e, tile, buf, first_row=None):
        first_row = aligned(e, tile) if first_row is None else first_row
        return pltpu.make_async_copy(stage_ref.at[buf, e], xe_hbm.at[e, pl.ds(first_row, GATHER_BLOCK)],
                                     sem_ref.at[buf])

    @pl.when(t == 0)
    def _():
        tail_ref[...] = jnp.zeros_like(tail_ref)
        stage_ref[1] = jnp.zeros(stage_ref.shape[1:], BF16)
        for e in range(N_EXPERTS):
            write(e, 0, 1, first_row=cap).start()

    k = lax.bitcast_convert_type(afft_ref[...], I32)
    tau = tau_ref[:, :1]
    lane = lax.broadcasted_iota(I32, (N_EXPERTS, LANES), 1)
    beq = jnp.sum(jnp.where(lane == t, beq_ref[...].astype(F32), 0.0), axis=1, keepdims=True)
    eq = k == tau
    eq_cum = _dot(_ones_where(eq, BF16), u_ref[...]) + beq
    sel = (k > tau) | (eq & (eq_cum <= need_ref[:, :1].astype(F32)))
    rank = jnp.where(sel, _dot(_ones_where(sel, BF16), u_ref[...]) - 1.0, -1e4)

    row = lax.broadcasted_iota(I32, (GATHER_STACK, ROUTE_TILE), 0)
    in_block = row < GATHER_BLOCK
    row_f = row.astype(F32)
    offs, shifts, pieces = [], [], []
    for e in range(N_EXPERTS):
        off = (bsel_s[e, t] - aligned(e, t)).astype(F32)
        shift = _floor_pow2(bsel_s[e, t + 1], ROW_ALIGN) - aligned(e, t)
        target = jnp.where(in_block, row_f, row_f - float(GATHER_BLOCK) + shift.astype(F32))
        pieces.append(_ones_where(rank[e:e + 1, :] + off == target, BF16))
        offs.append(off)
        shifts.append(shift)
    res = _dot(jnp.concatenate(pieces, axis=0), x_ref[...])
    for e in range(N_EXPERTS):
        base = e * GATHER_STACK
        old = tail_ref[e]
        stage_ref[par, e, 0:ROW_ALIGN, :] = (res[base:base + ROW_ALIGN] + old).astype(BF16)
        stage_ref[par, e, ROW_ALIGN:GATHER_BLOCK, :] = res[base + ROW_ALIGN:base + GATHER_BLOCK].astype(BF16)
        tail_ref[e] = res[base + GATHER_BLOCK:base + GATHER_STACK] + jnp.where(shifts[e] == 0, old, 0.0)
    for e in range(N_EXPERTS):
        write(e, jnp.maximum(t - 1, 0), 1 - par).wait()
    for e in range(N_EXPERTS):
        write(e, t, par).start(priority=e % 2)

    extra = [_cdiv_pow2(jnp.maximum(bsel_s[e, t + 1] - aligned(e, t) - GATHER_BLOCK, 0), SLOT_CHUNK)
             for e in range(N_EXPERTS)]

    @pl.when(functools.reduce(jnp.maximum, extra) > 0)
    def _():
        row64 = lax.broadcasted_iota(I32, (SLOT_CHUNK, ROUTE_TILE), 0).astype(F32)
        for e in range(N_EXPERTS):
            def chunk(c, carry):
                first = GATHER_BLOCK + c * SLOT_CHUNK
                onehot = _ones_where(rank[e:e + 1, :] + offs[e] == row64 + first.astype(F32), BF16)
                xbuf_ref[...] = _dot(onehot, x_ref[...]).astype(BF16)
                dst = pl.multiple_of(aligned(e, t) + first, ROW_ALIGN)
                cp = pltpu.make_async_copy(xbuf_ref, xe_hbm.at[e, pl.ds(dst, SLOT_CHUNK)], xsem_ref.at[0])
                cp.start()
                cp.wait()
                return carry

            lax.fori_loop(0, extra[e], chunk, 0)

    @pl.when(t == ntile - 1)
    def _():
        for e in range(N_EXPERTS):
            write(e, t, par).wait()


def _gather(bsel_i, afft, tau, need, beq_i, xn, u):
    tokens = xn.shape[0]
    cap = CAPACITY_FACTOR * tokens // N_EXPERTS
    ntile = tokens // ROUTE_TILE
    table = pl.BlockSpec((N_EXPERTS, LANES), lambda t, *_: (0, 0))
    grid_spec = pltpu.PrefetchScalarGridSpec(
        num_scalar_prefetch=1,
        grid=(ntile,),
        in_specs=[
            pl.BlockSpec((N_EXPERTS, ROUTE_TILE), lambda t, *_: (0, t)),
            table, table, table,
            pl.BlockSpec((ROUTE_TILE, D_MODEL), lambda t, *_: (t, 0)),
            pl.BlockSpec(u.shape, lambda t, *_: (0, 0)),
        ],
        out_specs=pl.BlockSpec(memory_space=pl.ANY),
        scratch_shapes=[
            pltpu.VMEM((2, N_EXPERTS, GATHER_BLOCK, D_MODEL), BF16),
            pltpu.VMEM((N_EXPERTS, ROW_ALIGN, D_MODEL), F32),
            pltpu.VMEM((SLOT_CHUNK, D_MODEL), BF16),
            pltpu.SemaphoreType.DMA((2,)),
            pltpu.SemaphoreType.DMA((1,)),
        ],
    )
    return pl.pallas_call(
        functools.partial(_gather_kernel, ntile=ntile, cap=cap),
        grid_spec=grid_spec,
        out_shape=jax.ShapeDtypeStruct((N_EXPERTS, cap + GATHER_PAD, D_MODEL), BF16),
        compiler_params=_params(("arbitrary",)),
        name="gather",
    )(bsel_i, afft, tau, need, beq_i, xn, u)


def _ffn_kernel(xe_ref, wg_ref, wu_ref, wd_ref, ye_ref, acc_ref, *, cap, nf, tm):
    f = pl.program_id(1)
    tf = wg_ref.shape[2]
    chunks = [slice(j * FFN_CHUNK, (j + 1) * FFN_CHUNK) for j in range(tf // FFN_CHUNK)]
    cast = {}

    def weight(name, ref, j):
        if (name, j) not in cast:
            cast[name, j] = (ref[0, chunks[j], :] if name == "d" else ref[0, :, chunks[j]]).astype(BF16)
        return cast[name, j]

    @pl.when(f == 0)
    def _():
        acc_ref[...] = jnp.zeros_like(acc_ref)

    for i in range(cap // tm):
        r = slice(i * tm, (i + 1) * tm)
        x = xe_ref[0, r, :]
        y = None
        for j in range(len(chunks)):
            hg = _dot(x, weight("g", wg_ref, j))
            hu = _dot(x, weight("u", wu_ref, j))
            h = (hg * (1.0 / (1.0 + jnp.exp(-hg))) * hu).astype(BF16)
            part = _dot(h, weight("d", wd_ref, j))
            y = part if y is None else y + part
        acc_ref[r, :] += y

    @pl.when(f == nf - 1)
    def _():
        ye_ref[...] = acc_ref[...].astype(BF16)


def _ffn(xe, w_eg, w_eu, w_ed):
    cap = xe.shape[1] - GATHER_PAD
    tf = 512
    nf = D_FF // tf
    tm = min(cap, 1024)
    return pl.pallas_call(
        functools.partial(_ffn_kernel, cap=cap, nf=nf, tm=tm),
        grid=(N_EXPERTS, nf),
        in_specs=[
            pl.BlockSpec((1, cap, D_MODEL), lambda e, f: (e, 0, 0)),
            pl.BlockSpec((1, D_MODEL, tf), lambda e, f: (e, 0, f)),
            pl.BlockSpec((1, D_MODEL, tf), lambda e, f: (e, 0, f)),
            pl.BlockSpec((1, tf, D_MODEL), lambda e, f: (e, f, 0)),
        ],
        out_specs=pl.BlockSpec((cap, D_MODEL), lambda e, f: (e, 0)),
        out_shape=jax.ShapeDtypeStruct((N_EXPERTS * cap, D_MODEL), BF16),
        scratch_shapes=[pltpu.VMEM((cap, D_MODEL), F32)],
        compiler_params=_params(("arbitrary", "arbitrary")),
        name="ffn",
    )(xe, w_eg, w_eu, w_ed)


def _combine_kernel(bsel_s, x1_ref, aff_ref, taut_ref, needt_ref, beqt_ref, bselt_ref, low_ref, spread_ref, gf_ref,
                    ye_hbm, y_ref, buf_ref, xbuf_ref, sem_ref, xsem_ref, *, cap, total, nstep):
    step = pl.program_id(0)
    par = step & 1
    per_tile = N_EXPERTS * SLOT_CHUNK

    def aligned(e, tile):
        return _floor_pow2(bsel_s[e, tile], ROW_ALIGN)

    def window(e, tile, c):
        start = jnp.minimum(e * cap + aligned(e, tile) + c * SLOT_CHUNK, total - SLOT_CHUNK)
        return pl.multiple_of(start, ROW_ALIGN)

    def first_chunks(stp, buf, sub):
        tile = stp * COMBINE_TILES + sub
        return [pltpu.make_async_copy(ye_hbm.at[pl.ds(window(e, tile, 0), SLOT_CHUNK)],
                                      buf_ref.at[buf, pl.ds(sub * per_tile + e * SLOT_CHUNK, SLOT_CHUNK)],
                                      sem_ref.at[buf])
                for e in range(N_EXPERTS)]

    @pl.when(step == 0)
    def _():
        for sub in range(COMBINE_TILES):
            for cp in first_chunks(0, 0, sub):
                cp.start()

    for sub in range(COMBINE_TILES):
        for e, cp in enumerate(first_chunks(jnp.minimum(step + 1, nstep - 1), 1 - par, sub)):
            cp.start(priority=e % 2)
    for sub in range(COMBINE_TILES):
        for cp in first_chunks(step, par, sub):
            cp.wait()

    tau = taut_ref[0:1, :]
    low = low_ref[...]
    spread = spread_ref[...]
    lane = lax.broadcasted_iota(I32, (1, LANES), 1)
    wide = lax.broadcasted_iota(I32, (ROUTE_TILE, per_tile), 1)
    in_chunk = (wide & (SLOT_CHUNK - 1)).astype(F32)
    slots, affs = [], []
    for sub in range(COMBINE_TILES):
        tile = step * COMBINE_TILES + sub
        rows = slice(sub * ROUTE_TILE, (sub + 1) * ROUTE_TILE)
        aff = aff_ref[rows, :]
        k = lax.bitcast_convert_type(aff, I32)
        eq = k == tau
        eq_cum = _dot(low, _ones_where(eq, BF16)) + beqt_ref[sub]
        sel = (k > tau) | (eq & (eq_cum <= needt_ref[0:1, :]))
        slot = jnp.where(sel, _dot(low, _ones_where(sel, BF16)) + (bselt_ref[sub] - 1.0), -1.0)

        rel = jnp.zeros((1, LANES), F32)
        for e in range(N_EXPERTS):
            rel = jnp.where(lane == e, (window(e, tile, 0) - e * cap).astype(F32), rel)
        d = slot - rel
        d = jnp.where(sel & (d >= 0.0) & (d < float(SLOT_CHUNK)), d, -1.0)
        hit = _dot(d.astype(BF16), spread) == in_chunk
        onehot_gate = jnp.where(hit, _dot(aff.astype(BF16), spread), 0.0).astype(BF16)
        y_ref[rows, :] = x1_ref[rows, :] + _dot(onehot_gate, buf_ref[par, sub * per_tile:(sub + 1) * per_tile, :])
        slots.append(slot)
        affs.append(aff)

    nch = [[_cdiv_pow2(bsel_s[e, step * COMBINE_TILES + sub + 1] - aligned(e, step * COMBINE_TILES + sub), SLOT_CHUNK)
            for e in range(N_EXPERTS)] for sub in range(COMBINE_TILES)]

    @pl.when(functools.reduce(jnp.maximum, [n for per_sub in nch for n in per_sub]) > 1)
    def _():
        lane64 = lax.broadcasted_iota(I32, (ROUTE_TILE, SLOT_CHUNK), 1).astype(F32)
        for sub in range(COMBINE_TILES):
            tile = step * COMBINE_TILES + sub
            rows = slice(sub * ROUTE_TILE, (sub + 1) * ROUTE_TILE)
            for e in range(N_EXPERTS):
                slot_e = slots[sub][:, e:e + 1]

                def extra(c, carry):
                    w = window(e, tile, c)
                    cp = pltpu.make_async_copy(ye_hbm.at[pl.ds(w, SLOT_CHUNK)], xbuf_ref, xsem_ref.at[0])
                    cp.start()
                    cp.wait()
                    first = (aligned(e, tile) + c * SLOT_CHUNK).astype(F32)
                    hit = (lane64 + (w - e * cap).astype(F32) == slot_e) & (slot_e >= first)
                    y_ref[rows, :] += affs[sub][:, e:e + 1] * _dot(_ones_where(hit, BF16), xbuf_ref[...])
                    return carry

                lax.fori_loop(1, nch[sub][e], extra, 0)

    acc = y_ref[...]
    ms = jnp.mean(acc * acc, axis=-1, keepdims=True)
    y_ref[...] = acc * lax.rsqrt(ms + EPS) * gf_ref[...]

    @pl.when(step == nstep - 1)
    def _():
        for sub in range(COMBINE_TILES):
            for cp in first_chunks(step, 1 - par, sub):
                cp.wait()


def _combine(bsel_i, x1, aff, tables_t, ye, gf):
    tokens = x1.shape[0]
    cap = CAPACITY_FACTOR * tokens // N_EXPERTS
    ntile = tokens // ROUTE_TILE
    idx = np.arange(ROUTE_TILE)
    low = jnp.asarray(idx[:, None] >= idx[None, :], BF16)
    spread = jnp.asarray(np.arange(LANES)[:, None] == np.arange(N_EXPERTS * SLOT_CHUNK)[None, :] // SLOT_CHUNK, BF16)
    taut, needt, beqt, bselt = tables_t
    rows = COMBINE_TILES * ROUTE_TILE
    rowvec = pl.BlockSpec((8, LANES), lambda t, *_: (0, 0))
    tilevec = pl.BlockSpec((COMBINE_TILES, 1, LANES), lambda t, *_: (t, 0, 0))
    grid_spec = pltpu.PrefetchScalarGridSpec(
        num_scalar_prefetch=1,
        grid=(ntile // COMBINE_TILES,),
        in_specs=[
            pl.BlockSpec((rows, D_MODEL), lambda t, *_: (t, 0)),
            pl.BlockSpec((rows, LANES), lambda t, *_: (t, 0)),
            rowvec, rowvec, tilevec, tilevec,
            pl.BlockSpec(low.shape, lambda t, *_: (0, 0)),
            pl.BlockSpec(spread.shape, lambda t, *_: (0, 0)),
            pl.BlockSpec((1, D_MODEL), lambda t, *_: (0, 0)),
            pl.BlockSpec(memory_space=pl.ANY),
        ],
        out_specs=pl.BlockSpec((rows, D_MODEL), lambda t, *_: (t, 0)),
        scratch_shapes=[
            pltpu.VMEM((2, COMBINE_TILES * N_EXPERTS * SLOT_CHUNK, D_MODEL), BF16),
            pltpu.VMEM((SLOT_CHUNK, D_MODEL), BF16),
            pltpu.SemaphoreType.DMA((2,)),
            pltpu.SemaphoreType.DMA((1,)),
        ],
    )
    return pl.pallas_call(
        functools.partial(_combine_kernel, cap=cap, total=N_EXPERTS * cap, nstep=ntile // COMBINE_TILES),
        grid_spec=grid_spec,
        out_shape=jax.ShapeDtypeStruct((tokens, D_MODEL), F32),
        compiler_params=_params(("arbitrary",)),
        name="combine",
    )(bsel_i, x1, aff, taut, needt, beqt.reshape(LANES, 1, LANES), bselt.reshape(LANES, 1, LANES), low, spread,
      gf, ye)


def _encoder(x, w):
    batch, seq, width = x.shape
    tokens = batch * seq
    max_dil = max(dil for _, dil in GROUPS)
    assert width == D_MODEL and x.dtype == F32
    assert seq % (ATT_SUB * max_dil) == 0 and seq % (LANES * ROW_ALIGN) == 0 and seq % TOKEN_TILE == 0
    assert LANES % min(LANES, FFT_STEP_ROWS // (seq // LANES)) == 0
    assert tokens % (ROUTE_TILE * COMBINE_TILES) == 0 and tokens % (16 * LANES) == 0 and tokens // ROUTE_TILE < LANES
    assert (CAPACITY_FACTOR * tokens) % (N_EXPERTS * ROW_ALIGN) == 0
    xt = x.reshape(tokens, D_MODEL)
    *qkvs, vr, vi, gates = _in_proj(xt, w["g1"], w["w_in"], w["w_gate"], w["b_gate"], w["cs"], batch, seq)
    outs, lses = [], []
    for g in range(N_GROUPS):
        o, lse = _attention(qkvs[g], w["bias"][g], g)
        outs.append(o)
        lses.append(lse)
    four = _fourier(vr, vi, batch, seq)
    x1, xn, aff, afft = _mix(xt, outs, lses, four, gates, w["w_attn"], w["w_four"], w["w_out"], w["g2"],
                             w["w_router"])
    tau, need, beq_i, bsel_i, taut, needt, beqt, bselt = _route(afft)
    idx = np.arange(ROUTE_TILE)
    u = jnp.asarray(idx[:, None] <= idx[None, :], BF16)
    xe = _gather(bsel_i, afft, tau, need, beq_i, xn, u)
    ye = _ffn(xe, w["w_eg"], w["w_eu"], w["w_ed"])
    y = _combine(bsel_i, x1, aff, (taut, needt, beqt, bselt), ye, w["gf"])
    return y.reshape(batch, seq, D_MODEL)


def _prepare_weights(rel_bias, norm1_g, w_in, w_attn_br, w_four_br, w_gate, b_gate, w_out,
                     norm2_g, w_router, w_exp_gate, w_exp_up, w_exp_down, final_g):
    c, s = _dft_mats(F_CH)
    qkv_cols = w_in[0][:, :QKV_W].reshape(D_MODEL, 3, N_GROUPS, GROUP_W).transpose(0, 2, 1, 3)
    w_in_grouped = jnp.concatenate([qkv_cols.reshape(D_MODEL, QKV_W), w_in[0][:, QKV_W:]], axis=1)
    w_router = jnp.pad(w_router[0], ((0, 0), (0, LANES - N_EXPERTS)))
    w_router_hi = w_router.astype(BF16)
    return {
        "g1": norm1_g[0].reshape(1, D_MODEL),
        "w_in": w_in_grouped.astype(BF16),
        "w_gate": w_gate[0].astype(BF16),
        "b_gate": b_gate[0].reshape(1, 2 * D_MODEL),
        "cs": jnp.asarray(np.concatenate([c, s], axis=1), BF16),
        "bias": [_attention_bias(rel_bias, g) for g in range(N_GROUPS)],
        "w_attn": w_attn_br[0].astype(BF16),
        "w_four": w_four_br[0].astype(BF16),
        "w_out": w_out[0].astype(BF16),
        "g2": norm2_g[0].reshape(1, D_MODEL),
        "w_router": jnp.concatenate([w_router_hi, (w_router - w_router_hi.astype(F32)).astype(BF16)], axis=1),
        "w_eg": w_exp_gate[0],
        "w_eu": w_exp_up[0],
        "w_ed": w_exp_down[0],
        "gf": final_g.reshape(1, D_MODEL),
    }


def kernel(x_prompt, x_sample, rel_bias, norm1_g, w_in, w_attn_br, w_four_br, w_gate, b_gate, w_out,
           norm2_g, w_router, w_exp_gate, w_exp_up, w_exp_down, final_g):
    w = _prepare_weights(rel_bias, norm1_g, w_in, w_attn_br, w_four_br, w_gate, b_gate, w_out,
                         norm2_g, w_router, w_exp_gate, w_exp_up, w_exp_down, final_g)
    return (_encoder(x_prompt, w), _encoder(x_sample, w))
```
